```python
import math
import jax
import jax.numpy as jnp
from jax import lax
import numpy as np

D_MODEL = 1024
BATCH = 4
SEQ = 4096
DEPTH = 2
DEC_BATCH = 128
DEC_SEQ = 1
PAST_LEN = 8192
PAGE_SIZE = 128

GDN_HEADS = 8
GDN_DK = 128
GDN_DV = 128
GDN_QK_W = GDN_HEADS * GDN_DK
GDN_V_W = GDN_HEADS * GDN_DV
CONV_W = 4
CONV_CH = 2 * GDN_QK_W + GDN_V_W
CHUNK = 64
N_Q_HEADS = 16
N_KV_HEADS = 4
HEAD_DIM = 64
Q_GROUP = N_Q_HEADS // N_KV_HEADS
ATT_W = N_Q_HEADS * HEAD_DIM
KV_W = N_KV_HEADS * HEAD_DIM
WINDOW = 128
BLOCK = 128
EPS = 1e-6

kernel_name = 'yoco_gdn_swa_sink_step'


def rmsnorm(x, g):
    xf = x.astype(jnp.float32)
    y = xf * lax.rsqrt(jnp.mean(xf * xf, axis=-1, keepdims=True) + EPS)
    return (y * g.astype(jnp.float32)).astype(x.dtype)


def l2norm(x):
    xf = x.astype(jnp.float32)
    return xf * lax.rsqrt(jnp.sum(xf * xf, axis=-1, keepdims=True) + EPS)


def alibi_slopes(n):
    return jnp.exp2(-8.0 * jnp.arange(1, n + 1, dtype=jnp.float32) / n)


def causal_conv_silu(u, buf, w):
    L = u.shape[1]
    full = jnp.concatenate([buf.astype(u.dtype), u], axis=1)
    out = sum(full[:, i:i + L] * w[i] for i in range(CONV_W))
    return jax.nn.silu(out), full[:, L:]


def _to_chunks(a, n, c):
    a = a.reshape((a.shape[0], n, c) + a.shape[2:])
    return jnp.moveaxis(a, 3, 2)


def gated_delta_rule(q, k, v, g, beta, s0):
    b, L, h, dk = q.shape
    dv = v.shape[-1]
    c = min(CHUNK, L)
    pad = (-L) % c
    if pad:
        q, k, v, g, beta = [jnp.pad(a, [(0, 0), (0, pad)] + [(0, 0)] * (a.ndim - 2)) for a in (q, k, v, g, beta)]
    n = (L + pad) // c
    qc, kc, vc, gc, bc = [_to_chunks(a, n, c) for a in (q, k, v, g, beta)]
    G = jnp.cumsum(gc, axis=-1)
    idx = jnp.arange(c)
    causal = idx[:, None] >= idx[None, :]
    strict = idx[:, None] > idx[None, :]
    decay = jnp.exp(jnp.where(causal, G[..., :, None] - G[..., None, :], -jnp.inf))
    kb = kc * bc[..., None]
    m = jnp.where(strict, jnp.einsum('bnhid,bnhjd->bnhij', kb, kc) * decay, 0.0)
    eye = jnp.eye(c, dtype=jnp.float32)
    rhs = jnp.concatenate([vc * bc[..., None], kb * jnp.exp(G)[..., None]], axis=-1)
    sol = lax.linalg.triangular_solve(eye + m, rhs, left_side=True, lower=True, unit_diagonal=True)
    u_val, w_k = sol[..., :dv], sol[..., dv:]
    qk = jnp.where(causal, jnp.einsum('bnhid,bnhjd->bnhij', qc, kc) * decay, 0.0)
    q_dec = qc * jnp.exp(G)[..., None]
    k_dec = kc * jnp.exp(G[..., -1:] - G)[..., None]
    g_tot = jnp.exp(G[..., -1])

    def step(s, xs):
        u_i, w_i, qk_i, qd_i, kd_i, gt_i = xs
        v_new = u_i - jnp.einsum('bhcd,bhde->bhce', w_i, s)
        o_i = jnp.einsum('bhcd,bhde->bhce', qd_i, s) + jnp.einsum('bhij,bhje->bhie', qk_i, v_new)
        s = s * gt_i[..., None, None] + jnp.einsum('bhcd,bhce->bhde', kd_i, v_new)
        return s, o_i

    xs = tuple(jnp.moveaxis(a, 1, 0) for a in (u_val, w_k, qk, q_dec, k_dec, g_tot))
    s_fin, o = lax.scan(step, s0, xs)
    o = jnp.moveaxis(o, (0, 2), (1, 3)).reshape(b, n * c, h, dv)[:, :L]
    return o, s_fin


def gdn_layer(x, conv_buf, s0, norm_g, w_in, conv_w, a_log, dt_bias, o_norm, w_out):
    b, L, _ = x.shape
    proj = rmsnorm(x, norm_g) @ w_in
    qkv, gate, a, beta_logit = jnp.split(proj, [CONV_CH, CONV_CH + GDN_V_W, CONV_CH + GDN_V_W + GDN_HEADS], axis=-1)
    qkv, new_buf = causal_conv_silu(qkv, conv_buf, conv_w)
    q, k, v = jnp.split(qkv, [GDN_QK_W, 2 * GDN_QK_W], axis=-1)
    q = l2norm(q.reshape(b, L, GDN_HEADS, GDN_DK)) * (GDN_DK ** -0.5)
    k = l2norm(k.reshape(b, L, GDN_HEADS, GDN_DK))
    v = v.reshape(b, L, GDN_HEADS, GDN_DV).astype(jnp.float32)
    beta = jax.nn.sigmoid(beta_logit.astype(jnp.float32))
    g = -jnp.exp(a_log.astype(jnp.float32)) * jax.nn.softplus(a.astype(jnp.float32) + dt_bias.astype(jnp.float32))
    o, s_fin = gated_delta_rule(q, k, v, g, beta, s0.astype(jnp.float32))
    o = rmsnorm(o, o_norm) * jax.nn.silu(gate.reshape(b, L, GDN_HEADS, GDN_DV).astype(jnp.float32))
    y = o.reshape(b, L, GDN_V_W).astype(x.dtype) @ w_out
    return x + y, new_buf, s_fin.astype(s0.dtype)


def shared_kv(h, kv_norm, w_kv, k_norm):
    b, L, _ = h.shape
    k, v = jnp.split(rmsnorm(h, kv_norm) @ w_kv, 2, axis=-1)
    k = rmsnorm(k.reshape(b, L, N_KV_HEADS, HEAD_DIM), k_norm)
    return k, v.reshape(b, L, N_KV_HEADS, HEAD_DIM)


def band_blocks(t, nb):
    tb = t.reshape((t.shape[0], nb, BLOCK) + t.shape[2:])
    prev = jnp.concatenate([jnp.zeros_like(tb[:, :1]), tb[:, :-1]], axis=1)
    return jnp.concatenate([prev, tb], axis=2)


def windowed_sink_attention(q, k, v, dist, valid, sinks):
    slopes = alibi_slopes(N_Q_HEADS).reshape(N_KV_HEADS, Q_GROUP, 1, 1)
    s = jnp.einsum('bnqhgd,bnshd->bnhgqs', q, k, preferred_element_type=jnp.float32) * (HEAD_DIM ** -0.5)
    s = jnp.where(valid[None, :, None, None], s - slopes * dist.astype(jnp.float32), -jnp.inf)
    sink = sinks.astype(jnp.float32).reshape(1, 1, N_KV_HEADS, Q_GROUP, 1, 1)
    mx = jnp.maximum(jnp.max(s, axis=-1, keepdims=True), sink)
    p = jnp.exp(s - mx)
    p = p / (jnp.sum(p, axis=-1, keepdims=True) + jnp.exp(sink - mx))
    return jnp.einsum('bnhgqs,bnshd->bnqhgd', p.astype(v.dtype), v)


def swa_layer(x, kb, vb, dist, valid, norm_g, w_in, q_norm, sinks, w_out):
    b, L, _ = x.shape
    n_blk = kb.shape[1]
    q, gate = jnp.split(rmsnorm(x, norm_g) @ w_in, 2, axis=-1)
    q = rmsnorm(q.reshape(b, L, N_Q_HEADS, HEAD_DIM), q_norm)
    q = q.reshape(b, n_blk, L // n_blk, N_KV_HEADS, Q_GROUP, HEAD_DIM)
    o = windowed_sink_attention(q, kb, vb, dist, valid, sinks).reshape(b, L, ATT_W)
    o = o * jax.nn.silu(gate)
    return x + o @ w_out


def setup_inputs(seed: int = 0) -> dict:
    key = jax.random.key(seed)
    ks = jax.random.split(key, 24)
    n_a = DEPTH // 2
    n_b = DEPTH - n_a
    f32 = jnp.float32

    def nrm(k, shape, scale):
        return jax.random.normal(k, shape, f32) * scale

    def gain(k, shape):
        return 1.0 + 0.05 * jax.random.normal(k, shape, f32)

    in_a = CONV_CH + GDN_V_W + 2 * GDN_HEADS
    dt = jnp.exp(jax.random.uniform(ks[9], (n_a, GDN_HEADS), f32, math.log(1e-3), math.log(1e-1)))
    return {
        'x_prompt': nrm(ks[0], (BATCH, SEQ, D_MODEL), 1.0),
        'x_sample': nrm(ks[1], (DEC_BATCH, DEC_SEQ, D_MODEL), 1.0),
        'state_conv': nrm(ks[2], (n_a, DEC_BATCH, CONV_W - 1, CONV_CH), 1.0),
        'state_ssm': nrm(ks[3], (n_a, DEC_BATCH, GDN_HEADS, GDN_DK, GDN_DV), GDN_DK ** -0.5),
        'cache_k_win': nrm(ks[4], (DEC_BATCH, WINDOW, N_KV_HEADS, HEAD_DIM), 1.0),
        'cache_v_win': nrm(ks[5], (DEC_BATCH, WINDOW, N_KV_HEADS, HEAD_DIM), 1.0),
        'norm_a': gain(ks[6], (n_a, D_MODEL)),
        'w_in_a': nrm(ks[7], (n_a, D_MODEL, in_a), D_MODEL ** -0.5),
        'conv_w_a': nrm(ks[8], (n_a, CONV_W, CONV_CH), CONV_W ** -0.5),
        'a_log': jnp.log(jax.random.uniform(ks[10], (n_a, GDN_HEADS), f32, 1.0, 16.0)),
        'dt_bias': dt + jnp.log(-jnp.expm1(-dt)),
        'o_norm_a': gain(ks[11], (n_a, GDN_DV)),
        'w_out_a': nrm(ks[12], (n_a, GDN_V_W, D_MODEL), GDN_V_W ** -0.5),
        'kv_norm': gain(ks[13], (D_MODEL,)),
        'w_kv': nrm(ks[14], (D_MODEL, 2 * KV_W), D_MODEL ** -0.5),
        'k_norm': gain(ks[15], (HEAD_DIM,)),
        'norm_b': gain(ks[16], (n_b, D_MODEL)),
        'w_in_b': nrm(ks[17], (n_b, D_MODEL, 2 * ATT_W), D_MODEL ** -0.5),
        'q_norm': gain(ks[18], (n_b, HEAD_DIM)),
        'sinks': nrm(ks[19], (n_b, N_Q_HEADS), 0.5),
        'w_out_b': nrm(ks[20], (n_b, ATT_W, D_MODEL), ATT_W ** -0.5),
    }


def reference(x_prompt, x_sample, state_conv, state_ssm, cache_k_win, cache_v_win,
              norm_a, w_in_a, conv_w_a, a_log, dt_bias, o_norm_a, w_out_a,
              kv_norm, w_kv, k_norm, norm_b, w_in_b, q_norm, sinks, w_out_b):
    n_a = DEPTH // 2
    bp, lp, _ = x_prompt.shape
    ls = x_sample.shape[1]
    nb = lp // BLOCK
    qi = jnp.arange(BLOCK)[:, None]
    kj = jnp.arange(2 * BLOCK)[None, :]
    dist_p = qi - kj + BLOCK
    valid_p = (dist_p >= 0) & (dist_p <= WINDOW) & ((jnp.arange(nb)[:, None, None] > 0) | (kj >= BLOCK))
    dist_s = jnp.arange(ls)[:, None] - jnp.arange(-WINDOW, ls)[None, :]
    valid_s = ((dist_s >= 0) & (dist_s <= WINDOW))[None]

    hp, hs = x_prompt, x_sample
    conv_p, ssm_p, conv_s, ssm_s = [], [], [], []
    for layer in range(DEPTH):
        if layer < n_a:
            wa = (norm_a[layer], w_in_a[layer], conv_w_a[layer], a_log[layer], dt_bias[layer],
                  o_norm_a[layer], w_out_a[layer])
            hp, cbuf, st = gdn_layer(hp, jnp.zeros((bp, CONV_W - 1, CONV_CH), hp.dtype),
                                     jnp.zeros((bp,) + state_ssm.shape[2:], state_ssm.dtype), *wa)
            conv_p.append(cbuf)
            ssm_p.append(st)
            hs, cbuf, st = gdn_layer(hs, state_conv[layer], state_ssm[layer], *wa)
            conv_s.append(cbuf)
            ssm_s.append(st)
        else:
            if layer == n_a:
                kp, vp = shared_kv(hp, kv_norm, w_kv, k_norm)
                kn, vn = shared_kv(hs, kv_norm, w_kv, k_norm)
                k_win_p, v_win_p = kp[:, -WINDOW:], vp[:, -WINDOW:]
                k_all_s = jnp.concatenate([cache_k_win.astype(kn.dtype), kn], axis=1)
                v_all_s = jnp.concatenate([cache_v_win.astype(vn.dtype), vn], axis=1)
                k_win_s, v_win_s = k_all_s[:, -WINDOW:], v_all_s[:, -WINDOW:]
                k_band, v_band = band_blocks(kp, nb), band_blocks(vp, nb)
                k_s_blk, v_s_blk = k_all_s[:, None], v_all_s[:, None]
            j = layer - n_a
            wb = (norm_b[j], w_in_b[j], q_norm[j], sinks[j], w_out_b[j])
            hp = swa_layer(hp, k_band, v_band, dist_p, valid_p, *wb)
            hs = swa_layer(hs, k_s_blk, v_s_blk, dist_s, valid_s, *wb)
    conv_prompt, ssm_prompt = jnp.stack(conv_p), jnp.stack(ssm_p)
    conv_sample, ssm_sample = jnp.stack(conv_s), jnp.stack(ssm_s)
    return (hp, hs, conv_prompt, ssm_prompt, k_win_p, v_win_p, conv_sample, ssm_sample, k_win_s, v_win_s)
```

```python
import functools

import jax
import jax.numpy as jnp
from jax import lax
from jax.experimental import pallas as pl
from jax.experimental.pallas import tpu as pltpu

F32 = jnp.float32
BF16 = jnp.bfloat16
EPS = 1e-6

D_MODEL = 1024
GDN_HEADS = 8
GDN_D = 128
QKV_W = 3 * GDN_HEADS * GDN_D
CONV_W = 4
CHUNK = 64
N_Q_HEADS = 16
N_KV_HEADS = 4
Q_GROUP = N_Q_HEADS // N_KV_HEADS
HEAD_DIM = 64
KV_W = N_KV_HEADS * HEAD_DIM
ATT_W = N_Q_HEADS * HEAD_DIM
WINDOW = 128

TM_A = 256
CARRY = 8
VMEM_LIMIT = 52 * 1024 * 1024

_NT = (((1,), (1,)), ((), ()))
_TN = (((0,), (0,)), ((), ()))


def _dot(a, b):
    return jnp.dot(a, b, preferred_element_type=F32)


def _dot_nt(a, b):
    return lax.dot_general(a, b, _NT, preferred_element_type=F32)


def _dot_tn(a, b):
    return lax.dot_general(a, b, _TN, preferred_element_type=F32)


def _split(x):
    hi = x.astype(BF16)
    lo = (x - hi.astype(F32)).astype(BF16)
    return hi, lo


def _dot3(a, b):
    ah, al = _split(a)
    bh, bl = _split(b)
    return (_dot(al, bh) + _dot(ah, bl)) + _dot(ah, bh)


def _dot_exact_lhs(a_bf, b):
    b0 = b.astype(BF16)
    r1 = b - b0.astype(F32)
    b1 = r1.astype(BF16)
    b2 = (r1 - b1.astype(F32)).astype(BF16)
    return (_dot(a_bf, b2) + _dot(a_bf, b1)) + _dot(a_bf, b0)


def _dot_exact_rhs(a, b_bf):
    a0 = a.astype(BF16)
    r1 = a - a0.astype(F32)
    a1 = r1.astype(BF16)
    a2 = (r1 - a1.astype(F32)).astype(BF16)
    return (_dot(a2, b_bf) + _dot(a1, b_bf)) + _dot(a0, b_bf)


def _silu(x):
    return x * jax.nn.sigmoid(x)


def _softplus(x):
    return jnp.maximum(x, 0.0) + jnp.log1p(jnp.exp(-jnp.abs(x)))


def _unit_lower_inverse(m):
    n = m.shape[0]
    row = lax.broadcasted_iota(jnp.int32, (n, n), 0)
    col = lax.broadcasted_iota(jnp.int32, (n, n), 1)
    t = jnp.where(row == col, 1.0, 0.0).astype(F32) - m
    p = m
    steps = max(1, (n - 1).bit_length()) - 1
    for _ in range(steps):
        p = _dot3(p, p)
        t = t + _dot3(t, p)
    return t


def _full_spec(shape):
    nd = len(shape)
    return pl.BlockSpec(shape, lambda *_: (0,) * nd)


def _gdn_gates(ab, ab_t, alog_r, dtb_r, alog_c, dtb_c, tm, chunk):
    h = GDN_HEADS
    g_c = -jnp.exp(alog_r) * _softplus(ab[:, :h] + dtb_r)
    beta = jax.nn.sigmoid(ab[:, h:])
    g_r = -jnp.exp(alog_c) * _softplus(ab_t[:h, :] + dtb_c)
    row = lax.broadcasted_iota(jnp.int32, (tm, tm), 0)
    col = lax.broadcasted_iota(jnp.int32, (tm, tm), 1)
    same = (row // chunk) == (col // chunk)
    lower = jnp.where(same & (row >= col), 1.0, 0.0).astype(BF16)
    upper = jnp.where(same & (row <= col), 1.0, 0.0).astype(BF16)
    gc = _dot_exact_lhs(lower, g_c)
    gr = _dot_exact_rhs(g_r, upper)
    return beta, gc, gr


def _gdn_prompt_kernel(x_ref, norm_ref, wqkv_ref, wg_ref, wab_ref, wabt_ref, convw_ref,
                       alog_r_ref, dtb_r_ref, alog_c_ref, dtb_c_ref, onorm_ref, wout_ref,
                       y_ref, conv_ref, ssm_ref,
                       pre_scr, gate_scr, s_scr, k_scr, kb_scr, q_scr, qd_scr, kd_scr,
                       rhs_scr, dec_scr, gt_scr, o_scr):
    tm = x_ref.shape[1]
    n_chunks = tm // CHUNK
    l = pl.program_id(1)

    @pl.when(l == 0)
    def _():
        pre_scr[0:CARRY, :] = jnp.zeros((CARRY, QKV_W), F32)
        s_scr[...] = jnp.zeros(s_scr.shape, F32)

    x = x_ref[0]
    xn = x * lax.rsqrt(jnp.mean(x * x, axis=-1, keepdims=True) + EPS) * norm_ref[...]
    xb = xn.astype(BF16)
    pre_scr[CARRY:CARRY + tm, :] = _dot(xb, wqkv_ref[...])
    gate_scr[...] = _dot(xb, wg_ref[...])
    ab = _dot(xb, wab_ref[...])
    ab_t = _dot_nt(wabt_ref[...], xb)
    beta, gc, gr = _gdn_gates(ab, ab_t, alog_r_ref[...], dtb_r_ref[...],
                              alog_c_ref[...], dtb_c_ref[...], tm, CHUNK)
    e_g = jnp.exp(gc)
    g_last = jnp.concatenate(
        [jnp.broadcast_to(gc[c * CHUNK + CHUNK - 1:c * CHUNK + CHUNK, :], (CHUNK, GDN_HEADS))
         for c in range(n_chunks)], axis=0)
    e_kd = jnp.exp(g_last - gc)
    e_tot = jnp.exp(g_last)

    ci = lax.broadcasted_iota(jnp.int32, (CHUNK, CHUNK), 0)
    cj = lax.broadcasted_iota(jnp.int32, (CHUNK, CHUNK), 1)
    causal = ci >= cj
    strict = ci > cj

    def conv_act(j):
        cols = slice(j * GDN_D, (j + 1) * GDN_D)
        acc = pre_scr[CARRY - 3:CARRY - 3 + tm, cols] * convw_ref[0:1, cols]
        for i in range(1, CONV_W):
            acc = acc + pre_scr[CARRY - 3 + i:CARRY - 3 + i + tm, cols] * convw_ref[i:i + 1, cols]
        return _silu(acc)

    for h in range(GDN_HEADS):
        qh = conv_act(h)
        kh = conv_act(GDN_HEADS + h)
        vh = conv_act(2 * GDN_HEADS + h)
        qn = qh * lax.rsqrt(jnp.sum(qh * qh, axis=-1, keepdims=True) + EPS) * (GDN_D ** -0.5)
        kn = kh * lax.rsqrt(jnp.sum(kh * kh, axis=-1, keepdims=True) + EPS)
        b_col = beta[:, h:h + 1]
        eg_col = e_g[:, h:h + 1]
        kb = kn * b_col
        k_scr[h] = kn.astype(BF16)
        kb_scr[h] = kb.astype(BF16)
        q_scr[h] = qn.astype(BF16)
        qd_scr[h] = (qn * eg_col).astype(BF16)
        kd_scr[h] = (kn * e_kd[:, h:h + 1]).astype(BF16)
        rhs_scr[h, :, 0:GDN_D] = vh * b_col
        rhs_scr[h, :, GDN_D:2 * GDN_D] = kb * eg_col
        for c in range(n_chunks):
            rows = slice(c * CHUNK, (c + 1) * CHUNK)
            diff = gc[rows, h:h + 1] - gr[h:h + 1, rows]
            dec_scr[h, rows, :] = jnp.exp(jnp.where(causal, diff, -jnp.inf))
            gt_scr[h, c:c + 1, :] = jnp.broadcast_to(e_tot[c * CHUNK:c * CHUNK + 1, h:h + 1], (1, GDN_D))

    conv_ref[0] = pre_scr[tm + CARRY - 3:tm + CARRY, :]
    pre_scr[0:CARRY, :] = pre_scr[tm:tm + CARRY, :]

    def head_body(h, carry):
        s = s_scr[h]
        for c in range(n_chunks):
            rows = slice(c * CHUNK, (c + 1) * CHUNK)
            k = k_scr[h, rows, :]
            dec = dec_scr[h, rows, :]
            m = jnp.where(strict, _dot_nt(kb_scr[h, rows, :], k) * dec, 0.0)
            t = _unit_lower_inverse(m)
            sol = _dot3(t, rhs_scr[h, rows, :])
            u = sol[:, :GDN_D]
            w = sol[:, GDN_D:]
            qk = jnp.where(causal, _dot_nt(q_scr[h, rows, :], k) * dec, 0.0)
            sb = s.astype(BF16)
            v_new = u - _dot(w.astype(BF16), sb)
            vb = v_new.astype(BF16)
            o_scr[h, rows, :] = _dot(qd_scr[h, rows, :], sb) + _dot(qk.astype(BF16), vb)
            s = s * gt_scr[h, c:c + 1, :] + _dot_tn(kd_scr[h, rows, :], vb)
        s_scr[h] = s
        return carry

    lax.fori_loop(0, GDN_HEADS, head_body, 0)

    @pl.when(l == pl.num_programs(1) - 1)
    def _():
        ssm_ref[0] = s_scr[...]

    pieces = []
    for h in range(GDN_HEADS):
        o = o_scr[h]
        on = o * lax.rsqrt(jnp.mean(o * o, axis=-1, keepdims=True) + EPS) * onorm_ref[...]
        pieces.append((on * _silu(gate_scr[:, h * GDN_D:(h + 1) * GDN_D])).astype(BF16))
    y_ref[0] = x + _dot(jnp.concatenate(pieces, axis=1), wout_ref[...])


def _gdn_prompt(x, norm, wqkv, wg, wab, wabt, convw, alog, dtb, onorm, wout):
    b, l, d = x.shape
    tm = TM_A
    h = GDN_HEADS
    grid = (b, l // tm)
    in_specs = [
        pl.BlockSpec((1, tm, d), lambda i, j: (i, j, 0)),
        _full_spec((1, d)), _full_spec(wqkv.shape), _full_spec(wg.shape), _full_spec(wab.shape),
        _full_spec(wabt.shape), _full_spec(convw.shape),
        _full_spec((1, h)), _full_spec((1, h)), _full_spec((h, 1)), _full_spec((h, 1)),
        _full_spec((1, GDN_D)), _full_spec(wout.shape),
    ]
    out_specs = [
        pl.BlockSpec((1, tm, d), lambda i, j: (i, j, 0)),
        pl.BlockSpec((1, CONV_W - 1, QKV_W), lambda i, j: (i, 0, 0)),
        pl.BlockSpec((1, h, GDN_D, GDN_D), lambda i, j: (i, 0, 0, 0)),
    ]
    out_shape = [
        jax.ShapeDtypeStruct((b, l, d), F32),
        jax.ShapeDtypeStruct((b, CONV_W - 1, QKV_W), F32),
        jax.ShapeDtypeStruct((b, h, GDN_D, GDN_D), F32),
    ]
    scratch = [
        pltpu.VMEM((tm + CARRY, QKV_W), F32),
        pltpu.VMEM((tm, h * GDN_D), F32),
        pltpu.VMEM((h, GDN_D, GDN_D), F32),
        pltpu.VMEM((h, tm, GDN_D), BF16),
        pltpu.VMEM((h, tm, GDN_D), BF16),
        pltpu.VMEM((h, tm, GDN_D), BF16),
        pltpu.VMEM((h, tm, GDN_D), BF16),
        pltpu.VMEM((h, tm, GDN_D), BF16),
        pltpu.VMEM((h, tm, 2 * GDN_D), F32),
        pltpu.VMEM((h, tm, CHUNK), F32),
        pltpu.VMEM((h, 8, GDN_D), F32),
        pltpu.VMEM((h, tm, GDN_D), F32),
    ]
    return pl.pallas_call(
        _gdn_prompt_kernel,
        grid=grid, in_specs=in_specs, out_specs=out_specs, out_shape=out_shape,
        scratch_shapes=scratch,
        compiler_params=pltpu.CompilerParams(
            dimension_semantics=("arbitrary", "arbitrary"), vmem_limit_bytes=VMEM_LIMIT),
        name="gdn_prompt",
    )(x, norm.reshape(1, d), wqkv, wg, wab, wabt, convw,
      alog.reshape(1, h), dtb.reshape(1, h), alog.reshape(h, 1), dtb.reshape(h, 1),
      onorm.reshape(1, GDN_D), wout)


def _gdn_sample_front_kernel(x_ref, norm_ref, wqkv_ref, wg_ref, wab_ref, convw_ref,
                             c0_ref, c1_ref, c2_ref, alog_r_ref, dtb_r_ref,
                             pre_ref, q_ref, k_ref, u_ref, w_ref, qd_ref, gate_ref, eg_ref):
    x = x_ref[...]
    xn = x * lax.rsqrt(jnp.mean(x * x, axis=-1, keepdims=True) + EPS) * norm_ref[...]
    xb = xn.astype(BF16)
    pre = _dot(xb, wqkv_ref[...])
    pre_ref[...] = pre
    gate_ref[...] = _dot(xb, wg_ref[...])
    ab = _dot(xb, wab_ref[...])
    g = -jnp.exp(alog_r_ref[...]) * _softplus(ab[:, :GDN_HEADS] + dtb_r_ref[...])
    beta = jax.nn.sigmoid(ab[:, GDN_HEADS:])
    e_g = jnp.exp(g)
    eg_ref[...] = e_g
    act = _silu(((c0_ref[...] * convw_ref[0:1, :] + c1_ref[...] * convw_ref[1:2, :])
                 + c2_ref[...] * convw_ref[2:3, :]) + pre * convw_ref[3:4, :])
    w_all = GDN_HEADS * GDN_D
    for h in range(GDN_HEADS):
        cols = slice(h * GDN_D, (h + 1) * GDN_D)
        qh = act[:, h * GDN_D:(h + 1) * GDN_D]
        kh = act[:, w_all + h * GDN_D:w_all + (h + 1) * GDN_D]
        vh = act[:, 2 * w_all + h * GDN_D:2 * w_all + (h + 1) * GDN_D]
        qn = qh * lax.rsqrt(jnp.sum(qh * qh, axis=-1, keepdims=True) + EPS) * (GDN_D ** -0.5)
        kn = kh * lax.rsqrt(jnp.sum(kh * kh, axis=-1, keepdims=True) + EPS)
        b_col = beta[:, h:h + 1]
        eg_col = e_g[:, h:h + 1]
        q_ref[:, cols] = qn
        k_ref[:, cols] = kn
        u_ref[:, cols] = vh * b_col
        w_ref[:, cols] = kn * b_col * eg_col
        qd_ref[:, cols] = qn * eg_col


def _gdn_sample_state_kernel(q_ref, k_ref, u_ref, w_ref, qd_ref, eg_ref, s_ref, o_ref, s_out_ref):
    bb = s_ref.shape[0]
    row = lax.broadcasted_iota(jnp.int32, (8, GDN_D), 0)
    for b in range(bb):
        for h in range(GDN_HEADS):
            cols = slice(h * GDN_D, (h + 1) * GDN_D)
            s = s_ref[b, h]
            sb = s.astype(BF16)
            q = q_ref[b:b + 1, cols].astype(BF16).astype(F32)
            k = k_ref[b:b + 1, cols].astype(BF16).astype(F32)
            lhs = jnp.where(row == 0, w_ref[b:b + 1, cols],
                            jnp.where(row == 1, qd_ref[b:b + 1, cols], 0.0))
            prod = _dot(lhs.astype(BF16), sb)
            v_new = u_ref[b:b + 1, cols] - prod[0:1, :]
            vb = v_new.astype(BF16).astype(F32)
            qk = jnp.sum(q * k, axis=-1, keepdims=True)
            o_ref[b:b + 1, cols] = prod[1:2, :] + qk.astype(BF16).astype(F32) * vb
            k8 = jnp.where(row == 0, k, 0.0).astype(BF16)
            v8 = jnp.where(row == 0, vb, 0.0).astype(BF16)
            s_out_ref[b, h] = s * eg_ref[b:b + 1, h:h + 1] + _dot_tn(k8, v8)


def _out_proj_kernel(x_ref, o_ref, gate_ref, onorm_ref, wout_ref, y_ref, *, head_w, norm_heads):
    pieces = []
    for h in range(x_ref.shape[1] // head_w):
        cols = slice(h * head_w, (h + 1) * head_w)
        o = o_ref[:, cols]
        if norm_heads:
            o = o * lax.rsqrt(jnp.mean(o * o, axis=-1, keepdims=True) + EPS) * onorm_ref[...]
        pieces.append((o * _silu(gate_ref[:, cols])).astype(BF16))
    y_ref[...] = x_ref[...] + _dot(jnp.concatenate(pieces, axis=1), wout_ref[...])


def _gdn_sample(x, conv_state, ssm_state, norm, wqkv, wg, wab, convw, alog, dtb, onorm, wout):
    n, d = x.shape
    h = GDN_HEADS
    hw = h * GDN_D
    conv_t = jnp.transpose(conv_state, (1, 0, 2))
    f = lambda shape: jax.ShapeDtypeStruct(shape, F32)
    pre, q, k, u, w, qd, gate, eg = pl.pallas_call(
        _gdn_sample_front_kernel,
        out_shape=[f((n, QKV_W)), f((n, hw)), f((n, hw)), f((n, hw)), f((n, hw)), f((n, hw)),
                   f((n, hw)), f((n, h))],
        compiler_params=pltpu.CompilerParams(vmem_limit_bytes=VMEM_LIMIT),
        name="gdn_sample_front",
    )(x, norm.reshape(1, d), wqkv, wg, wab, convw, conv_t[0], conv_t[1], conv_t[2],
      alog.reshape(1, h), dtb.reshape(1, h))
    bb = 8
    row_spec = lambda width: pl.BlockSpec((bb, width), lambda i: (i, 0))
    st_spec = pl.BlockSpec((bb, h, GDN_D, GDN_D), lambda i: (i, 0, 0, 0))
    o, s_new = pl.pallas_call(
        _gdn_sample_state_kernel,
        grid=(n // bb,),
        in_specs=[row_spec(hw)] * 5 + [row_spec(h), st_spec],
        out_specs=[row_spec(hw), st_spec],
        out_shape=[f((n, hw)), f(ssm_state.shape)],
        compiler_params=pltpu.CompilerParams(dimension_semantics=("arbitrary",)),
        name="gdn_sample_state",
    )(q, k, u, w, qd, eg, ssm_state)
    y = pl.pallas_call(
        functools.partial(_out_proj_kernel, head_w=GDN_D, norm_heads=True),
        out_shape=f((n, d)),
        name="gdn_sample_out",
    )(x, o, gate, onorm.reshape(1, GDN_D), wout)
    conv_new = jnp.stack([conv_t[1], conv_t[2], pre], axis=1)
    return y, conv_new, s_new


def _head_rms(x, gain_row, n_heads):
    pieces = []
    for h in range(n_heads):
        xh = x[:, h * HEAD_DIM:(h + 1) * HEAD_DIM]
        pieces.append(xh * lax.rsqrt(jnp.mean(xh * xh, axis=-1, keepdims=True) + EPS) * gain_row)
    return pieces


def _swa_prompt_kernel(y_ref, kvnorm_ref, wkv_ref, knorm_ref, normb_ref, win_ref, qnorm_ref,
                       sinks_ref, wout_ref, out_ref, kwin_ref, vwin_ref, k_scr, v_scr):
    tq = y_ref.shape[1]
    blk = pl.program_id(1)

    @pl.when(blk == 0)
    def _():
        k_scr[0:tq, :] = jnp.zeros((tq, KV_W), BF16)
        v_scr[0:tq, :] = jnp.zeros((tq, KV_W), BF16)

    @pl.when(blk > 0)
    def _():
        k_scr[0:tq, :] = k_scr[tq:2 * tq, :]
        v_scr[0:tq, :] = v_scr[tq:2 * tq, :]

    y = y_ref[0]
    yn = y * lax.rsqrt(jnp.mean(y * y, axis=-1, keepdims=True) + EPS)
    kv = _dot((yn * kvnorm_ref[...]).astype(BF16), wkv_ref[...])
    k_new = jnp.concatenate(_head_rms(kv[:, :KV_W], knorm_ref[...], N_KV_HEADS), axis=1)
    v_new = kv[:, KV_W:]
    kwin_ref[0] = k_new
    vwin_ref[0] = v_new
    k_scr[tq:2 * tq, :] = k_new.astype(BF16)
    v_scr[tq:2 * tq, :] = v_new.astype(BF16)

    qg = _dot((yn * normb_ref[...]).astype(BF16), win_ref[...])
    q_heads = _head_rms(qg[:, :ATT_W], qnorm_ref[...] * (HEAD_DIM ** -0.5), N_Q_HEADS)

    qi = lax.broadcasted_iota(jnp.int32, (Q_GROUP * tq, 2 * tq), 0) % tq
    kj = lax.broadcasted_iota(jnp.int32, (Q_GROUP * tq, 2 * tq), 1)
    dist = qi - kj + tq
    valid = (dist >= 0) & (dist <= WINDOW) & ((blk > 0) | (kj >= tq))
    dist_f = dist.astype(F32)
    grp = lax.broadcasted_iota(jnp.int32, (Q_GROUP * tq, 1), 0) // tq

    o_heads = []
    for hk in range(N_KV_HEADS):
        kh = k_scr[:, hk * HEAD_DIM:(hk + 1) * HEAD_DIM]
        vh = v_scr[:, hk * HEAD_DIM:(hk + 1) * HEAD_DIM]
        q4 = jnp.concatenate([q_heads[hk * Q_GROUP + g] for g in range(Q_GROUP)], axis=0).astype(BF16)
        slope = jnp.zeros((Q_GROUP * tq, 1), F32)
        sink = jnp.zeros((Q_GROUP * tq, 1), F32)
        for g in range(Q_GROUP):
            hq = hk * Q_GROUP + g
            slope = jnp.where(grp == g, 2.0 ** (-8.0 * (hq + 1) / N_Q_HEADS), slope)
            sink = jnp.where(grp == g, sinks_ref[0:1, hq:hq + 1], sink)
        s = _dot_nt(q4, kh)
        s = jnp.where(valid, s - slope * dist_f, -jnp.inf)
        mx = jnp.maximum(jnp.max(s, axis=-1, keepdims=True), sink)
        p = jnp.exp(s - mx)
        p = p / (jnp.sum(p, axis=-1, keepdims=True) + jnp.exp(sink - mx))
        o4 = _dot(p.astype(BF16), vh)
        for g in range(Q_GROUP):
            o_heads.append(o4[g * tq:(g + 1) * tq, :])
    o = jnp.concatenate(o_heads, axis=1)
    o = (o * _silu(qg[:, ATT_W:])).astype(BF16)
    out_ref[0] = y + _dot(o, wout_ref[...])


def _swa_prompt(y, kvnorm, wkv, knorm, normb, win, qnorm, sinks, wout):
    b, l, d = y.shape
    tq = WINDOW
    nb = l // tq
    f = lambda shape: jax.ShapeDtypeStruct(shape, F32)
    in_specs = [
        pl.BlockSpec((1, tq, d), lambda i, j: (i, j, 0)),
        _full_spec((1, d)), _full_spec(wkv.shape), _full_spec((1, HEAD_DIM)), _full_spec((1, d)),
        _full_spec(win.shape), _full_spec((1, HEAD_DIM)), _full_spec((1, N_Q_HEADS)),
        _full_spec(wout.shape),
    ]
    out_specs = [
        pl.BlockSpec((1, tq, d), lambda i, j: (i, j, 0)),
        pl.BlockSpec((1, tq, KV_W), lambda i, j: (i, 0, 0)),
        pl.BlockSpec((1, tq, KV_W), lambda i, j: (i, 0, 0)),
    ]
    return pl.pallas_call(
        _swa_prompt_kernel,
        grid=(b, nb), in_specs=in_specs, out_specs=out_specs,
        out_shape=[f((b, l, d)), f((b, tq, KV_W)), f((b, tq, KV_W))],
        scratch_shapes=[pltpu.VMEM((2 * tq, KV_W), BF16), pltpu.VMEM((2 * tq, KV_W), BF16)],
        compiler_params=pltpu.CompilerParams(
            dimension_semantics=("arbitrary", "arbitrary"), vmem_limit_bytes=VMEM_LIMIT),
        name="swa_prompt",
    )(y, kvnorm.reshape(1, d), wkv, knorm.reshape(1, HEAD_DIM), normb.reshape(1, d), win,
      qnorm.reshape(1, HEAD_DIM), sinks.reshape(1, N_Q_HEADS), wout)


def _swa_sample_front_kernel(y_ref, kvnorm_ref, wkv_ref, knorm_ref, normb_ref, win_ref, qnorm_ref,
                             k_ref, v_ref, q_ref, gate_ref):
    y = y_ref[...]
    yn = y * lax.rsqrt(jnp.mean(y * y, axis=-1, keepdims=True) + EPS)
    kv = _dot((yn * kvnorm_ref[...]).astype(BF16), wkv_ref[...])
    k_ref[...] = jnp.concatenate(_head_rms(kv[:, :KV_W], knorm_ref[...], N_KV_HEADS), axis=1)
    v_ref[...] = kv[:, KV_W:]
    qg = _dot((yn * normb_ref[...]).astype(BF16), win_ref[...])
    q_ref[...] = jnp.concatenate(
        _head_rms(qg[:, :ATT_W], qnorm_ref[...] * (HEAD_DIM ** -0.5), N_Q_HEADS), axis=1)
    gate_ref[...] = qg[:, ATT_W:]


def _swa_sample_attn_kernel(q_ref, kn_ref, vn_ref, kc_ref, vc_ref, sinks_ref, slopes_ref,
                            o_ref, kwin_ref, vwin_ref):
    bb = kc_ref.shape[0]
    w = kc_ref.shape[1]
    hrow = lax.broadcasted_iota(jnp.int32, (N_Q_HEADS, KV_W), 0) // Q_GROUP
    lblk = lax.broadcasted_iota(jnp.int32, (N_Q_HEADS, KV_W), 1) // HEAD_DIM
    own = hrow == lblk
    dist_c = (w - lax.broadcasted_iota(jnp.int32, (1, w), 1)).astype(F32)
    krow = lax.broadcasted_iota(jnp.int32, (w, KV_W), 0)
    slopes = slopes_ref[...]
    sink = sinks_ref[...]
    for b in range(bb):
        q = q_ref[b]
        qm = jnp.where(own, jnp.concatenate([q] * N_KV_HEADS, axis=1), 0.0).astype(BF16)
        kc = kc_ref[b]
        vc = vc_ref[b]
        kn = kn_ref[b]
        vn = vn_ref[b]
        s_c = _dot_nt(qm, kc.astype(BF16)) - slopes * dist_c
        s_n = jnp.sum(qm.astype(F32) * kn.astype(BF16).astype(F32), axis=-1, keepdims=True)
        mx = jnp.maximum(jnp.maximum(jnp.max(s_c, axis=-1, keepdims=True), s_n), sink)
        p_c = jnp.exp(s_c - mx)
        p_n = jnp.exp(s_n - mx)
        den = jnp.sum(p_c, axis=-1, keepdims=True) + p_n + jnp.exp(sink - mx)
        p_c = p_c / den
        p_n = p_n / den
        r = _dot(p_c.astype(BF16), vc.astype(BF16))
        r = r + p_n.astype(BF16).astype(F32) * vn.astype(BF16).astype(F32)
        r = jnp.where(own, r, 0.0)
        acc = r[:, 0:HEAD_DIM]
        for blk in range(1, N_KV_HEADS):
            acc = acc + r[:, blk * HEAD_DIM:(blk + 1) * HEAD_DIM]
        o_ref[b] = acc
        kwin_ref[b] = jnp.where(krow == w - 1, kn, pltpu.roll(kc, w - 1, 0))
        vwin_ref[b] = jnp.where(krow == w - 1, vn, pltpu.roll(vc, w - 1, 0))


def _swa_sample(y, cache_k, cache_v, kvnorm, wkv, knorm, normb, win, qnorm, sinks, wout):
    n, d = y.shape
    w = cache_k.shape[1]
    f = lambda shape: jax.ShapeDtypeStruct(shape, F32)
    kn, vn, q, gate = pl.pallas_call(
        _swa_sample_front_kernel,
        out_shape=[f((n, KV_W)), f((n, KV_W)), f((n, ATT_W)), f((n, ATT_W))],
        compiler_params=pltpu.CompilerParams(vmem_limit_bytes=VMEM_LIMIT),
        name="swa_sample_front",
    )(y, kvnorm.reshape(1, d), wkv, knorm.reshape(1, HEAD_DIM), normb.reshape(1, d), win,
      qnorm.reshape(1, HEAD_DIM))
    bb = 8
    slopes = (2.0 ** (-8.0 * jnp.arange(1, N_Q_HEADS + 1, dtype=F32) / N_Q_HEADS)).reshape(N_Q_HEADS, 1)
    spec3 = lambda a, c: pl.BlockSpec((bb, a, c), lambda i: (i, 0, 0))
    o, kwin, vwin = pl.pallas_call(
        _swa_sample_attn_kernel,
        grid=(n // bb,),
        in_specs=[spec3(N_Q_HEADS, HEAD_DIM), spec3(1, KV_W), spec3(1, KV_W), spec3(w, KV_W),
                  spec3(w, KV_W), _full_spec((N_Q_HEADS, 1)), _full_spec((N_Q_HEADS, 1))],
        out_specs=[spec3(N_Q_HEADS, HEAD_DIM), spec3(w, KV_W), spec3(w, KV_W)],
        out_shape=[f((n, N_Q_HEADS, HEAD_DIM)), f((n, w, KV_W)), f((n, w, KV_W))],
        compiler_params=pltpu.CompilerParams(dimension_semantics=("arbitrary",)),
        name="swa_sample_attn",
    )(q.reshape(n, N_Q_HEADS, HEAD_DIM), kn.reshape(n, 1, KV_W), vn.reshape(n, 1, KV_W),
      cache_k.reshape(n, w, KV_W), cache_v.reshape(n, w, KV_W),
      sinks.reshape(N_Q_HEADS, 1), slopes)
    out = pl.pallas_call(
        functools.partial(_out_proj_kernel, head_w=ATT_W, norm_heads=False),
        out_shape=f((n, d)),
        name="swa_sample_out",
    )(y, o.reshape(n, ATT_W), gate, jnp.ones((1, ATT_W), F32), wout)
    return out, kwin, vwin


def kernel(x_prompt, x_sample, state_conv, state_ssm, cache_k_win, cache_v_win, norm_a, w_in_a, conv_w_a, a_log, dt_bias, o_norm_a, w_out_a, kv_norm, w_kv, k_norm, norm_b, w_in_b, q_norm, sinks, w_out_b):
    n_a = w_in_a.shape[0]
    n_b = w_in_b.shape[0]
    assert n_a == 1 and n_b == 1, "kernel is written for DEPTH == 2"
    bp, lp, d = x_prompt.shape
    n = x_sample.shape[0]
    hw = GDN_HEADS * GDN_D

    hp, hs = x_prompt, x_sample.reshape(n, d)
    conv_p, ssm_p, conv_s, ssm_s = [], [], [], []
    for layer in range(n_a):
        w_in = w_in_a[layer]
        wqkv = w_in[:, :QKV_W].astype(BF16)
        wg = w_in[:, QKV_W:QKV_W + hw].astype(BF16)
        wab = w_in[:, QKV_W + hw:].astype(BF16)
        wabt = wab.T
        wout = w_out_a[layer].astype(BF16)
        hp, cbuf, st = _gdn_prompt(hp, norm_a[layer], wqkv, wg, wab, wabt, conv_w_a[layer],
                                   a_log[layer], dt_bias[layer], o_norm_a[layer], wout)
        conv_p.append(cbuf)
        ssm_p.append(st)
        hs, cbuf, st = _gdn_sample(hs, state_conv[layer], state_ssm[layer], norm_a[layer], wqkv, wg,
                                   wab, conv_w_a[layer], a_log[layer], dt_bias[layer],
                                   o_norm_a[layer], wout)
        conv_s.append(cbuf)
        ssm_s.append(st)

    wkv = w_kv.astype(BF16)
    win = w_in_b[0].astype(BF16)
    woutb = w_out_b[0].astype(BF16)
    hp, k_win_p, v_win_p = _swa_prompt(hp, kv_norm, wkv, k_norm, norm_b[0], win, q_norm[0], sinks[0], woutb)
    hs, k_win_s, v_win_s = _swa_sample(hs, cache_k_win, cache_v_win, kv_norm, wkv, k_norm, norm_b[0],
                                       win, q_norm[0], sinks[0], woutb)
    kv_shape = (N_KV_HEADS, HEAD_DIM)
    return (hp, hs.reshape(n, 1, d), jnp.stack(conv_p), jnp.stack(ssm_p),
            k_win_p.reshape(bp, WINDOW, *kv_shape), v_win_p.reshape(bp, WINDOW, *kv_shape),
            jnp.stack(conv_s), jnp.stack(ssm_s),
            k_win_s.reshape(n, WINDOW, *kv_shape), v_win_s.reshape(n, WINDOW, *kv_shape))
```

```python
import functools

import jax
import jax.numpy as jnp
from jax import lax
from jax.experimental import pallas as pl
from jax.experimental.pallas import tpu as pltpu

F32 = jnp.float32
BF16 = jnp.bfloat16
EPS = 1e-6

D_MODEL = 1024
GDN_HEADS = 8
GDN_D = 128
QKV_W = 3 * GDN_HEADS * GDN_D
CONV_W = 4
CHUNK = 64
N_Q_HEADS = 16
N_KV_HEADS = 4
Q_GROUP = N_Q_HEADS // N_KV_HEADS
HEAD_DIM = 64
KV_W = N_KV_HEADS * HEAD_DIM
ATT_W = N_Q_HEADS * HEAD_DIM
WINDOW = 128

TM_A = 256
HEADS_PER_ITER = 2
CARRY = 8
VMEM_LIMIT = 52 * 1024 * 1024

_NT = (((1,), (1,)), ((), ()))
_TN = (((0,), (0,)), ((), ()))


def _dot(a, b):
    return jnp.dot(a, b, preferred_element_type=F32)


def _dot_nt(a, b):
    return lax.dot_general(a, b, _NT, preferred_element_type=F32)


def _dot_tn(a, b):
    return lax.dot_general(a, b, _TN, preferred_element_type=F32)


def _split(x):
    hi = x.astype(BF16)
    lo = (x - hi.astype(F32)).astype(BF16)
    return hi, lo


def _dot3_parts(a_parts, b_parts):
    ah, al = a_parts
    bh, bl = b_parts
    return (_dot(al, bh) + _dot(ah, bl)) + _dot(ah, bh)


def _dot3(a, b):
    return _dot3_parts(_split(a), _split(b))


def _dot_exact_lhs(a_bf, b):
    b0 = b.astype(BF16)
    r1 = b - b0.astype(F32)
    b1 = r1.astype(BF16)
    b2 = (r1 - b1.astype(F32)).astype(BF16)
    return (_dot(a_bf, b2) + _dot(a_bf, b1)) + _dot(a_bf, b0)


def _dot_exact_rhs(a, b_bf):
    a0 = a.astype(BF16)
    r1 = a - a0.astype(F32)
    a1 = r1.astype(BF16)
    a2 = (r1 - a1.astype(F32)).astype(BF16)
    return (_dot(a2, b_bf) + _dot(a1, b_bf)) + _dot(a0, b_bf)


def _silu(x):
    return x * jax.nn.sigmoid(x)


def _softplus(x):
    return jnp.maximum(x, 0.0) + jnp.log1p(jnp.exp(-jnp.abs(x)))


def _unit_lower_inverses(ms):
    n = ms[0].shape[0]
    row = lax.broadcasted_iota(jnp.int32, (n, n), 0)
    col = lax.broadcasted_iota(jnp.int32, (n, n), 1)
    eye = jnp.where(row == col, 1.0, 0.0).astype(F32)
    ts = [eye - m for m in ms]
    p_parts = [_split(m) for m in ms]
    steps = max(1, (n - 1).bit_length()) - 1
    for _ in range(steps):
        p_parts = [_split(_dot3_parts(pp, pp)) for pp in p_parts]
        ts = [t + _dot3_parts(_split(t), pp) for t, pp in zip(ts, p_parts)]
    return ts


def _full_spec(shape):
    nd = len(shape)
    return pl.BlockSpec(shape, lambda *_: (0,) * nd)


def _gdn_gates(ab, ab_t, alog_r, dtb_r, alog_c, dtb_c, tm, chunk):
    h = GDN_HEADS
    g_c = -jnp.exp(alog_r) * _softplus(ab[:, :h] + dtb_r)
    beta = jax.nn.sigmoid(ab[:, h:])
    g_r = -jnp.exp(alog_c) * _softplus(ab_t[:h, :] + dtb_c)
    row = lax.broadcasted_iota(jnp.int32, (tm, tm), 0)
    col = lax.broadcasted_iota(jnp.int32, (tm, tm), 1)
    same = (row // chunk) == (col // chunk)
    lower = jnp.where(same & (row >= col), 1.0, 0.0).astype(BF16)
    upper = jnp.where(same & (row <= col), 1.0, 0.0).astype(BF16)
    gc = _dot_exact_lhs(lower, g_c)
    gr = _dot_exact_rhs(g_r, upper)
    return beta, gc, gr


def _gdn_prompt_kernel(x_ref, norm_ref, wqkv_ref, wg_ref, wab_ref, wabt_ref, convw_ref,
                       alog_r_ref, dtb_r_ref, alog_c_ref, dtb_c_ref, onorm_ref, wout_ref,
                       y_ref, conv_ref, ssm_ref,
                       pre_scr, gate_scr, s_scr, k_scr, kb_scr, q_scr, qd_scr, kd_scr,
                       rhs_scr, dec_scr, gt_scr, o_scr, lhs_scr, su_scr):
    tm = x_ref.shape[1]
    n_chunks = tm // CHUNK
    l = pl.program_id(1)

    @pl.when(l == 0)
    def _():
        pre_scr[0:CARRY, :] = jnp.zeros((CARRY, QKV_W), F32)
        s_scr[...] = jnp.zeros(s_scr.shape, F32)

    x = x_ref[0]
    xn = x * lax.rsqrt(jnp.mean(x * x, axis=-1, keepdims=True) + EPS) * norm_ref[...]
    xb = xn.astype(BF16)
    pre_scr[CARRY:CARRY + tm, :] = _dot(xb, wqkv_ref[...])
    gate_scr[...] = _dot(xb, wg_ref[...])
    ab = _dot(xb, wab_ref[...])
    ab_t = _dot_nt(wabt_ref[...], xb)
    beta, gc, gr = _gdn_gates(ab, ab_t, alog_r_ref[...], dtb_r_ref[...],
                              alog_c_ref[...], dtb_c_ref[...], tm, CHUNK)
    e_g = jnp.exp(gc)
    g_last = jnp.concatenate(
        [jnp.broadcast_to(gc[c * CHUNK + CHUNK - 1:c * CHUNK + CHUNK, :], (CHUNK, GDN_HEADS))
         for c in range(n_chunks)], axis=0)
    e_kd = jnp.exp(g_last - gc)
    e_tot = jnp.exp(g_last)

    ci = lax.broadcasted_iota(jnp.int32, (CHUNK, CHUNK), 0)
    cj = lax.broadcasted_iota(jnp.int32, (CHUNK, CHUNK), 1)
    causal = ci >= cj
    strict = ci > cj

    def conv_act(j):
        cols = slice(j * GDN_D, (j + 1) * GDN_D)
        acc = pre_scr[CARRY - 3:CARRY - 3 + tm, cols] * convw_ref[0:1, cols]
        for i in range(1, CONV_W):
            acc = acc + pre_scr[CARRY - 3 + i:CARRY - 3 + i + tm, cols] * convw_ref[i:i + 1, cols]
        return _silu(acc)

    for h in range(GDN_HEADS):
        qh = conv_act(h)
        kh = conv_act(GDN_HEADS + h)
        vh = conv_act(2 * GDN_HEADS + h)
        qn = qh * lax.rsqrt(jnp.sum(qh * qh, axis=-1, keepdims=True) + EPS) * (GDN_D ** -0.5)
        kn = kh * lax.rsqrt(jnp.sum(kh * kh, axis=-1, keepdims=True) + EPS)
        b_col = beta[:, h:h + 1]
        eg_col = e_g[:, h:h + 1]
        kb = kn * b_col
        k_scr[h] = kn.astype(BF16)
        kb_scr[h] = kb.astype(BF16)
        q_scr[h] = qn.astype(BF16)
        qd_scr[h] = qn * eg_col
        kd_scr[h] = (kn * e_kd[:, h:h + 1]).astype(BF16)
        rhs_scr[h, :, 0:GDN_D] = vh * b_col
        rhs_scr[h, :, GDN_D:2 * GDN_D] = kb * eg_col
        for c in range(n_chunks):
            rows = slice(c * CHUNK, (c + 1) * CHUNK)
            diff = gc[rows, h:h + 1] - gr[h:h + 1, rows]
            dec_scr[h, rows, :] = jnp.exp(jnp.where(causal, diff, -jnp.inf))
            gt_scr[h, c:c + 1, :] = jnp.broadcast_to(e_tot[c * CHUNK:c * CHUNK + 1, h:h + 1], (1, GDN_D))

    conv_ref[0] = pre_scr[tm + CARRY - 3:tm + CARRY, :]
    pre_scr[0:CARRY, :] = pre_scr[tm:tm + CARRY, :]

    def chunk_terms(i, carry):
        items = [(i * HEADS_PER_ITER + j, c) for j in range(HEADS_PER_ITER) for c in range(n_chunks)]
        rows = [slice(c * CHUNK, (c + 1) * CHUNK) for _, c in items]
        ks = [k_scr[h, r, :] for (h, _), r in zip(items, rows)]
        decs = [dec_scr[h, r, :] for (h, _), r in zip(items, rows)]
        ms = [jnp.where(strict, _dot_nt(kb_scr[h, r, :], k) * d, 0.0)
              for (h, _), r, k, d in zip(items, rows, ks, decs)]
        qks = [jnp.where(causal, _dot_nt(q_scr[h, r, :], k) * d, 0.0).astype(BF16)
               for (h, _), r, k, d in zip(items, rows, ks, decs)]
        ts = _unit_lower_inverses(ms)
        sols = [_dot3(t, rhs_scr[h, r, :]).astype(BF16)
                for (h, _), r, t in zip(items, rows, ts)]
        kd_uws = [_dot_tn(kd_scr[h, r, :], sol)
                  for (h, _), r, sol in zip(items, rows, sols)]
        qk_uws = [_dot(qk, sol) for qk, sol in zip(qks, sols)]
        for (h, c), r, kd_uw, qk_uw in zip(items, rows, kd_uws, qk_uws):
            lhs_scr[h, c, 0:GDN_D, :] = kd_uw[:, GDN_D:].astype(BF16)
            lhs_scr[h, c, GDN_D:GDN_D + CHUNK, :] = (qd_scr[h, r, :] - qk_uw[:, GDN_D:]).astype(BF16)
            su_scr[h, c] = kd_uw[:, :GDN_D]
            o_scr[h, r, :] = qk_uw[:, :GDN_D]
        return carry

    lax.fori_loop(0, GDN_HEADS // HEADS_PER_ITER, chunk_terms, 0)

    for c in range(n_chunks):
        rows = slice(c * CHUNK, (c + 1) * CHUNK)
        for h in range(GDN_HEADS):
            s = s_scr[h]
            prod = _dot(lhs_scr[h, c], s.astype(BF16))
            o_scr[h, rows, :] = o_scr[h, rows, :] + prod[GDN_D:, :]
            s_scr[h] = s * gt_scr[h, c:c + 1, :] + (su_scr[h, c] - prod[:GDN_D, :])

    @pl.when(l == pl.num_programs(1) - 1)
    def _():
        ssm_ref[0] = s_scr[...]

    pieces = []
    for h in range(GDN_HEADS):
        o = o_scr[h]
        on = o * lax.rsqrt(jnp.mean(o * o, axis=-1, keepdims=True) + EPS) * onorm_ref[...]
        pieces.append((on * _silu(gate_scr[:, h * GDN_D:(h + 1) * GDN_D])).astype(BF16))
    y_ref[0] = x + _dot(jnp.concatenate(pieces, axis=1), wout_ref[...])


def _gdn_prompt(x, norm, wqkv, wg, wab, wabt, convw, alog, dtb, onorm, wout):
    b, l, d = x.shape
    tm = TM_A
    h = GDN_HEADS
    grid = (b, l // tm)
    in_specs = [
        pl.BlockSpec((1, tm, d), lambda i, j: (i, j, 0)),
        _full_spec((1, d)), _full_spec(wqkv.shape), _full_spec(wg.shape), _full_spec(wab.shape),
        _full_spec(wabt.shape), _full_spec(convw.shape),
        _full_spec((1, h)), _full_spec((1, h)), _full_spec((h, 1)), _full_spec((h, 1)),
        _full_spec((1, GDN_D)), _full_spec(wout.shape),
    ]
    out_specs = [
        pl.BlockSpec((1, tm, d), lambda i, j: (i, j, 0)),
        pl.BlockSpec((1, CONV_W - 1, QKV_W), lambda i, j: (i, 0, 0)),
        pl.BlockSpec((1, h, GDN_D, GDN_D), lambda i, j: (i, 0, 0, 0)),
    ]
    out_shape = [
        jax.ShapeDtypeStruct((b, l, d), F32),
        jax.ShapeDtypeStruct((b, CONV_W - 1, QKV_W), F32),
        jax.ShapeDtypeStruct((b, h, GDN_D, GDN_D), F32),
    ]
    scratch = [
        pltpu.VMEM((tm + CARRY, QKV_W), F32),
        pltpu.VMEM((tm, h * GDN_D), F32),
        pltpu.VMEM((h, GDN_D, GDN_D), F32),
        pltpu.VMEM((h, tm, GDN_D), BF16),
        pltpu.VMEM((h, tm, GDN_D), BF16),
        pltpu.VMEM((h, tm, GDN_D), BF16),
        pltpu.VMEM((h, tm, GDN_D), F32),
        pltpu.VMEM((h, tm, GDN_D), BF16),
        pltpu.VMEM((h, tm, 2 * GDN_D), F32),
        pltpu.VMEM((h, tm, CHUNK), F32),
        pltpu.VMEM((h, 8, GDN_D), F32),
        pltpu.VMEM((h, tm, GDN_D), F32),
        pltpu.VMEM((h, tm // CHUNK, GDN_D + CHUNK, GDN_D), BF16),
        pltpu.VMEM((h, tm // CHUNK, GDN_D, GDN_D), F32),
    ]
    return pl.pallas_call(
        _gdn_prompt_kernel,
        grid=grid, in_specs=in_specs, out_specs=out_specs, out_shape=out_shape,
        scratch_shapes=scratch,
        compiler_params=pltpu.CompilerParams(
            dimension_semantics=("arbitrary", "arbitrary"), vmem_limit_bytes=VMEM_LIMIT),
        name="gdn_prompt",
    )(x, norm.reshape(1, d), wqkv, wg, wab, wabt, convw,
      alog.reshape(1, h), dtb.reshape(1, h), alog.reshape(h, 1), dtb.reshape(h, 1),
      onorm.reshape(1, GDN_D), wout)


def _gdn_sample_front_kernel(x_ref, norm_ref, wqkv_ref, wg_ref, wab_ref, convw_ref,
                             c0_ref, c1_ref, c2_ref, alog_r_ref, dtb_r_ref,
                             pre_ref, q_ref, k_ref, u_ref, w_ref, qd_ref, gate_ref, eg_ref):
    x = x_ref[...]
    xn = x * lax.rsqrt(jnp.mean(x * x, axis=-1, keepdims=True) + EPS) * norm_ref[...]
    xb = xn.astype(BF16)
    pre = _dot(xb, wqkv_ref[...])
    pre_ref[...] = pre
    gate_ref[...] = _dot(xb, wg_ref[...])
    ab = _dot(xb, wab_ref[...])
    g = -jnp.exp(alog_r_ref[...]) * _softplus(ab[:, :GDN_HEADS] + dtb_r_ref[...])
    beta = jax.nn.sigmoid(ab[:, GDN_HEADS:])
    e_g = jnp.exp(g)
    eg_ref[...] = e_g
    act = _silu(((c0_ref[...] * convw_ref[0:1, :] + c1_ref[...] * convw_ref[1:2, :])
                 + c2_ref[...] * convw_ref[2:3, :]) + pre * convw_ref[3:4, :])
    w_all = GDN_HEADS * GDN_D
    for h in range(GDN_HEADS):
        cols = slice(h * GDN_D, (h + 1) * GDN_D)
        qh = act[:, h * GDN_D:(h + 1) * GDN_D]
        kh = act[:, w_all + h * GDN_D:w_all + (h + 1) * GDN_D]
        vh = act[:, 2 * w_all + h * GDN_D:2 * w_all + (h + 1) * GDN_D]
        qn = qh * lax.rsqrt(jnp.sum(qh * qh, axis=-1, keepdims=True) + EPS) * (GDN_D ** -0.5)
        kn = kh * lax.rsqrt(jnp.sum(kh * kh, axis=-1, keepdims=True) + EPS)
        b_col = beta[:, h:h + 1]
        eg_col = e_g[:, h:h + 1]
        q_ref[:, cols] = qn
        k_ref[:, cols] = kn
        u_ref[:, cols] = vh * b_col
        w_ref[:, cols] = kn * b_col * eg_col
        qd_ref[:, cols] = qn * eg_col


def _gdn_sample_state_kernel(q_ref, k_ref, u_ref, w_ref, qd_ref, eg_ref, s_ref, o_ref, s_out_ref):
    bb = s_ref.shape[0]
    row = lax.broadcasted_iota(jnp.int32, (8, GDN_D), 0)
    for b in range(bb):
        for h in range(GDN_HEADS):
            cols = slice(h * GDN_D, (h + 1) * GDN_D)
            s = s_ref[b, h]
            sb = s.astype(BF16)
            q = q_ref[b:b + 1, cols].astype(BF16).astype(F32)
            k = k_ref[b:b + 1, cols].astype(BF16).astype(F32)
            lhs = jnp.where(row == 0, w_ref[b:b + 1, cols],
                            jnp.where(row == 1, qd_ref[b:b + 1, cols], 0.0))
            prod = _dot(lhs.astype(BF16), sb)
            v_new = u_ref[b:b + 1, cols] - prod[0:1, :]
            vb = v_new.astype(BF16).astype(F32)
            qk = jnp.sum(q * k, axis=-1, keepdims=True)
            o_ref[b:b + 1, cols] = prod[1:2, :] + qk.astype(BF16).astype(F32) * vb
            k8 = jnp.where(row == 0, k, 0.0).astype(BF16)
            v8 = jnp.where(row == 0, vb, 0.0).astype(BF16)
            s_out_ref[b, h] = s * eg_ref[b:b + 1, h:h + 1] + _dot_tn(k8, v8)


def _out_proj_kernel(x_ref, o_ref, gate_ref, onorm_ref, wout_ref, y_ref, *, head_w, norm_heads):
    pieces = []
    for h in range(x_ref.shape[1] // head_w):
        cols = slice(h * head_w, (h + 1) * head_w)
        o = o_ref[:, cols]
        if norm_heads:
            o = o * lax.rsqrt(jnp.mean(o * o, axis=-1, keepdims=True) + EPS) * onorm_ref[...]
        pieces.append((o * _silu(gate_ref[:, cols])).astype(BF16))
    y_ref[...] = x_ref[...] + _dot(jnp.concatenate(pieces, axis=1), wout_ref[...])


def _gdn_sample(x, conv_state, ssm_state, norm, wqkv, wg, wab, convw, alog, dtb, onorm, wout):
    n, d = x.shape
    h = GDN_HEADS
    hw = h * GDN_D
    conv_t = jnp.transpose(conv_state, (1, 0, 2))
    f = lambda shape: jax.ShapeDtypeStruct(shape, F32)
    pre, q, k, u, w, qd, gate, eg = pl.pallas_call(
        _gdn_sample_front_kernel,
        out_shape=[f((n, QKV_W)), f((n, hw)), f((n, hw)), f((n, hw)), f((n, hw)), f((n, hw)),
                   f((n, hw)), f((n, h))],
        compiler_params=pltpu.CompilerParams(vmem_limit_bytes=VMEM_LIMIT),
        name="gdn_sample_front",
    )(x, norm.reshape(1, d), wqkv, wg, wab, convw, conv_t[0], conv_t[1], conv_t[2],
      alog.reshape(1, h), dtb.reshape(1, h))
    bb = 8
    row_spec = lambda width: pl.BlockSpec((bb, width), lambda i: (i, 0))
    st_spec = pl.BlockSpec((bb, h, GDN_D, GDN_D), lambda i: (i, 0, 0, 0))
    o, s_new = pl.pallas_call(
        _gdn_sample_state_kernel,
        grid=(n // bb,),
        in_specs=[row_spec(hw)] * 5 + [row_spec(h), st_spec],
        out_specs=[row_spec(hw), st_spec],
        out_shape=[f((n, hw)), f(ssm_state.shape)],
        compiler_params=pltpu.CompilerParams(dimension_semantics=("arbitrary",)),
        name="gdn_sample_state",
    )(q, k, u, w, qd, eg, ssm_state)
    y = pl.pallas_call(
        functools.partial(_out_proj_kernel, head_w=GDN_D, norm_heads=True),
        out_shape=f((n, d)),
        name="gdn_sample_out",
    )(x, o, gate, onorm.reshape(1, GDN_D), wout)
    conv_new = jnp.stack([conv_t[1], conv_t[2], pre], axis=1)
    return y, conv_new, s_new


def _head_rms(x, gain_row, n_heads):
    pieces = []
    for h in range(n_heads):
        xh = x[:, h * HEAD_DIM:(h + 1) * HEAD_DIM]
        pieces.append(xh * lax.rsqrt(jnp.mean(xh * xh, axis=-1, keepdims=True) + EPS) * gain_row)
    return pieces


def _swa_prompt_kernel(y_ref, kvnorm_ref, wkv_ref, knorm_ref, normb_ref, win_ref, qnorm_ref,
                       sinks_ref, wout_ref, out_ref, kwin_ref, vwin_ref, k_scr, v_scr):
    tq = y_ref.shape[1]
    blk = pl.program_id(1)

    @pl.when(blk == 0)
    def _():
        k_scr[0:tq, :] = jnp.zeros((tq, KV_W), BF16)
        v_scr[0:tq, :] = jnp.zeros((tq, KV_W), BF16)

    @pl.when(blk > 0)
    def _():
        k_scr[0:tq, :] = k_scr[tq:2 * tq, :]
        v_scr[0:tq, :] = v_scr[tq:2 * tq, :]

    y = y_ref[0]
    yn = y * lax.rsqrt(jnp.mean(y * y, axis=-1, keepdims=True) + EPS)
    kv = _dot((yn * kvnorm_ref[...]).astype(BF16), wkv_ref[...])
    k_new = jnp.concatenate(_head_rms(kv[:, :KV_W], knorm_ref[...], N_KV_HEADS), axis=1)
    v_new = kv[:, KV_W:]
    kwin_ref[0] = k_new
    vwin_ref[0] = v_new
    k_scr[tq:2 * tq, :] = k_new.astype(BF16)
    v_scr[tq:2 * tq, :] = v_new.astype(BF16)

    qg = _dot((yn * normb_ref[...]).astype(BF16), win_ref[...])
    q_heads = _head_rms(qg[:, :ATT_W], qnorm_ref[...] * (HEAD_DIM ** -0.5), N_Q_HEADS)

    qi = lax.broadcasted_iota(jnp.int32, (Q_GROUP * tq, 2 * tq), 0) % tq
    kj = lax.broadcasted_iota(jnp.int32, (Q_GROUP * tq, 2 * tq), 1)
    dist = qi - kj + tq
    valid = (dist >= 0) & (dist <= WINDOW) & ((blk > 0) | (kj >= tq))
    dist_f = dist.astype(F32)
    grp = lax.broadcasted_iota(jnp.int32, (Q_GROUP * tq, 1), 0) // tq

    o_heads = []
    for hk in range(N_KV_HEADS):
        kh = k_scr[:, hk * HEAD_DIM:(hk + 1) * HEAD_DIM]
        vh = v_scr[:, hk * HEAD_DIM:(hk + 1) * HEAD_DIM]
        q4 = jnp.concatenate([q_heads[hk * Q_GROUP + g] for g in range(Q_GROUP)], axis=0).astype(BF16)
        slope = jnp.zeros((Q_GROUP * tq, 1), F32)
        sink = jnp.zeros((Q_GROUP * tq, 1), F32)
        for g in range(Q_GROUP):
            hq = hk * Q_GROUP + g
            slope = jnp.where(grp == g, 2.0 ** (-8.0 * (hq + 1) / N_Q_HEADS), slope)
            sink = jnp.where(grp == g, sinks_ref[0:1, hq:hq + 1], sink)
        s = _dot_nt(q4, kh)
        s = jnp.where(valid, s - slope * dist_f, -jnp.inf)
        mx = jnp.maximum(jnp.max(s, axis=-1, keepdims=True), sink)
        p = jnp.exp(s - mx)
        p = p / (jnp.sum(p, axis=-1, keepdims=True) + jnp.exp(sink - mx))
        o4 = _dot(p.astype(BF16), vh)
        for g in range(Q_GROUP):
            o_heads.append(o4[g * tq:(g + 1) * tq, :])
    o = jnp.concatenate(o_heads, axis=1)
    o = (o * _silu(qg[:, ATT_W:])).astype(BF16)
    out_ref[0] = y + _dot(o, wout_ref[...])


def _swa_prompt(y, kvnorm, wkv, knorm, normb, win, qnorm, sinks, wout):
    b, l, d = y.shape
    tq = WINDOW
    nb = l // tq
    f = lambda shape: jax.ShapeDtypeStruct(shape, F32)
    in_specs = [
        pl.BlockSpec((1, tq, d), lambda i, j: (i, j, 0)),
        _full_spec((1, d)), _full_spec(wkv.shape), _full_spec((1, HEAD_DIM)), _full_spec((1, d)),
        _full_spec(win.shape), _full_spec((1, HEAD_DIM)), _full_spec((1, N_Q_HEADS)),
        _full_spec(wout.shape),
    ]
    out_specs = [
        pl.BlockSpec((1, tq, d), lambda i, j: (i, j, 0)),
        pl.BlockSpec((1, tq, KV_W), lambda i, j: (i, 0, 0)),
        pl.BlockSpec((1, tq, KV_W), lambda i, j: (i, 0, 0)),
    ]
    return pl.pallas_call(
        _swa_prompt_kernel,
        grid=(b, nb), in_specs=in_specs, out_specs=out_specs,
        out_shape=[f((b, l, d)), f((b, tq, KV_W)), f((b, tq, KV_W))],
        scratch_shapes=[pltpu.VMEM((2 * tq, KV_W), BF16), pltpu.VMEM((2 * tq, KV_W), BF16)],
        compiler_params=pltpu.CompilerParams(
            dimension_semantics=("arbitrary", "arbitrary"), vmem_limit_bytes=VMEM_LIMIT),
        name="swa_prompt",
    )(y, kvnorm.reshape(1, d), wkv, knorm.reshape(1, HEAD_DIM), normb.reshape(1, d), win,
      qnorm.reshape(1, HEAD_DIM), sinks.reshape(1, N_Q_HEADS), wout)


def _swa_sample_front_kernel(y_ref, kvnorm_ref, wkv_ref, knorm_ref, normb_ref, win_ref, qnorm_ref,
                             k_ref, v_ref, q_ref, gate_ref):
    y = y_ref[...]
    yn = y * lax.rsqrt(jnp.mean(y * y, axis=-1, keepdims=True) + EPS)
    kv = _dot((yn * kvnorm_ref[...]).astype(BF16), wkv_ref[...])
    k_ref[...] = jnp.concatenate(_head_rms(kv[:, :KV_W], knorm_ref[...], N_KV_HEADS), axis=1)
    v_ref[...] = kv[:, KV_W:]
    qg = _dot((yn * normb_ref[...]).astype(BF16), win_ref[...])
    q_ref[...] = jnp.concatenate(
        _head_rms(qg[:, :ATT_W], qnorm_ref[...] * (HEAD_DIM ** -0.5), N_Q_HEADS), axis=1)
    gate_ref[...] = qg[:, ATT_W:]


def _swa_sample_attn_kernel(q_ref, kn_ref, vn_ref, kc_ref, vc_ref, sinks_ref, slopes_ref,
                            o_ref, kwin_ref, vwin_ref):
    bb = kc_ref.shape[0]
    w = kc_ref.shape[1]
    hrow = lax.broadcasted_iota(jnp.int32, (N_Q_HEADS, KV_W), 0) // Q_GROUP
    lblk = lax.broadcasted_iota(jnp.int32, (N_Q_HEADS, KV_W), 1) // HEAD_DIM
    own = hrow == lblk
    dist_c = (w - lax.broadcasted_iota(jnp.int32, (1, w), 1)).astype(F32)
    krow = lax.broadcasted_iota(jnp.int32, (w, KV_W), 0)
    slopes = slopes_ref[...]
    sink = sinks_ref[...]
    for b in range(bb):
        q = q_ref[b]
        qm = jnp.where(own, jnp.concatenate([q] * N_KV_HEADS, axis=1), 0.0).astype(BF16)
        kc = kc_ref[b]
        vc = vc_ref[b]
        kn = kn_ref[b]
        vn = vn_ref[b]
        s_c = _dot_nt(qm, kc.astype(BF16)) - slopes * dist_c
        s_n = jnp.sum(qm.astype(F32) * kn.astype(BF16).astype(F32), axis=-1, keepdims=True)
        mx = jnp.maximum(jnp.maximum(jnp.max(s_c, axis=-1, keepdims=True), s_n), sink)
        p_c = jnp.exp(s_c - mx)
        p_n = jnp.exp(s_n - mx)
        den = jnp.sum(p_c, axis=-1, keepdims=True) + p_n + jnp.exp(sink - mx)
        p_c = p_c / den
        p_n = p_n / den
        r = _dot(p_c.astype(BF16), vc.astype(BF16))
        r = r + p_n.astype(BF16).astype(F32) * vn.astype(BF16).astype(F32)
        r = jnp.where(own, r, 0.0)
        acc = r[:, 0:HEAD_DIM]
        for blk in range(1, N_KV_HEADS):
            acc = acc + r[:, blk * HEAD_DIM:(blk + 1) * HEAD_DIM]
        o_ref[b] = acc
        kwin_ref[b] = jnp.where(krow == w - 1, kn, pltpu.roll(kc, w - 1, 0))
        vwin_ref[b] = jnp.where(krow == w - 1, vn, pltpu.roll(vc, w - 1, 0))


def _swa_sample(y, cache_k, cache_v, kvnorm, wkv, knorm, normb, win, qnorm, sinks, wout):
    n, d = y.shape
    w = cache_k.shape[1]
    f = lambda shape: jax.ShapeDtypeStruct(shape, F32)
    kn, vn, q, gate = pl.pallas_call(
        _swa_sample_front_kernel,
        out_shape=[f((n, KV_W)), f((n, KV_W)), f((n, ATT_W)), f((n, ATT_W))],
        compiler_params=pltpu.CompilerParams(vmem_limit_bytes=VMEM_LIMIT),
        name="swa_sample_front",
    )(y, kvnorm.reshape(1, d), wkv, knorm.reshape(1, HEAD_DIM), normb.reshape(1, d), win,
      qnorm.reshape(1, HEAD_DIM))
    bb = 8
    slopes = (2.0 ** (-8.0 * jnp.arange(1, N_Q_HEADS + 1, dtype=F32) / N_Q_HEADS)).reshape(N_Q_HEADS, 1)
    spec3 = lambda a, c: pl.BlockSpec((bb, a, c), lambda i: (i, 0, 0))
    o, kwin, vwin = pl.pallas_call(
        _swa_sample_attn_kernel,
        grid=(n // bb,),
        in_specs=[spec3(N_Q_HEADS, HEAD_DIM), spec3(1, KV_W), spec3(1, KV_W), spec3(w, KV_W),
                  spec3(w, KV_W), _full_spec((N_Q_HEADS, 1)), _full_spec((N_Q_HEADS, 1))],
        out_specs=[spec3(N_Q_HEADS, HEAD_DIM), spec3(w, KV_W), spec3(w, KV_W)],
        out_shape=[f((n, N_Q_HEADS, HEAD_DIM)), f((n, w, KV_W)), f((n, w, KV_W))],
        compiler_params=pltpu.CompilerParams(dimension_semantics=("arbitrary",)),
        name="swa_sample_attn",
    )(q.reshape(n, N_Q_HEADS, HEAD_DIM), kn.reshape(n, 1, KV_W), vn.reshape(n, 1, KV_W),
      cache_k.reshape(n, w, KV_W), cache_v.reshape(n, w, KV_W),
      sinks.reshape(N_Q_HEADS, 1), slopes)
    out = pl.pallas_call(
        functools.partial(_out_proj_kernel, head_w=ATT_W, norm_heads=False),
        out_shape=f((n, d)),
        name="swa_sample_out",
    )(y, o.reshape(n, ATT_W), gate, jnp.ones((1, ATT_W), F32), wout)
    return out, kwin, vwin


def kernel(x_prompt, x_sample, state_conv, state_ssm, cache_k_win, cache_v_win, norm_a, w_in_a, conv_w_a, a_log, dt_bias, o_norm_a, w_out_a, kv_norm, w_kv, k_norm, norm_b, w_in_b, q_norm, sinks, w_out_b):
    n_a = w_in_a.shape[0]
    n_b = w_in_b.shape[0]
    assert n_a == 1 and n_b == 1, "kernel is written for DEPTH == 2"
    bp, lp, d = x_prompt.shape
    n = x_sample.shape[0]
    hw = GDN_HEADS * GDN_D

    hp, hs = x_prompt, x_sample.reshape(n, d)
    conv_p, ssm_p, conv_s, ssm_s = [], [], [], []
    for layer in range(n_a):
        w_in = w_in_a[layer]
        wqkv = w_in[:, :QKV_W].astype(BF16)
        wg = w_in[:, QKV_W:QKV_W + hw].astype(BF16)
        wab = w_in[:, QKV_W + hw:].astype(BF16)
        wabt = wab.T
        wout = w_out_a[layer].astype(BF16)
        hp, cbuf, st = _gdn_prompt(hp, norm_a[layer], wqkv, wg, wab, wabt, conv_w_a[layer],
                                   a_log[layer], dt_bias[layer], o_norm_a[layer], wout)
        conv_p.append(cbuf)
        ssm_p.append(st)
        hs, cbuf, st = _gdn_sample(hs, state_conv[layer], state_ssm[layer], norm_a[layer], wqkv, wg,
                                   wab, conv_w_a[layer], a_log[layer], dt_bias[layer],
                                   o_norm_a[layer], wout)
        conv_s.append(cbuf)
        ssm_s.append(st)

    wkv = w_kv.astype(BF16)
    win = w_in_b[0].astype(BF16)
    woutb = w_out_b[0].astype(BF16)
    hp, k_win_p, v_win_p = _swa_prompt(hp, kv_norm, wkv, k_norm, norm_b[0], win, q_norm[0], sinks[0], woutb)
    hs, k_win_s, v_win_s = _swa_sample(hs, cache_k_win, cache_v_win, kv_norm, wkv, k_norm, norm_b[0],
                                       win, q_norm[0], sinks[0], woutb)
    kv_shape = (N_KV_HEADS, HEAD_DIM)
    return (hp, hs.reshape(n, 1, d), jnp.stack(conv_p), jnp.stack(ssm_p),
            k_win_p.reshape(bp, WINDOW, *kv_shape), v_win_p.reshape(bp, WINDOW, *kv_shape),
            jnp.stack(conv_s), jnp.stack(ssm_s),
            k_win_s.reshape(n, WINDOW, *kv_shape), v_win_s.reshape(n, WINDOW, *kv_shape))
```

```python
import functools

import jax
import jax.numpy as jnp
from jax import lax
from jax.experimental import pallas as pl
from jax.experimental.pallas import tpu as pltpu

F32 = jnp.float32
BF16 = jnp.bfloat16
EPS = 1e-6

D_MODEL = 1024
GDN_HEADS = 8
GDN_D = 128
QKV_W = 3 * GDN_HEADS * GDN_D
CONV_W = 4
CHUNK = 64
N_Q_HEADS = 16
N_KV_HEADS = 4
Q_GROUP = N_Q_HEADS // N_KV_HEADS
HEAD_DIM = 64
KV_W = N_KV_HEADS * HEAD_DIM
ATT_W = N_Q_HEADS * HEAD_DIM
WINDOW = 128

TM_A = 256
TQ_B = 256
HEADS_PER_ITER = 2
CARRY = 8
VMEM_LIMIT = 52 * 1024 * 1024

_NT = (((1,), (1,)), ((), ()))
_TN = (((0,), (0,)), ((), ()))


def _dot(a, b):
    return jnp.dot(a, b, preferred_element_type=F32)


def _dot_nt(a, b):
    return lax.dot_general(a, b, _NT, preferred_element_type=F32)


def _dot_tn(a, b):
    return lax.dot_general(a, b, _TN, preferred_element_type=F32)


def _split(x):
    hi = x.astype(BF16)
    lo = (x - hi.astype(F32)).astype(BF16)
    return hi, lo


def _dot3_parts(a_parts, b_parts):
    ah, al = a_parts
    bh, bl = b_parts
    return (_dot(al, bh) + _dot(ah, bl)) + _dot(ah, bh)


def _dot3(a, b):
    return _dot3_parts(_split(a), _split(b))


def _dot_exact_lhs(a_bf, b):
    b0 = b.astype(BF16)
    r1 = b - b0.astype(F32)
    b1 = r1.astype(BF16)
    b2 = (r1 - b1.astype(F32)).astype(BF16)
    return (_dot(a_bf, b2) + _dot(a_bf, b1)) + _dot(a_bf, b0)


def _dot_exact_rhs(a, b_bf):
    a0 = a.astype(BF16)
    r1 = a - a0.astype(F32)
    a1 = r1.astype(BF16)
    a2 = (r1 - a1.astype(F32)).astype(BF16)
    return (_dot(a2, b_bf) + _dot(a1, b_bf)) + _dot(a0, b_bf)


def _silu(x):
    return x * jax.nn.sigmoid(x)


def _softplus(x):
    return jnp.maximum(x, 0.0) + jnp.log1p(jnp.exp(-jnp.abs(x)))


def _unit_lower_inverses(ms):
    n = ms[0].shape[0]
    row = lax.broadcasted_iota(jnp.int32, (n, n), 0)
    col = lax.broadcasted_iota(jnp.int32, (n, n), 1)
    eye = jnp.where(row == col, 1.0, 0.0).astype(F32)
    ts = [eye - m for m in ms]
    p_parts = [_split(m) for m in ms]
    steps = max(1, (n - 1).bit_length()) - 1
    for _ in range(steps):
        p_parts = [_split(_dot3_parts(pp, pp)) for pp in p_parts]
        ts = [t + _dot3_parts(_split(t), pp) for t, pp in zip(ts, p_parts)]
    return ts


def _full_spec(shape):
    nd = len(shape)
    return pl.BlockSpec(shape, lambda *_: (0,) * nd)


def _gdn_gates(ab, ab_t, alog_r, dtb_r, alog_c, dtb_c, tm, chunk):
    h = GDN_HEADS
    g_c = -jnp.exp(alog_r) * _softplus(ab[:, :h] + dtb_r)
    beta = jax.nn.sigmoid(ab[:, h:])
    g_r = -jnp.exp(alog_c) * _softplus(ab_t[:h, :] + dtb_c)
    row = lax.broadcasted_iota(jnp.int32, (tm, tm), 0)
    col = lax.broadcasted_iota(jnp.int32, (tm, tm), 1)
    same = (row // chunk) == (col // chunk)
    lower = jnp.where(same & (row >= col), 1.0, 0.0).astype(BF16)
    upper = jnp.where(same & (row <= col), 1.0, 0.0).astype(BF16)
    gc = _dot_exact_lhs(lower, g_c)
    gr = _dot_exact_rhs(g_r, upper)
    return beta, gc, gr


def _gdn_prompt_kernel(x_ref, norm_ref, wqkv_ref, wg_ref, wab_ref, wabt_ref, convw_ref,
                       alog_r_ref, dtb_r_ref, alog_c_ref, dtb_c_ref, onorm_ref, wout_ref,
                       y_ref, conv_ref, ssm_ref,
                       pre_scr, gate_scr, s_scr, k_scr, kb_scr, q_scr, qd_scr, kd_scr,
                       rhs_scr, dec_scr, gt_scr, o_scr, lhs_scr, su_scr):
    tm = x_ref.shape[1]
    n_chunks = tm // CHUNK
    l = pl.program_id(1)

    @pl.when(l == 0)
    def _():
        pre_scr[0:CARRY, :] = jnp.zeros((CARRY, QKV_W), F32)
        s_scr[...] = jnp.zeros(s_scr.shape, F32)

    x = x_ref[0]
    xn = x * lax.rsqrt(jnp.mean(x * x, axis=-1, keepdims=True) + EPS) * norm_ref[...]
    xb = xn.astype(BF16)
    pre_scr[CARRY:CARRY + tm, :] = _dot(xb, wqkv_ref[...])
    gate_scr[...] = _dot(xb, wg_ref[...])
    ab = _dot(xb, wab_ref[...])
    ab_t = _dot_nt(wabt_ref[...], xb)
    beta, gc, gr = _gdn_gates(ab, ab_t, alog_r_ref[...], dtb_r_ref[...],
                              alog_c_ref[...], dtb_c_ref[...], tm, CHUNK)
    e_g = jnp.exp(gc)
    g_last = jnp.concatenate(
        [jnp.broadcast_to(gc[c * CHUNK + CHUNK - 1:c * CHUNK + CHUNK, :], (CHUNK, GDN_HEADS))
         for c in range(n_chunks)], axis=0)
    e_kd = jnp.exp(g_last - gc)
    e_tot = jnp.exp(g_last)

    ci = lax.broadcasted_iota(jnp.int32, (CHUNK, CHUNK), 0)
    cj = lax.broadcasted_iota(jnp.int32, (CHUNK, CHUNK), 1)
    causal = ci >= cj
    strict = ci > cj

    def conv_act(j):
        cols = slice(j * GDN_D, (j + 1) * GDN_D)
        acc = pre_scr[CARRY - 3:CARRY - 3 + tm, cols] * convw_ref[0:1, cols]
        for i in range(1, CONV_W):
            acc = acc + pre_scr[CARRY - 3 + i:CARRY - 3 + i + tm, cols] * convw_ref[i:i + 1, cols]
        return _silu(acc)

    for h in range(GDN_HEADS):
        qh = conv_act(h)
        kh = conv_act(GDN_HEADS + h)
        vh = conv_act(2 * GDN_HEADS + h)
        qn = qh * lax.rsqrt(jnp.sum(qh * qh, axis=-1, keepdims=True) + EPS) * (GDN_D ** -0.5)
        kn = kh * lax.rsqrt(jnp.sum(kh * kh, axis=-1, keepdims=True) + EPS)
        b_col = beta[:, h:h + 1]
        eg_col = e_g[:, h:h + 1]
        kb = kn * b_col
        k_scr[h] = kn.astype(BF16)
        kb_scr[h] = kb.astype(BF16)
        q_scr[h] = qn.astype(BF16)
        qd_scr[h] = qn * eg_col
        kd_scr[h] = (kn * e_kd[:, h:h + 1]).astype(BF16)
        rhs_scr[h, :, 0:GDN_D] = vh * b_col
        rhs_scr[h, :, GDN_D:2 * GDN_D] = kb * eg_col
        for c in range(n_chunks):
            rows = slice(c * CHUNK, (c + 1) * CHUNK)
            diff = gc[rows, h:h + 1] - gr[h:h + 1, rows]
            dec_scr[h, rows, :] = jnp.exp(jnp.where(causal, diff, -jnp.inf))
            gt_scr[h, c:c + 1, :] = jnp.broadcast_to(e_tot[c * CHUNK:c * CHUNK + 1, h:h + 1], (1, GDN_D))

    conv_ref[0] = pre_scr[tm + CARRY - 3:tm + CARRY, :]
    pre_scr[0:CARRY, :] = pre_scr[tm:tm + CARRY, :]

    def chunk_terms(i, carry):
        items = [(i * HEADS_PER_ITER + j, c) for j in range(HEADS_PER_ITER) for c in range(n_chunks)]
        rows = [slice(c * CHUNK, (c + 1) * CHUNK) for _, c in items]
        ks = [k_scr[h, r, :] for (h, _), r in zip(items, rows)]
        decs = [dec_scr[h, r, :] for (h, _), r in zip(items, rows)]
        ms = [jnp.where(strict, _dot_nt(kb_scr[h, r, :], k) * d, 0.0)
              for (h, _), r, k, d in zip(items, rows, ks, decs)]
        qks = [jnp.where(causal, _dot_nt(q_scr[h, r, :], k) * d, 0.0).astype(BF16)
               for (h, _), r, k, d in zip(items, rows, ks, decs)]
        ts = _unit_lower_inverses(ms)
        sols = [_dot3(t, rhs_scr[h, r, :]).astype(BF16)
                for (h, _), r, t in zip(items, rows, ts)]
        kd_uws = [_dot_tn(kd_scr[h, r, :], sol)
                  for (h, _), r, sol in zip(items, rows, sols)]
        qk_uws = [_dot(qk, sol) for qk, sol in zip(qks, sols)]
        for (h, c), r, kd_uw, qk_uw in zip(items, rows, kd_uws, qk_uws):
            lhs_scr[h, c, 0:GDN_D, :] = kd_uw[:, GDN_D:].astype(BF16)
            lhs_scr[h, c, GDN_D:GDN_D + CHUNK, :] = (qd_scr[h, r, :] - qk_uw[:, GDN_D:]).astype(BF16)
            su_scr[h, c] = kd_uw[:, :GDN_D]
            o_scr[h, r, :] = qk_uw[:, :GDN_D]
        return carry

    lax.fori_loop(0, GDN_HEADS // HEADS_PER_ITER, chunk_terms, 0)

    for c in range(n_chunks):
        rows = slice(c * CHUNK, (c + 1) * CHUNK)
        for h in range(GDN_HEADS):
            s = s_scr[h]
            prod = _dot(lhs_scr[h, c], s.astype(BF16))
            o_scr[h, rows, :] = o_scr[h, rows, :] + prod[GDN_D:, :]
            s_scr[h] = s * gt_scr[h, c:c + 1, :] + (su_scr[h, c] - prod[:GDN_D, :])

    @pl.when(l == pl.num_programs(1) - 1)
    def _():
        ssm_ref[0] = s_scr[...]

    pieces = []
    for h in range(GDN_HEADS):
        o = o_scr[h]
        on = o * lax.rsqrt(jnp.mean(o * o, axis=-1, keepdims=True) + EPS) * onorm_ref[...]
        pieces.append((on * _silu(gate_scr[:, h * GDN_D:(h + 1) * GDN_D])).astype(BF16))
    y_ref[0] = x + _dot(jnp.concatenate(pieces, axis=1), wout_ref[...])


def _gdn_prompt(x, norm, wqkv, wg, wab, wabt, convw, alog, dtb, onorm, wout):
    b, l, d = x.shape
    tm = TM_A
    h = GDN_HEADS
    grid = (b, l // tm)
    in_specs = [
        pl.BlockSpec((1, tm, d), lambda i, j: (i, j, 0)),
        _full_spec((1, d)), _full_spec(wqkv.shape), _full_spec(wg.shape), _full_spec(wab.shape),
        _full_spec(wabt.shape), _full_spec(convw.shape),
        _full_spec((1, h)), _full_spec((1, h)), _full_spec((h, 1)), _full_spec((h, 1)),
        _full_spec((1, GDN_D)), _full_spec(wout.shape),
    ]
    out_specs = [
        pl.BlockSpec((1, tm, d), lambda i, j: (i, j, 0)),
        pl.BlockSpec((1, CONV_W - 1, QKV_W), lambda i, j: (i, 0, 0)),
        pl.BlockSpec((1, h, GDN_D, GDN_D), lambda i, j: (i, 0, 0, 0)),
    ]
    out_shape = [
        jax.ShapeDtypeStruct((b, l, d), F32),
        jax.ShapeDtypeStruct((b, CONV_W - 1, QKV_W), F32),
        jax.ShapeDtypeStruct((b, h, GDN_D, GDN_D), F32),
    ]
    scratch = [
        pltpu.VMEM((tm + CARRY, QKV_W), F32),
        pltpu.VMEM((tm, h * GDN_D), F32),
        pltpu.VMEM((h, GDN_D, GDN_D), F32),
        pltpu.VMEM((h, tm, GDN_D), BF16),
        pltpu.VMEM((h, tm, GDN_D), BF16),
        pltpu.VMEM((h, tm, GDN_D), BF16),
        pltpu.VMEM((h, tm, GDN_D), F32),
        pltpu.VMEM((h, tm, GDN_D), BF16),
        pltpu.VMEM((h, tm, 2 * GDN_D), F32),
        pltpu.VMEM((h, tm, CHUNK), F32),
        pltpu.VMEM((h, 8, GDN_D), F32),
        pltpu.VMEM((h, tm, GDN_D), F32),
        pltpu.VMEM((h, tm // CHUNK, GDN_D + CHUNK, GDN_D), BF16),
        pltpu.VMEM((h, tm // CHUNK, GDN_D, GDN_D), F32),
    ]
    return pl.pallas_call(
        _gdn_prompt_kernel,
        grid=grid, in_specs=in_specs, out_specs=out_specs, out_shape=out_shape,
        scratch_shapes=scratch,
        compiler_params=pltpu.CompilerParams(
            dimension_semantics=("arbitrary", "arbitrary"), vmem_limit_bytes=VMEM_LIMIT),
        name="gdn_prompt",
    )(x, norm.reshape(1, d), wqkv, wg, wab, wabt, convw,
      alog.reshape(1, h), dtb.reshape(1, h), alog.reshape(h, 1), dtb.reshape(h, 1),
      onorm.reshape(1, GDN_D), wout)


def _gdn_sample_front_kernel(x_ref, norm_ref, wqkv_ref, wg_ref, wab_ref, convw_ref,
                             c0_ref, c1_ref, c2_ref, alog_r_ref, dtb_r_ref,
                             pre_ref, q_ref, k_ref, u_ref, w_ref, qd_ref, gate_ref, eg_ref):
    x = x_ref[...]
    xn = x * lax.rsqrt(jnp.mean(x * x, axis=-1, keepdims=True) + EPS) * norm_ref[...]
    xb = xn.astype(BF16)
    pre = _dot(xb, wqkv_ref[...])
    pre_ref[...] = pre
    gate_ref[...] = _dot(xb, wg_ref[...])
    ab = _dot(xb, wab_ref[...])
    g = -jnp.exp(alog_r_ref[...]) * _softplus(ab[:, :GDN_HEADS] + dtb_r_ref[...])
    beta = jax.nn.sigmoid(ab[:, GDN_HEADS:])
    e_g = jnp.exp(g)
    eg_ref[...] = e_g
    act = _silu(((c0_ref[...] * convw_ref[0:1, :] + c1_ref[...] * convw_ref[1:2, :])
                 + c2_ref[...] * convw_ref[2:3, :]) + pre * convw_ref[3:4, :])
    w_all = GDN_HEADS * GDN_D
    for h in range(GDN_HEADS):
        cols = slice(h * GDN_D, (h + 1) * GDN_D)
        qh = act[:, h * GDN_D:(h + 1) * GDN_D]
        kh = act[:, w_all + h * GDN_D:w_all + (h + 1) * GDN_D]
        vh = act[:, 2 * w_all + h * GDN_D:2 * w_all + (h + 1) * GDN_D]
        qn = qh * lax.rsqrt(jnp.sum(qh * qh, axis=-1, keepdims=True) + EPS) * (GDN_D ** -0.5)
        kn = kh * lax.rsqrt(jnp.sum(kh * kh, axis=-1, keepdims=True) + EPS)
        b_col = beta[:, h:h + 1]
        eg_col = e_g[:, h:h + 1]
        q_ref[:, cols] = qn
        k_ref[:, cols] = kn
        u_ref[:, cols] = vh * b_col
        w_ref[:, cols] = kn * b_col * eg_col
        qd_ref[:, cols] = qn * eg_col


def _gdn_sample_state_kernel(q_ref, k_ref, u_ref, w_ref, qd_ref, eg_ref, s_ref, o_ref, s_out_ref):
    bb = s_ref.shape[0]
    row = lax.broadcasted_iota(jnp.int32, (8, GDN_D), 0)
    for b in range(bb):
        for h in range(GDN_HEADS):
            cols = slice(h * GDN_D, (h + 1) * GDN_D)
            s = s_ref[b, h]
            sb = s.astype(BF16)
            q = q_ref[b:b + 1, cols].astype(BF16).astype(F32)
            k = k_ref[b:b + 1, cols].astype(BF16).astype(F32)
            lhs = jnp.where(row == 0, w_ref[b:b + 1, cols],
                            jnp.where(row == 1, qd_ref[b:b + 1, cols], 0.0))
            prod = _dot(lhs.astype(BF16), sb)
            v_new = u_ref[b:b + 1, cols] - prod[0:1, :]
            vb = v_new.astype(BF16).astype(F32)
            qk = jnp.sum(q * k, axis=-1, keepdims=True)
            o_ref[b:b + 1, cols] = prod[1:2, :] + qk.astype(BF16).astype(F32) * vb
            k8 = jnp.where(row == 0, k, 0.0).astype(BF16)
            v8 = jnp.where(row == 0, vb, 0.0).astype(BF16)
            s_out_ref[b, h] = s * eg_ref[b:b + 1, h:h + 1] + _dot_tn(k8, v8)


def _out_proj_kernel(x_ref, o_ref, gate_ref, onorm_ref, wout_ref, y_ref, *, head_w, norm_heads):
    pieces = []
    for h in range(x_ref.shape[1] // head_w):
        cols = slice(h * head_w, (h + 1) * head_w)
        o = o_ref[:, cols]
        if norm_heads:
            o = o * lax.rsqrt(jnp.mean(o * o, axis=-1, keepdims=True) + EPS) * onorm_ref[...]
        pieces.append((o * _silu(gate_ref[:, cols])).astype(BF16))
    y_ref[...] = x_ref[...] + _dot(jnp.concatenate(pieces, axis=1), wout_ref[...])


def _gdn_sample(x, conv_state, ssm_state, norm, wqkv, wg, wab, convw, alog, dtb, onorm, wout):
    n, d = x.shape
    h = GDN_HEADS
    hw = h * GDN_D
    conv_t = jnp.transpose(conv_state, (1, 0, 2))
    f = lambda shape: jax.ShapeDtypeStruct(shape, F32)
    pre, q, k, u, w, qd, gate, eg = pl.pallas_call(
        _gdn_sample_front_kernel,
        out_shape=[f((n, QKV_W)), f((n, hw)), f((n, hw)), f((n, hw)), f((n, hw)), f((n, hw)),
                   f((n, hw)), f((n, h))],
        compiler_params=pltpu.CompilerParams(vmem_limit_bytes=VMEM_LIMIT),
        name="gdn_sample_front",
    )(x, norm.reshape(1, d), wqkv, wg, wab, convw, conv_t[0], conv_t[1], conv_t[2],
      alog.reshape(1, h), dtb.reshape(1, h))
    bb = 8
    row_spec = lambda width: pl.BlockSpec((bb, width), lambda i: (i, 0))
    st_spec = pl.BlockSpec((bb, h, GDN_D, GDN_D), lambda i: (i, 0, 0, 0))
    o, s_new = pl.pallas_call(
        _gdn_sample_state_kernel,
        grid=(n // bb,),
        in_specs=[row_spec(hw)] * 5 + [row_spec(h), st_spec],
        out_specs=[row_spec(hw), st_spec],
        out_shape=[f((n, hw)), f(ssm_state.shape)],
        compiler_params=pltpu.CompilerParams(dimension_semantics=("arbitrary",)),
        name="gdn_sample_state",
    )(q, k, u, w, qd, eg, ssm_state)
    y = pl.pallas_call(
        functools.partial(_out_proj_kernel, head_w=GDN_D, norm_heads=True),
        out_shape=f((n, d)),
        name="gdn_sample_out",
    )(x, o, gate, onorm.reshape(1, GDN_D), wout)
    conv_new = jnp.stack([conv_t[1], conv_t[2], pre], axis=1)
    return y, conv_new, s_new


def _head_rms(x, gain_row, n_heads):
    pieces = []
    for h in range(n_heads):
        xh = x[:, h * HEAD_DIM:(h + 1) * HEAD_DIM]
        pieces.append(xh * lax.rsqrt(jnp.mean(xh * xh, axis=-1, keepdims=True) + EPS) * gain_row)
    return pieces


def _pair_rms(x, gain2, left):
    sq = x * x
    lo = jnp.sum(jnp.where(left, sq, 0.0), axis=-1, keepdims=True)
    hi = jnp.sum(jnp.where(left, 0.0, sq), axis=-1, keepdims=True)
    ms = jnp.where(left, lo, hi) * (1.0 / HEAD_DIM)
    return x * lax.rsqrt(ms + EPS) * gain2


def _swa_prompt_kernel(y_ref, kvnorm_ref, wkv_ref, knorm2_ref, normb_ref, win_ref, qnorm2_ref,
                       sinks_ref, wout_ref, out_ref, kwin_ref, vwin_ref, k_scr, v_scr, o_scr):
    tq = y_ref.shape[1]
    w = WINDOW
    step = pl.program_id(1)
    lane = lax.broadcasted_iota(jnp.int32, (1, 2 * HEAD_DIM), 1)
    left = lane < HEAD_DIM

    @pl.when(step == 0)
    def _():
        k_scr[:, 0:w, :] = jnp.zeros((N_KV_HEADS, w, 2 * HEAD_DIM), BF16)
        v_scr[:, 0:w, :] = jnp.zeros((N_KV_HEADS, w, 2 * HEAD_DIM), BF16)

    @pl.when(step > 0)
    def _():
        k_scr[:, 0:w, :] = k_scr[:, tq:tq + w, :]
        v_scr[:, 0:w, :] = v_scr[:, tq:tq + w, :]

    y = y_ref[0]
    yn = y * lax.rsqrt(jnp.mean(y * y, axis=-1, keepdims=True) + EPS)
    kv = _dot((yn * kvnorm_ref[...]).astype(BF16), wkv_ref[...])
    for c in range(N_KV_HEADS // 2):
        cols = slice(c * 2 * HEAD_DIM, (c + 1) * 2 * HEAD_DIM)
        kp = _pair_rms(kv[:, cols], knorm2_ref[...], left)
        vp = kv[:, KV_W + c * 2 * HEAD_DIM:KV_W + (c + 1) * 2 * HEAD_DIM]
        kwin_ref[0, :, cols] = kp[tq - w:, :]
        vwin_ref[0, :, cols] = vp[tq - w:, :]
        kr = pltpu.roll(kp, HEAD_DIM, 1)
        vr = pltpu.roll(vp, HEAD_DIM, 1)
        k_scr[2 * c, w:w + tq, :] = jnp.where(left, kp, kr).astype(BF16)
        k_scr[2 * c + 1, w:w + tq, :] = jnp.where(left, kr, kp).astype(BF16)
        v_scr[2 * c, w:w + tq, :] = jnp.where(left, vp, vr).astype(BF16)
        v_scr[2 * c + 1, w:w + tq, :] = jnp.where(left, vr, vp).astype(BF16)

    qg = _dot((yn * normb_ref[...]).astype(BF16), win_ref[...])
    q_pairs = [_pair_rms(qg[:, c * 2 * HEAD_DIM:(c + 1) * 2 * HEAD_DIM], qnorm2_ref[...], left)
               for c in range(N_Q_HEADS // 2)]

    qi = lax.broadcasted_iota(jnp.int32, (w, 2 * w), 0)
    kj = lax.broadcasted_iota(jnp.int32, (w, 2 * w), 1)
    dist = qi - kj + w
    band = (dist >= 0) & (dist <= w)
    j_rel = (lax.broadcasted_iota(jnp.int32, (1, 2 * w), 1) - w).astype(F32)
    i_col = lax.broadcasted_iota(jnp.int32, (w, 1), 0).astype(F32)

    def scores(blk, hk):
        q_rows = slice(blk * w, (blk + 1) * w)
        lhs = []
        for c in (2 * hk, 2 * hk + 1):
            qp = q_pairs[c][q_rows, :]
            lhs.append(jnp.where(left, qp, 0.0))
            lhs.append(jnp.where(left, 0.0, qp))
        return _dot_nt(jnp.concatenate(lhs, axis=0).astype(BF16),
                       k_scr[hk, blk * w:blk * w + 2 * w, :])

    def attend(blk, hk, s4):
        q_rows = slice(blk * w, (blk + 1) * w)
        valid = band & ((step > 0) | (kj >= w)) if blk == 0 else band
        ps, inv = [], []
        for g in range(Q_GROUP):
            hq = hk * Q_GROUP + g
            slope = 2.0 ** (-8.0 * (hq + 1) / N_Q_HEADS)
            a = jnp.where(valid, s4[g * w:(g + 1) * w, :] + slope * j_rel, -jnp.inf)
            sink = sinks_ref[0:1, hq:hq + 1] + slope * i_col
            mx = jnp.maximum(jnp.max(a, axis=-1, keepdims=True), sink)
            p = jnp.exp(a - mx)
            inv.append(1.0 / (jnp.sum(p, axis=-1, keepdims=True) + jnp.exp(sink - mx)))
            ps.append(p.astype(BF16))
        o4 = _dot(jnp.concatenate(ps, axis=0),
                  v_scr[hk, blk * w:blk * w + 2 * w, :])
        for j, c in enumerate((2 * hk, 2 * hk + 1)):
            o_even = o4[(2 * j) * w:(2 * j + 1) * w, :] * inv[2 * j]
            o_odd = o4[(2 * j + 1) * w:(2 * j + 2) * w, :] * inv[2 * j + 1]
            o_scr[q_rows, c * 2 * HEAD_DIM:(c + 1) * 2 * HEAD_DIM] = jnp.where(left, o_even, o_odd)

    items = [(blk, hk) for blk in range(tq // w) for hk in range(N_KV_HEADS)]
    s_next = scores(*items[0])
    for n, item in enumerate(items):
        s_cur = s_next
        if n + 1 < len(items):
            s_next = scores(*items[n + 1])
        attend(*item, s_cur)

    o = (o_scr[...] * _silu(qg[:, ATT_W:])).astype(BF16)
    out_ref[0] = y + _dot(o, wout_ref[...])


def _swa_prompt(y, kvnorm, wkv, knorm, normb, win, qnorm, sinks, wout):
    b, l, d = y.shape
    tq = TQ_B
    w = WINDOW
    f = lambda shape: jax.ShapeDtypeStruct(shape, F32)
    knorm2 = jnp.concatenate([knorm, knorm]).reshape(1, 2 * HEAD_DIM)
    qnorm2 = (jnp.concatenate([qnorm, qnorm]) * (HEAD_DIM ** -0.5)).reshape(1, 2 * HEAD_DIM)
    in_specs = [
        pl.BlockSpec((1, tq, d), lambda i, j: (i, j, 0)),
        _full_spec((1, d)), _full_spec(wkv.shape), _full_spec((1, 2 * HEAD_DIM)), _full_spec((1, d)),
        _full_spec(win.shape), _full_spec((1, 2 * HEAD_DIM)), _full_spec((1, N_Q_HEADS)),
        _full_spec(wout.shape),
    ]
    out_specs = [
        pl.BlockSpec((1, tq, d), lambda i, j: (i, j, 0)),
        pl.BlockSpec((1, w, KV_W), lambda i, j: (i, 0, 0)),
        pl.BlockSpec((1, w, KV_W), lambda i, j: (i, 0, 0)),
    ]
    return pl.pallas_call(
        _swa_prompt_kernel,
        grid=(b, l // tq), in_specs=in_specs, out_specs=out_specs,
        out_shape=[f((b, l, d)), f((b, w, KV_W)), f((b, w, KV_W))],
        scratch_shapes=[pltpu.VMEM((N_KV_HEADS, w + tq, 2 * HEAD_DIM), BF16),
                        pltpu.VMEM((N_KV_HEADS, w + tq, 2 * HEAD_DIM), BF16),
                        pltpu.VMEM((tq, ATT_W), F32)],
        compiler_params=pltpu.CompilerParams(
            dimension_semantics=("arbitrary", "arbitrary"), vmem_limit_bytes=VMEM_LIMIT),
        name="swa_prompt",
    )(y, kvnorm.reshape(1, d), wkv, knorm2, normb.reshape(1, d), win, qnorm2,
      sinks.reshape(1, N_Q_HEADS), wout)


def _swa_sample_front_kernel(y_ref, kvnorm_ref, wkv_ref, knorm_ref, normb_ref, win_ref, qnorm_ref,
                             k_ref, v_ref, q_ref, gate_ref):
    y = y_ref[...]
    yn = y * lax.rsqrt(jnp.mean(y * y, axis=-1, keepdims=True) + EPS)
    kv = _dot((yn * kvnorm_ref[...]).astype(BF16), wkv_ref[...])
    k_ref[...] = jnp.concatenate(_head_rms(kv[:, :KV_W], knorm_ref[...], N_KV_HEADS), axis=1)
    v_ref[...] = kv[:, KV_W:]
    qg = _dot((yn * normb_ref[...]).astype(BF16), win_ref[...])
    q_ref[...] = jnp.concatenate(
        _head_rms(qg[:, :ATT_W], qnorm_ref[...] * (HEAD_DIM ** -0.5), N_Q_HEADS), axis=1)
    gate_ref[...] = qg[:, ATT_W:]


def _swa_sample_attn_kernel(q_ref, kn_ref, vn_ref, kc_ref, vc_ref, sinks_ref, slopes_ref,
                            o_ref, kwin_ref, vwin_ref):
    bb = kc_ref.shape[0]
    w = kc_ref.shape[1]
    hrow = lax.broadcasted_iota(jnp.int32, (N_Q_HEADS, KV_W), 0) // Q_GROUP
    lblk = lax.broadcasted_iota(jnp.int32, (N_Q_HEADS, KV_W), 1) // HEAD_DIM
    own = hrow == lblk
    dist_c = (w - lax.broadcasted_iota(jnp.int32, (1, w), 1)).astype(F32)
    krow = lax.broadcasted_iota(jnp.int32, (w, KV_W), 0)
    slopes = slopes_ref[...]
    sink = sinks_ref[...]
    for b in range(bb):
        q = q_ref[b]
        qm = jnp.where(own, jnp.concatenate([q] * N_KV_HEADS, axis=1), 0.0).astype(BF16)
        kc = kc_ref[b]
        vc = vc_ref[b]
        kn = kn_ref[b]
        vn = vn_ref[b]
        s_c = _dot_nt(qm, kc.astype(BF16)) - slopes * dist_c
        s_n = jnp.sum(qm.astype(F32) * kn.astype(BF16).astype(F32), axis=-1, keepdims=True)
        mx = jnp.maximum(jnp.maximum(jnp.max(s_c, axis=-1, keepdims=True), s_n), sink)
        p_c = jnp.exp(s_c - mx)
        p_n = jnp.exp(s_n - mx)
        den = jnp.sum(p_c, axis=-1, keepdims=True) + p_n + jnp.exp(sink - mx)
        p_c = p_c / den
        p_n = p_n / den
        r = _dot(p_c.astype(BF16), vc.astype(BF16))
        r = r + p_n.astype(BF16).astype(F32) * vn.astype(BF16).astype(F32)
        r = jnp.where(own, r, 0.0)
        acc = r[:, 0:HEAD_DIM]
        for blk in range(1, N_KV_HEADS):
            acc = acc + r[:, blk * HEAD_DIM:(blk + 1) * HEAD_DIM]
        o_ref[b] = acc
        kwin_ref[b] = jnp.where(krow == w - 1, kn, pltpu.roll(kc, w - 1, 0))
        vwin_ref[b] = jnp.where(krow == w - 1, vn, pltpu.roll(vc, w - 1, 0))


def _swa_sample(y, cache_k, cache_v, kvnorm, wkv, knorm, normb, win, qnorm, sinks, wout):
    n, d = y.shape
    w = cache_k.shape[1]
    f = lambda shape: jax.ShapeDtypeStruct(shape, F32)
    kn, vn, q, gate = pl.pallas_call(
        _swa_sample_front_kernel,
        out_shape=[f((n, KV_W)), f((n, KV_W)), f((n, ATT_W)), f((n, ATT_W))],
        compiler_params=pltpu.CompilerParams(vmem_limit_bytes=VMEM_LIMIT),
        name="swa_sample_front",
    )(y, kvnorm.reshape(1, d), wkv, knorm.reshape(1, HEAD_DIM), normb.reshape(1, d), win,
      qnorm.reshape(1, HEAD_DIM))
    bb = 8
    slopes = (2.0 ** (-8.0 * jnp.arange(1, N_Q_HEADS + 1, dtype=F32) / N_Q_HEADS)).reshape(N_Q_HEADS, 1)
    spec3 = lambda a, c: pl.BlockSpec((bb, a, c), lambda i: (i, 0, 0))
    o, kwin, vwin = pl.pallas_call(
        _swa_sample_attn_kernel,
        grid=(n // bb,),
        in_specs=[spec3(N_Q_HEADS, HEAD_DIM), spec3(1, KV_W), spec3(1, KV_W), spec3(w, KV_W),
                  spec3(w, KV_W), _full_spec((N_Q_HEADS, 1)), _full_spec((N_Q_HEADS, 1))],
        out_specs=[spec3(N_Q_HEADS, HEAD_DIM), spec3(w, KV_W), spec3(w, KV_W)],
        out_shape=[f((n, N_Q_HEADS, HEAD_DIM)), f((n, w, KV_W)), f((n, w, KV_W))],
        compiler_params=pltpu.CompilerParams(dimension_semantics=("arbitrary",)),
        name="swa_sample_attn",
    )(q.reshape(n, N_Q_HEADS, HEAD_DIM), kn.reshape(n, 1, KV_W), vn.reshape(n, 1, KV_W),
      cache_k.reshape(n, w, KV_W), cache_v.reshape(n, w, KV_W),
      sinks.reshape(N_Q_HEADS, 1), slopes)
    out = pl.pallas_call(
        functools.partial(_out_proj_kernel, head_w=ATT_W, norm_heads=False),
        out_shape=f((n, d)),
        name="swa_sample_out",
    )(y, o.reshape(n, ATT_W), gate, jnp.ones((1, ATT_W), F32), wout)
    return out, kwin, vwin


def kernel(x_prompt, x_sample, state_conv, state_ssm, cache_k_win, cache_v_win, norm_a, w_in_a, conv_w_a, a_log, dt_bias, o_norm_a, w_out_a, kv_norm, w_kv, k_norm, norm_b, w_in_b, q_norm, sinks, w_out_b):
    n_a = w_in_a.shape[0]
    n_b = w_in_b.shape[0]
    assert n_a == 1 and n_b == 1, "kernel is written for DEPTH == 2"
    bp, lp, d = x_prompt.shape
    n = x_sample.shape[0]
    hw = GDN_HEADS * GDN_D

    hp, hs = x_prompt, x_sample.reshape(n, d)
    conv_p, ssm_p, conv_s, ssm_s = [], [], [], []
    for layer in range(n_a):
        w_in = w_in_a[layer]
        wqkv = w_in[:, :QKV_W].astype(BF16)
        wg = w_in[:, QKV_W:QKV_W + hw].astype(BF16)
        wab = w_in[:, QKV_W + hw:].astype(BF16)
        wabt = wab.T
        wout = w_out_a[layer].astype(BF16)
        hp, cbuf, st = _gdn_prompt(hp, norm_a[layer], wqkv, wg, wab, wabt, conv_w_a[layer],
                                   a_log[layer], dt_bias[layer], o_norm_a[layer], wout)
        conv_p.append(cbuf)
        ssm_p.append(st)
        hs, cbuf, st = _gdn_sample(hs, state_conv[layer], state_ssm[layer], norm_a[layer], wqkv, wg,
                                   wab, conv_w_a[layer], a_log[layer], dt_bias[layer],
                                   o_norm_a[layer], wout)
        conv_s.append(cbuf)
        ssm_s.append(st)

    wkv = w_kv.astype(BF16)
    win = w_in_b[0].astype(BF16)
    woutb = w_out_b[0].astype(BF16)
    hp, k_win_p, v_win_p = _swa_prompt(hp, kv_norm, wkv, k_norm, norm_b[0], win, q_norm[0], sinks[0], woutb)
    hs, k_win_s, v_win_s = _swa_sample(hs, cache_k_win, cache_v_win, kv_norm, wkv, k_norm, norm_b[0],
                                       win, q_norm[0], sinks[0], woutb)
    kv_shape = (N_KV_HEADS, HEAD_DIM)
    return (hp, hs.reshape(n, 1, d), jnp.stack(conv_p), jnp.stack(ssm_p),
            k_win_p.reshape(bp, WINDOW, *kv_shape), v_win_p.reshape(bp, WINDOW, *kv_shape),
            jnp.stack(conv_s), jnp.stack(ssm_s),
            k_win_s.reshape(n, WINDOW, *kv_shape), v_win_s.reshape(n, WINDOW, *kv_shape))
```

```python
import functools

import jax
import jax.numpy as jnp
from jax import lax
from jax.experimental import pallas as pl
from jax.experimental.pallas import tpu as pltpu

F32 = jnp.float32
BF16 = jnp.bfloat16
EPS = 1e-6

D_MODEL = 1024
GDN_HEADS = 8
GDN_D = 128
QKV_W = 3 * GDN_HEADS * GDN_D
CONV_W = 4
CHUNK = 64
N_Q_HEADS = 16
N_KV_HEADS = 4
Q_GROUP = N_Q_HEADS // N_KV_HEADS
HEAD_DIM = 64
KV_W = N_KV_HEADS * HEAD_DIM
ATT_W = N_Q_HEADS * HEAD_DIM
WINDOW = 128

TM_A = 256
TQ_B = 256
HEADS_PER_ITER = 8
CARRY = 8
VMEM_LIMIT = 52 * 1024 * 1024

_NT = (((1,), (1,)), ((), ()))
_TN = (((0,), (0,)), ((), ()))


def _dot(a, b):
    return jnp.dot(a, b, preferred_element_type=F32)


def _dot_nt(a, b):
    return lax.dot_general(a, b, _NT, preferred_element_type=F32)


def _dot_tn(a, b):
    return lax.dot_general(a, b, _TN, preferred_element_type=F32)


def _split(x):
    hi = x.astype(BF16)
    lo = (x - hi.astype(F32)).astype(BF16)
    return hi, lo


def _dot3_parts(a_parts, b_parts):
    ah, al = a_parts
    bh, bl = b_parts
    return (_dot(al, bh) + _dot(ah, bl)) + _dot(ah, bh)


def _dot3(a, b):
    return _dot3_parts(_split(a), _split(b))


def _dot_exact_lhs(a_bf, b):
    b0 = b.astype(BF16)
    r1 = b - b0.astype(F32)
    b1 = r1.astype(BF16)
    b2 = (r1 - b1.astype(F32)).astype(BF16)
    return (_dot(a_bf, b2) + _dot(a_bf, b1)) + _dot(a_bf, b0)


def _dot_exact_rhs(a, b_bf):
    a0 = a.astype(BF16)
    r1 = a - a0.astype(F32)
    a1 = r1.astype(BF16)
    a2 = (r1 - a1.astype(F32)).astype(BF16)
    return (_dot(a2, b_bf) + _dot(a1, b_bf)) + _dot(a0, b_bf)


def _silu(x):
    return x * jax.nn.sigmoid(x)


def _softplus(x):
    return jnp.maximum(x, 0.0) + jnp.log1p(jnp.exp(-jnp.abs(x)))


def _split_dup(x2, left):
    hi = x2.astype(BF16)
    lo = (x2 - hi.astype(F32)).astype(BF16)
    lhs = jnp.where(left, hi, lo)
    rhs = jnp.concatenate([jnp.concatenate([hi, lo], axis=1),
                           jnp.concatenate([hi, jnp.zeros_like(hi)], axis=1)], axis=0)
    return lhs, rhs


def _dot_split(a_lhs, b_rhs):
    y = _dot(a_lhs, b_rhs)
    half = y.shape[1] // 2
    return y[:, :half] + y[:, half:]


def _unit_lower_inverses(ms2, left):
    n = ms2[0].shape[0]
    row = lax.broadcasted_iota(jnp.int32, (n, 2 * n), 0)
    col = lax.broadcasted_iota(jnp.int32, (n, 2 * n), 1) % n
    eye2 = jnp.where(row == col, 1.0, 0.0).astype(F32)
    zeros = jnp.zeros((n, 2 * n), BF16)

    def times(a2, b2):
        return _dot(a2.astype(BF16), jnp.concatenate([b2.astype(BF16), zeros], axis=0))

    ts = [eye2 - m for m in ms2]
    ps = list(ms2)
    steps = max(1, (n - 1).bit_length()) - 1
    for _ in range(steps):
        ps = [times(p, p) for p in ps]
        ts = [t + times(t, p) for t, p in zip(ts, ps)]
    rs = [(eye2 - t) - _dot_split(_split_dup(m, left)[0], _split_dup(t, left)[1]) for m, t in zip(ms2, ts)]
    return [t + times(t, r) for t, r in zip(ts, rs)]


def _full_spec(shape):
    nd = len(shape)
    return pl.BlockSpec(shape, lambda *_: (0,) * nd)


def _gdn_gates(ab, ab_t, alog_r, dtb_r, alog_c, dtb_c, tm, chunk):
    h = GDN_HEADS
    g_c = -jnp.exp(alog_r) * _softplus(ab[:, :h] + dtb_r)
    beta = jax.nn.sigmoid(ab[:, h:])
    g_r = -jnp.exp(alog_c) * _softplus(ab_t[:h, :] + dtb_c)
    row = lax.broadcasted_iota(jnp.int32, (tm, tm), 0)
    col = lax.broadcasted_iota(jnp.int32, (tm, tm), 1)
    same = (row // chunk) == (col // chunk)
    lower = jnp.where(same & (row >= col), 1.0, 0.0).astype(BF16)
    upper = jnp.where(same & (row <= col), 1.0, 0.0).astype(BF16)
    gc = _dot_exact_lhs(lower, g_c)
    gr = _dot_exact_rhs(g_r, upper)
    return beta, gc, gr


def _gdn_prompt_kernel(x_ref, norm_ref, wqkv_ref, wg_ref, wab_ref, wabt_ref, convw_ref,
                       alog_r_ref, dtb_r_ref, alog_c_ref, dtb_c_ref, onorm_ref, wout_ref,
                       y_ref, conv_ref, ssm_ref,
                       pre_scr, gate_scr, s_scr, k_scr, kb_scr, q_scr, qd_scr, kd_scr,
                       rhs_scr, dec_scr, gt_scr, o_scr, lhs_scr, su_scr):
    tm = x_ref.shape[1]
    n_chunks = tm // CHUNK
    l = pl.program_id(1)

    @pl.when(l == 0)
    def _():
        pre_scr[0:CARRY, :] = jnp.zeros((CARRY, QKV_W), F32)
        s_scr[...] = jnp.zeros(s_scr.shape, F32)

    x = x_ref[0]
    xn = x * lax.rsqrt(jnp.mean(x * x, axis=-1, keepdims=True) + EPS) * norm_ref[...]
    xb = xn.astype(BF16)
    ab = _dot(xb, wab_ref[...])
    ab_t = _dot_nt(wabt_ref[...], xb)
    beta, gc, gr = _gdn_gates(ab, ab_t, alog_r_ref[...], dtb_r_ref[...],
                              alog_c_ref[...], dtb_c_ref[...], tm, CHUNK)
    e_g = jnp.exp(gc)
    g_last = jnp.concatenate(
        [jnp.broadcast_to(gc[c * CHUNK + CHUNK - 1:c * CHUNK + CHUNK, :], (CHUNK, GDN_HEADS))
         for c in range(n_chunks)], axis=0)
    e_kd = jnp.exp(g_last - gc)
    e_tot = jnp.exp(g_last)

    ci = lax.broadcasted_iota(jnp.int32, (CHUNK, 2 * CHUNK), 0)
    cj = lax.broadcasted_iota(jnp.int32, (CHUNK, 2 * CHUNK), 1) % CHUNK
    causal = ci >= cj
    strict = ci > cj
    left = lax.broadcasted_iota(jnp.int32, (1, 2 * CHUNK), 1) < CHUNK

    def conv_act(j):
        cols = slice(j * GDN_D, (j + 1) * GDN_D)
        half = pre_scr[CARRY - 3:CARRY - 3 + tm, cols] * (0.5 * convw_ref[0:1, cols])
        for i in range(1, CONV_W):
            half = half + pre_scr[CARRY - 3 + i:CARRY - 3 + i + tm, cols] * (0.5 * convw_ref[i:i + 1, cols])
        return half + half * jnp.tanh(half)

    def head_prep(h):
        qh = conv_act(h)
        kh = conv_act(GDN_HEADS + h)
        vh = conv_act(2 * GDN_HEADS + h)
        qn = qh * lax.rsqrt(jnp.sum(qh * qh, axis=-1, keepdims=True) + EPS) * (GDN_D ** -0.5)
        kn = kh * lax.rsqrt(jnp.sum(kh * kh, axis=-1, keepdims=True) + EPS)
        b_col = beta[:, h:h + 1]
        eg_col = e_g[:, h:h + 1]
        kb = kn * b_col
        k_scr[h] = kn.astype(BF16)
        kb_scr[h] = kb.astype(BF16)
        q_scr[h] = qn.astype(BF16)
        qd_scr[h] = qn * eg_col
        kd_scr[h] = (kn * e_kd[:, h:h + 1]).astype(BF16)
        rhs_scr[h, :, 0:GDN_D] = (vh * b_col).astype(BF16)
        rhs_scr[h, :, GDN_D:2 * GDN_D] = (kb * eg_col).astype(BF16)
        for c in range(n_chunks):
            rows = slice(c * CHUNK, (c + 1) * CHUNK)
            g_row = gr[h:h + 1, rows]
            diff = gc[rows, h:h + 1] - jnp.concatenate([g_row, g_row], axis=1)
            dec_scr[h, rows, :] = jnp.exp(jnp.where(causal, diff, -jnp.inf))
            gt_scr[h, c:c + 1, :] = jnp.broadcast_to(e_tot[c * CHUNK:c * CHUNK + 1, h:h + 1], (1, GDN_D))

    def chunk_terms(heads):
        items = [(h, c) for h in heads for c in range(n_chunks)]
        rows = [slice(c * CHUNK, (c + 1) * CHUNK) for _, c in items]
        decs = [dec_scr[h, r, :] for (h, _), r in zip(items, rows)]
        aqs = []
        for (h, _), r in zip(items, rows):
            k = k_scr[h, r, :]
            aqs.append(_dot_nt(jnp.concatenate([kb_scr[h, r, :], q_scr[h, r, :]], axis=0),
                               jnp.concatenate([k, k], axis=0)))
        ms = [jnp.where(strict, aq[:CHUNK, :] * d, 0.0) for aq, d in zip(aqs, decs)]
        qks = [jnp.where(causal, aq[CHUNK:, :] * d, 0.0)[:, :CHUNK].astype(BF16) for aq, d in zip(aqs, decs)]
        ts = _unit_lower_inverses(ms, left)
        sols = []
        for (h, _), r, t in zip(items, rows, ts):
            rhs = rhs_scr[h, r, :]
            sols.append(_dot(_split_dup(t, left)[0], jnp.concatenate([rhs, rhs], axis=0)).astype(BF16))
        kd_uws = [_dot_tn(kd_scr[h, r, :], sol)
                  for (h, _), r, sol in zip(items, rows, sols)]
        qk_uws = [_dot(qk, sol) for qk, sol in zip(qks, sols)]
        for (h, c), r, kd_uw, qk_uw in zip(items, rows, kd_uws, qk_uws):
            lhs_scr[h, c, 0:GDN_D, :] = kd_uw[:, GDN_D:].astype(BF16)
            lhs_scr[h, c, GDN_D:GDN_D + CHUNK, :] = (qd_scr[h, r, :] - qk_uw[:, GDN_D:]).astype(BF16)
            su_scr[h, c] = kd_uw[:, :GDN_D]
            o_scr[h, r, :] = qk_uw[:, :GDN_D]

    def project(heads):
        for part in range(3):
            cols = slice((part * GDN_HEADS + heads[0]) * GDN_D, (part * GDN_HEADS + heads[-1] + 1) * GDN_D)
            pre_scr[CARRY:CARRY + tm, cols] = _dot(xb, wqkv_ref[:, cols])

    groups = [list(range(i, i + HEADS_PER_ITER)) for i in range(0, GDN_HEADS, HEADS_PER_ITER)]
    n_groups = len(groups)
    for n in range(-2, n_groups):
        if 0 <= n + 2 < n_groups:
            project(groups[n + 2])
        if 0 <= n + 1 < n_groups:
            for h in groups[n + 1]:
                head_prep(h)
        if n == n_groups - 1:
            gate_scr[...] = _dot(xb, wg_ref[...])
        if n >= 0:
            chunk_terms(groups[n])

    conv_ref[0] = pre_scr[tm + CARRY - 3:tm + CARRY, :]
    pre_scr[0:CARRY, :] = pre_scr[tm:tm + CARRY, :]

    for c in range(n_chunks):
        rows = slice(c * CHUNK, (c + 1) * CHUNK)
        for h in range(GDN_HEADS):
            s = s_scr[h]
            prod = _dot(lhs_scr[h, c], s.astype(BF16))
            o_scr[h, rows, :] = o_scr[h, rows, :] + prod[GDN_D:, :]
            s_scr[h] = s * gt_scr[h, c:c + 1, :] + (su_scr[h, c] - prod[:GDN_D, :])

    @pl.when(l == pl.num_programs(1) - 1)
    def _():
        ssm_ref[0] = s_scr[...]

    pieces = []
    for h in range(GDN_HEADS):
        o = o_scr[h]
        on = o * lax.rsqrt(jnp.mean(o * o, axis=-1, keepdims=True) + EPS) * onorm_ref[...]
        pieces.append((on * _silu(gate_scr[:, h * GDN_D:(h + 1) * GDN_D])).astype(BF16))
    y_ref[0] = x + _dot(jnp.concatenate(pieces, axis=1), wout_ref[...])


def _gdn_prompt(x, norm, wqkv, wg, wab, wabt, convw, alog, dtb, onorm, wout):
    b, l, d = x.shape
    tm = TM_A
    h = GDN_HEADS
    grid = (b, l // tm)
    in_specs = [
        pl.BlockSpec((1, tm, d), lambda i, j: (i, j, 0)),
        _full_spec((1, d)), _full_spec(wqkv.shape), _full_spec(wg.shape), _full_spec(wab.shape),
        _full_spec(wabt.shape), _full_spec(convw.shape),
        _full_spec((1, h)), _full_spec((1, h)), _full_spec((h, 1)), _full_spec((h, 1)),
        _full_spec((1, GDN_D)), _full_spec(wout.shape),
    ]
    out_specs = [
        pl.BlockSpec((1, tm, d), lambda i, j: (i, j, 0)),
        pl.BlockSpec((1, CONV_W - 1, QKV_W), lambda i, j: (i, 0, 0)),
        pl.BlockSpec((1, h, GDN_D, GDN_D), lambda i, j: (i, 0, 0, 0)),
    ]
    out_shape = [
        jax.ShapeDtypeStruct((b, l, d), F32),
        jax.ShapeDtypeStruct((b, CONV_W - 1, QKV_W), F32),
        jax.ShapeDtypeStruct((b, h, GDN_D, GDN_D), F32),
    ]
    scratch = [
        pltpu.VMEM((tm + CARRY, QKV_W), F32),
        pltpu.VMEM((tm, h * GDN_D), F32),
        pltpu.VMEM((h, GDN_D, GDN_D), F32),
        pltpu.VMEM((h, tm, GDN_D), BF16),
        pltpu.VMEM((h, tm, GDN_D), BF16),
        pltpu.VMEM((h, tm, GDN_D), BF16),
        pltpu.VMEM((h, tm, GDN_D), F32),
        pltpu.VMEM((h, tm, GDN_D), BF16),
        pltpu.VMEM((h, tm, 2 * GDN_D), BF16),
        pltpu.VMEM((h, tm, 2 * CHUNK), F32),
        pltpu.VMEM((h, 8, GDN_D), F32),
        pltpu.VMEM((h, tm, GDN_D), F32),
        pltpu.VMEM((h, tm // CHUNK, GDN_D + CHUNK, GDN_D), BF16),
        pltpu.VMEM((h, tm // CHUNK, GDN_D, GDN_D), F32),
    ]
    return pl.pallas_call(
        _gdn_prompt_kernel,
        grid=grid, in_specs=in_specs, out_specs=out_specs, out_shape=out_shape,
        scratch_shapes=scratch,
        compiler_params=pltpu.CompilerParams(
            dimension_semantics=("arbitrary", "arbitrary"), vmem_limit_bytes=VMEM_LIMIT),
        name="gdn_prompt",
    )(x, norm.reshape(1, d), wqkv, wg, wab, wabt, convw,
      alog.reshape(1, h), dtb.reshape(1, h), alog.reshape(h, 1), dtb.reshape(h, 1),
      onorm.reshape(1, GDN_D), wout)


def _gdn_sample_front_kernel(x_ref, norm_ref, wqkv_ref, wg_ref, wab_ref, convw_ref,
                             c0_ref, c1_ref, c2_ref, alog_r_ref, dtb_r_ref,
                             pre_ref, q_ref, k_ref, u_ref, w_ref, qd_ref, gate_ref, eg_ref):
    x = x_ref[...]
    xn = x * lax.rsqrt(jnp.mean(x * x, axis=-1, keepdims=True) + EPS) * norm_ref[...]
    xb = xn.astype(BF16)
    pre = _dot(xb, wqkv_ref[...])
    pre_ref[...] = pre
    gate_ref[...] = _dot(xb, wg_ref[...])
    ab = _dot(xb, wab_ref[...])
    g = -jnp.exp(alog_r_ref[...]) * _softplus(ab[:, :GDN_HEADS] + dtb_r_ref[...])
    beta = jax.nn.sigmoid(ab[:, GDN_HEADS:])
    e_g = jnp.exp(g)
    eg_ref[...] = e_g
    act = _silu(((c0_ref[...] * convw_ref[0:1, :] + c1_ref[...] * convw_ref[1:2, :])
                 + c2_ref[...] * convw_ref[2:3, :]) + pre * convw_ref[3:4, :])
    w_all = GDN_HEADS * GDN_D
    for h in range(GDN_HEADS):
        cols = slice(h * GDN_D, (h + 1) * GDN_D)
        qh = act[:, h * GDN_D:(h + 1) * GDN_D]
        kh = act[:, w_all + h * GDN_D:w_all + (h + 1) * GDN_D]
        vh = act[:, 2 * w_all + h * GDN_D:2 * w_all + (h + 1) * GDN_D]
        qn = qh * lax.rsqrt(jnp.sum(qh * qh, axis=-1, keepdims=True) + EPS) * (GDN_D ** -0.5)
        kn = kh * lax.rsqrt(jnp.sum(kh * kh, axis=-1, keepdims=True) + EPS)
        b_col = beta[:, h:h + 1]
        eg_col = e_g[:, h:h + 1]
        q_ref[:, cols] = qn
        k_ref[:, cols] = kn
        u_ref[:, cols] = vh * b_col
        w_ref[:, cols] = kn * b_col * eg_col
        qd_ref[:, cols] = qn * eg_col


def _gdn_sample_state_kernel(q_ref, k_ref, u_ref, w_ref, qd_ref, eg_ref, s_ref, o_ref, s_out_ref):
    bb = s_ref.shape[0]
    row = lax.broadcasted_iota(jnp.int32, (8, GDN_D), 0)
    for b in range(bb):
        for h in range(GDN_HEADS):
            cols = slice(h * GDN_D, (h + 1) * GDN_D)
            s = s_ref[b, h]
            sb = s.astype(BF16)
            q = q_ref[b:b + 1, cols].astype(BF16).astype(F32)
            k = k_ref[b:b + 1, cols].astype(BF16).astype(F32)
            lhs = jnp.where(row == 0, w_ref[b:b + 1, cols],
                            jnp.where(row == 1, qd_ref[b:b + 1, cols], 0.0))
            prod = _dot(lhs.astype(BF16), sb)
            v_new = u_ref[b:b + 1, cols] - prod[0:1, :]
            vb = v_new.astype(BF16).astype(F32)
            qk = jnp.sum(q * k, axis=-1, keepdims=True)
            o_ref[b:b + 1, cols] = prod[1:2, :] + qk.astype(BF16).astype(F32) * vb
            k8 = jnp.where(row == 0, k, 0.0).astype(BF16)
            v8 = jnp.where(row == 0, vb, 0.0).astype(BF16)
            s_out_ref[b, h] = s * eg_ref[b:b + 1, h:h + 1] + _dot_tn(k8, v8)


def _out_proj_kernel(x_ref, o_ref, gate_ref, onorm_ref, wout_ref, y_ref, *, head_w, norm_heads):
    pieces = []
    for h in range(x_ref.shape[1] // head_w):
        cols = slice(h * head_w, (h + 1) * head_w)
        o = o_ref[:, cols]
        if norm_heads:
            o = o * lax.rsqrt(jnp.mean(o * o, axis=-1, keepdims=True) + EPS) * onorm_ref[...]
        pieces.append((o * _silu(gate_ref[:, cols])).astype(BF16))
    y_ref[...] = x_ref[...] + _dot(jnp.concatenate(pieces, axis=1), wout_ref[...])


def _gdn_sample(x, conv_state, ssm_state, norm, wqkv, wg, wab, convw, alog, dtb, onorm, wout):
    n, d = x.shape
    h = GDN_HEADS
    hw = h * GDN_D
    conv_t = jnp.transpose(conv_state, (1, 0, 2))
    f = lambda shape: jax.ShapeDtypeStruct(shape, F32)
    pre, q, k, u, w, qd, gate, eg = pl.pallas_call(
        _gdn_sample_front_kernel,
        out_shape=[f((n, QKV_W)), f((n, hw)), f((n, hw)), f((n, hw)), f((n, hw)), f((n, hw)),
                   f((n, hw)), f((n, h))],
        compiler_params=pltpu.CompilerParams(vmem_limit_bytes=VMEM_LIMIT),
        name="gdn_sample_front",
    )(x, norm.reshape(1, d), wqkv, wg, wab, convw, conv_t[0], conv_t[1], conv_t[2],
      alog.reshape(1, h), dtb.reshape(1, h))
    bb = 8
    row_spec = lambda width: pl.BlockSpec((bb, width), lambda i: (i, 0))
    st_spec = pl.BlockSpec((bb, h, GDN_D, GDN_D), lambda i: (i, 0, 0, 0))
    o, s_new = pl.pallas_call(
        _gdn_sample_state_kernel,
        grid=(n // bb,),
        in_specs=[row_spec(hw)] * 5 + [row_spec(h), st_spec],
        out_specs=[row_spec(hw), st_spec],
        out_shape=[f((n, hw)), f(ssm_state.shape)],
        compiler_params=pltpu.CompilerParams(dimension_semantics=("arbitrary",)),
        name="gdn_sample_state",
    )(q, k, u, w, qd, eg, ssm_state)
    y = pl.pallas_call(
        functools.partial(_out_proj_kernel, head_w=GDN_D, norm_heads=True),
        out_shape=f((n, d)),
        name="gdn_sample_out",
    )(x, o, gate, onorm.reshape(1, GDN_D), wout)
    conv_new = jnp.stack([conv_t[1], conv_t[2], pre], axis=1)
    return y, conv_new, s_new


def _head_rms(x, gain_row, n_heads):
    pieces = []
    for h in range(n_heads):
        xh = x[:, h * HEAD_DIM:(h + 1) * HEAD_DIM]
        pieces.append(xh * lax.rsqrt(jnp.mean(xh * xh, axis=-1, keepdims=True) + EPS) * gain_row)
    return pieces


def _pair_rms(x, gain2, left):
    sq = x * x
    lo = jnp.sum(jnp.where(left, sq, 0.0), axis=-1, keepdims=True)
    hi = jnp.sum(jnp.where(left, 0.0, sq), axis=-1, keepdims=True)
    ms = jnp.where(left, lo, hi) * (1.0 / HEAD_DIM)
    return x * lax.rsqrt(ms + EPS) * gain2


def _swa_prompt_kernel(y_ref, kvnorm_ref, wkv_ref, knorm2_ref, normb_ref, win_ref, qnorm2_ref,
                       sinks_ref, wout_ref, out_ref, kwin_ref, vwin_ref, k_scr, v_scr, o_scr):
    tq = y_ref.shape[1]
    w = WINDOW
    step = pl.program_id(1)
    lane = lax.broadcasted_iota(jnp.int32, (1, 2 * HEAD_DIM), 1)
    left = lane < HEAD_DIM

    @pl.when(step == 0)
    def _():
        k_scr[:, 0:w, :] = jnp.zeros((N_KV_HEADS, w, 2 * HEAD_DIM), BF16)
        v_scr[:, 0:w, :] = jnp.zeros((N_KV_HEADS, w, 2 * HEAD_DIM), BF16)

    @pl.when(step > 0)
    def _():
        k_scr[:, 0:w, :] = k_scr[:, tq:tq + w, :]
        v_scr[:, 0:w, :] = v_scr[:, tq:tq + w, :]

    y = y_ref[0]
    yn = y * lax.rsqrt(jnp.mean(y * y, axis=-1, keepdims=True) + EPS)
    kv = _dot((yn * kvnorm_ref[...]).astype(BF16), wkv_ref[...])
    for c in range(N_KV_HEADS // 2):
        cols = slice(c * 2 * HEAD_DIM, (c + 1) * 2 * HEAD_DIM)
        kp = _pair_rms(kv[:, cols], knorm2_ref[...], left)
        vp = kv[:, KV_W + c * 2 * HEAD_DIM:KV_W + (c + 1) * 2 * HEAD_DIM]
        kwin_ref[0, :, cols] = kp[tq - w:, :]
        vwin_ref[0, :, cols] = vp[tq - w:, :]
        kr = pltpu.roll(kp, HEAD_DIM, 1)
        vr = pltpu.roll(vp, HEAD_DIM, 1)
        k_scr[2 * c, w:w + tq, :] = jnp.where(left, kp, kr).astype(BF16)
        k_scr[2 * c + 1, w:w + tq, :] = jnp.where(left, kr, kp).astype(BF16)
        v_scr[2 * c, w:w + tq, :] = jnp.where(left, vp, vr).astype(BF16)
        v_scr[2 * c + 1, w:w + tq, :] = jnp.where(left, vr, vp).astype(BF16)

    qg = _dot((yn * normb_ref[...]).astype(BF16), win_ref[...])
    q_pairs = [_pair_rms(qg[:, c * 2 * HEAD_DIM:(c + 1) * 2 * HEAD_DIM], qnorm2_ref[...], left)
               for c in range(N_Q_HEADS // 2)]

    qi = lax.broadcasted_iota(jnp.int32, (w, 2 * w), 0)
    kj = lax.broadcasted_iota(jnp.int32, (w, 2 * w), 1)
    dist = qi - kj + w
    band = (dist >= 0) & (dist <= w)
    j_rel = (lax.broadcasted_iota(jnp.int32, (1, 2 * w), 1) - w).astype(F32)
    i_col = lax.broadcasted_iota(jnp.int32, (w, 1), 0).astype(F32)

    def scores(blk, hk):
        q_rows = slice(blk * w, (blk + 1) * w)
        lhs = []
        for c in (2 * hk, 2 * hk + 1):
            qp = q_pairs[c][q_rows, :]
            lhs.append(jnp.where(left, qp, 0.0))
            lhs.append(jnp.where(left, 0.0, qp))
        return _dot_nt(jnp.concatenate(lhs, axis=0).astype(BF16),
                       k_scr[hk, blk * w:blk * w + 2 * w, :])

    def attend(blk, hk, s4):
        q_rows = slice(blk * w, (blk + 1) * w)
        valid = band & ((step > 0) | (kj >= w)) if blk == 0 else band
        ps, inv = [], []
        for g in range(Q_GROUP):
            hq = hk * Q_GROUP + g
            slope = 2.0 ** (-8.0 * (hq + 1) / N_Q_HEADS)
            a = jnp.where(valid, s4[g * w:(g + 1) * w, :] + slope * j_rel, -jnp.inf)
            sink = sinks_ref[0:1, hq:hq + 1] + slope * i_col
            mx = jnp.maximum(jnp.max(a, axis=-1, keepdims=True), sink)
            p = jnp.exp(a - mx)
            inv.append(1.0 / (jnp.sum(p, axis=-1, keepdims=True) + jnp.exp(sink - mx)))
            ps.append(p.astype(BF16))
        o4 = _dot(jnp.concatenate(ps, axis=0),
                  v_scr[hk, blk * w:blk * w + 2 * w, :])
        for j, c in enumerate((2 * hk, 2 * hk + 1)):
            o_even = o4[(2 * j) * w:(2 * j + 1) * w, :] * inv[2 * j]
            o_odd = o4[(2 * j + 1) * w:(2 * j + 2) * w, :] * inv[2 * j + 1]
            o_scr[q_rows, c * 2 * HEAD_DIM:(c + 1) * 2 * HEAD_DIM] = jnp.where(left, o_even, o_odd)

    items = [(blk, hk) for blk in range(tq // w) for hk in range(N_KV_HEADS)]
    s_next = scores(*items[0])
    for n, item in enumerate(items):
        s_cur = s_next
        if n + 1 < len(items):
            s_next = scores(*items[n + 1])
        attend(*item, s_cur)

    o = (o_scr[...] * _silu(qg[:, ATT_W:])).astype(BF16)
    out_ref[0] = y + _dot(o, wout_ref[...])


def _swa_prompt(y, kvnorm, wkv, knorm, normb, win, qnorm, sinks, wout):
    b, l, d = y.shape
    tq = TQ_B
    w = WINDOW
    f = lambda shape: jax.ShapeDtypeStruct(shape, F32)
    knorm2 = jnp.concatenate([knorm, knorm]).reshape(1, 2 * HEAD_DIM)
    qnorm2 = (jnp.concatenate([qnorm, qnorm]) * (HEAD_DIM ** -0.5)).reshape(1, 2 * HEAD_DIM)
    in_specs = [
        pl.BlockSpec((1, tq, d), lambda i, j: (i, j, 0)),
        _full_spec((1, d)), _full_spec(wkv.shape), _full_spec((1, 2 * HEAD_DIM)), _full_spec((1, d)),
        _full_spec(win.shape), _full_spec((1, 2 * HEAD_DIM)), _full_spec((1, N_Q_HEADS)),
        _full_spec(wout.shape),
    ]
    out_specs = [
        pl.BlockSpec((1, tq, d), lambda i, j: (i, j, 0)),
        pl.BlockSpec((1, w, KV_W), lambda i, j: (i, 0, 0)),
        pl.BlockSpec((1, w, KV_W), lambda i, j: (i, 0, 0)),
    ]
    return pl.pallas_call(
        _swa_prompt_kernel,
        grid=(b, l // tq), in_specs=in_specs, out_specs=out_specs,
        out_shape=[f((b, l, d)), f((b, w, KV_W)), f((b, w, KV_W))],
        scratch_shapes=[pltpu.VMEM((N_KV_HEADS, w + tq, 2 * HEAD_DIM), BF16),
                        pltpu.VMEM((N_KV_HEADS, w + tq, 2 * HEAD_DIM), BF16),
                        pltpu.VMEM((tq, ATT_W), F32)],
        compiler_params=pltpu.CompilerParams(
            dimension_semantics=("arbitrary", "arbitrary"), vmem_limit_bytes=VMEM_LIMIT),
        name="swa_prompt",
    )(y, kvnorm.reshape(1, d), wkv, knorm2, normb.reshape(1, d), win, qnorm2,
      sinks.reshape(1, N_Q_HEADS), wout)


def _swa_sample_front_kernel(y_ref, kvnorm_ref, wkv_ref, knorm_ref, normb_ref, win_ref, qnorm_ref,
                             k_ref, v_ref, q_ref, gate_ref):
    y = y_ref[...]
    yn = y * lax.rsqrt(jnp.mean(y * y, axis=-1, keepdims=True) + EPS)
    kv = _dot((yn * kvnorm_ref[...]).astype(BF16), wkv_ref[...])
    k_ref[...] = jnp.concatenate(_head_rms(kv[:, :KV_W], knorm_ref[...], N_KV_HEADS), axis=1)
    v_ref[...] = kv[:, KV_W:]
    qg = _dot((yn * normb_ref[...]).astype(BF16), win_ref[...])
    q_ref[...] = jnp.concatenate(
        _head_rms(qg[:, :ATT_W], qnorm_ref[...] * (HEAD_DIM ** -0.5), N_Q_HEADS), axis=1)
    gate_ref[...] = qg[:, ATT_W:]


def _swa_sample_attn_kernel(q_ref, kn_ref, vn_ref, kc_ref, vc_ref, sinks_ref, slopes_ref,
                            o_ref, kwin_ref, vwin_ref):
    bb = kc_ref.shape[0]
    w = kc_ref.shape[1]
    hrow = lax.broadcasted_iota(jnp.int32, (N_Q_HEADS, KV_W), 0) // Q_GROUP
    lblk = lax.broadcasted_iota(jnp.int32, (N_Q_HEADS, KV_W), 1) // HEAD_DIM
    own = hrow == lblk
    dist_c = (w - lax.broadcasted_iota(jnp.int32, (1, w), 1)).astype(F32)
    krow = lax.broadcasted_iota(jnp.int32, (w, KV_W), 0)
    slopes = slopes_ref[...]
    sink = sinks_ref[...]
    for b in range(bb):
        q = q_ref[b]
        qm = jnp.where(own, jnp.concatenate([q] * N_KV_HEADS, axis=1), 0.0).astype(BF16)
        kc = kc_ref[b]
        vc = vc_ref[b]
        kn = kn_ref[b]
        vn = vn_ref[b]
        s_c = _dot_nt(qm, kc.astype(BF16)) - slopes * dist_c
        s_n = jnp.sum(qm.astype(F32) * kn.astype(BF16).astype(F32), axis=-1, keepdims=True)
        mx = jnp.maximum(jnp.maximum(jnp.max(s_c, axis=-1, keepdims=True), s_n), sink)
        p_c = jnp.exp(s_c - mx)
        p_n = jnp.exp(s_n - mx)
        den = jnp.sum(p_c, axis=-1, keepdims=True) + p_n + jnp.exp(sink - mx)
        p_c = p_c / den
        p_n = p_n / den
        r = _dot(p_c.astype(BF16), vc.astype(BF16))
        r = r + p_n.astype(BF16).astype(F32) * vn.astype(BF16).astype(F32)
        r = jnp.where(own, r, 0.0)
        acc = r[:, 0:HEAD_DIM]
        for blk in range(1, N_KV_HEADS):
            acc = acc + r[:, blk * HEAD_DIM:(blk + 1) * HEAD_DIM]
        o_ref[b] = acc
        kwin_ref[b] = jnp.where(krow == w - 1, kn, pltpu.roll(kc, w - 1, 0))
        vwin_ref[b] = jnp.where(krow == w - 1, vn, pltpu.roll(vc, w - 1, 0))


def _swa_sample(y, cache_k, cache_v, kvnorm, wkv, knorm, normb, win, qnorm, sinks, wout):
    n, d = y.shape
    w = cache_k.shape[1]
    f = lambda shape: jax.ShapeDtypeStruct(shape, F32)
    kn, vn, q, gate = pl.pallas_call(
        _swa_sample_front_kernel,
        out_shape=[f((n, KV_W)), f((n, KV_W)), f((n, ATT_W)), f((n, ATT_W))],
        compiler_params=pltpu.CompilerParams(vmem_limit_bytes=VMEM_LIMIT),
        name="swa_sample_front",
    )(y, kvnorm.reshape(1, d), wkv, knorm.reshape(1, HEAD_DIM), normb.reshape(1, d), win,
      qnorm.reshape(1, HEAD_DIM))
    bb = 8
    slopes = (2.0 ** (-8.0 * jnp.arange(1, N_Q_HEADS + 1, dtype=F32) / N_Q_HEADS)).reshape(N_Q_HEADS, 1)
    spec3 = lambda a, c: pl.BlockSpec((bb, a, c), lambda i: (i, 0, 0))
    o, kwin, vwin = pl.pallas_call(
        _swa_sample_attn_kernel,
        grid=(n // bb,),
        in_specs=[spec3(N_Q_HEADS, HEAD_DIM), spec3(1, KV_W), spec3(1, KV_W), spec3(w, KV_W),
                  spec3(w, KV_W), _full_spec((N_Q_HEADS, 1)), _full_spec((N_Q_HEADS, 1))],
        out_specs=[spec3(N_Q_HEADS, HEAD_DIM), spec3(w, KV_W), spec3(w, KV_W)],
        out_shape=[f((n, N_Q_HEADS, HEAD_DIM)), f((n, w, KV_W)), f((n, w, KV_W))],
        compiler_params=pltpu.CompilerParams(dimension_semantics=("arbitrary",)),
        name="swa_sample_attn",
    )(q.reshape(n, N_Q_HEADS, HEAD_DIM), kn.reshape(n, 1, KV_W), vn.reshape(n, 1, KV_W),
      cache_k.reshape(n, w, KV_W), cache_v.reshape(n, w, KV_W),
      sinks.reshape(N_Q_HEADS, 1), slopes)
    out = pl.pallas_call(
        functools.partial(_out_proj_kernel, head_w=ATT_W, norm_heads=False),
        out_shape=f((n, d)),
        name="swa_sample_out",
    )(y, o.reshape(n, ATT_W), gate, jnp.ones((1, ATT_W), F32), wout)
    return out, kwin, vwin


def kernel(x_prompt, x_sample, state_conv, state_ssm, cache_k_win, cache_v_win, norm_a, w_in_a, conv_w_a, a_log, dt_bias, o_norm_a, w_out_a, kv_norm, w_kv, k_norm, norm_b, w_in_b, q_norm, sinks, w_out_b):
    n_a = w_in_a.shape[0]
    n_b = w_in_b.shape[0]
    assert n_a == 1 and n_b == 1, "kernel is written for DEPTH == 2"
    bp, lp, d = x_prompt.shape
    n = x_sample.shape[0]
    hw = GDN_HEADS * GDN_D

    hp, hs = x_prompt, x_sample.reshape(n, d)
    conv_p, ssm_p, conv_s, ssm_s = [], [], [], []
    for layer in range(n_a):
        w_in = w_in_a[layer]
        wqkv = w_in[:, :QKV_W].astype(BF16)
        wg = w_in[:, QKV_W:QKV_W + hw].astype(BF16)
        wab = w_in[:, QKV_W + hw:].astype(BF16)
        wabt = wab.T
        wout = w_out_a[layer].astype(BF16)
        hp, cbuf, st = _gdn_prompt(hp, norm_a[layer], wqkv, wg, wab, wabt, conv_w_a[layer],
                                   a_log[layer], dt_bias[layer], o_norm_a[layer], wout)
        conv_p.append(cbuf)
        ssm_p.append(st)
        hs, cbuf, st = _gdn_sample(hs, state_conv[layer], state_ssm[layer], norm_a[layer], wqkv, wg,
                                   wab, conv_w_a[layer], a_log[layer], dt_bias[layer],
                                   o_norm_a[layer], wout)
        conv_s.append(cbuf)
        ssm_s.append(st)

    wkv = w_kv.astype(BF16)
    win = w_in_b[0].astype(BF16)
    woutb = w_out_b[0].astype(BF16)
    hp, k_win_p, v_win_p = _swa_prompt(hp, kv_norm, wkv, k_norm, norm_b[0], win, q_norm[0], sinks[0], woutb)
    hs, k_win_s, v_win_s = _swa_sample(hs, cache_k_win, cache_v_win, kv_norm, wkv, k_norm, norm_b[0],
                                       win, q_norm[0], sinks[0], woutb)
    kv_shape = (N_KV_HEADS, HEAD_DIM)
    return (hp, hs.reshape(n, 1, d), jnp.stack(conv_p), jnp.stack(ssm_p),
            k_win_p.reshape(bp, WINDOW, *kv_shape), v_win_p.reshape(bp, WINDOW, *kv_shape),
            jnp.stack(conv_s), jnp.stack(ssm_s),
            k_win_s.reshape(n, WINDOW, *kv_shape), v_win_s.reshape(n, WINDOW, *kv_shape))
```

```python
import functools

import jax
import jax.numpy as jnp
from jax import lax
from jax.experimental import pallas as pl
from jax.experimental.pallas import tpu as pltpu

F32 = jnp.float32
BF16 = jnp.bfloat16
EPS = 1e-6

D_MODEL = 1024
GDN_HEADS = 8
GDN_D = 128
QKV_W = 3 * GDN_HEADS * GDN_D
CONV_W = 4
CHUNK = 64
N_Q_HEADS = 16
N_KV_HEADS = 4
Q_GROUP = N_Q_HEADS // N_KV_HEADS
HEAD_DIM = 64
KV_W = N_KV_HEADS * HEAD_DIM
ATT_W = N_Q_HEADS * HEAD_DIM
WINDOW = 128

TM_A = 256
TQ_B = 256
HEADS_PER_ITER = 8
CARRY = 8
VMEM_LIMIT = 52 * 1024 * 1024

_NT = (((1,), (1,)), ((), ()))
_TN = (((0,), (0,)), ((), ()))


def _dot(a, b):
    return jnp.dot(a, b, preferred_element_type=F32)


def _dot_nt(a, b):
    return lax.dot_general(a, b, _NT, preferred_element_type=F32)


def _dot_tn(a, b):
    return lax.dot_general(a, b, _TN, preferred_element_type=F32)


def _split(x):
    hi = x.astype(BF16)
    lo = (x - hi.astype(F32)).astype(BF16)
    return hi, lo


def _dot3_parts(a_parts, b_parts):
    ah, al = a_parts
    bh, bl = b_parts
    return (_dot(al, bh) + _dot(ah, bl)) + _dot(ah, bh)


def _dot3(a, b):
    return _dot3_parts(_split(a), _split(b))


def _dot_exact_lhs(a_bf, b):
    b0 = b.astype(BF16)
    r1 = b - b0.astype(F32)
    b1 = r1.astype(BF16)
    b2 = (r1 - b1.astype(F32)).astype(BF16)
    return (_dot(a_bf, b2) + _dot(a_bf, b1)) + _dot(a_bf, b0)


def _dot_exact_rhs(a, b_bf):
    a0 = a.astype(BF16)
    r1 = a - a0.astype(F32)
    a1 = r1.astype(BF16)
    a2 = (r1 - a1.astype(F32)).astype(BF16)
    return (_dot(a2, b_bf) + _dot(a1, b_bf)) + _dot(a0, b_bf)


def _silu(x):
    return x * jax.nn.sigmoid(x)


def _softplus(x):
    return jnp.maximum(x, 0.0) + jnp.log1p(jnp.exp(-jnp.abs(x)))


def _split_dup(x2, left):
    hi = x2.astype(BF16)
    lo = (x2 - hi.astype(F32)).astype(BF16)
    lhs = jnp.where(left, hi, lo)
    rhs = jnp.concatenate([jnp.concatenate([hi, lo], axis=1),
                           jnp.concatenate([hi, jnp.zeros_like(hi)], axis=1)], axis=0)
    return lhs, rhs


def _dot_split(a_lhs, b_rhs):
    y = _dot(a_lhs, b_rhs)
    half = y.shape[1] // 2
    return y[:, :half] + y[:, half:]


def _unit_lower_inverses(ms2, left):
    n = ms2[0].shape[0]
    row = lax.broadcasted_iota(jnp.int32, (n, 2 * n), 0)
    col = lax.broadcasted_iota(jnp.int32, (n, 2 * n), 1) % n
    eye2 = jnp.where(row == col, 1.0, 0.0).astype(F32)
    zeros = jnp.zeros((n, 2 * n), BF16)

    def times(a2, b2):
        return _dot(a2.astype(BF16), jnp.concatenate([b2.astype(BF16), zeros], axis=0))

    ts = [eye2 - m for m in ms2]
    ps = list(ms2)
    steps = max(1, (n - 1).bit_length()) - 1
    for _ in range(steps):
        ps = [times(p, p) for p in ps]
        ts = [t + times(t, p) for t, p in zip(ts, ps)]
    rs = [(eye2 - t) - _dot_split(_split_dup(m, left)[0], _split_dup(t, left)[1]) for m, t in zip(ms2, ts)]
    return [t + times(t, r) for t, r in zip(ts, rs)]


def _full_spec(shape):
    nd = len(shape)
    return pl.BlockSpec(shape, lambda *_: (0,) * nd)


def _gdn_gates(ab, ab_t, alog_r, dtb_r, alog_c, dtb_c, tm, chunk):
    h = GDN_HEADS
    g_c = -jnp.exp(alog_r) * _softplus(ab[:, :h] + dtb_r)
    beta = jax.nn.sigmoid(ab[:, h:])
    g_r = -jnp.exp(alog_c) * _softplus(ab_t[:h, :] + dtb_c)
    row = lax.broadcasted_iota(jnp.int32, (tm, tm), 0)
    col = lax.broadcasted_iota(jnp.int32, (tm, tm), 1)
    same = (row // chunk) == (col // chunk)
    lower = jnp.where(same & (row >= col), 1.0, 0.0).astype(BF16)
    upper = jnp.where(same & (row <= col), 1.0, 0.0).astype(BF16)
    gc = _dot_exact_lhs(lower, g_c)
    gr = _dot_exact_rhs(g_r, upper)
    return beta, gc, gr


def _gdn_prompt_kernel(x_ref, norm_ref, wqkv_ref, wg_ref, wab_ref, wabt_ref, convw_ref,
                       alog_r_ref, dtb_r_ref, alog_c_ref, dtb_c_ref, onorm_ref, wout_ref,
                       y_ref, conv_ref, ssm_ref,
                       pre_scr, gate_scr, s_scr, k_scr, kb_scr, q_scr, qd_scr, kd_scr,
                       rhs_scr, dec_scr, gt_scr, o_scr, lhs_scr, su_scr):
    tm = x_ref.shape[1]
    n_chunks = tm // CHUNK
    l = pl.program_id(1)

    @pl.when(l == 0)
    def _():
        pre_scr[0:CARRY, :] = jnp.zeros((CARRY, QKV_W), F32)
        s_scr[...] = jnp.zeros(s_scr.shape, F32)

    x = x_ref[0]
    xn = x * lax.rsqrt(jnp.mean(x * x, axis=-1, keepdims=True) + EPS) * norm_ref[...]
    xb = xn.astype(BF16)
    ab = _dot(xb, wab_ref[...])
    ab_t = _dot_nt(wabt_ref[...], xb)
    beta, gc, gr = _gdn_gates(ab, ab_t, alog_r_ref[...], dtb_r_ref[...],
                              alog_c_ref[...], dtb_c_ref[...], tm, CHUNK)
    e_g = jnp.exp(gc)
    g_last = jnp.concatenate(
        [jnp.broadcast_to(gc[c * CHUNK + CHUNK - 1:c * CHUNK + CHUNK, :], (CHUNK, GDN_HEADS))
         for c in range(n_chunks)], axis=0)
    e_kd = jnp.exp(g_last - gc)
    e_tot = jnp.exp(g_last)

    ci = lax.broadcasted_iota(jnp.int32, (CHUNK, 2 * CHUNK), 0)
    cj = lax.broadcasted_iota(jnp.int32, (CHUNK, 2 * CHUNK), 1) % CHUNK
    causal = ci >= cj
    strict = ci > cj
    left = lax.broadcasted_iota(jnp.int32, (1, 2 * CHUNK), 1) < CHUNK

    def conv_act(j):
        cols = slice(j * GDN_D, (j + 1) * GDN_D)
        half = pre_scr[CARRY - 3:CARRY - 3 + tm, cols] * (0.5 * convw_ref[0:1, cols])
        for i in range(1, CONV_W):
            half = half + pre_scr[CARRY - 3 + i:CARRY - 3 + i + tm, cols] * (0.5 * convw_ref[i:i + 1, cols])
        return half + half * jnp.tanh(half)

    def head_prep(h):
        qh = conv_act(h)
        kh = conv_act(GDN_HEADS + h)
        vh = conv_act(2 * GDN_HEADS + h)
        qn = qh * lax.rsqrt(jnp.sum(qh * qh, axis=-1, keepdims=True) + EPS) * (GDN_D ** -0.5)
        kn = kh * lax.rsqrt(jnp.sum(kh * kh, axis=-1, keepdims=True) + EPS)
        b_col = beta[:, h:h + 1]
        eg_col = e_g[:, h:h + 1]
        kb = kn * b_col
        k_scr[h] = kn.astype(BF16)
        kb_scr[h] = kb.astype(BF16)
        q_scr[h] = qn.astype(BF16)
        qd_scr[h] = qn * eg_col
        kd_scr[h] = (kn * e_kd[:, h:h + 1]).astype(BF16)
        rhs_scr[h, :, 0:GDN_D] = (vh * b_col).astype(BF16)
        rhs_scr[h, :, GDN_D:2 * GDN_D] = (kb * eg_col).astype(BF16)
        for c in range(n_chunks):
            rows = slice(c * CHUNK, (c + 1) * CHUNK)
            g_row = gr[h:h + 1, rows]
            diff = gc[rows, h:h + 1] - jnp.concatenate([g_row, g_row], axis=1)
            dec_scr[h, rows, :] = jnp.exp(jnp.where(causal, diff, -jnp.inf))
            gt_scr[h, c:c + 1, :] = jnp.broadcast_to(e_tot[c * CHUNK:c * CHUNK + 1, h:h + 1], (1, GDN_D))

    def chunk_terms(heads):
        items = [(h, c) for h in heads for c in range(n_chunks)]
        rows = [slice(c * CHUNK, (c + 1) * CHUNK) for _, c in items]
        decs = [dec_scr[h, r, :] for (h, _), r in zip(items, rows)]
        aqs = []
        for (h, _), r in zip(items, rows):
            k = k_scr[h, r, :]
            aqs.append(_dot_nt(jnp.concatenate([kb_scr[h, r, :], q_scr[h, r, :]], axis=0),
                               jnp.concatenate([k, k], axis=0)))
        ms = [jnp.where(strict, aq[:CHUNK, :] * d, 0.0) for aq, d in zip(aqs, decs)]
        qks = [jnp.where(causal, aq[CHUNK:, :] * d, 0.0)[:, :CHUNK].astype(BF16) for aq, d in zip(aqs, decs)]
        ts = _unit_lower_inverses(ms, left)
        sols = []
        for (h, _), r, t in zip(items, rows, ts):
            rhs = rhs_scr[h, r, :]
            sols.append(_dot(_split_dup(t, left)[0], jnp.concatenate([rhs, rhs], axis=0)).astype(BF16))
        kd_uws = [_dot_tn(kd_scr[h, r, :], sol)
                  for (h, _), r, sol in zip(items, rows, sols)]
        qk_uws = [_dot(qk, sol) for qk, sol in zip(qks, sols)]
        for (h, c), r, kd_uw, qk_uw in zip(items, rows, kd_uws, qk_uws):
            lhs_scr[h, c, 0:GDN_D, :] = kd_uw[:, GDN_D:].astype(BF16)
            lhs_scr[h, c, GDN_D:GDN_D + CHUNK, :] = (qd_scr[h, r, :] - qk_uw[:, GDN_D:]).astype(BF16)
            su_scr[h, c] = kd_uw[:, :GDN_D]
            o_scr[h, r, :] = qk_uw[:, :GDN_D]

    def project(heads):
        for part in range(3):
            cols = slice((part * GDN_HEADS + heads[0]) * GDN_D, (part * GDN_HEADS + heads[-1] + 1) * GDN_D)
            pre_scr[CARRY:CARRY + tm, cols] = _dot(xb, wqkv_ref[:, cols])

    groups = [list(range(i, i + HEADS_PER_ITER)) for i in range(0, GDN_HEADS, HEADS_PER_ITER)]
    n_groups = len(groups)
    for n in range(-2, n_groups):
        if 0 <= n + 2 < n_groups:
            project(groups[n + 2])
        if 0 <= n + 1 < n_groups:
            for h in groups[n + 1]:
                head_prep(h)
        if n == n_groups - 1:
            gate_scr[...] = _dot(xb, wg_ref[...])
        if n >= 0:
            chunk_terms(groups[n])

    conv_ref[0] = pre_scr[tm + CARRY - 3:tm + CARRY, :]
    pre_scr[0:CARRY, :] = pre_scr[tm:tm + CARRY, :]

    for c in range(n_chunks):
        rows = slice(c * CHUNK, (c + 1) * CHUNK)
        for h in range(GDN_HEADS):
            s = s_scr[h]
            prod = _dot(lhs_scr[h, c], s.astype(BF16))
            o_scr[h, rows, :] = o_scr[h, rows, :] + prod[GDN_D:, :]
            s_scr[h] = s * gt_scr[h, c:c + 1, :] + (su_scr[h, c] - prod[:GDN_D, :])

    @pl.when(l == pl.num_programs(1) - 1)
    def _():
        ssm_ref[0] = s_scr[...]

    pieces = []
    for h in range(GDN_HEADS):
        o = o_scr[h]
        on = o * lax.rsqrt(jnp.mean(o * o, axis=-1, keepdims=True) + EPS) * onorm_ref[...]
        pieces.append((on * _silu(gate_scr[:, h * GDN_D:(h + 1) * GDN_D])).astype(BF16))
    y_ref[0] = x + _dot(jnp.concatenate(pieces, axis=1), wout_ref[...])


def _gdn_prompt(x, norm, wqkv, wg, wab, wabt, convw, alog, dtb, onorm, wout):
    b, l, d = x.shape
    tm = TM_A
    h = GDN_HEADS
    grid = (b, l // tm)
    in_specs = [
        pl.BlockSpec((1, tm, d), lambda i, j: (i, j, 0)),
        _full_spec((1, d)), _full_spec(wqkv.shape), _full_spec(wg.shape), _full_spec(wab.shape),
        _full_spec(wabt.shape), _full_spec(convw.shape),
        _full_spec((1, h)), _full_spec((1, h)), _full_spec((h, 1)), _full_spec((h, 1)),
        _full_spec((1, GDN_D)), _full_spec(wout.shape),
    ]
    out_specs = [
        pl.BlockSpec((1, tm, d), lambda i, j: (i, j, 0)),
        pl.BlockSpec((1, CONV_W - 1, QKV_W), lambda i, j: (i, 0, 0)),
        pl.BlockSpec((1, h, GDN_D, GDN_D), lambda i, j: (i, 0, 0, 0)),
    ]
    out_shape = [
        jax.ShapeDtypeStruct((b, l, d), F32),
        jax.ShapeDtypeStruct((b, CONV_W - 1, QKV_W), F32),
        jax.ShapeDtypeStruct((b, h, GDN_D, GDN_D), F32),
    ]
    scratch = [
        pltpu.VMEM((tm + CARRY, QKV_W), F32),
        pltpu.VMEM((tm, h * GDN_D), F32),
        pltpu.VMEM((h, GDN_D, GDN_D), F32),
        pltpu.VMEM((h, tm, GDN_D), BF16),
        pltpu.VMEM((h, tm, GDN_D), BF16),
        pltpu.VMEM((h, tm, GDN_D), BF16),
        pltpu.VMEM((h, tm, GDN_D), F32),
        pltpu.VMEM((h, tm, GDN_D), BF16),
        pltpu.VMEM((h, tm, 2 * GDN_D), BF16),
        pltpu.VMEM((h, tm, 2 * CHUNK), F32),
        pltpu.VMEM((h, 8, GDN_D), F32),
        pltpu.VMEM((h, tm, GDN_D), F32),
        pltpu.VMEM((h, tm // CHUNK, GDN_D + CHUNK, GDN_D), BF16),
        pltpu.VMEM((h, tm // CHUNK, GDN_D, GDN_D), F32),
    ]
    return pl.pallas_call(
        _gdn_prompt_kernel,
        grid=grid, in_specs=in_specs, out_specs=out_specs, out_shape=out_shape,
        scratch_shapes=scratch,
        compiler_params=pltpu.CompilerParams(
            dimension_semantics=("arbitrary", "arbitrary"), vmem_limit_bytes=VMEM_LIMIT),
        name="gdn_prompt",
    )(x, norm.reshape(1, d), wqkv, wg, wab, wabt, convw,
      alog.reshape(1, h), dtb.reshape(1, h), alog.reshape(h, 1), dtb.reshape(h, 1),
      onorm.reshape(1, GDN_D), wout)


def _gdn_sample_front_kernel(x_ref, norm_ref, wqkv_ref, wg_ref, wab_ref, convw_ref,
                             c0_ref, c1_ref, c2_ref, alog_r_ref, dtb_r_ref,
                             pre_ref, q_ref, k_ref, u_ref, w_ref, qd_ref, gate_ref, eg_ref):
    x = x_ref[...]
    xn = x * lax.rsqrt(jnp.mean(x * x, axis=-1, keepdims=True) + EPS) * norm_ref[...]
    xb = xn.astype(BF16)
    pre = _dot(xb, wqkv_ref[...])
    pre_ref[...] = pre
    gate_ref[...] = _dot(xb, wg_ref[...])
    ab = _dot(xb, wab_ref[...])
    g = -jnp.exp(alog_r_ref[...]) * _softplus(ab[:, :GDN_HEADS] + dtb_r_ref[...])
    beta = jax.nn.sigmoid(ab[:, GDN_HEADS:])
    e_g = jnp.exp(g)
    eg_ref[...] = e_g
    act = _silu(((c0_ref[...] * convw_ref[0:1, :] + c1_ref[...] * convw_ref[1:2, :])
                 + c2_ref[...] * convw_ref[2:3, :]) + pre * convw_ref[3:4, :])
    w_all = GDN_HEADS * GDN_D
    for h in range(GDN_HEADS):
        cols = slice(h * GDN_D, (h + 1) * GDN_D)
        qh = act[:, h * GDN_D:(h + 1) * GDN_D]
        kh = act[:, w_all + h * GDN_D:w_all + (h + 1) * GDN_D]
        vh = act[:, 2 * w_all + h * GDN_D:2 * w_all + (h + 1) * GDN_D]
        qn = qh * lax.rsqrt(jnp.sum(qh * qh, axis=-1, keepdims=True) + EPS) * (GDN_D ** -0.5)
        kn = kh * lax.rsqrt(jnp.sum(kh * kh, axis=-1, keepdims=True) + EPS)
        b_col = beta[:, h:h + 1]
        eg_col = e_g[:, h:h + 1]
        q_ref[:, cols] = qn
        k_ref[:, cols] = kn
        u_ref[:, cols] = vh * b_col
        w_ref[:, cols] = kn * b_col * eg_col
        qd_ref[:, cols] = qn * eg_col


def _gdn_sample_state_kernel(q_ref, k_ref, u_ref, w_ref, qd_ref, eg_ref, s_ref, o_ref, s_out_ref):
    bb = s_ref.shape[0]
    row = lax.broadcasted_iota(jnp.int32, (8, GDN_D), 0)
    for b in range(bb):
        for h in range(GDN_HEADS):
            cols = slice(h * GDN_D, (h + 1) * GDN_D)
            s = s_ref[b, h]
            sb = s.astype(BF16)
            q = q_ref[b:b + 1, cols].astype(BF16).astype(F32)
            k = k_ref[b:b + 1, cols].astype(BF16).astype(F32)
            lhs = jnp.where(row == 0, w_ref[b:b + 1, cols],
                            jnp.where(row == 1, qd_ref[b:b + 1, cols], 0.0))
            prod = _dot(lhs.astype(BF16), sb)
            v_new = u_ref[b:b + 1, cols] - prod[0:1, :]
            vb = v_new.astype(BF16).astype(F32)
            qk = jnp.sum(q * k, axis=-1, keepdims=True)
            o_ref[b:b + 1, cols] = prod[1:2, :] + qk.astype(BF16).astype(F32) * vb
            k8 = jnp.where(row == 0, k, 0.0).astype(BF16)
            v8 = jnp.where(row == 0, vb, 0.0).astype(BF16)
            s_out_ref[b, h] = s * eg_ref[b:b + 1, h:h + 1] + _dot_tn(k8, v8)


def _out_proj_kernel(x_ref, o_ref, gate_ref, onorm_ref, wout_ref, y_ref, *, head_w, norm_heads):
    pieces = []
    for h in range(x_ref.shape[1] // head_w):
        cols = slice(h * head_w, (h + 1) * head_w)
        o = o_ref[:, cols]
        if norm_heads:
            o = o * lax.rsqrt(jnp.mean(o * o, axis=-1, keepdims=True) + EPS) * onorm_ref[...]
        pieces.append((o * _silu(gate_ref[:, cols])).astype(BF16))
    y_ref[...] = x_ref[...] + _dot(jnp.concatenate(pieces, axis=1), wout_ref[...])


def _gdn_sample(x, conv_state, ssm_state, norm, wqkv, wg, wab, convw, alog, dtb, onorm, wout):
    n, d = x.shape
    h = GDN_HEADS
    hw = h * GDN_D
    conv_t = jnp.transpose(conv_state, (1, 0, 2))
    f = lambda shape: jax.ShapeDtypeStruct(shape, F32)
    pre, q, k, u, w, qd, gate, eg = pl.pallas_call(
        _gdn_sample_front_kernel,
        out_shape=[f((n, QKV_W)), f((n, hw)), f((n, hw)), f((n, hw)), f((n, hw)), f((n, hw)),
                   f((n, hw)), f((n, h))],
        compiler_params=pltpu.CompilerParams(vmem_limit_bytes=VMEM_LIMIT),
        name="gdn_sample_front",
    )(x, norm.reshape(1, d), wqkv, wg, wab, convw, conv_t[0], conv_t[1], conv_t[2],
      alog.reshape(1, h), dtb.reshape(1, h))
    bb = 8
    row_spec = lambda width: pl.BlockSpec((bb, width), lambda i: (i, 0))
    st_spec = pl.BlockSpec((bb, h, GDN_D, GDN_D), lambda i: (i, 0, 0, 0))
    o, s_new = pl.pallas_call(
        _gdn_sample_state_kernel,
        grid=(n // bb,),
        in_specs=[row_spec(hw)] * 5 + [row_spec(h), st_spec],
        out_specs=[row_spec(hw), st_spec],
        out_shape=[f((n, hw)), f(ssm_state.shape)],
        compiler_params=pltpu.CompilerParams(dimension_semantics=("arbitrary",)),
        name="gdn_sample_state",
    )(q, k, u, w, qd, eg, ssm_state)
    y = pl.pallas_call(
        functools.partial(_out_proj_kernel, head_w=GDN_D, norm_heads=True),
        out_shape=f((n, d)),
        name="gdn_sample_out",
    )(x, o, gate, onorm.reshape(1, GDN_D), wout)
    conv_new = jnp.stack([conv_t[1], conv_t[2], pre], axis=1)
    return y, conv_new, s_new


def _head_rms(x, gain_row, n_heads):
    pieces = []
    for h in range(n_heads):
        xh = x[:, h * HEAD_DIM:(h + 1) * HEAD_DIM]
        pieces.append(xh * lax.rsqrt(jnp.mean(xh * xh, axis=-1, keepdims=True) + EPS) * gain_row)
    return pieces


def _pair_rms(x, gain2, left):
    sq = x * x
    lo = jnp.sum(jnp.where(left, sq, 0.0), axis=-1, keepdims=True)
    hi = jnp.sum(jnp.where(left, 0.0, sq), axis=-1, keepdims=True)
    ms = jnp.where(left, lo, hi) * (1.0 / HEAD_DIM)
    return x * lax.rsqrt(ms + EPS) * gain2


def _swa_prompt_kernel(y_ref, kvnorm_ref, wkv_ref, knorm2_ref, normb_ref, win_ref, qnorm2_ref,
                       sinks_ref, wout_ref, out_ref, kwin_ref, vwin_ref, k_scr, v_scr, o_scr):
    tq = y_ref.shape[1]
    w = WINDOW
    step = pl.program_id(1)
    lane = lax.broadcasted_iota(jnp.int32, (1, 2 * HEAD_DIM), 1)
    left = lane < HEAD_DIM

    @pl.when(step == 0)
    def _():
        k_scr[:, 0:w, :] = jnp.zeros((N_KV_HEADS, w, 2 * HEAD_DIM), BF16)
        v_scr[:, 0:w, :] = jnp.zeros((N_KV_HEADS, w, 2 * HEAD_DIM), BF16)

    @pl.when(step > 0)
    def _():
        k_scr[:, 0:w, :] = k_scr[:, tq:tq + w, :]
        v_scr[:, 0:w, :] = v_scr[:, tq:tq + w, :]

    y = y_ref[0]
    yn = y * lax.rsqrt(jnp.mean(y * y, axis=-1, keepdims=True) + EPS)
    kv = _dot((yn * kvnorm_ref[...]).astype(BF16), wkv_ref[...])
    for c in range(N_KV_HEADS // 2):
        cols = slice(c * 2 * HEAD_DIM, (c + 1) * 2 * HEAD_DIM)
        kp = _pair_rms(kv[:, cols], knorm2_ref[...], left)
        vp = kv[:, KV_W + c * 2 * HEAD_DIM:KV_W + (c + 1) * 2 * HEAD_DIM]
        kwin_ref[0, :, cols] = kp[tq - w:, :]
        vwin_ref[0, :, cols] = vp[tq - w:, :]
        kr = pltpu.roll(kp, HEAD_DIM, 1)
        vr = pltpu.roll(vp, HEAD_DIM, 1)
        k_scr[2 * c, w:w + tq, :] = jnp.where(left, kp, kr).astype(BF16)
        k_scr[2 * c + 1, w:w + tq, :] = jnp.where(left, kr, kp).astype(BF16)
        v_scr[2 * c, w:w + tq, :] = jnp.where(left, vp, vr).astype(BF16)
        v_scr[2 * c + 1, w:w + tq, :] = jnp.where(left, vr, vp).astype(BF16)

    qg = _dot((yn * normb_ref[...]).astype(BF16), win_ref[...])
    q_pairs = [_pair_rms(qg[:, c * 2 * HEAD_DIM:(c + 1) * 2 * HEAD_DIM], qnorm2_ref[...], left)
               for c in range(N_Q_HEADS // 2)]

    kj = lax.broadcasted_iota(jnp.int32, (2 * w, w), 0)
    qi = lax.broadcasted_iota(jnp.int32, (2 * w, w), 1)
    dist = qi - kj + w
    band = (dist >= 0) & (dist <= w)
    j_rel = (kj - w).astype(F32)
    i_row = lax.broadcasted_iota(jnp.int32, (1, w), 1).astype(F32)

    def scores(blk, hk):
        q_rows = slice(blk * w, (blk + 1) * w)
        lhs = []
        for c in (2 * hk, 2 * hk + 1):
            qp = q_pairs[c][q_rows, :]
            lhs.append(jnp.where(left, qp, 0.0))
            lhs.append(jnp.where(left, 0.0, qp))
        return _dot_nt(k_scr[hk, blk * w:blk * w + 2 * w, :],
                       jnp.concatenate(lhs, axis=0).astype(BF16))

    def attend(blk, hk, s4):
        q_rows = slice(blk * w, (blk + 1) * w)
        valid = band & ((step > 0) | (kj >= w)) if blk == 0 else band
        ps = []
        for g in range(Q_GROUP):
            hq = hk * Q_GROUP + g
            slope = 2.0 ** (-8.0 * (hq + 1) / N_Q_HEADS)
            a = jnp.where(valid, s4[:, g * w:(g + 1) * w] + slope * j_rel, -jnp.inf)
            sink = sinks_ref[0:1, hq:hq + 1] + slope * i_row
            mx = jnp.maximum(jnp.max(a, axis=0, keepdims=True), sink)
            p = jnp.exp(a - mx)
            inv = 1.0 / (jnp.sum(p, axis=0, keepdims=True) + jnp.exp(sink - mx))
            ps.append((p * inv).astype(BF16))
        o4 = _dot_tn(jnp.concatenate(ps, axis=1),
                     v_scr[hk, blk * w:blk * w + 2 * w, :])
        for j, c in enumerate((2 * hk, 2 * hk + 1)):
            o_scr[q_rows, c * 2 * HEAD_DIM:(c + 1) * 2 * HEAD_DIM] = jnp.where(
                left, o4[(2 * j) * w:(2 * j + 1) * w, :], o4[(2 * j + 1) * w:(2 * j + 2) * w, :])

    items = [(blk, hk) for blk in range(tq // w) for hk in range(N_KV_HEADS)]
    s_next = scores(*items[0])
    for n, item in enumerate(items):
        s_cur = s_next
        if n + 1 < len(items):
            s_next = scores(*items[n + 1])
        attend(*item, s_cur)

    o = (o_scr[...] * _silu(qg[:, ATT_W:])).astype(BF16)
    out_ref[0] = y + _dot(o, wout_ref[...])


def _swa_prompt(y, kvnorm, wkv, knorm, normb, win, qnorm, sinks, wout):
    b, l, d = y.shape
    tq = TQ_B
    w = WINDOW
    f = lambda shape: jax.ShapeDtypeStruct(shape, F32)
    knorm2 = jnp.concatenate([knorm, knorm]).reshape(1, 2 * HEAD_DIM)
    qnorm2 = (jnp.concatenate([qnorm, qnorm]) * (HEAD_DIM ** -0.5)).reshape(1, 2 * HEAD_DIM)
    in_specs = [
        pl.BlockSpec((1, tq, d), lambda i, j: (i, j, 0)),
        _full_spec((1, d)), _full_spec(wkv.shape), _full_spec((1, 2 * HEAD_DIM)), _full_spec((1, d)),
        _full_spec(win.shape), _full_spec((1, 2 * HEAD_DIM)), _full_spec((1, N_Q_HEADS)),
        _full_spec(wout.shape),
    ]
    out_specs = [
        pl.BlockSpec((1, tq, d), lambda i, j: (i, j, 0)),
        pl.BlockSpec((1, w, KV_W), lambda i, j: (i, 0, 0)),
        pl.BlockSpec((1, w, KV_W), lambda i, j: (i, 0, 0)),
    ]
    return pl.pallas_call(
        _swa_prompt_kernel,
        grid=(b, l // tq), in_specs=in_specs, out_specs=out_specs,
        out_shape=[f((b, l, d)), f((b, w, KV_W)), f((b, w, KV_W))],
        scratch_shapes=[pltpu.VMEM((N_KV_HEADS, w + tq, 2 * HEAD_DIM), BF16),
                        pltpu.VMEM((N_KV_HEADS, w + tq, 2 * HEAD_DIM), BF16),
                        pltpu.VMEM((tq, ATT_W), F32)],
        compiler_params=pltpu.CompilerParams(
            dimension_semantics=("arbitrary", "arbitrary"), vmem_limit_bytes=VMEM_LIMIT),
        name="swa_prompt",
    )(y, kvnorm.reshape(1, d), wkv, knorm2, normb.reshape(1, d), win, qnorm2,
      sinks.reshape(1, N_Q_HEADS), wout)


def _swa_sample_front_kernel(y_ref, kvnorm_ref, wkv_ref, knorm_ref, normb_ref, win_ref, qnorm_ref,
                             k_ref, v_ref, q_ref, gate_ref):
    y = y_ref[...]
    yn = y * lax.rsqrt(jnp.mean(y * y, axis=-1, keepdims=True) + EPS)
    kv = _dot((yn * kvnorm_ref[...]).astype(BF16), wkv_ref[...])
    k_ref[...] = jnp.concatenate(_head_rms(kv[:, :KV_W], knorm_ref[...], N_KV_HEADS), axis=1)
    v_ref[...] = kv[:, KV_W:]
    qg = _dot((yn * normb_ref[...]).astype(BF16), win_ref[...])
    q_ref[...] = jnp.concatenate(
        _head_rms(qg[:, :ATT_W], qnorm_ref[...] * (HEAD_DIM ** -0.5), N_Q_HEADS), axis=1)
    gate_ref[...] = qg[:, ATT_W:]


def _swa_sample_attn_kernel(q_ref, kn_ref, vn_ref, kc_ref, vc_ref, sinks_ref, slopes_ref,
                            o_ref, kwin_ref, vwin_ref):
    bb = kc_ref.shape[0]
    w = kc_ref.shape[1]
    hrow = lax.broadcasted_iota(jnp.int32, (N_Q_HEADS, KV_W), 0) // Q_GROUP
    lblk = lax.broadcasted_iota(jnp.int32, (N_Q_HEADS, KV_W), 1) // HEAD_DIM
    own = hrow == lblk
    dist_c = (w - lax.broadcasted_iota(jnp.int32, (1, w), 1)).astype(F32)
    krow = lax.broadcasted_iota(jnp.int32, (w, KV_W), 0)
    slopes = slopes_ref[...]
    sink = sinks_ref[...]
    for b in range(bb):
        q = q_ref[b]
        qm = jnp.where(own, jnp.concatenate([q] * N_KV_HEADS, axis=1), 0.0).astype(BF16)
        kc = kc_ref[b]
        vc = vc_ref[b]
        kn = kn_ref[b]
        vn = vn_ref[b]
        s_c = _dot_nt(qm, kc.astype(BF16)) - slopes * dist_c
        s_n = jnp.sum(qm.astype(F32) * kn.astype(BF16).astype(F32), axis=-1, keepdims=True)
        mx = jnp.maximum(jnp.maximum(jnp.max(s_c, axis=-1, keepdims=True), s_n), sink)
        p_c = jnp.exp(s_c - mx)
        p_n = jnp.exp(s_n - mx)
        den = jnp.sum(p_c, axis=-1, keepdims=True) + p_n + jnp.exp(sink - mx)
        p_c = p_c / den
        p_n = p_n / den
        r = _dot(p_c.astype(BF16), vc.astype(BF16))
        r = r + p_n.astype(BF16).astype(F32) * vn.astype(BF16).astype(F32)
        r = jnp.where(own, r, 0.0)
        acc = r[:, 0:HEAD_DIM]
        for blk in range(1, N_KV_HEADS):
            acc = acc + r[:, blk * HEAD_DIM:(blk + 1) * HEAD_DIM]
        o_ref[b] = acc
        kwin_ref[b] = jnp.where(krow == w - 1, kn, pltpu.roll(kc, w - 1, 0))
        vwin_ref[b] = jnp.where(krow == w - 1, vn, pltpu.roll(vc, w - 1, 0))


def _swa_sample(y, cache_k, cache_v, kvnorm, wkv, knorm, normb, win, qnorm, sinks, wout):
    n, d = y.shape
    w = cache_k.shape[1]
    f = lambda shape: jax.ShapeDtypeStruct(shape, F32)
    kn, vn, q, gate = pl.pallas_call(
        _swa_sample_front_kernel,
        out_shape=[f((n, KV_W)), f((n, KV_W)), f((n, ATT_W)), f((n, ATT_W))],
        compiler_params=pltpu.CompilerParams(vmem_limit_bytes=VMEM_LIMIT),
        name="swa_sample_front",
    )(y, kvnorm.reshape(1, d), wkv, knorm.reshape(1, HEAD_DIM), normb.reshape(1, d), win,
      qnorm.reshape(1, HEAD_DIM))
    bb = 8
    slopes = (2.0 ** (-8.0 * jnp.arange(1, N_Q_HEADS + 1, dtype=F32) / N_Q_HEADS)).reshape(N_Q_HEADS, 1)
    spec3 = lambda a, c: pl.BlockSpec((bb, a, c), lambda i: (i, 0, 0))
    o, kwin, vwin = pl.pallas_call(
        _swa_sample_attn_kernel,
        grid=(n // bb,),
        in_specs=[spec3(N_Q_HEADS, HEAD_DIM), spec3(1, KV_W), spec3(1, KV_W), spec3(w, KV_W),
                  spec3(w, KV_W), _full_spec((N_Q_HEADS, 1)), _full_spec((N_Q_HEADS, 1))],
        out_specs=[spec3(N_Q_HEADS, HEAD_DIM), spec3(w, KV_W), spec3(w, KV_W)],
        out_shape=[f((n, N_Q_HEADS, HEAD_DIM)), f((n, w, KV_W)), f((n, w, KV_W))],
        compiler_params=pltpu.CompilerParams(dimension_semantics=("arbitrary",)),
        name="swa_sample_attn",
    )(q.reshape(n, N_Q_HEADS, HEAD_DIM), kn.reshape(n, 1, KV_W), vn.reshape(n, 1, KV_W),
      cache_k.reshape(n, w, KV_W), cache_v.reshape(n, w, KV_W),
      sinks.reshape(N_Q_HEADS, 1), slopes)
    out = pl.pallas_call(
        functools.partial(_out_proj_kernel, head_w=ATT_W, norm_heads=False),
        out_shape=f((n, d)),
        name="swa_sample_out",
    )(y, o.reshape(n, ATT_W), gate, jnp.ones((1, ATT_W), F32), wout)
    return out, kwin, vwin


def kernel(x_prompt, x_sample, state_conv, state_ssm, cache_k_win, cache_v_win, norm_a, w_in_a, conv_w_a, a_log, dt_bias, o_norm_a, w_out_a, kv_norm, w_kv, k_norm, norm_b, w_in_b, q_norm, sinks, w_out_b):
    n_a = w_in_a.shape[0]
    n_b = w_in_b.shape[0]
    assert n_a == 1 and n_b == 1, "kernel is written for DEPTH == 2"
    bp, lp, d = x_prompt.shape
    n = x_sample.shape[0]
    hw = GDN_HEADS * GDN_D

    hp, hs = x_prompt, x_sample.reshape(n, d)
    conv_p, ssm_p, conv_s, ssm_s = [], [], [], []
    for layer in range(n_a):
        w_in = w_in_a[layer]
        wqkv = w_in[:, :QKV_W].astype(BF16)
        wg = w_in[:, QKV_W:QKV_W + hw].astype(BF16)
        wab = w_in[:, QKV_W + hw:].astype(BF16)
        wabt = wab.T
        wout = w_out_a[layer].astype(BF16)
        hp, cbuf, st = _gdn_prompt(hp, norm_a[layer], wqkv, wg, wab, wabt, conv_w_a[layer],
                                   a_log[layer], dt_bias[layer], o_norm_a[layer], wout)
        conv_p.append(cbuf)
        ssm_p.append(st)
        hs, cbuf, st = _gdn_sample(hs, state_conv[layer], state_ssm[layer], norm_a[layer], wqkv, wg,
                                   wab, conv_w_a[layer], a_log[layer], dt_bias[layer],
                                   o_norm_a[layer], wout)
        conv_s.append(cbuf)
        ssm_s.append(st)

    wkv = w_kv.astype(BF16)
    win = w_in_b[0].astype(BF16)
    woutb = w_out_b[0].astype(BF16)
    hp, k_win_p, v_win_p = _swa_prompt(hp, kv_norm, wkv, k_norm, norm_b[0], win, q_norm[0], sinks[0], woutb)
    hs, k_win_s, v_win_s = _swa_sample(hs, cache_k_win, cache_v_win, kv_norm, wkv, k_norm, norm_b[0],
                                       win, q_norm[0], sinks[0], woutb)
    kv_shape = (N_KV_HEADS, HEAD_DIM)
    return (hp, hs.reshape(n, 1, d), jnp.stack(conv_p), jnp.stack(ssm_p),
            k_win_p.reshape(bp, WINDOW, *kv_shape), v_win_p.reshape(bp, WINDOW, *kv_shape),
            jnp.stack(conv_s), jnp.stack(ssm_s),
            k_win_s.reshape(n, WINDOW, *kv_shape), v_win_s.reshape(n, WINDOW, *kv_shape))
```

```python
import functools

import jax
import jax.numpy as jnp
from jax import lax
from jax.experimental import pallas as pl
from jax.experimental.pallas import tpu as pltpu

F32 = jnp.float32
BF16 = jnp.bfloat16
EPS = 1e-6

D_MODEL = 1024
GDN_HEADS = 8
GDN_D = 128
QKV_W = 3 * GDN_HEADS * GDN_D
CONV_W = 4
CHUNK = 64
N_Q_HEADS = 16
N_KV_HEADS = 4
Q_GROUP = N_Q_HEADS // N_KV_HEADS
HEAD_DIM = 64
KV_W = N_KV_HEADS * HEAD_DIM
ATT_W = N_Q_HEADS * HEAD_DIM
WINDOW = 128

TM_A = 256
TQ_B = 256
HEADS_PER_ITER = 8
CARRY = 8
VMEM_LIMIT = 52 * 1024 * 1024

_NT = (((1,), (1,)), ((), ()))
_TN = (((0,), (0,)), ((), ()))


def _dot(a, b):
    return jnp.dot(a, b, preferred_element_type=F32)


def _dot_nt(a, b):
    return lax.dot_general(a, b, _NT, preferred_element_type=F32)


def _dot_tn(a, b):
    return lax.dot_general(a, b, _TN, preferred_element_type=F32)


def _split(x):
    hi = x.astype(BF16)
    lo = (x - hi.astype(F32)).astype(BF16)
    return hi, lo


def _dot3_parts(a_parts, b_parts):
    ah, al = a_parts
    bh, bl = b_parts
    return (_dot(al, bh) + _dot(ah, bl)) + _dot(ah, bh)


def _dot3(a, b):
    return _dot3_parts(_split(a), _split(b))


def _dot_exact_lhs(a_bf, b):
    b0 = b.astype(BF16)
    r1 = b - b0.astype(F32)
    b1 = r1.astype(BF16)
    b2 = (r1 - b1.astype(F32)).astype(BF16)
    return (_dot(a_bf, b2) + _dot(a_bf, b1)) + _dot(a_bf, b0)


def _dot_exact_rhs(a, b_bf):
    a0 = a.astype(BF16)
    r1 = a - a0.astype(F32)
    a1 = r1.astype(BF16)
    a2 = (r1 - a1.astype(F32)).astype(BF16)
    return (_dot(a2, b_bf) + _dot(a1, b_bf)) + _dot(a0, b_bf)


def _silu(x):
    return x * jax.nn.sigmoid(x)


def _softplus(x):
    return jnp.maximum(x, 0.0) + jnp.log1p(jnp.exp(-jnp.abs(x)))


def _split_dup(x2, left):
    hi = x2.astype(BF16)
    lo = (x2 - hi.astype(F32)).astype(BF16)
    lhs = jnp.where(left, hi, lo)
    rhs = jnp.concatenate([jnp.concatenate([hi, lo], axis=1),
                           jnp.concatenate([hi, jnp.zeros_like(hi)], axis=1)], axis=0)
    return lhs, rhs


def _dot_split(a_lhs, b_rhs):
    y = _dot(a_lhs, b_rhs)
    half = y.shape[1] // 2
    return y[:, :half] + y[:, half:]


def _unit_lower_inverses(ms2, left):
    n = ms2[0].shape[0]
    row = lax.broadcasted_iota(jnp.int32, (n, 2 * n), 0)
    col = lax.broadcasted_iota(jnp.int32, (n, 2 * n), 1) % n
    eye2 = jnp.where(row == col, 1.0, 0.0).astype(F32)
    zeros = jnp.zeros((n, 2 * n), BF16)

    def times(a2, b2):
        return _dot(a2.astype(BF16), jnp.concatenate([b2.astype(BF16), zeros], axis=0))

    ts = [eye2 - m for m in ms2]
    ps = list(ms2)
    steps = max(1, (n - 1).bit_length()) - 1
    for _ in range(steps):
        ps = [times(p, p) for p in ps]
        ts = [t + times(t, p) for t, p in zip(ts, ps)]
    rs = [(eye2 - t) - _dot_split(_split_dup(m, left)[0], _split_dup(t, left)[1]) for m, t in zip(ms2, ts)]
    return [t + times(t, r) for t, r in zip(ts, rs)]


def _full_spec(shape):
    nd = len(shape)
    return pl.BlockSpec(shape, lambda *_: (0,) * nd)


def _gdn_gates(ab, ab_t, alog_r, dtb_r, alog_c, dtb_c, tm, chunk):
    h = GDN_HEADS
    g_c = -jnp.exp(alog_r) * _softplus(ab[:, :h] + dtb_r)
    beta = jax.nn.sigmoid(ab[:, h:])
    g_r = -jnp.exp(alog_c) * _softplus(ab_t[:h, :] + dtb_c)
    row = lax.broadcasted_iota(jnp.int32, (tm, tm), 0)
    col = lax.broadcasted_iota(jnp.int32, (tm, tm), 1)
    same = (row // chunk) == (col // chunk)
    lower = jnp.where(same & (row >= col), 1.0, 0.0).astype(BF16)
    upper = jnp.where(same & (row <= col), 1.0, 0.0).astype(BF16)
    gc = _dot_exact_lhs(lower, g_c)
    gr = _dot_exact_rhs(g_r, upper)
    return beta, gc, gr


def _gdn_prompt_kernel(x_ref, norm_ref, wqkv_ref, wg_ref, wab_ref, wabt_ref, convw_ref,
                       alog_r_ref, dtb_r_ref, alog_c_ref, dtb_c_ref, onorm_ref, wout_ref,
                       y_ref, conv_ref, ssm_ref,
                       pre_scr, gate_scr, s_scr, k_scr, kb_scr, q_scr, qd_scr, kd_scr,
                       rhs_scr, dec_scr, gt_scr, o_scr, lhs_scr, su_scr):
    tm = x_ref.shape[1]
    n_chunks = tm // CHUNK
    l = pl.program_id(1)

    @pl.when(l == 0)
    def _():
        pre_scr[0:CARRY, :] = jnp.zeros((CARRY, QKV_W), F32)
        s_scr[...] = jnp.zeros(s_scr.shape, F32)

    x = x_ref[0]
    xn = x * lax.rsqrt(jnp.mean(x * x, axis=-1, keepdims=True) + EPS) * norm_ref[...]
    xb = xn.astype(BF16)
    ab = _dot(xb, wab_ref[...])
    ab_t = _dot_nt(wabt_ref[...], xb)
    beta, gc, gr = _gdn_gates(ab, ab_t, alog_r_ref[...], dtb_r_ref[...],
                              alog_c_ref[...], dtb_c_ref[...], tm, CHUNK)
    e_g = jnp.exp(gc)
    g_last = jnp.concatenate(
        [jnp.broadcast_to(gc[c * CHUNK + CHUNK - 1:c * CHUNK + CHUNK, :], (CHUNK, GDN_HEADS))
         for c in range(n_chunks)], axis=0)
    e_kd = jnp.exp(g_last - gc)
    e_tot = jnp.exp(g_last)

    ci = lax.broadcasted_iota(jnp.int32, (CHUNK, 2 * CHUNK), 0)
    cj = lax.broadcasted_iota(jnp.int32, (CHUNK, 2 * CHUNK), 1) % CHUNK
    causal = ci >= cj
    strict = ci > cj
    left = lax.broadcasted_iota(jnp.int32, (1, 2 * CHUNK), 1) < CHUNK

    def conv_act(j):
        cols = slice(j * GDN_D, (j + 1) * GDN_D)
        half = pre_scr[CARRY - 3:CARRY - 3 + tm, cols] * (0.5 * convw_ref[0:1, cols])
        for i in range(1, CONV_W):
            half = half + pre_scr[CARRY - 3 + i:CARRY - 3 + i + tm, cols] * (0.5 * convw_ref[i:i + 1, cols])
        return half + half * jnp.tanh(half)

    def head_prep(h):
        qh = conv_act(h)
        kh = conv_act(GDN_HEADS + h)
        vh = conv_act(2 * GDN_HEADS + h)
        qn = qh * lax.rsqrt(jnp.sum(qh * qh, axis=-1, keepdims=True) + EPS) * (GDN_D ** -0.5)
        kn = kh * lax.rsqrt(jnp.sum(kh * kh, axis=-1, keepdims=True) + EPS)
        b_col = beta[:, h:h + 1]
        eg_col = e_g[:, h:h + 1]
        kb = kn * b_col
        k_scr[h] = kn.astype(BF16)
        kb_scr[h] = kb.astype(BF16)
        q_scr[h] = qn.astype(BF16)
        qd_scr[h] = qn * eg_col
        kd_scr[h] = (kn * e_kd[:, h:h + 1]).astype(BF16)
        rhs_scr[h, :, 0:GDN_D] = (vh * b_col).astype(BF16)
        rhs_scr[h, :, GDN_D:2 * GDN_D] = (kb * eg_col).astype(BF16)
        for c in range(n_chunks):
            rows = slice(c * CHUNK, (c + 1) * CHUNK)
            g_row = gr[h:h + 1, rows]
            diff = gc[rows, h:h + 1] - jnp.concatenate([g_row, g_row], axis=1)
            dec_scr[h, rows, :] = jnp.exp(jnp.where(causal, diff, -jnp.inf))
            gt_scr[h, c:c + 1, :] = jnp.broadcast_to(e_tot[c * CHUNK:c * CHUNK + 1, h:h + 1], (1, GDN_D))

    def chunk_terms(heads):
        items = [(h, c) for h in heads for c in range(n_chunks)]
        rows = [slice(c * CHUNK, (c + 1) * CHUNK) for _, c in items]
        decs = [dec_scr[h, r, :] for (h, _), r in zip(items, rows)]
        aqs = []
        for (h, _), r in zip(items, rows):
            k = k_scr[h, r, :]
            aqs.append(_dot_nt(jnp.concatenate([kb_scr[h, r, :], q_scr[h, r, :]], axis=0),
                               jnp.concatenate([k, k], axis=0)))
        ms = [jnp.where(strict, aq[:CHUNK, :] * d, 0.0) for aq, d in zip(aqs, decs)]
        qks = [jnp.where(causal, aq[CHUNK:, :] * d, 0.0)[:, :CHUNK].astype(BF16) for aq, d in zip(aqs, decs)]
        ts = _unit_lower_inverses(ms, left)
        sols = []
        for (h, _), r, t in zip(items, rows, ts):
            rhs = rhs_scr[h, r, :]
            sols.append(_dot(_split_dup(t, left)[0], jnp.concatenate([rhs, rhs], axis=0)).astype(BF16))
        kd_uws = [_dot_tn(kd_scr[h, r, :], sol)
                  for (h, _), r, sol in zip(items, rows, sols)]
        qk_uws = [_dot(qk, sol) for qk, sol in zip(qks, sols)]
        for (h, c), r, kd_uw, qk_uw in zip(items, rows, kd_uws, qk_uws):
            lhs_scr[h, c, 0:GDN_D, :] = kd_uw[:, GDN_D:].astype(BF16)
            lhs_scr[h, c, GDN_D:GDN_D + CHUNK, :] = (qd_scr[h, r, :] - qk_uw[:, GDN_D:]).astype(BF16)
            su_scr[h, c] = kd_uw[:, :GDN_D]
            o_scr[h, r, :] = qk_uw[:, :GDN_D]

    def project(heads):
        for part in range(3):
            cols = slice((part * GDN_HEADS + heads[0]) * GDN_D, (part * GDN_HEADS + heads[-1] + 1) * GDN_D)
            pre_scr[CARRY:CARRY + tm, cols] = _dot(xb, wqkv_ref[:, cols])

    groups = [list(range(i, i + HEADS_PER_ITER)) for i in range(0, GDN_HEADS, HEADS_PER_ITER)]
    n_groups = len(groups)
    for n in range(-2, n_groups):
        if 0 <= n + 2 < n_groups:
            project(groups[n + 2])
        if 0 <= n + 1 < n_groups:
            for h in groups[n + 1]:
                head_prep(h)
        if n == n_groups - 1:
            gate_scr[...] = _dot(xb, wg_ref[...])
        if n >= 0:
            chunk_terms(groups[n])

    conv_ref[0] = pre_scr[tm + CARRY - 3:tm + CARRY, :]
    pre_scr[0:CARRY, :] = pre_scr[tm:tm + CARRY, :]

    for c in range(n_chunks):
        rows = slice(c * CHUNK, (c + 1) * CHUNK)
        for h in range(GDN_HEADS):
            s = s_scr[h]
            prod = _dot(lhs_scr[h, c], s.astype(BF16))
            o_scr[h, rows, :] = o_scr[h, rows, :] + prod[GDN_D:, :]
            s_scr[h] = s * gt_scr[h, c:c + 1, :] + (su_scr[h, c] - prod[:GDN_D, :])

    @pl.when(l == pl.num_programs(1) - 1)
    def _():
        ssm_ref[0] = s_scr[...]

    pieces = []
    for h in range(GDN_HEADS):
        o = o_scr[h]
        on = o * lax.rsqrt(jnp.mean(o * o, axis=-1, keepdims=True) + EPS) * onorm_ref[...]
        pieces.append((on * _silu(gate_scr[:, h * GDN_D:(h + 1) * GDN_D])).astype(BF16))
    y_ref[0] = x + _dot(jnp.concatenate(pieces, axis=1), wout_ref[...])


def _gdn_prompt(x, norm, wqkv, wg, wab, wabt, convw, alog, dtb, onorm, wout):
    b, l, d = x.shape
    tm = TM_A
    h = GDN_HEADS
    grid = (b, l // tm)
    in_specs = [
        pl.BlockSpec((1, tm, d), lambda i, j: (i, j, 0)),
        _full_spec((1, d)), _full_spec(wqkv.shape), _full_spec(wg.shape), _full_spec(wab.shape),
        _full_spec(wabt.shape), _full_spec(convw.shape),
        _full_spec((1, h)), _full_spec((1, h)), _full_spec((h, 1)), _full_spec((h, 1)),
        _full_spec((1, GDN_D)), _full_spec(wout.shape),
    ]
    out_specs = [
        pl.BlockSpec((1, tm, d), lambda i, j: (i, j, 0)),
        pl.BlockSpec((1, CONV_W - 1, QKV_W), lambda i, j: (i, 0, 0)),
        pl.BlockSpec((1, h, GDN_D, GDN_D), lambda i, j: (i, 0, 0, 0)),
    ]
    out_shape = [
        jax.ShapeDtypeStruct((b, l, d), F32),
        jax.ShapeDtypeStruct((b, CONV_W - 1, QKV_W), F32),
        jax.ShapeDtypeStruct((b, h, GDN_D, GDN_D), F32),
    ]
    scratch = [
        pltpu.VMEM((tm + CARRY, QKV_W), F32),
        pltpu.VMEM((tm, h * GDN_D), F32),
        pltpu.VMEM((h, GDN_D, GDN_D), F32),
        pltpu.VMEM((h, tm, GDN_D), BF16),
        pltpu.VMEM((h, tm, GDN_D), BF16),
        pltpu.VMEM((h, tm, GDN_D), BF16),
        pltpu.VMEM((h, tm, GDN_D), F32),
        pltpu.VMEM((h, tm, GDN_D), BF16),
        pltpu.VMEM((h, tm, 2 * GDN_D), BF16),
        pltpu.VMEM((h, tm, 2 * CHUNK), F32),
        pltpu.VMEM((h, 8, GDN_D), F32),
        pltpu.VMEM((h, tm, GDN_D), F32),
        pltpu.VMEM((h, tm // CHUNK, GDN_D + CHUNK, GDN_D), BF16),
        pltpu.VMEM((h, tm // CHUNK, GDN_D, GDN_D), F32),
    ]
    return pl.pallas_call(
        _gdn_prompt_kernel,
        grid=grid, in_specs=in_specs, out_specs=out_specs, out_shape=out_shape,
        scratch_shapes=scratch,
        compiler_params=pltpu.CompilerParams(
            dimension_semantics=("arbitrary", "arbitrary"), vmem_limit_bytes=VMEM_LIMIT),
        name="gdn_prompt",
    )(x, norm.reshape(1, d), wqkv, wg, wab, wabt, convw,
      alog.reshape(1, h), dtb.reshape(1, h), alog.reshape(h, 1), dtb.reshape(h, 1),
      onorm.reshape(1, GDN_D), wout)


def _gdn_sample_front_kernel(x_ref, norm_ref, wqkv_ref, wg_ref, wab_ref, convw_ref,
                             c0_ref, c1_ref, c2_ref, alog_r_ref, dtb_r_ref,
                             pre_ref, q_ref, k_ref, u_ref, w_ref, qd_ref, gate_ref, eg_ref):
    x = x_ref[...]
    xn = x * lax.rsqrt(jnp.mean(x * x, axis=-1, keepdims=True) + EPS) * norm_ref[...]
    xb = xn.astype(BF16)
    pre = _dot(xb, wqkv_ref[...])
    pre_ref[...] = pre
    gate_ref[...] = _dot(xb, wg_ref[...])
    ab = _dot(xb, wab_ref[...])
    g = -jnp.exp(alog_r_ref[...]) * _softplus(ab[:, :GDN_HEADS] + dtb_r_ref[...])
    beta = jax.nn.sigmoid(ab[:, GDN_HEADS:])
    e_g = jnp.exp(g)
    eg_ref[...] = e_g
    act = _silu(((c0_ref[...] * convw_ref[0:1, :] + c1_ref[...] * convw_ref[1:2, :])
                 + c2_ref[...] * convw_ref[2:3, :]) + pre * convw_ref[3:4, :])
    w_all = GDN_HEADS * GDN_D
    for h in range(GDN_HEADS):
        cols = slice(h * GDN_D, (h + 1) * GDN_D)
        qh = act[:, h * GDN_D:(h + 1) * GDN_D]
        kh = act[:, w_all + h * GDN_D:w_all + (h + 1) * GDN_D]
        vh = act[:, 2 * w_all + h * GDN_D:2 * w_all + (h + 1) * GDN_D]
        qn = qh * lax.rsqrt(jnp.sum(qh * qh, axis=-1, keepdims=True) + EPS) * (GDN_D ** -0.5)
        kn = kh * lax.rsqrt(jnp.sum(kh * kh, axis=-1, keepdims=True) + EPS)
        b_col = beta[:, h:h + 1]
        eg_col = e_g[:, h:h + 1]
        q_ref[:, cols] = qn
        k_ref[:, cols] = kn
        u_ref[:, cols] = vh * b_col
        w_ref[:, cols] = kn * b_col * eg_col
        qd_ref[:, cols] = qn * eg_col


def _gdn_sample_state_kernel(q_ref, k_ref, u_ref, w_ref, qd_ref, eg_ref, s_ref, o_ref, s_out_ref):
    bb = s_ref.shape[0]
    row = lax.broadcasted_iota(jnp.int32, (8, GDN_D), 0)
    pairs = [(b, h) for b in range(bb) for h in range(GDN_HEADS)]
    prods = []
    for b, h in pairs:
        cols = slice(h * GDN_D, (h + 1) * GDN_D)
        lhs = jnp.where(row == 0, w_ref[b:b + 1, cols], jnp.where(row == 1, qd_ref[b:b + 1, cols], 0.0))
        prods.append(_dot(lhs.astype(BF16), s_ref[b, h].astype(BF16)))
    for (b, h), prod in zip(pairs, prods):
        cols = slice(h * GDN_D, (h + 1) * GDN_D)
        q = q_ref[b:b + 1, cols].astype(BF16).astype(F32)
        k = k_ref[b:b + 1, cols].astype(BF16).astype(F32)
        v_new = u_ref[b:b + 1, cols] - prod[0:1, :]
        vb = v_new.astype(BF16).astype(F32)
        qk = jnp.sum(q * k, axis=-1, keepdims=True)
        o_ref[b:b + 1, cols] = prod[1:2, :] + qk.astype(BF16).astype(F32) * vb
        k8 = jnp.where(row == 0, k, 0.0).astype(BF16)
        v8 = jnp.where(row == 0, vb, 0.0).astype(BF16)
        s_out_ref[b, h] = s_ref[b, h] * eg_ref[b:b + 1, h:h + 1] + _dot_tn(k8, v8)


def _out_proj_kernel(x_ref, o_ref, gate_ref, onorm_ref, wout_ref, y_ref, *, head_w, norm_heads):
    pieces = []
    for h in range(x_ref.shape[1] // head_w):
        cols = slice(h * head_w, (h + 1) * head_w)
        o = o_ref[:, cols]
        if norm_heads:
            o = o * lax.rsqrt(jnp.mean(o * o, axis=-1, keepdims=True) + EPS) * onorm_ref[...]
        pieces.append((o * _silu(gate_ref[:, cols])).astype(BF16))
    y_ref[...] = x_ref[...] + _dot(jnp.concatenate(pieces, axis=1), wout_ref[...])


def _gdn_sample(x, conv_state, ssm_state, norm, wqkv, wg, wab, convw, alog, dtb, onorm, wout):
    n, d = x.shape
    h = GDN_HEADS
    hw = h * GDN_D
    conv_t = jnp.transpose(conv_state, (1, 0, 2))
    f = lambda shape: jax.ShapeDtypeStruct(shape, F32)
    pre, q, k, u, w, qd, gate, eg = pl.pallas_call(
        _gdn_sample_front_kernel,
        out_shape=[f((n, QKV_W)), f((n, hw)), f((n, hw)), f((n, hw)), f((n, hw)), f((n, hw)),
                   f((n, hw)), f((n, h))],
        compiler_params=pltpu.CompilerParams(vmem_limit_bytes=VMEM_LIMIT),
        name="gdn_sample_front",
    )(x, norm.reshape(1, d), wqkv, wg, wab, convw, conv_t[0], conv_t[1], conv_t[2],
      alog.reshape(1, h), dtb.reshape(1, h))
    bb = 8
    row_spec = lambda width: pl.BlockSpec((bb, width), lambda i: (i, 0))
    st_spec = pl.BlockSpec((bb, h, GDN_D, GDN_D), lambda i: (i, 0, 0, 0))
    o, s_new = pl.pallas_call(
        _gdn_sample_state_kernel,
        grid=(n // bb,),
        in_specs=[row_spec(hw)] * 5 + [row_spec(h), st_spec],
        out_specs=[row_spec(hw), st_spec],
        out_shape=[f((n, hw)), f(ssm_state.shape)],
        compiler_params=pltpu.CompilerParams(dimension_semantics=("arbitrary",)),
        name="gdn_sample_state",
    )(q, k, u, w, qd, eg, ssm_state)
    y = pl.pallas_call(
        functools.partial(_out_proj_kernel, head_w=GDN_D, norm_heads=True),
        out_shape=f((n, d)),
        name="gdn_sample_out",
    )(x, o, gate, onorm.reshape(1, GDN_D), wout)
    conv_new = jnp.stack([conv_t[1], conv_t[2], pre], axis=1)
    return y, conv_new, s_new


def _head_rms(x, gain_row, n_heads):
    pieces = []
    for h in range(n_heads):
        xh = x[:, h * HEAD_DIM:(h + 1) * HEAD_DIM]
        pieces.append(xh * lax.rsqrt(jnp.mean(xh * xh, axis=-1, keepdims=True) + EPS) * gain_row)
    return pieces


def _pair_rms(x, gain2, left):
    sq = x * x
    lo = jnp.sum(jnp.where(left, sq, 0.0), axis=-1, keepdims=True)
    hi = jnp.sum(jnp.where(left, 0.0, sq), axis=-1, keepdims=True)
    ms = jnp.where(left, lo, hi) * (1.0 / HEAD_DIM)
    return x * lax.rsqrt(ms + EPS) * gain2


def _swa_prompt_kernel(y_ref, kvnorm_ref, wkv_ref, knorm2_ref, normb_ref, win_ref, qnorm2_ref,
                       sinks_ref, wout_ref, out_ref, kwin_ref, vwin_ref, k_scr, v_scr, o_scr):
    tq = y_ref.shape[1]
    w = WINDOW
    step = pl.program_id(1)
    lane = lax.broadcasted_iota(jnp.int32, (1, 2 * HEAD_DIM), 1)
    left = lane < HEAD_DIM

    @pl.when(step == 0)
    def _():
        k_scr[:, 0:w, :] = jnp.zeros((N_KV_HEADS, w, 2 * HEAD_DIM), BF16)
        v_scr[:, 0:w, :] = jnp.zeros((N_KV_HEADS, w, 2 * HEAD_DIM), BF16)

    @pl.when(step > 0)
    def _():
        k_scr[:, 0:w, :] = k_scr[:, tq:tq + w, :]
        v_scr[:, 0:w, :] = v_scr[:, tq:tq + w, :]

    y = y_ref[0]
    yn = y * lax.rsqrt(jnp.mean(y * y, axis=-1, keepdims=True) + EPS)
    kv = _dot((yn * kvnorm_ref[...]).astype(BF16), wkv_ref[...])
    for c in range(N_KV_HEADS // 2):
        cols = slice(c * 2 * HEAD_DIM, (c + 1) * 2 * HEAD_DIM)
        kp = _pair_rms(kv[:, cols], knorm2_ref[...], left)
        vp = kv[:, KV_W + c * 2 * HEAD_DIM:KV_W + (c + 1) * 2 * HEAD_DIM]
        kwin_ref[0, :, cols] = kp[tq - w:, :]
        vwin_ref[0, :, cols] = vp[tq - w:, :]
        kr = pltpu.roll(kp, HEAD_DIM, 1)
        vr = pltpu.roll(vp, HEAD_DIM, 1)
        k_scr[2 * c, w:w + tq, :] = jnp.where(left, kp, kr).astype(BF16)
        k_scr[2 * c + 1, w:w + tq, :] = jnp.where(left, kr, kp).astype(BF16)
        v_scr[2 * c, w:w + tq, :] = jnp.where(left, vp, vr).astype(BF16)
        v_scr[2 * c + 1, w:w + tq, :] = jnp.where(left, vr, vp).astype(BF16)

    qg = _dot((yn * normb_ref[...]).astype(BF16), win_ref[...])
    q_pairs = [_pair_rms(qg[:, c * 2 * HEAD_DIM:(c + 1) * 2 * HEAD_DIM], qnorm2_ref[...], left)
               for c in range(N_Q_HEADS // 2)]

    kj = lax.broadcasted_iota(jnp.int32, (2 * w, w), 0)
    qi = lax.broadcasted_iota(jnp.int32, (2 * w, w), 1)
    dist = qi - kj + w
    band = (dist >= 0) & (dist <= w)
    j_rel = (kj - w).astype(F32)
    i_row = lax.broadcasted_iota(jnp.int32, (1, w), 1).astype(F32)

    def scores(blk, hk):
        q_rows = slice(blk * w, (blk + 1) * w)
        lhs = []
        for c in (2 * hk, 2 * hk + 1):
            qp = q_pairs[c][q_rows, :]
            lhs.append(jnp.where(left, qp, 0.0))
            lhs.append(jnp.where(left, 0.0, qp))
        return _dot_nt(k_scr[hk, blk * w:blk * w + 2 * w, :],
                       jnp.concatenate(lhs, axis=0).astype(BF16))

    def attend(blk, hk, s4):
        q_rows = slice(blk * w, (blk + 1) * w)
        valid = band & ((step > 0) | (kj >= w)) if blk == 0 else band
        ps = []
        for g in range(Q_GROUP):
            hq = hk * Q_GROUP + g
            slope = 2.0 ** (-8.0 * (hq + 1) / N_Q_HEADS)
            a = jnp.where(valid, s4[:, g * w:(g + 1) * w] + slope * j_rel, -jnp.inf)
            sink = sinks_ref[0:1, hq:hq + 1] + slope * i_row
            mx = jnp.maximum(jnp.max(a, axis=0, keepdims=True), sink)
            p = jnp.exp(a - mx)
            inv = 1.0 / (jnp.sum(p, axis=0, keepdims=True) + jnp.exp(sink - mx))
            ps.append((p * inv).astype(BF16))
        o4 = _dot_tn(jnp.concatenate(ps, axis=1),
                     v_scr[hk, blk * w:blk * w + 2 * w, :])
        for j, c in enumerate((2 * hk, 2 * hk + 1)):
            o_scr[q_rows, c * 2 * HEAD_DIM:(c + 1) * 2 * HEAD_DIM] = jnp.where(
                left, o4[(2 * j) * w:(2 * j + 1) * w, :], o4[(2 * j + 1) * w:(2 * j + 2) * w, :])

    items = [(blk, hk) for blk in range(tq // w) for hk in range(N_KV_HEADS)]
    s_next = scores(*items[0])
    for n, item in enumerate(items):
        s_cur = s_next
        if n + 1 < len(items):
            s_next = scores(*items[n + 1])
        attend(*item, s_cur)

    o = (o_scr[...] * _silu(qg[:, ATT_W:])).astype(BF16)
    out_ref[0] = y + _dot(o, wout_ref[...])


def _swa_prompt(y, kvnorm, wkv, knorm, normb, win, qnorm, sinks, wout):
    b, l, d = y.shape
    tq = TQ_B
    w = WINDOW
    f = lambda shape: jax.ShapeDtypeStruct(shape, F32)
    knorm2 = jnp.concatenate([knorm, knorm]).reshape(1, 2 * HEAD_DIM)
    qnorm2 = (jnp.concatenate([qnorm, qnorm]) * (HEAD_DIM ** -0.5)).reshape(1, 2 * HEAD_DIM)
    in_specs = [
        pl.BlockSpec((1, tq, d), lambda i, j: (i, j, 0)),
        _full_spec((1, d)), _full_spec(wkv.shape), _full_spec((1, 2 * HEAD_DIM)), _full_spec((1, d)),
        _full_spec(win.shape), _full_spec((1, 2 * HEAD_DIM)), _full_spec((1, N_Q_HEADS)),
        _full_spec(wout.shape),
    ]
    out_specs = [
        pl.BlockSpec((1, tq, d), lambda i, j: (i, j, 0)),
        pl.BlockSpec((1, w, KV_W), lambda i, j: (i, 0, 0)),
        pl.BlockSpec((1, w, KV_W), lambda i, j: (i, 0, 0)),
    ]
    return pl.pallas_call(
        _swa_prompt_kernel,
        grid=(b, l // tq), in_specs=in_specs, out_specs=out_specs,
        out_shape=[f((b, l, d)), f((b, w, KV_W)), f((b, w, KV_W))],
        scratch_shapes=[pltpu.VMEM((N_KV_HEADS, w + tq, 2 * HEAD_DIM), BF16),
                        pltpu.VMEM((N_KV_HEADS, w + tq, 2 * HEAD_DIM), BF16),
                        pltpu.VMEM((tq, ATT_W), F32)],
        compiler_params=pltpu.CompilerParams(
            dimension_semantics=("arbitrary", "arbitrary"), vmem_limit_bytes=VMEM_LIMIT),
        name="swa_prompt",
    )(y, kvnorm.reshape(1, d), wkv, knorm2, normb.reshape(1, d), win, qnorm2,
      sinks.reshape(1, N_Q_HEADS), wout)


def _swa_sample_front_kernel(y_ref, kvnorm_ref, wkv_ref, knorm_ref, normb_ref, win_ref, qnorm_ref,
                             k_ref, v_ref, q_ref, gate_ref):
    y = y_ref[...]
    yn = y * lax.rsqrt(jnp.mean(y * y, axis=-1, keepdims=True) + EPS)
    kv = _dot((yn * kvnorm_ref[...]).astype(BF16), wkv_ref[...])
    k_ref[...] = jnp.concatenate(_head_rms(kv[:, :KV_W], knorm_ref[...], N_KV_HEADS), axis=1)
    v_ref[...] = kv[:, KV_W:]
    qg = _dot((yn * normb_ref[...]).astype(BF16), win_ref[...])
    q_ref[...] = jnp.concatenate(
        _head_rms(qg[:, :ATT_W], qnorm_ref[...] * (HEAD_DIM ** -0.5), N_Q_HEADS), axis=1)
    gate_ref[...] = qg[:, ATT_W:]


def _swa_sample_attn_kernel(q_ref, kn_ref, vn_ref, kc_ref, vc_ref, sinks_ref, slopes_ref,
                            o_ref, kwin_ref, vwin_ref):
    bb = kc_ref.shape[0]
    w = kc_ref.shape[1]
    hrow = lax.broadcasted_iota(jnp.int32, (N_Q_HEADS, KV_W), 0) // Q_GROUP
    lblk = lax.broadcasted_iota(jnp.int32, (N_Q_HEADS, KV_W), 1) // HEAD_DIM
    own = hrow == lblk
    dist_c = (w - lax.broadcasted_iota(jnp.int32, (1, w), 1)).astype(F32)
    krow = lax.broadcasted_iota(jnp.int32, (w, KV_W), 0)
    slopes = slopes_ref[...]
    sink = sinks_ref[...]
    qms, scs = [], []
    for b in range(bb):
        q = q_ref[b]
        qm = jnp.where(own, jnp.concatenate([q] * N_KV_HEADS, axis=1), 0.0).astype(BF16)
        qms.append(qm)
        scs.append(_dot_nt(qm, kc_ref[b].astype(BF16)))
    pcs, pns = [], []
    for b in range(bb):
        s_c = scs[b] - slopes * dist_c
        s_n = jnp.sum(qms[b].astype(F32) * kn_ref[b].astype(BF16).astype(F32), axis=-1, keepdims=True)
        mx = jnp.maximum(jnp.maximum(jnp.max(s_c, axis=-1, keepdims=True), s_n), sink)
        p_c = jnp.exp(s_c - mx)
        p_n = jnp.exp(s_n - mx)
        den = jnp.sum(p_c, axis=-1, keepdims=True) + p_n + jnp.exp(sink - mx)
        pcs.append((p_c / den).astype(BF16))
        pns.append(p_n / den)
    rs = [_dot(pcs[b], vc_ref[b].astype(BF16)) for b in range(bb)]
    for b in range(bb):
        kn = kn_ref[b]
        vn = vn_ref[b]
        r = rs[b] + pns[b].astype(BF16).astype(F32) * vn.astype(BF16).astype(F32)
        r = jnp.where(own, r, 0.0)
        acc = r[:, 0:HEAD_DIM]
        for blk in range(1, N_KV_HEADS):
            acc = acc + r[:, blk * HEAD_DIM:(blk + 1) * HEAD_DIM]
        o_ref[b] = acc
        kwin_ref[b] = jnp.where(krow == w - 1, kn, pltpu.roll(kc_ref[b], w - 1, 0))
        vwin_ref[b] = jnp.where(krow == w - 1, vn, pltpu.roll(vc_ref[b], w - 1, 0))


def _swa_sample(y, cache_k, cache_v, kvnorm, wkv, knorm, normb, win, qnorm, sinks, wout):
    n, d = y.shape
    w = cache_k.shape[1]
    f = lambda shape: jax.ShapeDtypeStruct(shape, F32)
    kn, vn, q, gate = pl.pallas_call(
        _swa_sample_front_kernel,
        out_shape=[f((n, KV_W)), f((n, KV_W)), f((n, ATT_W)), f((n, ATT_W))],
        compiler_params=pltpu.CompilerParams(vmem_limit_bytes=VMEM_LIMIT),
        name="swa_sample_front",
    )(y, kvnorm.reshape(1, d), wkv, knorm.reshape(1, HEAD_DIM), normb.reshape(1, d), win,
      qnorm.reshape(1, HEAD_DIM))
    bb = 8
    slopes = (2.0 ** (-8.0 * jnp.arange(1, N_Q_HEADS + 1, dtype=F32) / N_Q_HEADS)).reshape(N_Q_HEADS, 1)
    spec3 = lambda a, c: pl.BlockSpec((bb, a, c), lambda i: (i, 0, 0))
    o, kwin, vwin = pl.pallas_call(
        _swa_sample_attn_kernel,
        grid=(n // bb,),
        in_specs=[spec3(N_Q_HEADS, HEAD_DIM), spec3(1, KV_W), spec3(1, KV_W), spec3(w, KV_W),
                  spec3(w, KV_W), _full_spec((N_Q_HEADS, 1)), _full_spec((N_Q_HEADS, 1))],
        out_specs=[spec3(N_Q_HEADS, HEAD_DIM), spec3(w, KV_W), spec3(w, KV_W)],
        out_shape=[f((n, N_Q_HEADS, HEAD_DIM)), f((n, w, KV_W)), f((n, w, KV_W))],
        compiler_params=pltpu.CompilerParams(dimension_semantics=("arbitrary",)),
        name="swa_sample_attn",
    )(q.reshape(n, N_Q_HEADS, HEAD_DIM), kn.reshape(n, 1, KV_W), vn.reshape(n, 1, KV_W),
      cache_k.reshape(n, w, KV_W), cache_v.reshape(n, w, KV_W),
      sinks.reshape(N_Q_HEADS, 1), slopes)
    out = pl.pallas_call(
        functools.partial(_out_proj_kernel, head_w=ATT_W, norm_heads=False),
        out_shape=f((n, d)),
        name="swa_sample_out",
    )(y, o.reshape(n, ATT_W), gate, jnp.ones((1, ATT_W), F32), wout)
    return out, kwin, vwin


def kernel(x_prompt, x_sample, state_conv, state_ssm, cache_k_win, cache_v_win, norm_a, w_in_a, conv_w_a, a_log, dt_bias, o_norm_a, w_out_a, kv_norm, w_kv, k_norm, norm_b, w_in_b, q_norm, sinks, w_out_b):
    n_a = w_in_a.shape[0]
    n_b = w_in_b.shape[0]
    assert n_a == 1 and n_b == 1, "kernel is written for DEPTH == 2"
    bp, lp, d = x_prompt.shape
    n = x_sample.shape[0]
    hw = GDN_HEADS * GDN_D

    hp, hs = x_prompt, x_sample.reshape(n, d)
    conv_p, ssm_p, conv_s, ssm_s = [], [], [], []
    for layer in range(n_a):
        w_in = w_in_a[layer]
        wqkv = w_in[:, :QKV_W].astype(BF16)
        wg = w_in[:, QKV_W:QKV_W + hw].astype(BF16)
        wab = w_in[:, QKV_W + hw:].astype(BF16)
        wabt = wab.T
        wout = w_out_a[layer].astype(BF16)
        hp, cbuf, st = _gdn_prompt(hp, norm_a[layer], wqkv, wg, wab, wabt, conv_w_a[layer],
                                   a_log[layer], dt_bias[layer], o_norm_a[layer], wout)
        conv_p.append(cbuf)
        ssm_p.append(st)
        hs, cbuf, st = _gdn_sample(hs, state_conv[layer], state_ssm[layer], norm_a[layer], wqkv, wg,
                                   wab, conv_w_a[layer], a_log[layer], dt_bias[layer],
                                   o_norm_a[layer], wout)
        conv_s.append(cbuf)
        ssm_s.append(st)

    wkv = w_kv.astype(BF16)
    win = w_in_b[0].astype(BF16)
    woutb = w_out_b[0].astype(BF16)
    hp, k_win_p, v_win_p = _swa_prompt(hp, kv_norm, wkv, k_norm, norm_b[0], win, q_norm[0], sinks[0], woutb)
    hs, k_win_s, v_win_s = _swa_sample(hs, cache_k_win, cache_v_win, kv_norm, wkv, k_norm, norm_b[0],
                                       win, q_norm[0], sinks[0], woutb)
    kv_shape = (N_KV_HEADS, HEAD_DIM)
    return (hp, hs.reshape(n, 1, d), jnp.stack(conv_p), jnp.stack(ssm_p),
            k_win_p.reshape(bp, WINDOW, *kv_shape), v_win_p.reshape(bp, WINDOW, *kv_shape),
            jnp.stack(conv_s), jnp.stack(ssm_s),
            k_win_s.reshape(n, WINDOW, *kv_shape), v_win_s.reshape(n, WINDOW, *kv_shape))
```

```python
import functools

import jax
import jax.numpy as jnp
from jax import lax
from jax.experimental import pallas as pl
from jax.experimental.pallas import tpu as pltpu

F32 = jnp.float32
BF16 = jnp.bfloat16
EPS = 1e-6

D_MODEL = 1024
GDN_HEADS = 8
GDN_D = 128
QKV_W = 3 * GDN_HEADS * GDN_D
CONV_W = 4
CHUNK = 64
N_Q_HEADS = 16
N_KV_HEADS = 4
Q_GROUP = N_Q_HEADS // N_KV_HEADS
HEAD_DIM = 64
KV_W = N_KV_HEADS * HEAD_DIM
ATT_W = N_Q_HEADS * HEAD_DIM
WINDOW = 128

TM_A = 256
TQ_B = 256
HEADS_PER_ITER = 8
CARRY = 8
VMEM_LIMIT = 52 * 1024 * 1024

_NT = (((1,), (1,)), ((), ()))
_TN = (((0,), (0,)), ((), ()))


def _dot(a, b):
    return jnp.dot(a, b, preferred_element_type=F32)


def _dot_nt(a, b):
    return lax.dot_general(a, b, _NT, preferred_element_type=F32)


def _dot_tn(a, b):
    return lax.dot_general(a, b, _TN, preferred_element_type=F32)


def _split(x):
    hi = x.astype(BF16)
    lo = (x - hi.astype(F32)).astype(BF16)
    return hi, lo


def _dot_exact_lhs(a_bf, b):
    b0 = b.astype(BF16)
    r1 = b - b0.astype(F32)
    b1 = r1.astype(BF16)
    b2 = (r1 - b1.astype(F32)).astype(BF16)
    return (_dot(a_bf, b2) + _dot(a_bf, b1)) + _dot(a_bf, b0)


def _dot_exact_rhs(a, b_bf):
    a0 = a.astype(BF16)
    r1 = a - a0.astype(F32)
    a1 = r1.astype(BF16)
    a2 = (r1 - a1.astype(F32)).astype(BF16)
    return (_dot(a2, b_bf) + _dot(a1, b_bf)) + _dot(a0, b_bf)


def _silu(x):
    return x * jax.nn.sigmoid(x)


def _softplus(x):
    return jnp.maximum(x, 0.0) + jnp.log1p(jnp.exp(-jnp.abs(x)))


def _halves(x, left):
    zero = jnp.zeros_like(x)
    return jnp.where(left, x, zero), jnp.where(left, zero, x)


def _block_diag(x, left):
    return jnp.concatenate(_halves(x, left), axis=0)


def _pair_split_product(a, b, left):
    ah, al = _split(a)
    bh, bl = _split(b)
    bh1, bh2 = _halves(bh, left)
    bl1, bl2 = _halves(bl, left)
    zero = jnp.zeros_like(bh)
    rhs = jnp.concatenate([jnp.concatenate([bh1, bl1], axis=1), jnp.concatenate([bh2, bl2], axis=1),
                           jnp.concatenate([bh1, zero], axis=1), jnp.concatenate([bh2, zero], axis=1)], axis=0)
    y = _dot(jnp.concatenate([ah, al], axis=1), rhs)
    half = y.shape[1] // 2
    return y[:, :half] + y[:, half:]


def _unit_lower_inverse_pairs(ms, left):
    n = ms[0].shape[0]
    row = lax.broadcasted_iota(jnp.int32, (n, 2 * n), 0)
    col = lax.broadcasted_iota(jnp.int32, (n, 2 * n), 1) % n
    eye2 = jnp.where(row == col, 1.0, 0.0).astype(F32)

    def times(a, b):
        return _dot(a.astype(BF16), _block_diag(b.astype(BF16), left))

    ts = [eye2 - m for m in ms]
    ps = [times(m, m) for m in ms]
    steps = max(1, (n - 1).bit_length()) - 1
    for _ in range(steps - 1):
        both = [_dot(jnp.concatenate([t.astype(BF16), p.astype(BF16)], axis=0), _block_diag(p.astype(BF16), left))
                for t, p in zip(ts, ps)]
        ts = [t + tp[:n, :] for t, tp in zip(ts, both)]
        ps = [tp[n:, :] for tp in both]
    ts = [t + times(t, p) for t, p in zip(ts, ps)]
    rs = [(eye2 - t) - _pair_split_product(m, t, left) for m, t in zip(ms, ts)]
    return [t + times(t, r) for t, r in zip(ts, rs)]


def _full_spec(shape):
    nd = len(shape)
    return pl.BlockSpec(shape, lambda *_: (0,) * nd)


def _gdn_gates(ab, ab_t, alog_r, dtb_r, alog_c, dtb_c, tm, chunk):
    h = GDN_HEADS
    g_c = -jnp.exp(alog_r) * _softplus(ab[:, :h] + dtb_r)
    beta = jax.nn.sigmoid(ab[:, h:])
    g_r = -jnp.exp(alog_c) * _softplus(ab_t[:h, :] + dtb_c)
    row = lax.broadcasted_iota(jnp.int32, (tm, tm), 0)
    col = lax.broadcasted_iota(jnp.int32, (tm, tm), 1)
    same = (row // chunk) == (col // chunk)
    lower = jnp.where(same & (row >= col), 1.0, 0.0).astype(BF16)
    upper = jnp.where(same & (row <= col), 1.0, 0.0).astype(BF16)
    gc = _dot_exact_lhs(lower, g_c)
    gr = _dot_exact_rhs(g_r, upper)
    return beta, gc, gr


def _gdn_prompt_kernel(x_ref, norm_ref, win_ref, wabt_ref, convw_ref,
                       alog_r_ref, dtb_r_ref, alog_c_ref, dtb_c_ref, onorm_ref, wout_ref,
                       y_ref, conv_ref, ssm_ref,
                       pre_scr, gate_scr, s_scr, k_scr, kb_scr, q_scr, qd_scr, kd_scr,
                       rhs_scr, dec_scr, gt_scr, o_scr, lhs_scr, su_scr):
    tm = x_ref.shape[1]
    n_chunks = tm // CHUNK
    l = pl.program_id(1)

    @pl.when(l == 0)
    def _():
        pre_scr[0:CARRY, :] = jnp.zeros((CARRY, QKV_W), F32)
        s_scr[...] = jnp.zeros(s_scr.shape, F32)

    x = x_ref[0]
    xn = x * lax.rsqrt(jnp.mean(x * x, axis=-1, keepdims=True) + EPS) * norm_ref[...]
    xb = xn.astype(BF16)
    gate_w = GDN_HEADS * GDN_D
    ab = _dot(xb, win_ref[:, QKV_W + gate_w:])
    ab_t = _dot_nt(wabt_ref[...], xb)
    beta, gc, gr = _gdn_gates(ab, ab_t, alog_r_ref[...], dtb_r_ref[...],
                              alog_c_ref[...], dtb_c_ref[...], tm, CHUNK)
    e_g = jnp.exp(gc)
    g_last = jnp.concatenate(
        [jnp.broadcast_to(gc[c * CHUNK + CHUNK - 1:c * CHUNK + CHUNK, :], (CHUNK, GDN_HEADS))
         for c in range(n_chunks)], axis=0)
    e_kd = jnp.exp(g_last - gc)
    e_tot = jnp.exp(g_last)

    ci = lax.broadcasted_iota(jnp.int32, (CHUNK, 2 * CHUNK), 0)
    cj = lax.broadcasted_iota(jnp.int32, (CHUNK, 2 * CHUNK), 1) % CHUNK
    causal = ci >= cj
    strict = ci > cj
    left = lax.broadcasted_iota(jnp.int32, (1, 2 * CHUNK), 1) < CHUNK

    def conv_act(j):
        cols = slice(j * GDN_D, (j + 1) * GDN_D)
        ext = pre_scr[:, cols]
        half = ext[CARRY:, :] * (0.5 * convw_ref[CONV_W - 1:CONV_W, cols])
        for back in range(1, CONV_W):
            tap = CONV_W - 1 - back
            half = half + pltpu.roll(ext, back, 0)[CARRY:, :] * (0.5 * convw_ref[tap:tap + 1, cols])
        return half + half * jnp.tanh(half)

    def head_prep(h):
        qh = conv_act(h)
        kh = conv_act(GDN_HEADS + h)
        vh = conv_act(2 * GDN_HEADS + h)
        qn = qh * lax.rsqrt(jnp.sum(qh * qh, axis=-1, keepdims=True) + EPS) * (GDN_D ** -0.5)
        kn = kh * lax.rsqrt(jnp.sum(kh * kh, axis=-1, keepdims=True) + EPS)
        b_col = beta[:, h:h + 1]
        eg_col = e_g[:, h:h + 1]
        kb = kn * b_col
        k_scr[h] = kn.astype(BF16)
        kb_scr[h] = kb.astype(BF16)
        q_scr[h] = qn.astype(BF16)
        qd_scr[h] = qn * eg_col
        kd_scr[h] = (kn * e_kd[:, h:h + 1]).astype(BF16)
        rhs_scr[h, :, 0:GDN_D] = (vh * b_col).astype(BF16)
        rhs_scr[h, :, GDN_D:2 * GDN_D] = (kb * eg_col).astype(BF16)
        for c in range(n_chunks):
            gt_scr[h, c:c + 1, :] = jnp.broadcast_to(e_tot[c * CHUNK:c * CHUNK + 1, h:h + 1], (1, GDN_D))
        for j in range(n_chunks // 2):
            r1 = slice(2 * j * CHUNK, (2 * j + 1) * CHUNK)
            r2 = slice((2 * j + 1) * CHUNK, (2 * j + 2) * CHUNK)
            diff = jnp.where(left, gc[r1, h:h + 1], gc[r2, h:h + 1]) - gr[h:h + 1, 2 * j * CHUNK:(2 * j + 2) * CHUNK]
            dec_scr[h, j * CHUNK:(j + 1) * CHUNK, :] = jnp.exp(jnp.where(causal, diff, -jnp.inf))

    def chunk_terms(heads):
        items = [(h, j) for h in heads for j in range(n_chunks // 2)]
        aqs = []
        for h, j in items:
            r12 = slice(2 * j * CHUNK, (2 * j + 2) * CHUNK)
            aqs.append(_dot_nt(jnp.concatenate([kb_scr[h, r12, :], q_scr[h, r12, :]], axis=0), k_scr[h, r12, :]))
        decs = [dec_scr[h, j * CHUNK:(j + 1) * CHUNK, :] for h, j in items]
        ms = [jnp.where(strict, jnp.where(left, aq[:CHUNK, :], aq[CHUNK:2 * CHUNK, :]) * d, 0.0)
              for aq, d in zip(aqs, decs)]
        qks = [jnp.where(causal, jnp.where(left, aq[2 * CHUNK:3 * CHUNK, :], aq[3 * CHUNK:, :]) * d, 0.0)
               for aq, d in zip(aqs, decs)]
        ts = _unit_lower_inverse_pairs(ms, left)
        sols = []
        for (h, j), t, qk in zip(items, ts, qks):
            lo = t - t.astype(BF16).astype(F32)
            t_parts = (jnp.where(left, t, pltpu.roll(lo, CHUNK, 1)).astype(BF16),
                       jnp.where(left, pltpu.roll(t, CHUNK, 1), lo).astype(BF16))
            qk_parts = (qk[:, :CHUNK].astype(BF16), pltpu.roll(qk, CHUNK, 1)[:, :CHUNK].astype(BF16))
            for half in range(2):
                c = 2 * j + half
                r = slice(c * CHUNK, (c + 1) * CHUNK)
                rhs = rhs_scr[h, r, :]
                sols.append((h, c, r, qk_parts[half],
                             _dot(t_parts[half], jnp.concatenate([rhs, rhs], axis=0)).astype(BF16)))
        outs = [(h, c, r, _dot_tn(kd_scr[h, r, :], sol),
                 _dot(qk_c, sol))
                for h, c, r, qk_c, sol in sols]
        for h, c, r, kd_uw, qk_uw in outs:
            lhs_scr[h, c, 0:GDN_D, :] = kd_uw[:, GDN_D:].astype(BF16)
            lhs_scr[h, c, GDN_D:GDN_D + CHUNK, :] = (qd_scr[h, r, :] - qk_uw[:, GDN_D:]).astype(BF16)
            su_scr[h, c] = kd_uw[:, :GDN_D]
            o_scr[h, r, :] = qk_uw[:, :GDN_D]

    def project(heads):
        for part in range(3):
            cols = slice((part * GDN_HEADS + heads[0]) * GDN_D, (part * GDN_HEADS + heads[-1] + 1) * GDN_D)
            pre_scr[CARRY:CARRY + tm, cols] = _dot(xb, win_ref[:, cols])

    groups = [list(range(i, i + HEADS_PER_ITER)) for i in range(0, GDN_HEADS, HEADS_PER_ITER)]
    n_groups = len(groups)
    for n in range(-2, n_groups):
        if 0 <= n + 2 < n_groups:
            project(groups[n + 2])
        if 0 <= n + 1 < n_groups:
            for h in groups[n + 1]:
                head_prep(h)
        if n == n_groups - 1:
            gate_scr[...] = _dot(xb, win_ref[:, QKV_W:QKV_W + gate_w])
        if n >= 0:
            chunk_terms(groups[n])

    conv_ref[0] = pre_scr[tm + CARRY - 3:tm + CARRY, :]
    pre_scr[0:CARRY, :] = pre_scr[tm:tm + CARRY, :]

    for c in range(n_chunks):
        rows = slice(c * CHUNK, (c + 1) * CHUNK)
        for h in range(GDN_HEADS):
            s = s_scr[h]
            prod = _dot(lhs_scr[h, c], s.astype(BF16))
            o_scr[h, rows, :] = o_scr[h, rows, :] + prod[GDN_D:, :]
            s_scr[h] = s * gt_scr[h, c:c + 1, :] + (su_scr[h, c] - prod[:GDN_D, :])

    @pl.when(l == pl.num_programs(1) - 1)
    def _():
        ssm_ref[0] = s_scr[...]

    pieces = []
    for h in range(GDN_HEADS):
        o = o_scr[h]
        on = o * lax.rsqrt(jnp.mean(o * o, axis=-1, keepdims=True) + EPS) * onorm_ref[...]
        pieces.append((on * _silu(gate_scr[:, h * GDN_D:(h + 1) * GDN_D])).astype(BF16))
    y_ref[0] = x + _dot(jnp.concatenate(pieces, axis=1), wout_ref[...])


def _gdn_prompt(x, norm, win, wabt, convw, alog, dtb, onorm, wout):
    b, l, d = x.shape
    tm = TM_A
    h = GDN_HEADS
    grid = (b, l // tm)
    in_specs = [
        pl.BlockSpec((1, tm, d), lambda i, j: (i, j, 0)),
        _full_spec((1, d)), _full_spec(win.shape), _full_spec(wabt.shape), _full_spec(convw.shape),
        _full_spec((1, h)), _full_spec((1, h)), _full_spec((h, 1)), _full_spec((h, 1)),
        _full_spec((1, GDN_D)), _full_spec(wout.shape),
    ]
    out_specs = [
        pl.BlockSpec((1, tm, d), lambda i, j: (i, j, 0)),
        pl.BlockSpec((1, CONV_W - 1, QKV_W), lambda i, j: (i, 0, 0)),
        pl.BlockSpec((1, h, GDN_D, GDN_D), lambda i, j: (i, 0, 0, 0)),
    ]
    out_shape = [
        jax.ShapeDtypeStruct((b, l, d), F32),
        jax.ShapeDtypeStruct((b, CONV_W - 1, QKV_W), F32),
        jax.ShapeDtypeStruct((b, h, GDN_D, GDN_D), F32),
    ]
    scratch = [
        pltpu.VMEM((tm + CARRY, QKV_W), F32),
        pltpu.VMEM((tm, h * GDN_D), F32),
        pltpu.VMEM((h, GDN_D, GDN_D), F32),
        pltpu.VMEM((h, tm, GDN_D), BF16),
        pltpu.VMEM((h, tm, GDN_D), BF16),
        pltpu.VMEM((h, tm, GDN_D), BF16),
        pltpu.VMEM((h, tm, GDN_D), F32),
        pltpu.VMEM((h, tm, GDN_D), BF16),
        pltpu.VMEM((h, tm, 2 * GDN_D), BF16),
        pltpu.VMEM((h, tm // 2, 2 * CHUNK), F32),
        pltpu.VMEM((h, 8, GDN_D), F32),
        pltpu.VMEM((h, tm, GDN_D), F32),
        pltpu.VMEM((h, tm // CHUNK, GDN_D + CHUNK, GDN_D), BF16),
        pltpu.VMEM((h, tm // CHUNK, GDN_D, GDN_D), F32),
    ]
    return pl.pallas_call(
        _gdn_prompt_kernel,
        grid=grid, in_specs=in_specs, out_specs=out_specs, out_shape=out_shape,
        scratch_shapes=scratch,
        compiler_params=pltpu.CompilerParams(
            dimension_semantics=("arbitrary", "arbitrary"), vmem_limit_bytes=VMEM_LIMIT),
        name="gdn_prompt",
    )(x, norm.reshape(1, d), win, wabt, convw,
      alog.reshape(1, h), dtb.reshape(1, h), alog.reshape(h, 1), dtb.reshape(h, 1),
      onorm.reshape(1, GDN_D), wout)


def _gdn_sample_front_kernel(x_ref, norm_ref, win_ref, convw_ref,
                             c0_ref, c1_ref, c2_ref, alog_r_ref, dtb_r_ref,
                             pre_ref, q_ref, k_ref, u_ref, w_ref, qd_ref, gate_ref, eg_ref):
    x = x_ref[...]
    xn = x * lax.rsqrt(jnp.mean(x * x, axis=-1, keepdims=True) + EPS) * norm_ref[...]
    xb = xn.astype(BF16)
    gate_w = GDN_HEADS * GDN_D
    pre = _dot(xb, win_ref[:, :QKV_W])
    pre_ref[...] = pre
    gate_ref[...] = _dot(xb, win_ref[:, QKV_W:QKV_W + gate_w])
    ab = _dot(xb, win_ref[:, QKV_W + gate_w:])
    g = -jnp.exp(alog_r_ref[...]) * _softplus(ab[:, :GDN_HEADS] + dtb_r_ref[...])
    beta = jax.nn.sigmoid(ab[:, GDN_HEADS:])
    e_g = jnp.exp(g)
    eg_ref[...] = e_g
    act = _silu(((c0_ref[...] * convw_ref[0:1, :] + c1_ref[...] * convw_ref[1:2, :])
                 + c2_ref[...] * convw_ref[2:3, :]) + pre * convw_ref[3:4, :])
    w_all = GDN_HEADS * GDN_D
    for h in range(GDN_HEADS):
        cols = slice(h * GDN_D, (h + 1) * GDN_D)
        qh = act[:, h * GDN_D:(h + 1) * GDN_D]
        kh = act[:, w_all + h * GDN_D:w_all + (h + 1) * GDN_D]
        vh = act[:, 2 * w_all + h * GDN_D:2 * w_all + (h + 1) * GDN_D]
        qn = qh * lax.rsqrt(jnp.sum(qh * qh, axis=-1, keepdims=True) + EPS) * (GDN_D ** -0.5)
        kn = kh * lax.rsqrt(jnp.sum(kh * kh, axis=-1, keepdims=True) + EPS)
        b_col = beta[:, h:h + 1]
        eg_col = e_g[:, h:h + 1]
        q_ref[:, cols] = qn
        k_ref[:, cols] = kn
        u_ref[:, cols] = vh * b_col
        w_ref[:, cols] = kn * b_col * eg_col
        qd_ref[:, cols] = qn * eg_col


def _gdn_sample_state_kernel(q_ref, k_ref, u_ref, w_ref, qd_ref, eg_ref, s_ref, o_ref, s_out_ref):
    bb = s_ref.shape[0]
    row = lax.broadcasted_iota(jnp.int32, (8, GDN_D), 0)
    pairs = [(b, h) for b in range(bb) for h in range(GDN_HEADS)]
    prods = []
    for b, h in pairs:
        cols = slice(h * GDN_D, (h + 1) * GDN_D)
        lhs = jnp.where(row == 0, w_ref[b:b + 1, cols], jnp.where(row == 1, qd_ref[b:b + 1, cols], 0.0))
        prods.append(_dot(lhs.astype(BF16), s_ref[b, h].astype(BF16)))
    for (b, h), prod in zip(pairs, prods):
        cols = slice(h * GDN_D, (h + 1) * GDN_D)
        q = q_ref[b:b + 1, cols].astype(BF16).astype(F32)
        k = k_ref[b:b + 1, cols].astype(BF16).astype(F32)
        v_new = u_ref[b:b + 1, cols] - prod[0:1, :]
        vb = v_new.astype(BF16).astype(F32)
        qk = jnp.sum(q * k, axis=-1, keepdims=True)
        o_ref[b:b + 1, cols] = prod[1:2, :] + qk.astype(BF16).astype(F32) * vb
        k8 = jnp.where(row == 0, k, 0.0).astype(BF16)
        v8 = jnp.where(row == 0, vb, 0.0).astype(BF16)
        s_out_ref[b, h] = s_ref[b, h] * eg_ref[b:b + 1, h:h + 1] + _dot_tn(k8, v8)


def _out_proj_kernel(x_ref, o_ref, gate_ref, onorm_ref, wout_ref, y_ref, *, head_w, norm_heads):
    pieces = []
    for h in range(x_ref.shape[1] // head_w):
        cols = slice(h * head_w, (h + 1) * head_w)
        o = o_ref[:, cols]
        if norm_heads:
            o = o * lax.rsqrt(jnp.mean(o * o, axis=-1, keepdims=True) + EPS) * onorm_ref[...]
        pieces.append((o * _silu(gate_ref[:, cols])).astype(BF16))
    y_ref[...] = x_ref[...] + _dot(jnp.concatenate(pieces, axis=1), wout_ref[...])


def _gdn_sample(x, conv_state, ssm_state, norm, win, convw, alog, dtb, onorm, wout):
    n, d = x.shape
    h = GDN_HEADS
    hw = h * GDN_D
    conv_t = jnp.transpose(conv_state, (1, 0, 2))
    f = lambda shape: jax.ShapeDtypeStruct(shape, F32)
    pre, q, k, u, w, qd, gate, eg = pl.pallas_call(
        _gdn_sample_front_kernel,
        out_shape=[f((n, QKV_W)), f((n, hw)), f((n, hw)), f((n, hw)), f((n, hw)), f((n, hw)),
                   f((n, hw)), f((n, h))],
        compiler_params=pltpu.CompilerParams(vmem_limit_bytes=VMEM_LIMIT),
        name="gdn_sample_front",
    )(x, norm.reshape(1, d), win, convw, conv_t[0], conv_t[1], conv_t[2],
      alog.reshape(1, h), dtb.reshape(1, h))
    bb = 8
    row_spec = lambda width: pl.BlockSpec((bb, width), lambda i: (i, 0))
    st_spec = pl.BlockSpec((bb, h, GDN_D, GDN_D), lambda i: (i, 0, 0, 0))
    o, s_new = pl.pallas_call(
        _gdn_sample_state_kernel,
        grid=(n // bb,),
        in_specs=[row_spec(hw)] * 5 + [row_spec(h), st_spec],
        out_specs=[row_spec(hw), st_spec],
        out_shape=[f((n, hw)), f(ssm_state.shape)],
        compiler_params=pltpu.CompilerParams(dimension_semantics=("arbitrary",)),
        name="gdn_sample_state",
    )(q, k, u, w, qd, eg, ssm_state)
    y = pl.pallas_call(
        functools.partial(_out_proj_kernel, head_w=GDN_D, norm_heads=True),
        out_shape=f((n, d)),
        name="gdn_sample_out",
    )(x, o, gate, onorm.reshape(1, GDN_D), wout)
    conv_new = jnp.stack([conv_t[1], conv_t[2], pre], axis=1)
    return y, conv_new, s_new


def _head_rms(x, gain_row, n_heads):
    pieces = []
    for h in range(n_heads):
        xh = x[:, h * HEAD_DIM:(h + 1) * HEAD_DIM]
        pieces.append(xh * lax.rsqrt(jnp.mean(xh * xh, axis=-1, keepdims=True) + EPS) * gain_row)
    return pieces


def _pair_rms(x, gain2, left):
    sq = x * x
    lo = jnp.sum(jnp.where(left, sq, 0.0), axis=-1, keepdims=True)
    hi = jnp.sum(jnp.where(left, 0.0, sq), axis=-1, keepdims=True)
    ms = jnp.where(left, lo, hi) * (1.0 / HEAD_DIM)
    return x * lax.rsqrt(ms + EPS) * gain2


def _swa_prompt_kernel(y_ref, kvnorm_ref, wkv_ref, knorm2_ref, normb_ref, win_ref, qnorm2_ref,
                       sinks_ref, wout_ref, out_ref, kwin_ref, vwin_ref, k_scr, v_scr, o_scr):
    tq = y_ref.shape[1]
    w = WINDOW
    step = pl.program_id(1)
    lane = lax.broadcasted_iota(jnp.int32, (1, 2 * HEAD_DIM), 1)
    left = lane < HEAD_DIM

    @pl.when(step == 0)
    def _():
        k_scr[:, 0:w, :] = jnp.zeros((N_KV_HEADS, w, 2 * HEAD_DIM), BF16)
        v_scr[:, 0:w, :] = jnp.zeros((N_KV_HEADS, w, 2 * HEAD_DIM), BF16)

    @pl.when(step > 0)
    def _():
        k_scr[:, 0:w, :] = k_scr[:, tq:tq + w, :]
        v_scr[:, 0:w, :] = v_scr[:, tq:tq + w, :]

    y = y_ref[0]
    yn = y * lax.rsqrt(jnp.mean(y * y, axis=-1, keepdims=True) + EPS)
    kv = _dot((yn * kvnorm_ref[...]).astype(BF16), wkv_ref[...])
    for c in range(N_KV_HEADS // 2):
        cols = slice(c * 2 * HEAD_DIM, (c + 1) * 2 * HEAD_DIM)
        kp = _pair_rms(kv[:, cols], knorm2_ref[...], left)
        vp = kv[:, KV_W + c * 2 * HEAD_DIM:KV_W + (c + 1) * 2 * HEAD_DIM]
        kwin_ref[0, :, cols] = kp[tq - w:, :]
        vwin_ref[0, :, cols] = vp[tq - w:, :]
        kr = pltpu.roll(kp, HEAD_DIM, 1)
        vr = pltpu.roll(vp, HEAD_DIM, 1)
        k_scr[2 * c, w:w + tq, :] = jnp.where(left, kp, kr).astype(BF16)
        k_scr[2 * c + 1, w:w + tq, :] = jnp.where(left, kr, kp).astype(BF16)
        v_scr[2 * c, w:w + tq, :] = jnp.where(left, vp, vr).astype(BF16)
        v_scr[2 * c + 1, w:w + tq, :] = jnp.where(left, vr, vp).astype(BF16)

    qg = _dot((yn * normb_ref[...]).astype(BF16), win_ref[...])
    q_pairs = [_pair_rms(qg[:, c * 2 * HEAD_DIM:(c + 1) * 2 * HEAD_DIM], qnorm2_ref[...], left)
               for c in range(N_Q_HEADS // 2)]

    kj = lax.broadcasted_iota(jnp.int32, (2 * w, w), 0)
    qi = lax.broadcasted_iota(jnp.int32, (2 * w, w), 1)
    dist = qi - kj + w
    band = (dist >= 0) & (dist <= w)
    j_rel = (kj - w).astype(F32)
    i_row = lax.broadcasted_iota(jnp.int32, (1, w), 1).astype(F32)

    def scores(blk, hk):
        q_rows = slice(blk * w, (blk + 1) * w)
        lhs = []
        for c in (2 * hk, 2 * hk + 1):
            qp = q_pairs[c][q_rows, :]
            lhs.append(jnp.where(left, qp, 0.0))
            lhs.append(jnp.where(left, 0.0, qp))
        return _dot_nt(k_scr[hk, blk * w:blk * w + 2 * w, :],
                       jnp.concatenate(lhs, axis=0).astype(BF16))

    def attend(blk, hk, s4):
        q_rows = slice(blk * w, (blk + 1) * w)
        valid = band & ((step > 0) | (kj >= w)) if blk == 0 else band
        ps = []
        for g in range(Q_GROUP):
            hq = hk * Q_GROUP + g
            slope = 2.0 ** (-8.0 * (hq + 1) / N_Q_HEADS)
            a = jnp.where(valid, s4[:, g * w:(g + 1) * w] + slope * j_rel, -jnp.inf)
            sink = sinks_ref[0:1, hq:hq + 1] + slope * i_row
            mx = jnp.maximum(jnp.max(a, axis=0, keepdims=True), sink)
            p = jnp.exp(a - mx)
            inv = 1.0 / (jnp.sum(p, axis=0, keepdims=True) + jnp.exp(sink - mx))
            ps.append((p * inv).astype(BF16))
        o4 = _dot_tn(jnp.concatenate(ps, axis=1),
                     v_scr[hk, blk * w:blk * w + 2 * w, :])
        for j, c in enumerate((2 * hk, 2 * hk + 1)):
            o_scr[q_rows, c * 2 * HEAD_DIM:(c + 1) * 2 * HEAD_DIM] = jnp.where(
                left, o4[(2 * j) * w:(2 * j + 1) * w, :], o4[(2 * j + 1) * w:(2 * j + 2) * w, :])

    items = [(blk, hk) for blk in range(tq // w) for hk in range(N_KV_HEADS)]
    s_next = scores(*items[0])
    for n, item in enumerate(items):
        s_cur = s_next
        if n + 1 < len(items):
            s_next = scores(*items[n + 1])
        attend(*item, s_cur)

    o = (o_scr[...] * _silu(qg[:, ATT_W:])).astype(BF16)
    out_ref[0] = y + _dot(o, wout_ref[...])


def _swa_prompt(y, kvnorm, wkv, knorm, normb, win, qnorm, sinks, wout):
    b, l, d = y.shape
    tq = TQ_B
    w = WINDOW
    f = lambda shape: jax.ShapeDtypeStruct(shape, F32)
    knorm2 = jnp.concatenate([knorm, knorm]).reshape(1, 2 * HEAD_DIM)
    qnorm2 = (jnp.concatenate([qnorm, qnorm]) * (HEAD_DIM ** -0.5)).reshape(1, 2 * HEAD_DIM)
    in_specs = [
        pl.BlockSpec((1, tq, d), lambda i, j: (i, j, 0)),
        _full_spec((1, d)), _full_spec(wkv.shape), _full_spec((1, 2 * HEAD_DIM)), _full_spec((1, d)),
        _full_spec(win.shape), _full_spec((1, 2 * HEAD_DIM)), _full_spec((1, N_Q_HEADS)),
        _full_spec(wout.shape),
    ]
    out_specs = [
        pl.BlockSpec((1, tq, d), lambda i, j: (i, j, 0)),
        pl.BlockSpec((1, w, KV_W), lambda i, j: (i, 0, 0)),
        pl.BlockSpec((1, w, KV_W), lambda i, j: (i, 0, 0)),
    ]
    return pl.pallas_call(
        _swa_prompt_kernel,
        grid=(b, l // tq), in_specs=in_specs, out_specs=out_specs,
        out_shape=[f((b, l, d)), f((b, w, KV_W)), f((b, w, KV_W))],
        scratch_shapes=[pltpu.VMEM((N_KV_HEADS, w + tq, 2 * HEAD_DIM), BF16),
                        pltpu.VMEM((N_KV_HEADS, w + tq, 2 * HEAD_DIM), BF16),
                        pltpu.VMEM((tq, ATT_W), F32)],
        compiler_params=pltpu.CompilerParams(
            dimension_semantics=("arbitrary", "arbitrary"), vmem_limit_bytes=VMEM_LIMIT),
        name="swa_prompt",
    )(y, kvnorm.reshape(1, d), wkv, knorm2, normb.reshape(1, d), win, qnorm2,
      sinks.reshape(1, N_Q_HEADS), wout)


def _swa_sample_front_kernel(y_ref, kvnorm_ref, wkv_ref, knorm_ref, normb_ref, win_ref, qnorm_ref,
                             k_ref, v_ref, q_ref, gate_ref):
    y = y_ref[...]
    yn = y * lax.rsqrt(jnp.mean(y * y, axis=-1, keepdims=True) + EPS)
    kv = _dot((yn * kvnorm_ref[...]).astype(BF16), wkv_ref[...])
    k_ref[...] = jnp.concatenate(_head_rms(kv[:, :KV_W], knorm_ref[...], N_KV_HEADS), axis=1)
    v_ref[...] = kv[:, KV_W:]
    qg = _dot((yn * normb_ref[...]).astype(BF16), win_ref[...])
    q_ref[...] = jnp.concatenate(
        _head_rms(qg[:, :ATT_W], qnorm_ref[...] * (HEAD_DIM ** -0.5), N_Q_HEADS), axis=1)
    gate_ref[...] = qg[:, ATT_W:]


def _swa_sample_attn_kernel(q_ref, kn_ref, vn_ref, kc_ref, vc_ref, sinks_ref, slopes_ref,
                            o_ref, kwin_ref, vwin_ref):
    bb = kc_ref.shape[0]
    w = kc_ref.shape[1]
    hrow = lax.broadcasted_iota(jnp.int32, (N_Q_HEADS, KV_W), 0) // Q_GROUP
    lblk = lax.broadcasted_iota(jnp.int32, (N_Q_HEADS, KV_W), 1) // HEAD_DIM
    own = hrow == lblk
    dist_c = (w - lax.broadcasted_iota(jnp.int32, (1, w), 1)).astype(F32)
    krow = lax.broadcasted_iota(jnp.int32, (w, KV_W), 0)
    slopes = slopes_ref[...]
    sink = sinks_ref[...]
    qms, scs = [], []
    for b in range(bb):
        q = q_ref[b]
        qm = jnp.where(own, jnp.concatenate([q] * N_KV_HEADS, axis=1), 0.0).astype(BF16)
        qms.append(qm)
        scs.append(_dot_nt(qm, kc_ref[b].astype(BF16)))
    pcs, pns = [], []
    for b in range(bb):
        s_c = scs[b] - slopes * dist_c
        s_n = jnp.sum(qms[b].astype(F32) * kn_ref[b].astype(BF16).astype(F32), axis=-1, keepdims=True)
        mx = jnp.maximum(jnp.maximum(jnp.max(s_c, axis=-1, keepdims=True), s_n), sink)
        p_c = jnp.exp(s_c - mx)
        p_n = jnp.exp(s_n - mx)
        den = jnp.sum(p_c, axis=-1, keepdims=True) + p_n + jnp.exp(sink - mx)
        pcs.append((p_c / den).astype(BF16))
        pns.append(p_n / den)
    rs = [_dot(pcs[b], vc_ref[b].astype(BF16)) for b in range(bb)]
    for b in range(bb):
        kn = kn_ref[b]
        vn = vn_ref[b]
        r = rs[b] + pns[b].astype(BF16).astype(F32) * vn.astype(BF16).astype(F32)
        r = jnp.where(own, r, 0.0)
        acc = r[:, 0:HEAD_DIM]
        for blk in range(1, N_KV_HEADS):
            acc = acc + r[:, blk * HEAD_DIM:(blk + 1) * HEAD_DIM]
        o_ref[b] = acc
        kwin_ref[b] = jnp.where(krow == w - 1, kn, pltpu.roll(kc_ref[b], w - 1, 0))
        vwin_ref[b] = jnp.where(krow == w - 1, vn, pltpu.roll(vc_ref[b], w - 1, 0))


def _swa_sample(y, cache_k, cache_v, kvnorm, wkv, knorm, normb, win, qnorm, sinks, wout):
    n, d = y.shape
    w = cache_k.shape[1]
    f = lambda shape: jax.ShapeDtypeStruct(shape, F32)
    kn, vn, q, gate = pl.pallas_call(
        _swa_sample_front_kernel,
        out_shape=[f((n, KV_W)), f((n, KV_W)), f((n, ATT_W)), f((n, ATT_W))],
        compiler_params=pltpu.CompilerParams(vmem_limit_bytes=VMEM_LIMIT),
        name="swa_sample_front",
    )(y, kvnorm.reshape(1, d), wkv, knorm.reshape(1, HEAD_DIM), normb.reshape(1, d), win,
      qnorm.reshape(1, HEAD_DIM))
    bb = 8
    slopes = (2.0 ** (-8.0 * jnp.arange(1, N_Q_HEADS + 1, dtype=F32) / N_Q_HEADS)).reshape(N_Q_HEADS, 1)
    spec3 = lambda a, c: pl.BlockSpec((bb, a, c), lambda i: (i, 0, 0))
    o, kwin, vwin = pl.pallas_call(
        _swa_sample_attn_kernel,
        grid=(n // bb,),
        in_specs=[spec3(N_Q_HEADS, HEAD_DIM), spec3(1, KV_W), spec3(1, KV_W), spec3(w, KV_W),
                  spec3(w, KV_W), _full_spec((N_Q_HEADS, 1)), _full_spec((N_Q_HEADS, 1))],
        out_specs=[spec3(N_Q_HEADS, HEAD_DIM), spec3(w, KV_W), spec3(w, KV_W)],
        out_shape=[f((n, N_Q_HEADS, HEAD_DIM)), f((n, w, KV_W)), f((n, w, KV_W))],
        compiler_params=pltpu.CompilerParams(dimension_semantics=("arbitrary",)),
        name="swa_sample_attn",
    )(q.reshape(n, N_Q_HEADS, HEAD_DIM), kn.reshape(n, 1, KV_W), vn.reshape(n, 1, KV_W),
      cache_k.reshape(n, w, KV_W), cache_v.reshape(n, w, KV_W),
      sinks.reshape(N_Q_HEADS, 1), slopes)
    out = pl.pallas_call(
        functools.partial(_out_proj_kernel, head_w=ATT_W, norm_heads=False),
        out_shape=f((n, d)),
        name="swa_sample_out",
    )(y, o.reshape(n, ATT_W), gate, jnp.ones((1, ATT_W), F32), wout)
    return out, kwin, vwin


def kernel(x_prompt, x_sample, state_conv, state_ssm, cache_k_win, cache_v_win, norm_a, w_in_a, conv_w_a, a_log, dt_bias, o_norm_a, w_out_a, kv_norm, w_kv, k_norm, norm_b, w_in_b, q_norm, sinks, w_out_b):
    n_a = w_in_a.shape[0]
    n_b = w_in_b.shape[0]
    assert n_a == 1 and n_b == 1, "kernel is written for DEPTH == 2"
    bp, lp, d = x_prompt.shape
    n = x_sample.shape[0]
    hw = GDN_HEADS * GDN_D

    hp, hs = x_prompt, x_sample.reshape(n, d)
    conv_p, ssm_p, conv_s, ssm_s = [], [], [], []
    for layer in range(n_a):
        win_a = w_in_a[layer].astype(BF16)
        wabt = win_a[:, QKV_W + hw:].T
        wout = w_out_a[layer].astype(BF16)
        hp, cbuf, st = _gdn_prompt(hp, norm_a[layer], win_a, wabt, conv_w_a[layer],
                                   a_log[layer], dt_bias[layer], o_norm_a[layer], wout)
        conv_p.append(cbuf)
        ssm_p.append(st)
        hs, cbuf, st = _gdn_sample(hs, state_conv[layer], state_ssm[layer], norm_a[layer], win_a,
                                   conv_w_a[layer], a_log[layer], dt_bias[layer],
                                   o_norm_a[layer], wout)
        conv_s.append(cbuf)
        ssm_s.append(st)

    wkv = w_kv.astype(BF16)
    win = w_in_b[0].astype(BF16)
    woutb = w_out_b[0].astype(BF16)
    hp, k_win_p, v_win_p = _swa_prompt(hp, kv_norm, wkv, k_norm, norm_b[0], win, q_norm[0], sinks[0], woutb)
    hs, k_win_s, v_win_s = _swa_sample(hs, cache_k_win, cache_v_win, kv_norm, wkv, k_norm, norm_b[0],
                                       win, q_norm[0], sinks[0], woutb)
    kv_shape = (N_KV_HEADS, HEAD_DIM)
    return (hp, hs.reshape(n, 1, d), jnp.stack(conv_p), jnp.stack(ssm_p),
            k_win_p.reshape(bp, WINDOW, *kv_shape), v_win_p.reshape(bp, WINDOW, *kv_shape),
            jnp.stack(conv_s), jnp.stack(ssm_s),
            k_win_s.reshape(n, WINDOW, *kv_shape), v_win_s.reshape(n, WINDOW, *kv_shape))
```

```python
import functools

import jax
import jax.numpy as jnp
from jax import lax
from jax.experimental import pallas as pl
from jax.experimental.pallas import tpu as pltpu

F32 = jnp.float32
BF16 = jnp.bfloat16
EPS = 1e-6

D_MODEL = 1024
GDN_HEADS = 8
GDN_D = 128
QKV_W = 3 * GDN_HEADS * GDN_D
CONV_W = 4
CHUNK = 64
N_Q_HEADS = 16
N_KV_HEADS = 4
Q_GROUP = N_Q_HEADS // N_KV_HEADS
HEAD_DIM = 64
KV_W = N_KV_HEADS * HEAD_DIM
ATT_W = N_Q_HEADS * HEAD_DIM
WINDOW = 128

TM_A = 256
TQ_B = 256
HEADS_PER_ITER = 4
CARRY = 8
VMEM_LIMIT = 52 * 1024 * 1024

_NT = (((1,), (1,)), ((), ()))
_TN = (((0,), (0,)), ((), ()))


def _dot(a, b):
    return jnp.dot(a, b, preferred_element_type=F32)


def _dot_nt(a, b):
    return lax.dot_general(a, b, _NT, preferred_element_type=F32)


def _dot_tn(a, b):
    return lax.dot_general(a, b, _TN, preferred_element_type=F32)


def _split(x):
    hi = x.astype(BF16)
    lo = (x - hi.astype(F32)).astype(BF16)
    return hi, lo


def _dot_exact_lhs(a_bf, b):
    b0 = b.astype(BF16)
    r1 = b - b0.astype(F32)
    b1 = r1.astype(BF16)
    b2 = (r1 - b1.astype(F32)).astype(BF16)
    return (_dot(a_bf, b2) + _dot(a_bf, b1)) + _dot(a_bf, b0)


def _dot_exact_rhs(a, b_bf):
    a0 = a.astype(BF16)
    r1 = a - a0.astype(F32)
    a1 = r1.astype(BF16)
    a2 = (r1 - a1.astype(F32)).astype(BF16)
    return (_dot(a2, b_bf) + _dot(a1, b_bf)) + _dot(a0, b_bf)


def _silu(x):
    return x * jax.nn.sigmoid(x)


def _softplus(x):
    return jnp.maximum(x, 0.0) + jnp.log1p(jnp.exp(-jnp.abs(x)))


def _halves(x, left):
    zero = jnp.zeros_like(x)
    return jnp.where(left, x, zero), jnp.where(left, zero, x)


def _block_diag(x, left):
    return jnp.concatenate(_halves(x, left), axis=0)


def _pair_split_product(a, b, left):
    ah, al = _split(a)
    bh, bl = _split(b)
    bh1, bh2 = _halves(bh, left)
    bl1, bl2 = _halves(bl, left)
    zero = jnp.zeros_like(bh)
    rhs = jnp.concatenate([jnp.concatenate([bh1, bl1], axis=1), jnp.concatenate([bh2, bl2], axis=1),
                           jnp.concatenate([bh1, zero], axis=1), jnp.concatenate([bh2, zero], axis=1)], axis=0)
    y = _dot(jnp.concatenate([ah, al], axis=1), rhs)
    half = y.shape[1] // 2
    return y[:, :half] + y[:, half:]


def _unit_lower_inverse_pairs(ms, left):
    n = ms[0].shape[0]
    row = lax.broadcasted_iota(jnp.int32, (n, 2 * n), 0)
    col = lax.broadcasted_iota(jnp.int32, (n, 2 * n), 1) % n
    eye2 = jnp.where(row == col, 1.0, 0.0).astype(F32)

    def times(a, b):
        return _dot(a.astype(BF16), _block_diag(b.astype(BF16), left))

    ts = [eye2 - m for m in ms]
    ps = [times(m, m) for m in ms]
    steps = max(1, (n - 1).bit_length()) - 1
    for _ in range(steps - 1):
        both = [_dot(jnp.concatenate([t.astype(BF16), p.astype(BF16)], axis=0), _block_diag(p.astype(BF16), left))
                for t, p in zip(ts, ps)]
        ts = [t + tp[:n, :] for t, tp in zip(ts, both)]
        ps = [tp[n:, :] for tp in both]
    ts = [t + times(t, p) for t, p in zip(ts, ps)]
    rs = [(eye2 - t) - _pair_split_product(m, t, left) for m, t in zip(ms, ts)]
    return [t + times(t, r) for t, r in zip(ts, rs)]


def _full_spec(shape):
    nd = len(shape)
    return pl.BlockSpec(shape, lambda *_: (0,) * nd)


def _gdn_gates(ab, ab_t, alog_r, dtb_r, alog_c, dtb_c, tm, chunk):
    h = GDN_HEADS
    g_c = -jnp.exp(alog_r) * _softplus(ab[:, :h] + dtb_r)
    beta = jax.nn.sigmoid(ab[:, h:])
    g_r = -jnp.exp(alog_c) * _softplus(ab_t[:h, :] + dtb_c)
    row = lax.broadcasted_iota(jnp.int32, (tm, tm), 0)
    col = lax.broadcasted_iota(jnp.int32, (tm, tm), 1)
    same = (row // chunk) == (col // chunk)
    lower = jnp.where(same & (row >= col), 1.0, 0.0).astype(BF16)
    upper = jnp.where(same & (row <= col), 1.0, 0.0).astype(BF16)
    gc = _dot_exact_lhs(lower, g_c)
    gr = _dot_exact_rhs(g_r, upper)
    return beta, gc, gr


def _gdn_prompt_kernel(x_ref, norm_ref, win_ref, wabt_ref, convw_ref,
                       alog_r_ref, dtb_r_ref, alog_c_ref, dtb_c_ref, onorm_ref, wout_ref,
                       y_ref, conv_ref, ssm_ref,
                       pre_scr, gate_scr, s_scr, k_scr, kb_scr, q_scr, qd_scr, kd_scr,
                       rhs_scr, dec_scr, gt_scr, o_scr, lhs_scr, su_scr):
    tm = x_ref.shape[1]
    n_chunks = tm // CHUNK
    l = pl.program_id(1)

    @pl.when(l == 0)
    def _():
        pre_scr[0:CARRY, :] = jnp.zeros((CARRY, QKV_W), F32)
        s_scr[...] = jnp.zeros(s_scr.shape, F32)

    x = x_ref[0]
    xn = x * lax.rsqrt(jnp.mean(x * x, axis=-1, keepdims=True) + EPS) * norm_ref[...]
    xb = xn.astype(BF16)
    gate_w = GDN_HEADS * GDN_D
    ab = _dot(xb, win_ref[:, QKV_W + gate_w:])
    ab_t = _dot_nt(wabt_ref[...], xb)
    beta, gc, gr = _gdn_gates(ab, ab_t, alog_r_ref[...], dtb_r_ref[...],
                              alog_c_ref[...], dtb_c_ref[...], tm, CHUNK)
    e_g = jnp.exp(gc)
    g_last = jnp.concatenate(
        [jnp.broadcast_to(gc[c * CHUNK + CHUNK - 1:c * CHUNK + CHUNK, :], (CHUNK, GDN_HEADS))
         for c in range(n_chunks)], axis=0)
    e_kd = jnp.exp(g_last - gc)
    e_tot = jnp.exp(g_last)

    ci = lax.broadcasted_iota(jnp.int32, (CHUNK, 2 * CHUNK), 0)
    cj = lax.broadcasted_iota(jnp.int32, (CHUNK, 2 * CHUNK), 1) % CHUNK
    causal = ci >= cj
    strict = ci > cj
    left = lax.broadcasted_iota(jnp.int32, (1, 2 * CHUNK), 1) < CHUNK

    def conv_act(j):
        cols = slice(j * GDN_D, (j + 1) * GDN_D)
        ext = pre_scr[:, cols]
        half = ext[CARRY:, :] * (0.5 * convw_ref[CONV_W - 1:CONV_W, cols])
        for back in range(1, CONV_W):
            tap = CONV_W - 1 - back
            half = half + pltpu.roll(ext, back, 0)[CARRY:, :] * (0.5 * convw_ref[tap:tap + 1, cols])
        return half + half * jnp.tanh(half)

    def head_prep(h):
        qh = conv_act(h)
        kh = conv_act(GDN_HEADS + h)
        vh = conv_act(2 * GDN_HEADS + h)
        qn = qh * lax.rsqrt(jnp.sum(qh * qh, axis=-1, keepdims=True) + EPS) * (GDN_D ** -0.5)
        kn = kh * lax.rsqrt(jnp.sum(kh * kh, axis=-1, keepdims=True) + EPS)
        b_col = beta[:, h:h + 1]
        eg_col = e_g[:, h:h + 1]
        kb = kn * b_col
        k_scr[h] = kn.astype(BF16)
        kb_scr[h] = kb.astype(BF16)
        q_scr[h] = qn.astype(BF16)
        qd_scr[h] = qn * eg_col
        kd_scr[h] = (kn * e_kd[:, h:h + 1]).astype(BF16)
        rhs_scr[h, :, 0:GDN_D] = (vh * b_col).astype(BF16)
        rhs_scr[h, :, GDN_D:2 * GDN_D] = (kb * eg_col).astype(BF16)
        for c in range(n_chunks):
            gt_scr[h, c:c + 1, :] = jnp.broadcast_to(e_tot[c * CHUNK:c * CHUNK + 1, h:h + 1], (1, GDN_D))
        for j in range(n_chunks // 2):
            r1 = slice(2 * j * CHUNK, (2 * j + 1) * CHUNK)
            r2 = slice((2 * j + 1) * CHUNK, (2 * j + 2) * CHUNK)
            diff = jnp.where(left, gc[r1, h:h + 1], gc[r2, h:h + 1]) - gr[h:h + 1, 2 * j * CHUNK:(2 * j + 2) * CHUNK]
            dec_scr[h, j * CHUNK:(j + 1) * CHUNK, :] = jnp.exp(jnp.where(causal, diff, -jnp.inf))

    def chunk_terms(heads):
        items = [(h, j) for h in heads for j in range(n_chunks // 2)]
        aqs = []
        for h, j in items:
            r12 = slice(2 * j * CHUNK, (2 * j + 2) * CHUNK)
            aqs.append(_dot_nt(jnp.concatenate([kb_scr[h, r12, :], q_scr[h, r12, :]], axis=0), k_scr[h, r12, :]))
        decs = [dec_scr[h, j * CHUNK:(j + 1) * CHUNK, :] for h, j in items]
        ms = [jnp.where(strict, jnp.where(left, aq[:CHUNK, :], aq[CHUNK:2 * CHUNK, :]) * d, 0.0)
              for aq, d in zip(aqs, decs)]
        qks = [jnp.where(causal, jnp.where(left, aq[2 * CHUNK:3 * CHUNK, :], aq[3 * CHUNK:, :]) * d, 0.0)
               for aq, d in zip(aqs, decs)]
        ts = _unit_lower_inverse_pairs(ms, left)
        sols = []
        for (h, j), t, qk in zip(items, ts, qks):
            lo = t - t.astype(BF16).astype(F32)
            t_parts = (jnp.where(left, t, pltpu.roll(lo, CHUNK, 1)).astype(BF16),
                       jnp.where(left, pltpu.roll(t, CHUNK, 1), lo).astype(BF16))
            qk_parts = (qk[:, :CHUNK].astype(BF16), pltpu.roll(qk, CHUNK, 1)[:, :CHUNK].astype(BF16))
            for half in range(2):
                c = 2 * j + half
                r = slice(c * CHUNK, (c + 1) * CHUNK)
                rhs = rhs_scr[h, r, :]
                sols.append((h, c, r, qk_parts[half],
                             _dot(t_parts[half], jnp.concatenate([rhs, rhs], axis=0)).astype(BF16)))
        outs = [(h, c, r, _dot_tn(kd_scr[h, r, :], sol),
                 _dot(qk_c, sol))
                for h, c, r, qk_c, sol in sols]
        for h, c, r, kd_uw, qk_uw in outs:
            lhs_scr[h, c, 0:GDN_D, :] = kd_uw[:, GDN_D:].astype(BF16)
            lhs_scr[h, c, GDN_D:GDN_D + CHUNK, :] = (qd_scr[h, r, :] - qk_uw[:, GDN_D:]).astype(BF16)
            su_scr[h, c] = kd_uw[:, :GDN_D]
            o_scr[h, r, :] = qk_uw[:, :GDN_D]

    def project(heads):
        for part in range(3):
            cols = slice((part * GDN_HEADS + heads[0]) * GDN_D, (part * GDN_HEADS + heads[-1] + 1) * GDN_D)
            pre_scr[CARRY:CARRY + tm, cols] = _dot(xb, win_ref[:, cols])

    groups = [list(range(i, i + HEADS_PER_ITER)) for i in range(0, GDN_HEADS, HEADS_PER_ITER)]
    n_groups = len(groups)
    for n in range(-2, n_groups):
        if 0 <= n + 2 < n_groups:
            project(groups[n + 2])
        if 0 <= n + 1 < n_groups:
            for h in groups[n + 1]:
                head_prep(h)
        if n == n_groups - 1:
            gate_scr[...] = _dot(xb, win_ref[:, QKV_W:QKV_W + gate_w])
        if n >= 0:
            chunk_terms(groups[n])

    conv_ref[0] = pre_scr[tm + CARRY - 3:tm + CARRY, :]
    pre_scr[0:CARRY, :] = pre_scr[tm:tm + CARRY, :]

    for c in range(n_chunks):
        rows = slice(c * CHUNK, (c + 1) * CHUNK)
        for h in range(GDN_HEADS):
            s = s_scr[h]
            prod = _dot(lhs_scr[h, c], s.astype(BF16))
            o_scr[h, rows, :] = o_scr[h, rows, :] + prod[GDN_D:, :]
            s_scr[h] = s * gt_scr[h, c:c + 1, :] + (su_scr[h, c] - prod[:GDN_D, :])

    @pl.when(l == pl.num_programs(1) - 1)
    def _():
        ssm_ref[0] = s_scr[...]

    pieces = []
    for h in range(GDN_HEADS):
        o = o_scr[h]
        on = o * lax.rsqrt(jnp.mean(o * o, axis=-1, keepdims=True) + EPS) * onorm_ref[...]
        pieces.append((on * _silu(gate_scr[:, h * GDN_D:(h + 1) * GDN_D])).astype(BF16))
    y_ref[0] = x + _dot(jnp.concatenate(pieces, axis=1), wout_ref[...])


def _gdn_prompt(x, norm, win, wabt, convw, alog, dtb, onorm, wout):
    b, l, d = x.shape
    tm = TM_A
    h = GDN_HEADS
    grid = (b, l // tm)
    in_specs = [
        pl.BlockSpec((1, tm, d), lambda i, j: (i, j, 0)),
        _full_spec((1, d)), _full_spec(win.shape), _full_spec(wabt.shape), _full_spec(convw.shape),
        _full_spec((1, h)), _full_spec((1, h)), _full_spec((h, 1)), _full_spec((h, 1)),
        _full_spec((1, GDN_D)), _full_spec(wout.shape),
    ]
    out_specs = [
        pl.BlockSpec((1, tm, d), lambda i, j: (i, j, 0)),
        pl.BlockSpec((1, CONV_W - 1, QKV_W), lambda i, j: (i, 0, 0)),
        pl.BlockSpec((1, h, GDN_D, GDN_D), lambda i, j: (i, 0, 0, 0)),
    ]
    out_shape = [
        jax.ShapeDtypeStruct((b, l, d), F32),
        jax.ShapeDtypeStruct((b, CONV_W - 1, QKV_W), F32),
        jax.ShapeDtypeStruct((b, h, GDN_D, GDN_D), F32),
    ]
    scratch = [
        pltpu.VMEM((tm + CARRY, QKV_W), F32),
        pltpu.VMEM((tm, h * GDN_D), F32),
        pltpu.VMEM((h, GDN_D, GDN_D), F32),
        pltpu.VMEM((h, tm, GDN_D), BF16),
        pltpu.VMEM((h, tm, GDN_D), BF16),
        pltpu.VMEM((h, tm, GDN_D), BF16),
        pltpu.VMEM((h, tm, GDN_D), F32),
        pltpu.VMEM((h, tm, GDN_D), BF16),
        pltpu.VMEM((h, tm, 2 * GDN_D), BF16),
        pltpu.VMEM((h, tm // 2, 2 * CHUNK), F32),
        pltpu.VMEM((h, 8, GDN_D), F32),
        pltpu.VMEM((h, tm, GDN_D), F32),
        pltpu.VMEM((h, tm // CHUNK, GDN_D + CHUNK, GDN_D), BF16),
        pltpu.VMEM((h, tm // CHUNK, GDN_D, GDN_D), F32),
    ]
    return pl.pallas_call(
        _gdn_prompt_kernel,
        grid=grid, in_specs=in_specs, out_specs=out_specs, out_shape=out_shape,
        scratch_shapes=scratch,
        compiler_params=pltpu.CompilerParams(
            dimension_semantics=("arbitrary", "arbitrary"), vmem_limit_bytes=VMEM_LIMIT),
        name="gdn_prompt",
    )(x, norm.reshape(1, d), win, wabt, convw,
      alog.reshape(1, h), dtb.reshape(1, h), alog.reshape(h, 1), dtb.reshape(h, 1),
      onorm.reshape(1, GDN_D), wout)


def _gdn_sample_front_kernel(x_ref, norm_ref, win_ref, convw_ref,
                             c0_ref, c1_ref, c2_ref, alog_r_ref, dtb_r_ref,
                             pre_ref, q_ref, k_ref, u_ref, w_ref, qd_ref, gate_ref, eg_ref):
    x = x_ref[...]
    xn = x * lax.rsqrt(jnp.mean(x * x, axis=-1, keepdims=True) + EPS) * norm_ref[...]
    xb = xn.astype(BF16)
    gate_w = GDN_HEADS * GDN_D
    pre = _dot(xb, win_ref[:, :QKV_W])
    pre_ref[...] = pre
    gate_ref[...] = _dot(xb, win_ref[:, QKV_W:QKV_W + gate_w])
    ab = _dot(xb, win_ref[:, QKV_W + gate_w:])
    g = -jnp.exp(alog_r_ref[...]) * _softplus(ab[:, :GDN_HEADS] + dtb_r_ref[...])
    beta = jax.nn.sigmoid(ab[:, GDN_HEADS:])
    e_g = jnp.exp(g)
    eg_ref[...] = e_g
    act = _silu(((c0_ref[...] * convw_ref[0:1, :] + c1_ref[...] * convw_ref[1:2, :])
                 + c2_ref[...] * convw_ref[2:3, :]) + pre * convw_ref[3:4, :])
    w_all = GDN_HEADS * GDN_D
    for h in range(GDN_HEADS):
        cols = slice(h * GDN_D, (h + 1) * GDN_D)
        qh = act[:, h * GDN_D:(h + 1) * GDN_D]
        kh = act[:, w_all + h * GDN_D:w_all + (h + 1) * GDN_D]
        vh = act[:, 2 * w_all + h * GDN_D:2 * w_all + (h + 1) * GDN_D]
        qn = qh * lax.rsqrt(jnp.sum(qh * qh, axis=-1, keepdims=True) + EPS) * (GDN_D ** -0.5)
        kn = kh * lax.rsqrt(jnp.sum(kh * kh, axis=-1, keepdims=True) + EPS)
        b_col = beta[:, h:h + 1]
        eg_col = e_g[:, h:h + 1]
        q_ref[:, cols] = qn
        k_ref[:, cols] = kn
        u_ref[:, cols] = vh * b_col
        w_ref[:, cols] = kn * b_col * eg_col
        qd_ref[:, cols] = qn * eg_col


def _gdn_sample_state_kernel(q_ref, k_ref, u_ref, w_ref, qd_ref, eg_ref, s_ref, o_ref, s_out_ref):
    bb = s_ref.shape[0]
    row = lax.broadcasted_iota(jnp.int32, (8, GDN_D), 0)
    pairs = [(b, h) for b in range(bb) for h in range(GDN_HEADS)]
    prods = []
    for b, h in pairs:
        cols = slice(h * GDN_D, (h + 1) * GDN_D)
        lhs = jnp.where(row == 0, w_ref[b:b + 1, cols], jnp.where(row == 1, qd_ref[b:b + 1, cols], 0.0))
        prods.append(_dot(lhs.astype(BF16), s_ref[b, h].astype(BF16)))
    for (b, h), prod in zip(pairs, prods):
        cols = slice(h * GDN_D, (h + 1) * GDN_D)
        q = q_ref[b:b + 1, cols].astype(BF16).astype(F32)
        k = k_ref[b:b + 1, cols].astype(BF16).astype(F32)
        v_new = u_ref[b:b + 1, cols] - prod[0:1, :]
        vb = v_new.astype(BF16).astype(F32)
        qk = jnp.sum(q * k, axis=-1, keepdims=True)
        o_ref[b:b + 1, cols] = prod[1:2, :] + qk.astype(BF16).astype(F32) * vb
        k8 = jnp.where(row == 0, k, 0.0).astype(BF16)
        v8 = jnp.where(row == 0, vb, 0.0).astype(BF16)
        s_out_ref[b, h] = s_ref[b, h] * eg_ref[b:b + 1, h:h + 1] + _dot_tn(k8, v8)


def _out_proj_kernel(x_ref, o_ref, gate_ref, onorm_ref, wout_ref, y_ref, *, head_w, norm_heads):
    pieces = []
    for h in range(x_ref.shape[1] // head_w):
        cols = slice(h * head_w, (h + 1) * head_w)
        o = o_ref[:, cols]
        if norm_heads:
            o = o * lax.rsqrt(jnp.mean(o * o, axis=-1, keepdims=True) + EPS) * onorm_ref[...]
        pieces.append((o * _silu(gate_ref[:, cols])).astype(BF16))
    y_ref[...] = x_ref[...] + _dot(jnp.concatenate(pieces, axis=1), wout_ref[...])


def _gdn_sample(x, conv_state, ssm_state, norm, win, convw, alog, dtb, onorm, wout):
    n, d = x.shape
    h = GDN_HEADS
    hw = h * GDN_D
    conv_t = jnp.transpose(conv_state, (1, 0, 2))
    f = lambda shape: jax.ShapeDtypeStruct(shape, F32)
    pre, q, k, u, w, qd, gate, eg = pl.pallas_call(
        _gdn_sample_front_kernel,
        out_shape=[f((n, QKV_W)), f((n, hw)), f((n, hw)), f((n, hw)), f((n, hw)), f((n, hw)),
                   f((n, hw)), f((n, h))],
        compiler_params=pltpu.CompilerParams(vmem_limit_bytes=VMEM_LIMIT),
        name="gdn_sample_front",
    )(x, norm.reshape(1, d), win, convw, conv_t[0], conv_t[1], conv_t[2],
      alog.reshape(1, h), dtb.reshape(1, h))
    bb = 8
    row_spec = lambda width: pl.BlockSpec((bb, width), lambda i: (i, 0))
    st_spec = pl.BlockSpec((bb, h, GDN_D, GDN_D), lambda i: (i, 0, 0, 0))
    o, s_new = pl.pallas_call(
        _gdn_sample_state_kernel,
        grid=(n // bb,),
        in_specs=[row_spec(hw)] * 5 + [row_spec(h), st_spec],
        out_specs=[row_spec(hw), st_spec],
        out_shape=[f((n, hw)), f(ssm_state.shape)],
        compiler_params=pltpu.CompilerParams(dimension_semantics=("arbitrary",)),
        name="gdn_sample_state",
    )(q, k, u, w, qd, eg, ssm_state)
    y = pl.pallas_call(
        functools.partial(_out_proj_kernel, head_w=GDN_D, norm_heads=True),
        out_shape=f((n, d)),
        name="gdn_sample_out",
    )(x, o, gate, onorm.reshape(1, GDN_D), wout)
    conv_new = jnp.stack([conv_t[1], conv_t[2], pre], axis=1)
    return y, conv_new, s_new


def _head_rms(x, gain_row, n_heads):
    pieces = []
    for h in range(n_heads):
        xh = x[:, h * HEAD_DIM:(h + 1) * HEAD_DIM]
        pieces.append(xh * lax.rsqrt(jnp.mean(xh * xh, axis=-1, keepdims=True) + EPS) * gain_row)
    return pieces


def _pair_rms(x, gain2, left):
    sq = x * x
    lo = jnp.sum(jnp.where(left, sq, 0.0), axis=-1, keepdims=True)
    hi = jnp.sum(jnp.where(left, 0.0, sq), axis=-1, keepdims=True)
    ms = jnp.where(left, lo, hi) * (1.0 / HEAD_DIM)
    return x * lax.rsqrt(ms + EPS) * gain2


def _swa_prompt_kernel(ycur_ref, yprev_ref, kvnorm_ref, wkv_ref, knorm2_ref, normb_ref, win_ref, qnorm2_ref,
                       sinks_ref, wout_ref, out_ref, kwin_ref, vwin_ref,
                       k_scr, v_scr, q_scr, gate_scr, o_scr):
    tq = ycur_ref.shape[1]
    w = WINDOW
    step = pl.program_id(1)
    lane = lax.broadcasted_iota(jnp.int32, (1, 2 * HEAD_DIM), 1)
    left = lane < HEAD_DIM
    n_pairs = N_Q_HEADS // 2

    @pl.when(step == 0)
    def _():
        k_scr[:, 0:w + tq, :] = jnp.zeros((N_KV_HEADS, w + tq, 2 * HEAD_DIM), BF16)
        v_scr[:, 0:w + tq, :] = jnp.zeros((N_KV_HEADS, w + tq, 2 * HEAD_DIM), BF16)
        q_scr[...] = jnp.zeros(q_scr.shape, F32)
        gate_scr[...] = jnp.zeros(gate_scr.shape, F32)

    @pl.when(step > 0)
    def _():
        k_scr[:, 0:w + tq, :] = k_scr[:, tq:w + 2 * tq, :]
        v_scr[:, 0:w + tq, :] = v_scr[:, tq:w + 2 * tq, :]

    q_pairs = [q_scr[:, c * 2 * HEAD_DIM:(c + 1) * 2 * HEAD_DIM] for c in range(n_pairs)]
    gate_prev = gate_scr[...]
    y_prev = yprev_ref[0]

    y = ycur_ref[0]
    yn = y * lax.rsqrt(jnp.mean(y * y, axis=-1, keepdims=True) + EPS)
    y_q = (yn * normb_ref[...]).astype(BF16)

    def project_kv():
        kv = _dot((yn * kvnorm_ref[...]).astype(BF16), wkv_ref[...])
        for c in range(N_KV_HEADS // 2):
            cols = slice(c * 2 * HEAD_DIM, (c + 1) * 2 * HEAD_DIM)
            kp = _pair_rms(kv[:, cols], knorm2_ref[...], left)
            vp = kv[:, KV_W + c * 2 * HEAD_DIM:KV_W + (c + 1) * 2 * HEAD_DIM]
            kwin_ref[0, :, cols] = kp[tq - w:, :]
            vwin_ref[0, :, cols] = vp[tq - w:, :]
            kr = pltpu.roll(kp, HEAD_DIM, 1)
            vr = pltpu.roll(vp, HEAD_DIM, 1)
            rows = slice(w + tq, w + 2 * tq)
            k_scr[2 * c, rows, :] = jnp.where(left, kp, kr).astype(BF16)
            k_scr[2 * c + 1, rows, :] = jnp.where(left, kr, kp).astype(BF16)
            v_scr[2 * c, rows, :] = jnp.where(left, vp, vr).astype(BF16)
            v_scr[2 * c + 1, rows, :] = jnp.where(left, vr, vp).astype(BF16)

    def project_q(piece):
        cols = slice(piece * 4 * HEAD_DIM, (piece + 1) * 4 * HEAD_DIM)
        part = _dot(y_q, win_ref[:, cols])
        if piece < ATT_W // (4 * HEAD_DIM):
            for half in range(2):
                c = 2 * piece + half
                q_scr[:, c * 2 * HEAD_DIM:(c + 1) * 2 * HEAD_DIM] = _pair_rms(
                    part[:, half * 2 * HEAD_DIM:(half + 1) * 2 * HEAD_DIM], qnorm2_ref[...], left)
        else:
            gate_scr[:, piece * 4 * HEAD_DIM - ATT_W:(piece + 1) * 4 * HEAD_DIM - ATT_W] = part

    kj = lax.broadcasted_iota(jnp.int32, (2 * w, w), 0)
    qi = lax.broadcasted_iota(jnp.int32, (2 * w, w), 1)
    dist = qi - kj + w
    band = (dist >= 0) & (dist <= w)
    j_rel = (kj - w).astype(F32)
    i_row = lax.broadcasted_iota(jnp.int32, (1, w), 1).astype(F32)

    def scores(blk, hk):
        q_rows = slice(blk * w, (blk + 1) * w)
        lhs = []
        for c in (2 * hk, 2 * hk + 1):
            qp = q_pairs[c][q_rows, :]
            lhs.append(jnp.where(left, qp, 0.0))
            lhs.append(jnp.where(left, 0.0, qp))
        return _dot_nt(k_scr[hk, blk * w:blk * w + 2 * w, :],
                       jnp.concatenate(lhs, axis=0).astype(BF16))

    def attend(blk, hk, s4):
        q_rows = slice(blk * w, (blk + 1) * w)
        valid = band & ((step > 1) | (kj >= w)) if blk == 0 else band
        ps = []
        for g in range(Q_GROUP):
            hq = hk * Q_GROUP + g
            slope = 2.0 ** (-8.0 * (hq + 1) / N_Q_HEADS)
            a = jnp.where(valid, s4[:, g * w:(g + 1) * w] + slope * j_rel, -jnp.inf)
            sink = sinks_ref[0:1, hq:hq + 1] + slope * i_row
            mx = jnp.maximum(jnp.max(a, axis=0, keepdims=True), sink)
            p = jnp.exp(a - mx)
            inv = 1.0 / (jnp.sum(p, axis=0, keepdims=True) + jnp.exp(sink - mx))
            ps.append((p * inv).astype(BF16))
        o4 = _dot_tn(jnp.concatenate(ps, axis=1),
                     v_scr[hk, blk * w:blk * w + 2 * w, :])
        for j, c in enumerate((2 * hk, 2 * hk + 1)):
            o_scr[q_rows, c * 2 * HEAD_DIM:(c + 1) * 2 * HEAD_DIM] = jnp.where(
                left, o4[(2 * j) * w:(2 * j + 1) * w, :], o4[(2 * j + 1) * w:(2 * j + 2) * w, :])

    items = [(blk, hk) for blk in range(tq // w) for hk in range(N_KV_HEADS)]
    assert len(items) == 2 * ATT_W // (4 * HEAD_DIM)
    s_next = scores(*items[0])
    project_kv()
    for n, item in enumerate(items):
        s_cur = s_next
        if n + 1 < len(items):
            s_next = scores(*items[n + 1])
        project_q(n)
        attend(*item, s_cur)

    o = (o_scr[...] * _silu(gate_prev)).astype(BF16)
    out_ref[0] = y_prev + _dot(o, wout_ref[...])


def _swa_prompt(y, kvnorm, wkv, knorm, normb, win, qnorm, sinks, wout):
    b, l, d = y.shape
    tq = TQ_B
    w = WINDOW
    n_tiles = l // tq
    f = lambda shape: jax.ShapeDtypeStruct(shape, F32)
    knorm2 = jnp.concatenate([knorm, knorm]).reshape(1, 2 * HEAD_DIM)
    qnorm2 = (jnp.concatenate([qnorm, qnorm]) * (HEAD_DIM ** -0.5)).reshape(1, 2 * HEAD_DIM)
    lagged = lambda i, j: (i, jnp.maximum(j - 1, 0), 0)
    in_specs = [
        pl.BlockSpec((1, tq, d), lambda i, j: (i, jnp.minimum(j, n_tiles - 1), 0)),
        pl.BlockSpec((1, tq, d), lagged),
        _full_spec((1, d)), _full_spec(wkv.shape), _full_spec((1, 2 * HEAD_DIM)), _full_spec((1, d)),
        _full_spec(win.shape), _full_spec((1, 2 * HEAD_DIM)), _full_spec((1, N_Q_HEADS)),
        _full_spec(wout.shape),
    ]
    out_specs = [
        pl.BlockSpec((1, tq, d), lagged),
        pl.BlockSpec((1, w, KV_W), lambda i, j: (i, 0, 0)),
        pl.BlockSpec((1, w, KV_W), lambda i, j: (i, 0, 0)),
    ]
    return pl.pallas_call(
        _swa_prompt_kernel,
        grid=(b, n_tiles + 1), in_specs=in_specs, out_specs=out_specs,
        out_shape=[f((b, l, d)), f((b, w, KV_W)), f((b, w, KV_W))],
        scratch_shapes=[pltpu.VMEM((N_KV_HEADS, w + 2 * tq, 2 * HEAD_DIM), BF16),
                        pltpu.VMEM((N_KV_HEADS, w + 2 * tq, 2 * HEAD_DIM), BF16),
                        pltpu.VMEM((tq, ATT_W), F32),
                        pltpu.VMEM((tq, ATT_W), F32),
                        pltpu.VMEM((tq, ATT_W), F32)],
        compiler_params=pltpu.CompilerParams(
            dimension_semantics=("arbitrary", "arbitrary"), vmem_limit_bytes=VMEM_LIMIT),
        name="swa_prompt",
    )(y, y, kvnorm.reshape(1, d), wkv, knorm2, normb.reshape(1, d), win, qnorm2,
      sinks.reshape(1, N_Q_HEADS), wout)


def _swa_sample_front_kernel(y_ref, kvnorm_ref, wkv_ref, knorm_ref, normb_ref, win_ref, qnorm_ref,
                             k_ref, v_ref, q_ref, gate_ref):
    y = y_ref[...]
    yn = y * lax.rsqrt(jnp.mean(y * y, axis=-1, keepdims=True) + EPS)
    kv = _dot((yn * kvnorm_ref[...]).astype(BF16), wkv_ref[...])
    k_ref[...] = jnp.concatenate(_head_rms(kv[:, :KV_W], knorm_ref[...], N_KV_HEADS), axis=1)
    v_ref[...] = kv[:, KV_W:]
    qg = _dot((yn * normb_ref[...]).astype(BF16), win_ref[...])
    q_ref[...] = jnp.concatenate(
        _head_rms(qg[:, :ATT_W], qnorm_ref[...] * (HEAD_DIM ** -0.5), N_Q_HEADS), axis=1)
    gate_ref[...] = qg[:, ATT_W:]


def _swa_sample_attn_kernel(q_ref, kn_ref, vn_ref, kc_ref, vc_ref, sinks_ref, slopes_ref,
                            o_ref, kwin_ref, vwin_ref):
    bb = kc_ref.shape[0]
    w = kc_ref.shape[1]
    hrow = lax.broadcasted_iota(jnp.int32, (N_Q_HEADS, KV_W), 0) // Q_GROUP
    lblk = lax.broadcasted_iota(jnp.int32, (N_Q_HEADS, KV_W), 1) // HEAD_DIM
    own = hrow == lblk
    dist_c = (w - lax.broadcasted_iota(jnp.int32, (1, w), 1)).astype(F32)
    krow = lax.broadcasted_iota(jnp.int32, (w, KV_W), 0)
    slopes = slopes_ref[...]
    sink = sinks_ref[...]
    qms, scs = [], []
    for b in range(bb):
        q = q_ref[b]
        qm = jnp.where(own, jnp.concatenate([q] * N_KV_HEADS, axis=1), 0.0).astype(BF16)
        qms.append(qm)
        scs.append(_dot_nt(qm, kc_ref[b].astype(BF16)))
    pcs, pns = [], []
    for b in range(bb):
        s_c = scs[b] - slopes * dist_c
        s_n = jnp.sum(qms[b].astype(F32) * kn_ref[b].astype(BF16).astype(F32), axis=-1, keepdims=True)
        mx = jnp.maximum(jnp.maximum(jnp.max(s_c, axis=-1, keepdims=True), s_n), sink)
        p_c = jnp.exp(s_c - mx)
        p_n = jnp.exp(s_n - mx)
        den = jnp.sum(p_c, axis=-1, keepdims=True) + p_n + jnp.exp(sink - mx)
        pcs.append((p_c / den).astype(BF16))
        pns.append(p_n / den)
    rs = [_dot(pcs[b], vc_ref[b].astype(BF16)) for b in range(bb)]
    for b in range(bb):
        kn = kn_ref[b]
        vn = vn_ref[b]
        r = rs[b] + pns[b].astype(BF16).astype(F32) * vn.astype(BF16).astype(F32)
        r = jnp.where(own, r, 0.0)
        acc = r[:, 0:HEAD_DIM]
        for blk in range(1, N_KV_HEADS):
            acc = acc + r[:, blk * HEAD_DIM:(blk + 1) * HEAD_DIM]
        o_ref[b] = acc
        kwin_ref[b] = jnp.where(krow == w - 1, kn, pltpu.roll(kc_ref[b], w - 1, 0))
        vwin_ref[b] = jnp.where(krow == w - 1, vn, pltpu.roll(vc_ref[b], w - 1, 0))


def _swa_sample(y, cache_k, cache_v, kvnorm, wkv, knorm, normb, win, qnorm, sinks, wout):
    n, d = y.shape
    w = cache_k.shape[1]
    f = lambda shape: jax.ShapeDtypeStruct(shape, F32)
    kn, vn, q, gate = pl.pallas_call(
        _swa_sample_front_kernel,
        out_shape=[f((n, KV_W)), f((n, KV_W)), f((n, ATT_W)), f((n, ATT_W))],
        compiler_params=pltpu.CompilerParams(vmem_limit_bytes=VMEM_LIMIT),
        name="swa_sample_front",
    )(y, kvnorm.reshape(1, d), wkv, knorm.reshape(1, HEAD_DIM), normb.reshape(1, d), win,
      qnorm.reshape(1, HEAD_DIM))
    bb = 8
    slopes = (2.0 ** (-8.0 * jnp.arange(1, N_Q_HEADS + 1, dtype=F32) / N_Q_HEADS)).reshape(N_Q_HEADS, 1)
    spec3 = lambda a, c: pl.BlockSpec((bb, a, c), lambda i: (i, 0, 0))
    o, kwin, vwin = pl.pallas_call(
        _swa_sample_attn_kernel,
        grid=(n // bb,),
        in_specs=[spec3(N_Q_HEADS, HEAD_DIM), spec3(1, KV_W), spec3(1, KV_W), spec3(w, KV_W),
                  spec3(w, KV_W), _full_spec((N_Q_HEADS, 1)), _full_spec((N_Q_HEADS, 1))],
        out_specs=[spec3(N_Q_HEADS, HEAD_DIM), spec3(w, KV_W), spec3(w, KV_W)],
        out_shape=[f((n, N_Q_HEADS, HEAD_DIM)), f((n, w, KV_W)), f((n, w, KV_W))],
        compiler_params=pltpu.CompilerParams(dimension_semantics=("arbitrary",)),
        name="swa_sample_attn",
    )(q.reshape(n, N_Q_HEADS, HEAD_DIM), kn.reshape(n, 1, KV_W), vn.reshape(n, 1, KV_W),
      cache_k.reshape(n, w, KV_W), cache_v.reshape(n, w, KV_W),
      sinks.reshape(N_Q_HEADS, 1), slopes)
    out = pl.pallas_call(
        functools.partial(_out_proj_kernel, head_w=ATT_W, norm_heads=False),
        out_shape=f((n, d)),
        name="swa_sample_out",
    )(y, o.reshape(n, ATT_W), gate, jnp.ones((1, ATT_W), F32), wout)
    return out, kwin, vwin


def kernel(x_prompt, x_sample, state_conv, state_ssm, cache_k_win, cache_v_win, norm_a, w_in_a, conv_w_a, a_log, dt_bias, o_norm_a, w_out_a, kv_norm, w_kv, k_norm, norm_b, w_in_b, q_norm, sinks, w_out_b):
    n_a = w_in_a.shape[0]
    n_b = w_in_b.shape[0]
    assert n_a == 1 and n_b == 1, "kernel is written for DEPTH == 2"
    bp, lp, d = x_prompt.shape
    n = x_sample.shape[0]
    hw = GDN_HEADS * GDN_D

    hp, hs = x_prompt, x_sample.reshape(n, d)
    conv_p, ssm_p, conv_s, ssm_s = [], [], [], []
    for layer in range(n_a):
        win_a = w_in_a[layer].astype(BF16)
        wabt = win_a[:, QKV_W + hw:].T
        wout = w_out_a[layer].astype(BF16)
        hp, cbuf, st = _gdn_prompt(hp, norm_a[layer], win_a, wabt, conv_w_a[layer],
                                   a_log[layer], dt_bias[layer], o_norm_a[layer], wout)
        conv_p.append(cbuf)
        ssm_p.append(st)
        hs, cbuf, st = _gdn_sample(hs, state_conv[layer], state_ssm[layer], norm_a[layer], win_a,
                                   conv_w_a[layer], a_log[layer], dt_bias[layer],
                                   o_norm_a[layer], wout)
        conv_s.append(cbuf)
        ssm_s.append(st)

    wkv = w_kv.astype(BF16)
    win = w_in_b[0].astype(BF16)
    woutb = w_out_b[0].astype(BF16)
    hp, k_win_p, v_win_p = _swa_prompt(hp, kv_norm, wkv, k_norm, norm_b[0], win, q_norm[0], sinks[0], woutb)
    hs, k_win_s, v_win_s = _swa_sample(hs, cache_k_win, cache_v_win, kv_norm, wkv, k_norm, norm_b[0],
                                       win, q_norm[0], sinks[0], woutb)
    kv_shape = (N_KV_HEADS, HEAD_DIM)
    return (hp, hs.reshape(n, 1, d), jnp.stack(conv_p), jnp.stack(ssm_p),
            k_win_p.reshape(bp, WINDOW, *kv_shape), v_win_p.reshape(bp, WINDOW, *kv_shape),
            jnp.stack(conv_s), jnp.stack(ssm_s),
            k_win_s.reshape(n, WINDOW, *kv_shape), v_win_s.reshape(n, WINDOW, *kv_shape))
```

```python
import functools

import jax
import jax.numpy as jnp
from jax import lax
from jax.experimental import pallas as pl
from jax.experimental.pallas import tpu as pltpu

F32 = jnp.float32
BF16 = jnp.bfloat16
EPS = 1e-6

D_MODEL = 1024
GDN_HEADS = 8
GDN_D = 128
QKV_W = 3 * GDN_HEADS * GDN_D
CONV_W = 4
CHUNK = 64
N_Q_HEADS = 16
N_KV_HEADS = 4
Q_GROUP = N_Q_HEADS // N_KV_HEADS
HEAD_DIM = 64
KV_W = N_KV_HEADS * HEAD_DIM
ATT_W = N_Q_HEADS * HEAD_DIM
WINDOW = 128

TM_A = 512
SUB_A = 256
TQ_B = 256
CARRY = 8
VMEM_LIMIT = 52 * 1024 * 1024

_NT = (((1,), (1,)), ((), ()))
_TN = (((0,), (0,)), ((), ()))


def _dot(a, b):
    return jnp.dot(a, b, preferred_element_type=F32)


def _dot_nt(a, b):
    return lax.dot_general(a, b, _NT, preferred_element_type=F32)


def _dot_tn(a, b):
    return lax.dot_general(a, b, _TN, preferred_element_type=F32)


def _split(x):
    hi = x.astype(BF16)
    lo = (x - hi.astype(F32)).astype(BF16)
    return hi, lo


def _dot_exact_lhs(a_bf, b):
    b0 = b.astype(BF16)
    r1 = b - b0.astype(F32)
    b1 = r1.astype(BF16)
    b2 = (r1 - b1.astype(F32)).astype(BF16)
    return (_dot(a_bf, b2) + _dot(a_bf, b1)) + _dot(a_bf, b0)


def _dot_exact_rhs(a, b_bf):
    a0 = a.astype(BF16)
    r1 = a - a0.astype(F32)
    a1 = r1.astype(BF16)
    a2 = (r1 - a1.astype(F32)).astype(BF16)
    return (_dot(a2, b_bf) + _dot(a1, b_bf)) + _dot(a0, b_bf)


def _silu(x):
    return x * jax.nn.sigmoid(x)


def _softplus(x):
    return jnp.maximum(x, 0.0) + jnp.log1p(jnp.exp(-jnp.abs(x)))


def _halves(x, left):
    zero = jnp.zeros_like(x)
    return jnp.where(left, x, zero), jnp.where(left, zero, x)


def _block_diag(x, left):
    return jnp.concatenate(_halves(x, left), axis=0)


def _pair_split_product(a, b, left):
    ah, al = _split(a)
    bh, bl = _split(b)
    bh1, bh2 = _halves(bh, left)
    bl1, bl2 = _halves(bl, left)
    zero = jnp.zeros_like(bh)
    rhs = jnp.concatenate([jnp.concatenate([bh1, bl1], axis=1), jnp.concatenate([bh2, bl2], axis=1),
                           jnp.concatenate([bh1, zero], axis=1), jnp.concatenate([bh2, zero], axis=1)], axis=0)
    y = _dot(jnp.concatenate([ah, al], axis=1), rhs)
    half = y.shape[1] // 2
    return y[:, :half] + y[:, half:]


def _unit_lower_inverse_pairs(ms, left):
    n = ms[0].shape[0]
    row = lax.broadcasted_iota(jnp.int32, (n, 2 * n), 0)
    col = lax.broadcasted_iota(jnp.int32, (n, 2 * n), 1) % n
    eye2 = jnp.where(row == col, 1.0, 0.0).astype(F32)

    def times(a, b):
        return _dot(a.astype(BF16), _block_diag(b.astype(BF16), left))

    ts = [eye2 - m for m in ms]
    ps = [times(m, m) for m in ms]
    yield
    steps = max(1, (n - 1).bit_length()) - 1
    for _ in range(steps - 1):
        both = [_dot(jnp.concatenate([t.astype(BF16), p.astype(BF16)], axis=0), _block_diag(p.astype(BF16), left))
                for t, p in zip(ts, ps)]
        ts = [t + tp[:n, :] for t, tp in zip(ts, both)]
        ps = [tp[n:, :] for tp in both]
        yield
    ts = [t + times(t, p) for t, p in zip(ts, ps)]
    yield
    rs = [(eye2 - t) - _pair_split_product(m, t, left) for m, t in zip(ms, ts)]
    yield
    ts = [t + times(t, r) for t, r in zip(ts, rs)]
    yield
    return ts


def _full_spec(shape):
    nd = len(shape)
    return pl.BlockSpec(shape, lambda *_: (0,) * nd)


def _gdn_gates(ab, ab_t, alog_r, dtb_r, alog_c, dtb_c, tm, chunk):
    h = GDN_HEADS
    g_c = -jnp.exp(alog_r) * _softplus(ab[:, :h] + dtb_r)
    beta = jax.nn.sigmoid(ab[:, h:])
    g_r = -jnp.exp(alog_c) * _softplus(ab_t[:h, :] + dtb_c)
    row = lax.broadcasted_iota(jnp.int32, (tm, tm), 0)
    col = lax.broadcasted_iota(jnp.int32, (tm, tm), 1)
    same = (row // chunk) == (col // chunk)
    lower = jnp.where(same & (row >= col), 1.0, 0.0).astype(BF16)
    upper = jnp.where(same & (row <= col), 1.0, 0.0).astype(BF16)
    gc = _dot_exact_lhs(lower, g_c)
    gr = _dot_exact_rhs(g_r, upper)
    return beta, gc, gr


def _gdn_prompt_kernel(x_ref, norm_ref, win_ref, wabt_ref, convw_ref,
                       alog_r_ref, dtb_r_ref, alog_c_ref, dtb_c_ref, onorm_ref, wout_ref,
                       y_ref, conv_ref, ssm_ref,
                       pre_scr, gate_scr, s_scr, k_scr, kb_scr, q_scr, qd_scr, kd_scr,
                       rhs_scr, dec_scr, gt_scr, o_scr, lhs_scr, su_scr):
    tm = x_ref.shape[1]
    sub = SUB_A
    n_sub = tm // sub
    n_chunks = sub // CHUNK
    gate_w = GDN_HEADS * GDN_D
    l = pl.program_id(1)

    @pl.when(l == 0)
    def _():
        pre_scr[0:CARRY, :] = jnp.zeros((CARRY, QKV_W), F32)
        s_scr[...] = jnp.zeros(s_scr.shape, F32)

    ci = lax.broadcasted_iota(jnp.int32, (CHUNK, 2 * CHUNK), 0)
    cj = lax.broadcasted_iota(jnp.int32, (CHUNK, 2 * CHUNK), 1) % CHUNK
    causal = ci >= cj
    strict = ci > cj
    left = lax.broadcasted_iota(jnp.int32, (1, 2 * CHUNK), 1) < CHUNK
    heads = list(range(GDN_HEADS))

    def start(s):
        x = x_ref[0, s * sub:(s + 1) * sub, :]
        xn = x * lax.rsqrt(jnp.mean(x * x, axis=-1, keepdims=True) + EPS) * norm_ref[...]
        xb = xn.astype(BF16)
        ab = _dot(xb, win_ref[:, QKV_W + gate_w:])
        ab_t = _dot_nt(wabt_ref[...], xb)
        beta, gc, gr = _gdn_gates(ab, ab_t, alog_r_ref[...], dtb_r_ref[...],
                                  alog_c_ref[...], dtb_c_ref[...], sub, CHUNK)
        g_last = jnp.concatenate(
            [jnp.broadcast_to(gc[c * CHUNK + CHUNK - 1:c * CHUNK + CHUNK, :], (CHUNK, GDN_HEADS))
             for c in range(n_chunks)], axis=0)
        return dict(x=x, xb=xb, beta=beta, gc=gc, gr=gr, e_g=jnp.exp(gc),
                    e_kd=jnp.exp(g_last - gc), e_tot=jnp.exp(g_last))

    def project(s, st):
        for part in range(3):
            cols = slice(part * gate_w, (part + 1) * gate_w)
            pre_scr[CARRY + s * sub:CARRY + (s + 1) * sub, cols] = _dot(st["xb"], win_ref[:, cols])

    def project_gate(s, st):
        gate_scr[s * sub:(s + 1) * sub, :] = _dot(st["xb"], win_ref[:, QKV_W:QKV_W + gate_w])

    def conv_act(s, j):
        cols = slice(j * GDN_D, (j + 1) * GDN_D)
        ext = pre_scr[s * sub:s * sub + CARRY + sub, cols]
        half = ext[CARRY:, :] * (0.5 * convw_ref[CONV_W - 1:CONV_W, cols])
        for back in range(1, CONV_W):
            tap = CONV_W - 1 - back
            half = half + pltpu.roll(ext, back, 0)[CARRY:, :] * (0.5 * convw_ref[tap:tap + 1, cols])
        return half + half * jnp.tanh(half)

    def head_prep(s, st, h):
        rows = slice(s * sub, (s + 1) * sub)
        qh = conv_act(s, h)
        kh = conv_act(s, GDN_HEADS + h)
        vh = conv_act(s, 2 * GDN_HEADS + h)
        qn = qh * lax.rsqrt(jnp.sum(qh * qh, axis=-1, keepdims=True) + EPS) * (GDN_D ** -0.5)
        kn = kh * lax.rsqrt(jnp.sum(kh * kh, axis=-1, keepdims=True) + EPS)
        b_col = st["beta"][:, h:h + 1]
        eg_col = st["e_g"][:, h:h + 1]
        kb = kn * b_col
        k_scr[h, rows, :] = kn.astype(BF16)
        kb_scr[h, rows, :] = kb.astype(BF16)
        q_scr[h, rows, :] = qn.astype(BF16)
        qd_scr[h, rows, :] = qn * eg_col
        kd_scr[h, rows, :] = (kn * st["e_kd"][:, h:h + 1]).astype(BF16)
        rhs_scr[h, rows, 0:GDN_D] = (vh * b_col).astype(BF16)
        rhs_scr[h, rows, GDN_D:2 * GDN_D] = (kb * eg_col).astype(BF16)
        gc, gr = st["gc"], st["gr"]
        for c in range(n_chunks):
            gt_scr[h, s * n_chunks + c:s * n_chunks + c + 1, :] = jnp.broadcast_to(
                st["e_tot"][c * CHUNK:c * CHUNK + 1, h:h + 1], (1, GDN_D))
        for j in range(n_chunks // 2):
            r1 = slice(2 * j * CHUNK, (2 * j + 1) * CHUNK)
            r2 = slice((2 * j + 1) * CHUNK, (2 * j + 2) * CHUNK)
            diff = jnp.where(left, gc[r1, h:h + 1], gc[r2, h:h + 1]) - gr[h:h + 1, 2 * j * CHUNK:(2 * j + 2) * CHUNK]
            pair = s * (n_chunks // 2) + j
            dec_scr[h, pair * CHUNK:(pair + 1) * CHUNK, :] = jnp.exp(jnp.where(causal, diff, -jnp.inf))

    def chunk_terms(s):
        items = [(h, s * (n_chunks // 2) + j) for h in heads for j in range(n_chunks // 2)]
        aqs = []
        for h, pair in items:
            r12 = slice(2 * pair * CHUNK, (2 * pair + 2) * CHUNK)
            aqs.append(_dot_nt(jnp.concatenate([kb_scr[h, r12, :], q_scr[h, r12, :]], axis=0), k_scr[h, r12, :]))
        yield
        decs = [dec_scr[h, pair * CHUNK:(pair + 1) * CHUNK, :] for h, pair in items]
        ms = [jnp.where(strict, jnp.where(left, aq[:CHUNK, :], aq[CHUNK:2 * CHUNK, :]) * d, 0.0)
              for aq, d in zip(aqs, decs)]
        qks = [jnp.where(causal, jnp.where(left, aq[2 * CHUNK:3 * CHUNK, :], aq[3 * CHUNK:, :]) * d, 0.0)
               for aq, d in zip(aqs, decs)]
        ts = yield from _unit_lower_inverse_pairs(ms, left)
        sols = []
        for (h, pair), t, qk in zip(items, ts, qks):
            lo = t - t.astype(BF16).astype(F32)
            t_parts = (jnp.where(left, t, pltpu.roll(lo, CHUNK, 1)).astype(BF16),
                       jnp.where(left, pltpu.roll(t, CHUNK, 1), lo).astype(BF16))
            qk_parts = (qk[:, :CHUNK].astype(BF16), pltpu.roll(qk, CHUNK, 1)[:, :CHUNK].astype(BF16))
            for half in range(2):
                c = 2 * pair + half
                r = slice(c * CHUNK, (c + 1) * CHUNK)
                rhs = rhs_scr[h, r, :]
                sols.append((h, c, r, qk_parts[half],
                             _dot(t_parts[half], jnp.concatenate([rhs, rhs], axis=0)).astype(BF16)))
        yield
        outs = [(h, c, r, _dot_tn(kd_scr[h, r, :], sol),
                 _dot(qk_c, sol))
                for h, c, r, qk_c, sol in sols]
        for h, c, r, kd_uw, qk_uw in outs:
            lhs_scr[h, c, 0:GDN_D, :] = kd_uw[:, GDN_D:].astype(BF16)
            lhs_scr[h, c, GDN_D:GDN_D + CHUNK, :] = (qd_scr[h, r, :] - qk_uw[:, GDN_D:]).astype(BF16)
            su_scr[h, c] = kd_uw[:, :GDN_D]
            o_scr[h, r, :] = qk_uw[:, :GDN_D]
        yield

    def state_updates(s):
        for c in range(s * n_chunks, (s + 1) * n_chunks):
            rows = slice(c * CHUNK, (c + 1) * CHUNK)
            for h in heads:
                st_h = s_scr[h]
                prod = _dot(lhs_scr[h, c], st_h.astype(BF16))
                o_scr[h, rows, :] = o_scr[h, rows, :] + prod[GDN_D:, :]
                s_scr[h] = st_h * gt_scr[h, c:c + 1, :] + (su_scr[h, c] - prod[:GDN_D, :])
            yield

    def finish(s, st):
        rows = slice(s * sub, (s + 1) * sub)
        pieces = []
        for h in heads:
            o = o_scr[h, rows, :]
            on = o * lax.rsqrt(jnp.mean(o * o, axis=-1, keepdims=True) + EPS) * onorm_ref[...]
            pieces.append((on * _silu(gate_scr[rows, h * GDN_D:(h + 1) * GDN_D])).astype(BF16))
        on_all = jnp.concatenate(pieces, axis=1)
        for piece in range(D_MODEL // (2 * GDN_D)):
            cols = slice(piece * 2 * GDN_D, (piece + 1) * 2 * GDN_D)
            y_ref[0, rows, cols] = st["x"][:, cols] + _dot(on_all, wout_ref[:, cols])
            yield

    def run(gen):
        for _ in gen:
            pass

    def interleave(main, side, every):
        for n, _ in enumerate(main):
            if (n + 1) % every == 0:
                next(side, None)
        run(side)

    sts = [start(0)]
    project(0, sts[0])
    for s in range(n_sub):
        if s + 1 < n_sub:
            sts.append(start(s + 1))
        for h in heads:
            head_prep(s, sts[s], h)
        if s + 1 < n_sub:
            project(s + 1, sts[s + 1])
        project_gate(s, sts[s])
        if s == 0:
            run(chunk_terms(s))
        else:
            interleave(chunk_terms(s), state_updates(s - 1), 2)
            if s >= 2:
                run(finish(s - 2, sts[s - 2]))
    conv_ref[0] = pre_scr[tm + CARRY - 3:tm + CARRY, :]
    pre_scr[0:CARRY, :] = pre_scr[tm:tm + CARRY, :]
    if n_sub >= 2:
        interleave(state_updates(n_sub - 1), finish(n_sub - 2, sts[n_sub - 2]), 1)
    else:
        run(state_updates(0))

    @pl.when(l == pl.num_programs(1) - 1)
    def _():
        ssm_ref[0] = s_scr[...]

    run(finish(n_sub - 1, sts[n_sub - 1]))


def _gdn_prompt(x, norm, win, wabt, convw, alog, dtb, onorm, wout):
    b, l, d = x.shape
    tm = TM_A
    h = GDN_HEADS
    grid = (b, l // tm)
    in_specs = [
        pl.BlockSpec((1, tm, d), lambda i, j: (i, j, 0)),
        _full_spec((1, d)), _full_spec(win.shape), _full_spec(wabt.shape), _full_spec(convw.shape),
        _full_spec((1, h)), _full_spec((1, h)), _full_spec((h, 1)), _full_spec((h, 1)),
        _full_spec((1, GDN_D)), _full_spec(wout.shape),
    ]
    out_specs = [
        pl.BlockSpec((1, tm, d), lambda i, j: (i, j, 0)),
        pl.BlockSpec((1, CONV_W - 1, QKV_W), lambda i, j: (i, 0, 0)),
        pl.BlockSpec((1, h, GDN_D, GDN_D), lambda i, j: (i, 0, 0, 0)),
    ]
    out_shape = [
        jax.ShapeDtypeStruct((b, l, d), F32),
        jax.ShapeDtypeStruct((b, CONV_W - 1, QKV_W), F32),
        jax.ShapeDtypeStruct((b, h, GDN_D, GDN_D), F32),
    ]
    scratch = [
        pltpu.VMEM((tm + CARRY, QKV_W), F32),
        pltpu.VMEM((tm, h * GDN_D), F32),
        pltpu.VMEM((h, GDN_D, GDN_D), F32),
        pltpu.VMEM((h, tm, GDN_D), BF16),
        pltpu.VMEM((h, tm, GDN_D), BF16),
        pltpu.VMEM((h, tm, GDN_D), BF16),
        pltpu.VMEM((h, tm, GDN_D), F32),
        pltpu.VMEM((h, tm, GDN_D), BF16),
        pltpu.VMEM((h, tm, 2 * GDN_D), BF16),
        pltpu.VMEM((h, tm // 2, 2 * CHUNK), F32),
        pltpu.VMEM((h, tm // CHUNK, GDN_D), F32),
        pltpu.VMEM((h, tm, GDN_D), F32),
        pltpu.VMEM((h, tm // CHUNK, GDN_D + CHUNK, GDN_D), BF16),
        pltpu.VMEM((h, tm // CHUNK, GDN_D, GDN_D), F32),
    ]
    return pl.pallas_call(
        _gdn_prompt_kernel,
        grid=grid, in_specs=in_specs, out_specs=out_specs, out_shape=out_shape,
        scratch_shapes=scratch,
        compiler_params=pltpu.CompilerParams(
            dimension_semantics=("arbitrary", "arbitrary"), vmem_limit_bytes=VMEM_LIMIT),
        name="gdn_prompt",
    )(x, norm.reshape(1, d), win, wabt, convw,
      alog.reshape(1, h), dtb.reshape(1, h), alog.reshape(h, 1), dtb.reshape(h, 1),
      onorm.reshape(1, GDN_D), wout)


def _gdn_sample_front_kernel(x_ref, norm_ref, win_ref, convw_ref,
                             c0_ref, c1_ref, c2_ref, alog_r_ref, dtb_r_ref,
                             pre_ref, q_ref, k_ref, u_ref, w_ref, qd_ref, gate_ref, eg_ref):
    x = x_ref[...]
    xn = x * lax.rsqrt(jnp.mean(x * x, axis=-1, keepdims=True) + EPS) * norm_ref[...]
    xb = xn.astype(BF16)
    gate_w = GDN_HEADS * GDN_D
    pre = _dot(xb, win_ref[:, :QKV_W])
    pre_ref[...] = pre
    gate_ref[...] = _dot(xb, win_ref[:, QKV_W:QKV_W + gate_w])
    ab = _dot(xb, win_ref[:, QKV_W + gate_w:])
    g = -jnp.exp(alog_r_ref[...]) * _softplus(ab[:, :GDN_HEADS] + dtb_r_ref[...])
    beta = jax.nn.sigmoid(ab[:, GDN_HEADS:])
    e_g = jnp.exp(g)
    eg_ref[...] = e_g
    act = _silu(((c0_ref[...] * convw_ref[0:1, :] + c1_ref[...] * convw_ref[1:2, :])
                 + c2_ref[...] * convw_ref[2:3, :]) + pre * convw_ref[3:4, :])
    w_all = GDN_HEADS * GDN_D
    for h in range(GDN_HEADS):
        cols = slice(h * GDN_D, (h + 1) * GDN_D)
        qh = act[:, h * GDN_D:(h + 1) * GDN_D]
        kh = act[:, w_all + h * GDN_D:w_all + (h + 1) * GDN_D]
        vh = act[:, 2 * w_all + h * GDN_D:2 * w_all + (h + 1) * GDN_D]
        qn = qh * lax.rsqrt(jnp.sum(qh * qh, axis=-1, keepdims=True) + EPS) * (GDN_D ** -0.5)
        kn = kh * lax.rsqrt(jnp.sum(kh * kh, axis=-1, keepdims=True) + EPS)
        b_col = beta[:, h:h + 1]
        eg_col = e_g[:, h:h + 1]
        q_ref[:, cols] = qn
        k_ref[:, cols] = kn
        u_ref[:, cols] = vh * b_col
        w_ref[:, cols] = kn * b_col * eg_col
        qd_ref[:, cols] = qn * eg_col


def _gdn_sample_state_kernel(q_ref, k_ref, u_ref, w_ref, qd_ref, eg_ref, s_ref, o_ref, s_out_ref):
    bb = s_ref.shape[0]
    row = lax.broadcasted_iota(jnp.int32, (8, GDN_D), 0)
    pairs = [(b, h) for b in range(bb) for h in range(GDN_HEADS)]
    prods = []
    for b, h in pairs:
        cols = slice(h * GDN_D, (h + 1) * GDN_D)
        lhs = jnp.where(row == 0, w_ref[b:b + 1, cols], jnp.where(row == 1, qd_ref[b:b + 1, cols], 0.0))
        prods.append(_dot(lhs.astype(BF16), s_ref[b, h].astype(BF16)))
    for (b, h), prod in zip(pairs, prods):
        cols = slice(h * GDN_D, (h + 1) * GDN_D)
        q = q_ref[b:b + 1, cols].astype(BF16).astype(F32)
        k = k_ref[b:b + 1, cols].astype(BF16).astype(F32)
        v_new = u_ref[b:b + 1, cols] - prod[0:1, :]
        vb = v_new.astype(BF16).astype(F32)
        qk = jnp.sum(q * k, axis=-1, keepdims=True)
        o_ref[b:b + 1, cols] = prod[1:2, :] + qk.astype(BF16).astype(F32) * vb
        k8 = jnp.where(row == 0, k, 0.0).astype(BF16)
        v8 = jnp.where(row == 0, vb, 0.0).astype(BF16)
        s_out_ref[b, h] = s_ref[b, h] * eg_ref[b:b + 1, h:h + 1] + _dot_tn(k8, v8)


def _out_proj_kernel(x_ref, o_ref, gate_ref, onorm_ref, wout_ref, y_ref, *, head_w, norm_heads):
    pieces = []
    for h in range(x_ref.shape[1] // head_w):
        cols = slice(h * head_w, (h + 1) * head_w)
        o = o_ref[:, cols]
        if norm_heads:
            o = o * lax.rsqrt(jnp.mean(o * o, axis=-1, keepdims=True) + EPS) * onorm_ref[...]
        pieces.append((o * _silu(gate_ref[:, cols])).astype(BF16))
    y_ref[...] = x_ref[...] + _dot(jnp.concatenate(pieces, axis=1), wout_ref[...])


def _gdn_sample(x, conv_state, ssm_state, norm, win, convw, alog, dtb, onorm, wout):
    n, d = x.shape
    h = GDN_HEADS
    hw = h * GDN_D
    conv_t = jnp.transpose(conv_state, (1, 0, 2))
    f = lambda shape: jax.ShapeDtypeStruct(shape, F32)
    pre, q, k, u, w, qd, gate, eg = pl.pallas_call(
        _gdn_sample_front_kernel,
        out_shape=[f((n, QKV_W)), f((n, hw)), f((n, hw)), f((n, hw)), f((n, hw)), f((n, hw)),
                   f((n, hw)), f((n, h))],
        compiler_params=pltpu.CompilerParams(vmem_limit_bytes=VMEM_LIMIT),
        name="gdn_sample_front",
    )(x, norm.reshape(1, d), win, convw, conv_t[0], conv_t[1], conv_t[2],
      alog.reshape(1, h), dtb.reshape(1, h))
    bb = 8
    row_spec = lambda width: pl.BlockSpec((bb, width), lambda i: (i, 0))
    st_spec = pl.BlockSpec((bb, h, GDN_D, GDN_D), lambda i: (i, 0, 0, 0))
    o, s_new = pl.pallas_call(
        _gdn_sample_state_kernel,
        grid=(n // bb,),
        in_specs=[row_spec(hw)] * 5 + [row_spec(h), st_spec],
        out_specs=[row_spec(hw), st_spec],
        out_shape=[f((n, hw)), f(ssm_state.shape)],
        compiler_params=pltpu.CompilerParams(dimension_semantics=("arbitrary",)),
        name="gdn_sample_state",
    )(q, k, u, w, qd, eg, ssm_state)
    y = pl.pallas_call(
        functools.partial(_out_proj_kernel, head_w=GDN_D, norm_heads=True),
        out_shape=f((n, d)),
        name="gdn_sample_out",
    )(x, o, gate, onorm.reshape(1, GDN_D), wout)
    conv_new = jnp.stack([conv_t[1], conv_t[2], pre], axis=1)
    return y, conv_new, s_new


def _head_rms(x, gain_row, n_heads):
    pieces = []
    for h in range(n_heads):
        xh = x[:, h * HEAD_DIM:(h + 1) * HEAD_DIM]
        pieces.append(xh * lax.rsqrt(jnp.mean(xh * xh, axis=-1, keepdims=True) + EPS) * gain_row)
    return pieces


def _pair_rms(x, gain2, left):
    sq = x * x
    lo = jnp.sum(jnp.where(left, sq, 0.0), axis=-1, keepdims=True)
    hi = jnp.sum(jnp.where(left, 0.0, sq), axis=-1, keepdims=True)
    ms = jnp.where(left, lo, hi) * (1.0 / HEAD_DIM)
    return x * lax.rsqrt(ms + EPS) * gain2


def _swa_prompt_kernel(y_ref, kvnorm_ref, wkv_ref, knorm2_ref, normb_ref, win_ref, qnorm2_ref,
                       sinks_ref, wout_ref, out_ref, kwin_ref, vwin_ref, k_scr, v_scr, o_scr):
    tq = y_ref.shape[1]
    w = WINDOW
    step = pl.program_id(1)
    lane = lax.broadcasted_iota(jnp.int32, (1, 2 * HEAD_DIM), 1)
    left = lane < HEAD_DIM

    @pl.when(step == 0)
    def _():
        k_scr[:, 0:w, :] = jnp.zeros((N_KV_HEADS, w, 2 * HEAD_DIM), BF16)
        v_scr[:, 0:w, :] = jnp.zeros((N_KV_HEADS, w, 2 * HEAD_DIM), BF16)

    @pl.when(step > 0)
    def _():
        k_scr[:, 0:w, :] = k_scr[:, tq:tq + w, :]
        v_scr[:, 0:w, :] = v_scr[:, tq:tq + w, :]

    y = y_ref[0]
    yn = y * lax.rsqrt(jnp.mean(y * y, axis=-1, keepdims=True) + EPS)
    kv = _dot((yn * kvnorm_ref[...]).astype(BF16), wkv_ref[...])
    for c in range(N_KV_HEADS // 2):
        cols = slice(c * 2 * HEAD_DIM, (c + 1) * 2 * HEAD_DIM)
        kp = _pair_rms(kv[:, cols], knorm2_ref[...], left)
        vp = kv[:, KV_W + c * 2 * HEAD_DIM:KV_W + (c + 1) * 2 * HEAD_DIM]
        kwin_ref[0, :, cols] = kp[tq - w:, :]
        vwin_ref[0, :, cols] = vp[tq - w:, :]
        kr = pltpu.roll(kp, HEAD_DIM, 1)
        vr = pltpu.roll(vp, HEAD_DIM, 1)
        k_scr[2 * c, w:w + tq, :] = jnp.where(left, kp, kr).astype(BF16)
        k_scr[2 * c + 1, w:w + tq, :] = jnp.where(left, kr, kp).astype(BF16)
        v_scr[2 * c, w:w + tq, :] = jnp.where(left, vp, vr).astype(BF16)
        v_scr[2 * c + 1, w:w + tq, :] = jnp.where(left, vr, vp).astype(BF16)

    qg = _dot((yn * normb_ref[...]).astype(BF16), win_ref[...])
    q_pairs = [_pair_rms(qg[:, c * 2 * HEAD_DIM:(c + 1) * 2 * HEAD_DIM], qnorm2_ref[...], left)
               for c in range(N_Q_HEADS // 2)]

    kj = lax.broadcasted_iota(jnp.int32, (2 * w, w), 0)
    qi = lax.broadcasted_iota(jnp.int32, (2 * w, w), 1)
    dist = qi - kj + w
    band = (dist >= 0) & (dist <= w)
    j_rel = (kj - w).astype(F32)
    i_row = lax.broadcasted_iota(jnp.int32, (1, w), 1).astype(F32)

    def scores(blk, hk):
        q_rows = slice(blk * w, (blk + 1) * w)
        lhs = []
        for c in (2 * hk, 2 * hk + 1):
            qp = q_pairs[c][q_rows, :]
            lhs.append(jnp.where(left, qp, 0.0))
            lhs.append(jnp.where(left, 0.0, qp))
        return _dot_nt(k_scr[hk, blk * w:blk * w + 2 * w, :],
                       jnp.concatenate(lhs, axis=0).astype(BF16))

    def attend(blk, hk, s4):
        q_rows = slice(blk * w, (blk + 1) * w)
        valid = band & ((step > 0) | (kj >= w)) if blk == 0 else band
        ps = []
        for g in range(Q_GROUP):
            hq = hk * Q_GROUP + g
            slope = 2.0 ** (-8.0 * (hq + 1) / N_Q_HEADS)
            a = jnp.where(valid, s4[:, g * w:(g + 1) * w] + slope * j_rel, -jnp.inf)
            sink = sinks_ref[0:1, hq:hq + 1] + slope * i_row
            mx = jnp.maximum(jnp.max(a, axis=0, keepdims=True), sink)
            p = jnp.exp(a - mx)
            inv = 1.0 / (jnp.sum(p, axis=0, keepdims=True) + jnp.exp(sink - mx))
            ps.append((p * inv).astype(BF16))
        o4 = _dot_tn(jnp.concatenate(ps, axis=1),
                     v_scr[hk, blk * w:blk * w + 2 * w, :])
        for j, c in enumerate((2 * hk, 2 * hk + 1)):
            o_scr[q_rows, c * 2 * HEAD_DIM:(c + 1) * 2 * HEAD_DIM] = jnp.where(
                left, o4[(2 * j) * w:(2 * j + 1) * w, :], o4[(2 * j + 1) * w:(2 * j + 2) * w, :])

    items = [(blk, hk) for blk in range(tq // w) for hk in range(N_KV_HEADS)]
    s_next = scores(*items[0])
    for n, item in enumerate(items):
        s_cur = s_next
        if n + 1 < len(items):
            s_next = scores(*items[n + 1])
        attend(*item, s_cur)

    o = (o_scr[...] * _silu(qg[:, ATT_W:])).astype(BF16)
    out_ref[0] = y + _dot(o, wout_ref[...])


def _swa_prompt(y, kvnorm, wkv, knorm, normb, win, qnorm, sinks, wout):
    b, l, d = y.shape
    tq = TQ_B
    w = WINDOW
    f = lambda shape: jax.ShapeDtypeStruct(shape, F32)
    knorm2 = jnp.concatenate([knorm, knorm]).reshape(1, 2 * HEAD_DIM)
    qnorm2 = (jnp.concatenate([qnorm, qnorm]) * (HEAD_DIM ** -0.5)).reshape(1, 2 * HEAD_DIM)
    in_specs = [
        pl.BlockSpec((1, tq, d), lambda i, j: (i, j, 0)),
        _full_spec((1, d)), _full_spec(wkv.shape), _full_spec((1, 2 * HEAD_DIM)), _full_spec((1, d)),
        _full_spec(win.shape), _full_spec((1, 2 * HEAD_DIM)), _full_spec((1, N_Q_HEADS)),
        _full_spec(wout.shape),
    ]
    out_specs = [
        pl.BlockSpec((1, tq, d), lambda i, j: (i, j, 0)),
        pl.BlockSpec((1, w, KV_W), lambda i, j: (i, 0, 0)),
        pl.BlockSpec((1, w, KV_W), lambda i, j: (i, 0, 0)),
    ]
    return pl.pallas_call(
        _swa_prompt_kernel,
        grid=(b, l // tq), in_specs=in_specs, out_specs=out_specs,
        out_shape=[f((b, l, d)), f((b, w, KV_W)), f((b, w, KV_W))],
        scratch_shapes=[pltpu.VMEM((N_KV_HEADS, w + tq, 2 * HEAD_DIM), BF16),
                        pltpu.VMEM((N_KV_HEADS, w + tq, 2 * HEAD_DIM), BF16),
                        pltpu.VMEM((tq, ATT_W), F32)],
        compiler_params=pltpu.CompilerParams(
            dimension_semantics=("arbitrary", "arbitrary"), vmem_limit_bytes=VMEM_LIMIT),
        name="swa_prompt",
    )(y, kvnorm.reshape(1, d), wkv, knorm2, normb.reshape(1, d), win, qnorm2,
      sinks.reshape(1, N_Q_HEADS), wout)


def _swa_sample_front_kernel(y_ref, kvnorm_ref, wkv_ref, knorm_ref, normb_ref, win_ref, qnorm_ref,
                             k_ref, v_ref, q_ref, gate_ref):
    y = y_ref[...]
    yn = y * lax.rsqrt(jnp.mean(y * y, axis=-1, keepdims=True) + EPS)
    kv = _dot((yn * kvnorm_ref[...]).astype(BF16), wkv_ref[...])
    k_ref[...] = jnp.concatenate(_head_rms(kv[:, :KV_W], knorm_ref[...], N_KV_HEADS), axis=1)
    v_ref[...] = kv[:, KV_W:]
    qg = _dot((yn * normb_ref[...]).astype(BF16), win_ref[...])
    q_ref[...] = jnp.concatenate(
        _head_rms(qg[:, :ATT_W], qnorm_ref[...] * (HEAD_DIM ** -0.5), N_Q_HEADS), axis=1)
    gate_ref[...] = qg[:, ATT_W:]


def _swa_sample_attn_kernel(q_ref, kn_ref, vn_ref, kc_ref, vc_ref, sinks_ref, slopes_ref,
                            o_ref, kwin_ref, vwin_ref):
    bb = kc_ref.shape[0]
    w = kc_ref.shape[1]
    hrow = lax.broadcasted_iota(jnp.int32, (N_Q_HEADS, KV_W), 0) // Q_GROUP
    lblk = lax.broadcasted_iota(jnp.int32, (N_Q_HEADS, KV_W), 1) // HEAD_DIM
    own = hrow == lblk
    dist_c = (w - lax.broadcasted_iota(jnp.int32, (1, w), 1)).astype(F32)
    krow = lax.broadcasted_iota(jnp.int32, (w, KV_W), 0)
    slopes = slopes_ref[...]
    sink = sinks_ref[...]
    qms, scs = [], []
    for b in range(bb):
        q = q_ref[b]
        qm = jnp.where(own, jnp.concatenate([q] * N_KV_HEADS, axis=1), 0.0).astype(BF16)
        qms.append(qm)
        scs.append(_dot_nt(qm, kc_ref[b].astype(BF16)))
    pcs, pns = [], []
    for b in range(bb):
        s_c = scs[b] - slopes * dist_c
        s_n = jnp.sum(qms[b].astype(F32) * kn_ref[b].astype(BF16).astype(F32), axis=-1, keepdims=True)
        mx = jnp.maximum(jnp.maximum(jnp.max(s_c, axis=-1, keepdims=True), s_n), sink)
        p_c = jnp.exp(s_c - mx)
        p_n = jnp.exp(s_n - mx)
        den = jnp.sum(p_c, axis=-1, keepdims=True) + p_n + jnp.exp(sink - mx)
        pcs.append((p_c / den).astype(BF16))
        pns.append(p_n / den)
    rs = [_dot(pcs[b], vc_ref[b].astype(BF16)) for b in range(bb)]
    for b in range(bb):
        kn = kn_ref[b]
        vn = vn_ref[b]
        r = rs[b] + pns[b].astype(BF16).astype(F32) * vn.astype(BF16).astype(F32)
        r = jnp.where(own, r, 0.0)
        acc = r[:, 0:HEAD_DIM]
        for blk in range(1, N_KV_HEADS):
            acc = acc + r[:, blk * HEAD_DIM:(blk + 1) * HEAD_DIM]
        o_ref[b] = acc
        kwin_ref[b] = jnp.where(krow == w - 1, kn, pltpu.roll(kc_ref[b], w - 1, 0))
        vwin_ref[b] = jnp.where(krow == w - 1, vn, pltpu.roll(vc_ref[b], w - 1, 0))


def _swa_sample(y, cache_k, cache_v, kvnorm, wkv, knorm, normb, win, qnorm, sinks, wout):
    n, d = y.shape
    w = cache_k.shape[1]
    f = lambda shape: jax.ShapeDtypeStruct(shape, F32)
    kn, vn, q, gate = pl.pallas_call(
        _swa_sample_front_kernel,
        out_shape=[f((n, KV_W)), f((n, KV_W)), f((n, ATT_W)), f((n, ATT_W))],
        compiler_params=pltpu.CompilerParams(vmem_limit_bytes=VMEM_LIMIT),
        name="swa_sample_front",
    )(y, kvnorm.reshape(1, d), wkv, knorm.reshape(1, HEAD_DIM), normb.reshape(1, d), win,
      qnorm.reshape(1, HEAD_DIM))
    bb = 8
    slopes = (2.0 ** (-8.0 * jnp.arange(1, N_Q_HEADS + 1, dtype=F32) / N_Q_HEADS)).reshape(N_Q_HEADS, 1)
    spec3 = lambda a, c: pl.BlockSpec((bb, a, c), lambda i: (i, 0, 0))
    o, kwin, vwin = pl.pallas_call(
        _swa_sample_attn_kernel,
        grid=(n // bb,),
        in_specs=[spec3(N_Q_HEADS, HEAD_DIM), spec3(1, KV_W), spec3(1, KV_W), spec3(w, KV_W),
                  spec3(w, KV_W), _full_spec((N_Q_HEADS, 1)), _full_spec((N_Q_HEADS, 1))],
        out_specs=[spec3(N_Q_HEADS, HEAD_DIM), spec3(w, KV_W), spec3(w, KV_W)],
        out_shape=[f((n, N_Q_HEADS, HEAD_DIM)), f((n, w, KV_W)), f((n, w, KV_W))],
        compiler_params=pltpu.CompilerParams(dimension_semantics=("arbitrary",)),
        name="swa_sample_attn",
    )(q.reshape(n, N_Q_HEADS, HEAD_DIM), kn.reshape(n, 1, KV_W), vn.reshape(n, 1, KV_W),
      cache_k.reshape(n, w, KV_W), cache_v.reshape(n, w, KV_W),
      sinks.reshape(N_Q_HEADS, 1), slopes)
    out = pl.pallas_call(
        functools.partial(_out_proj_kernel, head_w=ATT_W, norm_heads=False),
        out_shape=f((n, d)),
        name="swa_sample_out",
    )(y, o.reshape(n, ATT_W), gate, jnp.ones((1, ATT_W), F32), wout)
    return out, kwin, vwin


def kernel(x_prompt, x_sample, state_conv, state_ssm, cache_k_win, cache_v_win, norm_a, w_in_a, conv_w_a, a_log, dt_bias, o_norm_a, w_out_a, kv_norm, w_kv, k_norm, norm_b, w_in_b, q_norm, sinks, w_out_b):
    n_a = w_in_a.shape[0]
    n_b = w_in_b.shape[0]
    assert n_a == 1 and n_b == 1, "kernel is written for DEPTH == 2"
    bp, lp, d = x_prompt.shape
    n = x_sample.shape[0]
    hw = GDN_HEADS * GDN_D

    hp, hs = x_prompt, x_sample.reshape(n, d)
    conv_p, ssm_p, conv_s, ssm_s = [], [], [], []
    for layer in range(n_a):
        win_a = w_in_a[layer].astype(BF16)
        wabt = win_a[:, QKV_W + hw:].T
        wout = w_out_a[layer].astype(BF16)
        hp, cbuf, st = _gdn_prompt(hp, norm_a[layer], win_a, wabt, conv_w_a[layer],
                                   a_log[layer], dt_bias[layer], o_norm_a[layer], wout)
        conv_p.append(cbuf)
        ssm_p.append(st)
        hs, cbuf, st = _gdn_sample(hs, state_conv[layer], state_ssm[layer], norm_a[layer], win_a,
                                   conv_w_a[layer], a_log[layer], dt_bias[layer],
                                   o_norm_a[layer], wout)
        conv_s.append(cbuf)
        ssm_s.append(st)

    wkv = w_kv.astype(BF16)
    win = w_in_b[0].astype(BF16)
    woutb = w_out_b[0].astype(BF16)
    hp, k_win_p, v_win_p = _swa_prompt(hp, kv_norm, wkv, k_norm, norm_b[0], win, q_norm[0], sinks[0], woutb)
    hs, k_win_s, v_win_s = _swa_sample(hs, cache_k_win, cache_v_win, kv_norm, wkv, k_norm, norm_b[0],
                                       win, q_norm[0], sinks[0], woutb)
    kv_shape = (N_KV_HEADS, HEAD_DIM)
    return (hp, hs.reshape(n, 1, d), jnp.stack(conv_p), jnp.stack(ssm_p),
            k_win_p.reshape(bp, WINDOW, *kv_shape), v_win_p.reshape(bp, WINDOW, *kv_shape),
            jnp.stack(conv_s), jnp.stack(ssm_s),
            k_win_s.reshape(n, WINDOW, *kv_shape), v_win_s.reshape(n, WINDOW, *kv_shape))
```

```python
import functools

import jax
import jax.numpy as jnp
from jax import lax
from jax.experimental import pallas as pl
from jax.experimental.pallas import tpu as pltpu

F32 = jnp.float32
BF16 = jnp.bfloat16
EPS = 1e-6

D_MODEL = 1024
GDN_HEADS = 8
GDN_D = 128
QKV_W = 3 * GDN_HEADS * GDN_D
CONV_W = 4
CHUNK = 64
N_Q_HEADS = 16
N_KV_HEADS = 4
Q_GROUP = N_Q_HEADS // N_KV_HEADS
HEAD_DIM = 64
KV_W = N_KV_HEADS * HEAD_DIM
ATT_W = N_Q_HEADS * HEAD_DIM
WINDOW = 128

TM_A = 512
SUB_A = 256
TQ_B = 512
CARRY = 8
VMEM_LIMIT = 52 * 1024 * 1024

_NT = (((1,), (1,)), ((), ()))
_TN = (((0,), (0,)), ((), ()))


def _dot(a, b):
    return jnp.dot(a, b, preferred_element_type=F32)


def _dot_nt(a, b):
    return lax.dot_general(a, b, _NT, preferred_element_type=F32)


def _dot_tn(a, b):
    return lax.dot_general(a, b, _TN, preferred_element_type=F32)


def _split(x):
    hi = x.astype(BF16)
    lo = (x - hi.astype(F32)).astype(BF16)
    return hi, lo


def _dot_exact_lhs(a_bf, b):
    b0 = b.astype(BF16)
    r1 = b - b0.astype(F32)
    b1 = r1.astype(BF16)
    b2 = (r1 - b1.astype(F32)).astype(BF16)
    return (_dot(a_bf, b2) + _dot(a_bf, b1)) + _dot(a_bf, b0)


def _dot_exact_rhs(a, b_bf):
    a0 = a.astype(BF16)
    r1 = a - a0.astype(F32)
    a1 = r1.astype(BF16)
    a2 = (r1 - a1.astype(F32)).astype(BF16)
    return (_dot(a2, b_bf) + _dot(a1, b_bf)) + _dot(a0, b_bf)


def _silu(x):
    return x * jax.nn.sigmoid(x)


def _softplus(x):
    return jnp.maximum(x, 0.0) + jnp.log1p(jnp.exp(-jnp.abs(x)))


def _halves(x, left):
    zero = jnp.zeros_like(x)
    return jnp.where(left, x, zero), jnp.where(left, zero, x)


def _block_diag(x, left):
    return jnp.concatenate(_halves(x, left), axis=0)


def _pair_split_product(a, b, left):
    ah, al = _split(a)
    bh, bl = _split(b)
    bh1, bh2 = _halves(bh, left)
    bl1, bl2 = _halves(bl, left)
    zero = jnp.zeros_like(bh)
    rhs = jnp.concatenate([jnp.concatenate([bh1, bl1], axis=1), jnp.concatenate([bh2, bl2], axis=1),
                           jnp.concatenate([bh1, zero], axis=1), jnp.concatenate([bh2, zero], axis=1)], axis=0)
    y = _dot(jnp.concatenate([ah, al], axis=1), rhs)
    half = y.shape[1] // 2
    return y[:, :half] + y[:, half:]


def _unit_lower_inverse_pairs(ms, left):
    n = ms[0].shape[0]
    row = lax.broadcasted_iota(jnp.int32, (n, 2 * n), 0)
    col = lax.broadcasted_iota(jnp.int32, (n, 2 * n), 1) % n
    eye2 = jnp.where(row == col, 1.0, 0.0).astype(F32)

    def times(a, b):
        return _dot(a.astype(BF16), _block_diag(b.astype(BF16), left))

    ts = [eye2 - m for m in ms]
    ps = [times(m, m) for m in ms]
    yield
    steps = max(1, (n - 1).bit_length()) - 1
    for _ in range(steps - 1):
        both = [_dot(jnp.concatenate([t.astype(BF16), p.astype(BF16)], axis=0), _block_diag(p.astype(BF16), left))
                for t, p in zip(ts, ps)]
        ts = [t + tp[:n, :] for t, tp in zip(ts, both)]
        ps = [tp[n:, :] for tp in both]
        yield
    ts = [t + times(t, p) for t, p in zip(ts, ps)]
    yield
    rs = [(eye2 - t) - _pair_split_product(m, t, left) for m, t in zip(ms, ts)]
    yield
    ts = [t + times(t, r) for t, r in zip(ts, rs)]
    yield
    return ts


def _full_spec(shape):
    nd = len(shape)
    return pl.BlockSpec(shape, lambda *_: (0,) * nd)


def _gdn_gates(ab, ab_t, alog_r, dtb_r, alog_c, dtb_c, tm, chunk):
    h = GDN_HEADS
    g_c = -jnp.exp(alog_r) * _softplus(ab[:, :h] + dtb_r)
    beta = jax.nn.sigmoid(ab[:, h:])
    g_r = -jnp.exp(alog_c) * _softplus(ab_t[:h, :] + dtb_c)
    row = lax.broadcasted_iota(jnp.int32, (tm, tm), 0)
    col = lax.broadcasted_iota(jnp.int32, (tm, tm), 1)
    same = (row // chunk) == (col // chunk)
    lower = jnp.where(same & (row >= col), 1.0, 0.0).astype(BF16)
    upper = jnp.where(same & (row <= col), 1.0, 0.0).astype(BF16)
    gc = _dot_exact_lhs(lower, g_c)
    gr = _dot_exact_rhs(g_r, upper)
    return beta, gc, gr


def _gdn_prompt_kernel(x_ref, norm_ref, win_ref, wabt_ref, convw_ref,
                       alog_r_ref, dtb_r_ref, alog_c_ref, dtb_c_ref, onorm_ref, wout_ref,
                       y_ref, conv_ref, ssm_ref,
                       pre_scr, gate_scr, s_scr, k_scr, kb_scr, q_scr, qd_scr, kd_scr,
                       rhs_scr, dec_scr, gt_scr, o_scr, lhs_scr, su_scr):
    tm = x_ref.shape[1]
    sub = SUB_A
    n_sub = tm // sub
    n_chunks = sub // CHUNK
    gate_w = GDN_HEADS * GDN_D
    l = pl.program_id(1)

    @pl.when(l == 0)
    def _():
        pre_scr[0:CARRY, :] = jnp.zeros((CARRY, QKV_W), F32)
        s_scr[...] = jnp.zeros(s_scr.shape, F32)

    ci = lax.broadcasted_iota(jnp.int32, (CHUNK, 2 * CHUNK), 0)
    cj = lax.broadcasted_iota(jnp.int32, (CHUNK, 2 * CHUNK), 1) % CHUNK
    causal = ci >= cj
    strict = ci > cj
    left = lax.broadcasted_iota(jnp.int32, (1, 2 * CHUNK), 1) < CHUNK
    heads = list(range(GDN_HEADS))

    def start(s):
        x = x_ref[0, s * sub:(s + 1) * sub, :]
        xn = x * lax.rsqrt(jnp.mean(x * x, axis=-1, keepdims=True) + EPS) * norm_ref[...]
        xb = xn.astype(BF16)
        ab = _dot(xb, win_ref[:, QKV_W + gate_w:])
        ab_t = _dot_nt(wabt_ref[...], xb)
        beta, gc, gr = _gdn_gates(ab, ab_t, alog_r_ref[...], dtb_r_ref[...],
                                  alog_c_ref[...], dtb_c_ref[...], sub, CHUNK)
        g_last = jnp.concatenate(
            [jnp.broadcast_to(gc[c * CHUNK + CHUNK - 1:c * CHUNK + CHUNK, :], (CHUNK, GDN_HEADS))
             for c in range(n_chunks)], axis=0)
        return dict(x=x, xb=xb, beta=beta, gc=gc, gr=gr, e_g=jnp.exp(gc),
                    e_kd=jnp.exp(g_last - gc), e_tot=jnp.exp(g_last))

    def project(s, st):
        for part in range(3):
            cols = slice(part * gate_w, (part + 1) * gate_w)
            pre_scr[CARRY + s * sub:CARRY + (s + 1) * sub, cols] = _dot(st["xb"], win_ref[:, cols])

    def project_gate(s, st):
        gate_scr[s * sub:(s + 1) * sub, :] = _dot(st["xb"], win_ref[:, QKV_W:QKV_W + gate_w])

    def conv_act(s, j):
        cols = slice(j * GDN_D, (j + 1) * GDN_D)
        ext = pre_scr[s * sub:s * sub + CARRY + sub, cols]
        half = ext[CARRY:, :] * (0.5 * convw_ref[CONV_W - 1:CONV_W, cols])
        for back in range(1, CONV_W):
            tap = CONV_W - 1 - back
            half = half + pltpu.roll(ext, back, 0)[CARRY:, :] * (0.5 * convw_ref[tap:tap + 1, cols])
        return half + half * jnp.tanh(half)

    def head_prep(s, st, h):
        rows = slice(s * sub, (s + 1) * sub)
        qh = conv_act(s, h)
        kh = conv_act(s, GDN_HEADS + h)
        vh = conv_act(s, 2 * GDN_HEADS + h)
        qn = qh * lax.rsqrt(jnp.sum(qh * qh, axis=-1, keepdims=True) + EPS) * (GDN_D ** -0.5)
        kn = kh * lax.rsqrt(jnp.sum(kh * kh, axis=-1, keepdims=True) + EPS)
        b_col = st["beta"][:, h:h + 1]
        eg_col = st["e_g"][:, h:h + 1]
        kb = kn * b_col
        k_scr[h, rows, :] = kn.astype(BF16)
        kb_scr[h, rows, :] = kb.astype(BF16)
        q_scr[h, rows, :] = qn.astype(BF16)
        qd_scr[h, rows, :] = qn * eg_col
        kd_scr[h, rows, :] = (kn * st["e_kd"][:, h:h + 1]).astype(BF16)
        rhs_scr[h, rows, 0:GDN_D] = (vh * b_col).astype(BF16)
        rhs_scr[h, rows, GDN_D:2 * GDN_D] = (kb * eg_col).astype(BF16)
        gc, gr = st["gc"], st["gr"]
        for c in range(n_chunks):
            gt_scr[h, s * n_chunks + c:s * n_chunks + c + 1, :] = jnp.broadcast_to(
                st["e_tot"][c * CHUNK:c * CHUNK + 1, h:h + 1], (1, GDN_D))
        for j in range(n_chunks // 2):
            r1 = slice(2 * j * CHUNK, (2 * j + 1) * CHUNK)
            r2 = slice((2 * j + 1) * CHUNK, (2 * j + 2) * CHUNK)
            diff = jnp.where(left, gc[r1, h:h + 1], gc[r2, h:h + 1]) - gr[h:h + 1, 2 * j * CHUNK:(2 * j + 2) * CHUNK]
            pair = s * (n_chunks // 2) + j
            dec_scr[h, pair * CHUNK:(pair + 1) * CHUNK, :] = jnp.exp(jnp.where(causal, diff, -jnp.inf))

    def chunk_terms(s):
        items = [(h, s * (n_chunks // 2) + j) for h in heads for j in range(n_chunks // 2)]
        aqs = []
        for h, pair in items:
            r12 = slice(2 * pair * CHUNK, (2 * pair + 2) * CHUNK)
            aqs.append(_dot_nt(jnp.concatenate([kb_scr[h, r12, :], q_scr[h, r12, :]], axis=0), k_scr[h, r12, :]))
        yield
        decs = [dec_scr[h, pair * CHUNK:(pair + 1) * CHUNK, :] for h, pair in items]
        ms = [jnp.where(strict, jnp.where(left, aq[:CHUNK, :], aq[CHUNK:2 * CHUNK, :]) * d, 0.0)
              for aq, d in zip(aqs, decs)]
        qks = [jnp.where(causal, jnp.where(left, aq[2 * CHUNK:3 * CHUNK, :], aq[3 * CHUNK:, :]) * d, 0.0)
               for aq, d in zip(aqs, decs)]
        ts = yield from _unit_lower_inverse_pairs(ms, left)
        sols = []
        for (h, pair), t, qk in zip(items, ts, qks):
            lo = t - t.astype(BF16).astype(F32)
            t_parts = (jnp.where(left, t, pltpu.roll(lo, CHUNK, 1)).astype(BF16),
                       jnp.where(left, pltpu.roll(t, CHUNK, 1), lo).astype(BF16))
            qk_parts = (qk[:, :CHUNK].astype(BF16), pltpu.roll(qk, CHUNK, 1)[:, :CHUNK].astype(BF16))
            for half in range(2):
                c = 2 * pair + half
                r = slice(c * CHUNK, (c + 1) * CHUNK)
                rhs = rhs_scr[h, r, :]
                sols.append((h, c, r, qk_parts[half],
                             _dot(t_parts[half], jnp.concatenate([rhs, rhs], axis=0)).astype(BF16)))
        yield
        outs = [(h, c, r, _dot_tn(kd_scr[h, r, :], sol),
                 _dot(qk_c, sol))
                for h, c, r, qk_c, sol in sols]
        for h, c, r, kd_uw, qk_uw in outs:
            lhs_scr[h, c, 0:GDN_D, :] = kd_uw[:, GDN_D:].astype(BF16)
            lhs_scr[h, c, GDN_D:GDN_D + CHUNK, :] = (qd_scr[h, r, :] - qk_uw[:, GDN_D:]).astype(BF16)
            su_scr[h, c] = kd_uw[:, :GDN_D]
            o_scr[h, r, :] = qk_uw[:, :GDN_D]
        yield

    def state_updates(s):
        for c in range(s * n_chunks, (s + 1) * n_chunks):
            rows = slice(c * CHUNK, (c + 1) * CHUNK)
            for h in heads:
                st_h = s_scr[h]
                prod = _dot(lhs_scr[h, c], st_h.astype(BF16))
                o_scr[h, rows, :] = o_scr[h, rows, :] + prod[GDN_D:, :]
                s_scr[h] = st_h * gt_scr[h, c:c + 1, :] + (su_scr[h, c] - prod[:GDN_D, :])
            yield

    def finish(s, st):
        rows = slice(s * sub, (s + 1) * sub)
        pieces = []
        for h in heads:
            o = o_scr[h, rows, :]
            on = o * lax.rsqrt(jnp.mean(o * o, axis=-1, keepdims=True) + EPS) * onorm_ref[...]
            pieces.append((on * _silu(gate_scr[rows, h * GDN_D:(h + 1) * GDN_D])).astype(BF16))
        on_all = jnp.concatenate(pieces, axis=1)
        for piece in range(D_MODEL // (2 * GDN_D)):
            cols = slice(piece * 2 * GDN_D, (piece + 1) * 2 * GDN_D)
            y_ref[0, rows, cols] = st["x"][:, cols] + _dot(on_all, wout_ref[:, cols])
            yield

    def run(gen):
        for _ in gen:
            pass

    def interleave(main, side, every):
        for n, _ in enumerate(main):
            if (n + 1) % every == 0:
                next(side, None)
        run(side)

    sts = [start(0)]
    project(0, sts[0])
    for s in range(n_sub):
        if s + 1 < n_sub:
            sts.append(start(s + 1))
        for h in heads:
            head_prep(s, sts[s], h)
        if s + 1 < n_sub:
            project(s + 1, sts[s + 1])
        project_gate(s, sts[s])
        if s == 0:
            run(chunk_terms(s))
        else:
            interleave(chunk_terms(s), state_updates(s - 1), 2)
            if s >= 2:
                run(finish(s - 2, sts[s - 2]))
    conv_ref[0] = pre_scr[tm + CARRY - 3:tm + CARRY, :]
    pre_scr[0:CARRY, :] = pre_scr[tm:tm + CARRY, :]
    if n_sub >= 2:
        interleave(state_updates(n_sub - 1), finish(n_sub - 2, sts[n_sub - 2]), 1)
    else:
        run(state_updates(0))

    @pl.when(l == pl.num_programs(1) - 1)
    def _():
        ssm_ref[0] = s_scr[...]

    run(finish(n_sub - 1, sts[n_sub - 1]))


def _gdn_prompt(x, norm, win, wabt, convw, alog, dtb, onorm, wout):
    b, l, d = x.shape
    tm = TM_A
    h = GDN_HEADS
    grid = (b, l // tm)
    in_specs = [
        pl.BlockSpec((1, tm, d), lambda i, j: (i, j, 0)),
        _full_spec((1, d)), _full_spec(win.shape), _full_spec(wabt.shape), _full_spec(convw.shape),
        _full_spec((1, h)), _full_spec((1, h)), _full_spec((h, 1)), _full_spec((h, 1)),
        _full_spec((1, GDN_D)), _full_spec(wout.shape),
    ]
    out_specs = [
        pl.BlockSpec((1, tm, d), lambda i, j: (i, j, 0)),
        pl.BlockSpec((1, CONV_W - 1, QKV_W), lambda i, j: (i, 0, 0)),
        pl.BlockSpec((1, h, GDN_D, GDN_D), lambda i, j: (i, 0, 0, 0)),
    ]
    out_shape = [
        jax.ShapeDtypeStruct((b, l, d), F32),
        jax.ShapeDtypeStruct((b, CONV_W - 1, QKV_W), F32),
        jax.ShapeDtypeStruct((b, h, GDN_D, GDN_D), F32),
    ]
    scratch = [
        pltpu.VMEM((tm + CARRY, QKV_W), F32),
        pltpu.VMEM((tm, h * GDN_D), F32),
        pltpu.VMEM((h, GDN_D, GDN_D), F32),
        pltpu.VMEM((h, tm, GDN_D), BF16),
        pltpu.VMEM((h, tm, GDN_D), BF16),
        pltpu.VMEM((h, tm, GDN_D), BF16),
        pltpu.VMEM((h, tm, GDN_D), F32),
        pltpu.VMEM((h, tm, GDN_D), BF16),
        pltpu.VMEM((h, tm, 2 * GDN_D), BF16),
        pltpu.VMEM((h, tm // 2, 2 * CHUNK), F32),
        pltpu.VMEM((h, tm // CHUNK, GDN_D), F32),
        pltpu.VMEM((h, tm, GDN_D), F32),
        pltpu.VMEM((h, tm // CHUNK, GDN_D + CHUNK, GDN_D), BF16),
        pltpu.VMEM((h, tm // CHUNK, GDN_D, GDN_D), F32),
    ]
    return pl.pallas_call(
        _gdn_prompt_kernel,
        grid=grid, in_specs=in_specs, out_specs=out_specs, out_shape=out_shape,
        scratch_shapes=scratch,
        compiler_params=pltpu.CompilerParams(
            dimension_semantics=("arbitrary", "arbitrary"), vmem_limit_bytes=VMEM_LIMIT),
        name="gdn_prompt",
    )(x, norm.reshape(1, d), win, wabt, convw,
      alog.reshape(1, h), dtb.reshape(1, h), alog.reshape(h, 1), dtb.reshape(h, 1),
      onorm.reshape(1, GDN_D), wout)


def _gdn_sample_front_kernel(x_ref, norm_ref, win_ref, convw_ref,
                             c0_ref, c1_ref, c2_ref, alog_r_ref, dtb_r_ref,
                             pre_ref, q_ref, k_ref, u_ref, w_ref, qd_ref, gate_ref, eg_ref):
    x = x_ref[...]
    xn = x * lax.rsqrt(jnp.mean(x * x, axis=-1, keepdims=True) + EPS) * norm_ref[...]
    xb = xn.astype(BF16)
    gate_w = GDN_HEADS * GDN_D
    pre = _dot(xb, win_ref[:, :QKV_W])
    pre_ref[...] = pre
    gate_ref[...] = _dot(xb, win_ref[:, QKV_W:QKV_W + gate_w])
    ab = _dot(xb, win_ref[:, QKV_W + gate_w:])
    g = -jnp.exp(alog_r_ref[...]) * _softplus(ab[:, :GDN_HEADS] + dtb_r_ref[...])
    beta = jax.nn.sigmoid(ab[:, GDN_HEADS:])
    e_g = jnp.exp(g)
    eg_ref[...] = e_g
    act = _silu(((c0_ref[...] * convw_ref[0:1, :] + c1_ref[...] * convw_ref[1:2, :])
                 + c2_ref[...] * convw_ref[2:3, :]) + pre * convw_ref[3:4, :])
    w_all = GDN_HEADS * GDN_D
    for h in range(GDN_HEADS):
        cols = slice(h * GDN_D, (h + 1) * GDN_D)
        qh = act[:, h * GDN_D:(h + 1) * GDN_D]
        kh = act[:, w_all + h * GDN_D:w_all + (h + 1) * GDN_D]
        vh = act[:, 2 * w_all + h * GDN_D:2 * w_all + (h + 1) * GDN_D]
        qn = qh * lax.rsqrt(jnp.sum(qh * qh, axis=-1, keepdims=True) + EPS) * (GDN_D ** -0.5)
        kn = kh * lax.rsqrt(jnp.sum(kh * kh, axis=-1, keepdims=True) + EPS)
        b_col = beta[:, h:h + 1]
        eg_col = e_g[:, h:h + 1]
        q_ref[:, cols] = qn
        k_ref[:, cols] = kn
        u_ref[:, cols] = vh * b_col
        w_ref[:, cols] = kn * b_col * eg_col
        qd_ref[:, cols] = qn * eg_col


def _gdn_sample_state_kernel(q_ref, k_ref, u_ref, w_ref, qd_ref, eg_ref, s_ref, o_ref, s_out_ref):
    bb = s_ref.shape[0]
    row = lax.broadcasted_iota(jnp.int32, (8, GDN_D), 0)
    pairs = [(b, h) for b in range(bb) for h in range(GDN_HEADS)]
    prods = []
    for b, h in pairs:
        cols = slice(h * GDN_D, (h + 1) * GDN_D)
        lhs = jnp.where(row == 0, w_ref[b:b + 1, cols], jnp.where(row == 1, qd_ref[b:b + 1, cols], 0.0))
        prods.append(_dot(lhs.astype(BF16), s_ref[b, h].astype(BF16)))
    for (b, h), prod in zip(pairs, prods):
        cols = slice(h * GDN_D, (h + 1) * GDN_D)
        q = q_ref[b:b + 1, cols].astype(BF16).astype(F32)
        k = k_ref[b:b + 1, cols].astype(BF16).astype(F32)
        v_new = u_ref[b:b + 1, cols] - prod[0:1, :]
        vb = v_new.astype(BF16).astype(F32)
        qk = jnp.sum(q * k, axis=-1, keepdims=True)
        o_ref[b:b + 1, cols] = prod[1:2, :] + qk.astype(BF16).astype(F32) * vb
        k8 = jnp.where(row == 0, k, 0.0).astype(BF16)
        v8 = jnp.where(row == 0, vb, 0.0).astype(BF16)
        s_out_ref[b, h] = s_ref[b, h] * eg_ref[b:b + 1, h:h + 1] + _dot_tn(k8, v8)


def _out_proj_kernel(x_ref, o_ref, gate_ref, onorm_ref, wout_ref, y_ref, *, head_w, norm_heads):
    pieces = []
    for h in range(x_ref.shape[1] // head_w):
        cols = slice(h * head_w, (h + 1) * head_w)
        o = o_ref[:, cols]
        if norm_heads:
            o = o * lax.rsqrt(jnp.mean(o * o, axis=-1, keepdims=True) + EPS) * onorm_ref[...]
        pieces.append((o * _silu(gate_ref[:, cols])).astype(BF16))
    y_ref[...] = x_ref[...] + _dot(jnp.concatenate(pieces, axis=1), wout_ref[...])


def _gdn_sample(x, conv_state, ssm_state, norm, win, convw, alog, dtb, onorm, wout):
    n, d = x.shape
    h = GDN_HEADS
    hw = h * GDN_D
    conv_t = jnp.transpose(conv_state, (1, 0, 2))
    f = lambda shape: jax.ShapeDtypeStruct(shape, F32)
    pre, q, k, u, w, qd, gate, eg = pl.pallas_call(
        _gdn_sample_front_kernel,
        out_shape=[f((n, QKV_W)), f((n, hw)), f((n, hw)), f((n, hw)), f((n, hw)), f((n, hw)),
                   f((n, hw)), f((n, h))],
        compiler_params=pltpu.CompilerParams(vmem_limit_bytes=VMEM_LIMIT),
        name="gdn_sample_front",
    )(x, norm.reshape(1, d), win, convw, conv_t[0], conv_t[1], conv_t[2],
      alog.reshape(1, h), dtb.reshape(1, h))
    bb = 8
    row_spec = lambda width: pl.BlockSpec((bb, width), lambda i: (i, 0))
    st_spec = pl.BlockSpec((bb, h, GDN_D, GDN_D), lambda i: (i, 0, 0, 0))
    o, s_new = pl.pallas_call(
        _gdn_sample_state_kernel,
        grid=(n // bb,),
        in_specs=[row_spec(hw)] * 5 + [row_spec(h), st_spec],
        out_specs=[row_spec(hw), st_spec],
        out_shape=[f((n, hw)), f(ssm_state.shape)],
        compiler_params=pltpu.CompilerParams(dimension_semantics=("arbitrary",)),
        name="gdn_sample_state",
    )(q, k, u, w, qd, eg, ssm_state)
    y = pl.pallas_call(
        functools.partial(_out_proj_kernel, head_w=GDN_D, norm_heads=True),
        out_shape=f((n, d)),
        name="gdn_sample_out",
    )(x, o, gate, onorm.reshape(1, GDN_D), wout)
    conv_new = jnp.stack([conv_t[1], conv_t[2], pre], axis=1)
    return y, conv_new, s_new


def _head_rms(x, gain_row, n_heads):
    pieces = []
    for h in range(n_heads):
        xh = x[:, h * HEAD_DIM:(h + 1) * HEAD_DIM]
        pieces.append(xh * lax.rsqrt(jnp.mean(xh * xh, axis=-1, keepdims=True) + EPS) * gain_row)
    return pieces


def _pair_rms(x, gain2, left):
    sq = x * x
    lo = jnp.sum(jnp.where(left, sq, 0.0), axis=-1, keepdims=True)
    hi = jnp.sum(jnp.where(left, 0.0, sq), axis=-1, keepdims=True)
    ms = jnp.where(left, lo, hi) * (1.0 / HEAD_DIM)
    return x * lax.rsqrt(ms + EPS) * gain2


def _swa_prompt_kernel(y_ref, kvnorm_ref, wkv_ref, knorm2_ref, normb_ref, win_ref, qnorm2_ref,
                       sinks_ref, wout_ref, out_ref, kwin_ref, vwin_ref, k_scr, v_scr, o_scr):
    tq = y_ref.shape[1]
    w = WINDOW
    step = pl.program_id(1)
    lane = lax.broadcasted_iota(jnp.int32, (1, 2 * HEAD_DIM), 1)
    left = lane < HEAD_DIM

    @pl.when(step == 0)
    def _():
        k_scr[:, 0:w, :] = jnp.zeros((N_KV_HEADS, w, 2 * HEAD_DIM), BF16)
        v_scr[:, 0:w, :] = jnp.zeros((N_KV_HEADS, w, 2 * HEAD_DIM), BF16)

    @pl.when(step > 0)
    def _():
        k_scr[:, 0:w, :] = k_scr[:, tq:tq + w, :]
        v_scr[:, 0:w, :] = v_scr[:, tq:tq + w, :]

    y = y_ref[0]
    yn = y * lax.rsqrt(jnp.mean(y * y, axis=-1, keepdims=True) + EPS)
    kv = _dot((yn * kvnorm_ref[...]).astype(BF16), wkv_ref[...])
    for c in range(N_KV_HEADS // 2):
        cols = slice(c * 2 * HEAD_DIM, (c + 1) * 2 * HEAD_DIM)
        kp = _pair_rms(kv[:, cols], knorm2_ref[...], left)
        vp = kv[:, KV_W + c * 2 * HEAD_DIM:KV_W + (c + 1) * 2 * HEAD_DIM]
        kwin_ref[0, :, cols] = kp[tq - w:, :]
        vwin_ref[0, :, cols] = vp[tq - w:, :]
        kr = pltpu.roll(kp, HEAD_DIM, 1)
        vr = pltpu.roll(vp, HEAD_DIM, 1)
        k_scr[2 * c, w:w + tq, :] = jnp.where(left, kp, kr).astype(BF16)
        k_scr[2 * c + 1, w:w + tq, :] = jnp.where(left, kr, kp).astype(BF16)
        v_scr[2 * c, w:w + tq, :] = jnp.where(left, vp, vr).astype(BF16)
        v_scr[2 * c + 1, w:w + tq, :] = jnp.where(left, vr, vp).astype(BF16)

    qg = _dot((yn * normb_ref[...]).astype(BF16), win_ref[...])
    q_pairs = [_pair_rms(qg[:, c * 2 * HEAD_DIM:(c + 1) * 2 * HEAD_DIM], qnorm2_ref[...], left)
               for c in range(N_Q_HEADS // 2)]

    kj = lax.broadcasted_iota(jnp.int32, (2 * w, w), 0)
    qi = lax.broadcasted_iota(jnp.int32, (2 * w, w), 1)
    dist = qi - kj + w
    band = (dist >= 0) & (dist <= w)
    j_rel = (kj - w).astype(F32)
    i_row = lax.broadcasted_iota(jnp.int32, (1, w), 1).astype(F32)

    def scores(blk, hk):
        q_rows = slice(blk * w, (blk + 1) * w)
        lhs = []
        for c in (2 * hk, 2 * hk + 1):
            qp = q_pairs[c][q_rows, :]
            lhs.append(jnp.where(left, qp, 0.0))
            lhs.append(jnp.where(left, 0.0, qp))
        return _dot_nt(k_scr[hk, blk * w:blk * w + 2 * w, :],
                       jnp.concatenate(lhs, axis=0).astype(BF16))

    def attend(blk, hk, s4):
        q_rows = slice(blk * w, (blk + 1) * w)
        valid = band & ((step > 0) | (kj >= w)) if blk == 0 else band
        ps = []
        for g in range(Q_GROUP):
            hq = hk * Q_GROUP + g
            slope = 2.0 ** (-8.0 * (hq + 1) / N_Q_HEADS)
            a = jnp.where(valid, s4[:, g * w:(g + 1) * w] + slope * j_rel, -jnp.inf)
            sink = sinks_ref[0:1, hq:hq + 1] + slope * i_row
            mx = jnp.maximum(jnp.max(a, axis=0, keepdims=True), sink)
            p = jnp.exp(a - mx)
            inv = 1.0 / (jnp.sum(p, axis=0, keepdims=True) + jnp.exp(sink - mx))
            ps.append((p * inv).astype(BF16))
        o4 = _dot_tn(jnp.concatenate(ps, axis=1),
                     v_scr[hk, blk * w:blk * w + 2 * w, :])
        for j, c in enumerate((2 * hk, 2 * hk + 1)):
            o_scr[q_rows, c * 2 * HEAD_DIM:(c + 1) * 2 * HEAD_DIM] = jnp.where(
                left, o4[(2 * j) * w:(2 * j + 1) * w, :], o4[(2 * j + 1) * w:(2 * j + 2) * w, :])

    items = [(blk, hk) for blk in range(tq // w) for hk in range(N_KV_HEADS)]
    s_next = scores(*items[0])
    for n, item in enumerate(items):
        s_cur = s_next
        if n + 1 < len(items):
            s_next = scores(*items[n + 1])
        attend(*item, s_cur)

    o = (o_scr[...] * _silu(qg[:, ATT_W:])).astype(BF16)
    out_ref[0] = y + _dot(o, wout_ref[...])


def _swa_prompt(y, kvnorm, wkv, knorm, normb, win, qnorm, sinks, wout):
    b, l, d = y.shape
    tq = TQ_B
    w = WINDOW
    f = lambda shape: jax.ShapeDtypeStruct(shape, F32)
    knorm2 = jnp.concatenate([knorm, knorm]).reshape(1, 2 * HEAD_DIM)
    qnorm2 = (jnp.concatenate([qnorm, qnorm]) * (HEAD_DIM ** -0.5)).reshape(1, 2 * HEAD_DIM)
    in_specs = [
        pl.BlockSpec((1, tq, d), lambda i, j: (i, j, 0)),
        _full_spec((1, d)), _full_spec(wkv.shape), _full_spec((1, 2 * HEAD_DIM)), _full_spec((1, d)),
        _full_spec(win.shape), _full_spec((1, 2 * HEAD_DIM)), _full_spec((1, N_Q_HEADS)),
        _full_spec(wout.shape),
    ]
    out_specs = [
        pl.BlockSpec((1, tq, d), lambda i, j: (i, j, 0)),
        pl.BlockSpec((1, w, KV_W), lambda i, j: (i, 0, 0)),
        pl.BlockSpec((1, w, KV_W), lambda i, j: (i, 0, 0)),
    ]
    return pl.pallas_call(
        _swa_prompt_kernel,
        grid=(b, l // tq), in_specs=in_specs, out_specs=out_specs,
        out_shape=[f((b, l, d)), f((b, w, KV_W)), f((b, w, KV_W))],
        scratch_shapes=[pltpu.VMEM((N_KV_HEADS, w + tq, 2 * HEAD_DIM), BF16),
                        pltpu.VMEM((N_KV_HEADS, w + tq, 2 * HEAD_DIM), BF16),
                        pltpu.VMEM((tq, ATT_W), F32)],
        compiler_params=pltpu.CompilerParams(
            dimension_semantics=("arbitrary", "arbitrary"), vmem_limit_bytes=VMEM_LIMIT),
        name="swa_prompt",
    )(y, kvnorm.reshape(1, d), wkv, knorm2, normb.reshape(1, d), win, qnorm2,
      sinks.reshape(1, N_Q_HEADS), wout)


def _swa_sample_front_kernel(y_ref, kvnorm_ref, wkv_ref, knorm_ref, normb_ref, win_ref, qnorm_ref,
                             k_ref, v_ref, q_ref, gate_ref):
    y = y_ref[...]
    yn = y * lax.rsqrt(jnp.mean(y * y, axis=-1, keepdims=True) + EPS)
    kv = _dot((yn * kvnorm_ref[...]).astype(BF16), wkv_ref[...])
    k_ref[...] = jnp.concatenate(_head_rms(kv[:, :KV_W], knorm_ref[...], N_KV_HEADS), axis=1)
    v_ref[...] = kv[:, KV_W:]
    qg = _dot((yn * normb_ref[...]).astype(BF16), win_ref[...])
    q_ref[...] = jnp.concatenate(
        _head_rms(qg[:, :ATT_W], qnorm_ref[...] * (HEAD_DIM ** -0.5), N_Q_HEADS), axis=1)
    gate_ref[...] = qg[:, ATT_W:]


def _swa_sample_attn_kernel(q_ref, kn_ref, vn_ref, kc_ref, vc_ref, sinks_ref, slopes_ref,
                            o_ref, kwin_ref, vwin_ref):
    bb = kc_ref.shape[0]
    w = kc_ref.shape[1]
    hrow = lax.broadcasted_iota(jnp.int32, (N_Q_HEADS, KV_W), 0) // Q_GROUP
    lblk = lax.broadcasted_iota(jnp.int32, (N_Q_HEADS, KV_W), 1) // HEAD_DIM
    own = hrow == lblk
    dist_c = (w - lax.broadcasted_iota(jnp.int32, (1, w), 1)).astype(F32)
    krow = lax.broadcasted_iota(jnp.int32, (w, KV_W), 0)
    slopes = slopes_ref[...]
    sink = sinks_ref[...]
    qms, scs = [], []
    for b in range(bb):
        q = q_ref[b]
        qm = jnp.where(own, jnp.concatenate([q] * N_KV_HEADS, axis=1), 0.0).astype(BF16)
        qms.append(qm)
        scs.append(_dot_nt(qm, kc_ref[b].astype(BF16)))
    pcs, pns = [], []
    for b in range(bb):
        s_c = scs[b] - slopes * dist_c
        s_n = jnp.sum(qms[b].astype(F32) * kn_ref[b].astype(BF16).astype(F32), axis=-1, keepdims=True)
        mx = jnp.maximum(jnp.maximum(jnp.max(s_c, axis=-1, keepdims=True), s_n), sink)
        p_c = jnp.exp(s_c - mx)
        p_n = jnp.exp(s_n - mx)
        den = jnp.sum(p_c, axis=-1, keepdims=True) + p_n + jnp.exp(sink - mx)
        pcs.append((p_c / den).astype(BF16))
        pns.append(p_n / den)
    rs = [_dot(pcs[b], vc_ref[b].astype(BF16)) for b in range(bb)]
    for b in range(bb):
        kn = kn_ref[b]
        vn = vn_ref[b]
        r = rs[b] + pns[b].astype(BF16).astype(F32) * vn.astype(BF16).astype(F32)
        r = jnp.where(own, r, 0.0)
        acc = r[:, 0:HEAD_DIM]
        for blk in range(1, N_KV_HEADS):
            acc = acc + r[:, blk * HEAD_DIM:(blk + 1) * HEAD_DIM]
        o_ref[b] = acc
        kwin_ref[b] = jnp.where(krow == w - 1, kn, pltpu.roll(kc_ref[b], w - 1, 0))
        vwin_ref[b] = jnp.where(krow == w - 1, vn, pltpu.roll(vc_ref[b], w - 1, 0))


def _swa_sample(y, cache_k, cache_v, kvnorm, wkv, knorm, normb, win, qnorm, sinks, wout):
    n, d = y.shape
    w = cache_k.shape[1]
    f = lambda shape: jax.ShapeDtypeStruct(shape, F32)
    kn, vn, q, gate = pl.pallas_call(
        _swa_sample_front_kernel,
        out_shape=[f((n, KV_W)), f((n, KV_W)), f((n, ATT_W)), f((n, ATT_W))],
        compiler_params=pltpu.CompilerParams(vmem_limit_bytes=VMEM_LIMIT),
        name="swa_sample_front",
    )(y, kvnorm.reshape(1, d), wkv, knorm.reshape(1, HEAD_DIM), normb.reshape(1, d), win,
      qnorm.reshape(1, HEAD_DIM))
    bb = 8
    slopes = (2.0 ** (-8.0 * jnp.arange(1, N_Q_HEADS + 1, dtype=F32) / N_Q_HEADS)).reshape(N_Q_HEADS, 1)
    spec3 = lambda a, c: pl.BlockSpec((bb, a, c), lambda i: (i, 0, 0))
    o, kwin, vwin = pl.pallas_call(
        _swa_sample_attn_kernel,
        grid=(n // bb,),
        in_specs=[spec3(N_Q_HEADS, HEAD_DIM), spec3(1, KV_W), spec3(1, KV_W), spec3(w, KV_W),
                  spec3(w, KV_W), _full_spec((N_Q_HEADS, 1)), _full_spec((N_Q_HEADS, 1))],
        out_specs=[spec3(N_Q_HEADS, HEAD_DIM), spec3(w, KV_W), spec3(w, KV_W)],
        out_shape=[f((n, N_Q_HEADS, HEAD_DIM)), f((n, w, KV_W)), f((n, w, KV_W))],
        compiler_params=pltpu.CompilerParams(dimension_semantics=("arbitrary",)),
        name="swa_sample_attn",
    )(q.reshape(n, N_Q_HEADS, HEAD_DIM), kn.reshape(n, 1, KV_W), vn.reshape(n, 1, KV_W),
      cache_k.reshape(n, w, KV_W), cache_v.reshape(n, w, KV_W),
      sinks.reshape(N_Q_HEADS, 1), slopes)
    out = pl.pallas_call(
        functools.partial(_out_proj_kernel, head_w=ATT_W, norm_heads=False),
        out_shape=f((n, d)),
        name="swa_sample_out",
    )(y, o.reshape(n, ATT_W), gate, jnp.ones((1, ATT_W), F32), wout)
    return out, kwin, vwin


def kernel(x_prompt, x_sample, state_conv, state_ssm, cache_k_win, cache_v_win, norm_a, w_in_a, conv_w_a, a_log, dt_bias, o_norm_a, w_out_a, kv_norm, w_kv, k_norm, norm_b, w_in_b, q_norm, sinks, w_out_b):
    n_a = w_in_a.shape[0]
    n_b = w_in_b.shape[0]
    assert n_a == 1 and n_b == 1, "kernel is written for DEPTH == 2"
    bp, lp, d = x_prompt.shape
    n = x_sample.shape[0]
    hw = GDN_HEADS * GDN_D

    hp, hs = x_prompt, x_sample.reshape(n, d)
    conv_p, ssm_p, conv_s, ssm_s = [], [], [], []
    for layer in range(n_a):
        win_a = w_in_a[layer].astype(BF16)
        wabt = win_a[:, QKV_W + hw:].T
        wout = w_out_a[layer].astype(BF16)
        hp, cbuf, st = _gdn_prompt(hp, norm_a[layer], win_a, wabt, conv_w_a[layer],
                                   a_log[layer], dt_bias[layer], o_norm_a[layer], wout)
        conv_p.append(cbuf)
        ssm_p.append(st)
        hs, cbuf, st = _gdn_sample(hs, state_conv[layer], state_ssm[layer], norm_a[layer], win_a,
                                   conv_w_a[layer], a_log[layer], dt_bias[layer],
                                   o_norm_a[layer], wout)
        conv_s.append(cbuf)
        ssm_s.append(st)

    wkv = w_kv.astype(BF16)
    win = w_in_b[0].astype(BF16)
    woutb = w_out_b[0].astype(BF16)
    hp, k_win_p, v_win_p = _swa_prompt(hp, kv_norm, wkv, k_norm, norm_b[0], win, q_norm[0], sinks[0], woutb)
    hs, k_win_s, v_win_s = _swa_sample(hs, cache_k_win, cache_v_win, kv_norm, wkv, k_norm, norm_b[0],
                                       win, q_norm[0], sinks[0], woutb)
    kv_shape = (N_KV_HEADS, HEAD_DIM)
    return (hp, hs.reshape(n, 1, d), jnp.stack(conv_p), jnp.stack(ssm_p),
            k_win_p.reshape(bp, WINDOW, *kv_shape), v_win_p.reshape(bp, WINDOW, *kv_shape),
            jnp.stack(conv_s), jnp.stack(ssm_s),
            k_win_s.reshape(n, WINDOW, *kv_shape), v_win_s.reshape(n, WINDOW, *kv_shape))
```

```python
import functools

import jax
import jax.numpy as jnp
from jax import lax
from jax.experimental import pallas as pl
from jax.experimental.pallas import tpu as pltpu

F32 = jnp.float32
BF16 = jnp.bfloat16
EPS = 1e-6

D_MODEL = 1024
GDN_HEADS = 8
GDN_D = 128
QKV_W = 3 * GDN_HEADS * GDN_D
CONV_W = 4
CHUNK = 64
N_Q_HEADS = 16
N_KV_HEADS = 4
Q_GROUP = N_Q_HEADS // N_KV_HEADS
HEAD_DIM = 64
KV_W = N_KV_HEADS * HEAD_DIM
ATT_W = N_Q_HEADS * HEAD_DIM
WINDOW = 128

TM_A = 512
SUB_A = 256
TQ_B = 1024
CARRY = 8
VMEM_LIMIT = 52 * 1024 * 1024

_NT = (((1,), (1,)), ((), ()))
_TN = (((0,), (0,)), ((), ()))


def _dot(a, b):
    return jnp.dot(a, b, preferred_element_type=F32)


def _dot_nt(a, b):
    return lax.dot_general(a, b, _NT, preferred_element_type=F32)


def _dot_tn(a, b):
    return lax.dot_general(a, b, _TN, preferred_element_type=F32)


def _split(x):
    hi = x.astype(BF16)
    lo = (x - hi.astype(F32)).astype(BF16)
    return hi, lo


def _dot_exact_lhs(a_bf, b):
    b0 = b.astype(BF16)
    r1 = b - b0.astype(F32)
    b1 = r1.astype(BF16)
    b2 = (r1 - b1.astype(F32)).astype(BF16)
    return (_dot(a_bf, b2) + _dot(a_bf, b1)) + _dot(a_bf, b0)


def _dot_exact_rhs(a, b_bf):
    a0 = a.astype(BF16)
    r1 = a - a0.astype(F32)
    a1 = r1.astype(BF16)
    a2 = (r1 - a1.astype(F32)).astype(BF16)
    return (_dot(a2, b_bf) + _dot(a1, b_bf)) + _dot(a0, b_bf)


def _silu(x):
    return x * jax.nn.sigmoid(x)


def _softplus(x):
    return jnp.maximum(x, 0.0) + jnp.log1p(jnp.exp(-jnp.abs(x)))


def _halves(x, left):
    zero = jnp.zeros_like(x)
    return jnp.where(left, x, zero), jnp.where(left, zero, x)


def _block_diag(x, left):
    return jnp.concatenate(_halves(x, left), axis=0)


def _pair_split_product(a, b, left):
    ah, al = _split(a)
    bh, bl = _split(b)
    bh1, bh2 = _halves(bh, left)
    bl1, bl2 = _halves(bl, left)
    zero = jnp.zeros_like(bh)
    rhs = jnp.concatenate([jnp.concatenate([bh1, bl1], axis=1), jnp.concatenate([bh2, bl2], axis=1),
                           jnp.concatenate([bh1, zero], axis=1), jnp.concatenate([bh2, zero], axis=1)], axis=0)
    y = _dot(jnp.concatenate([ah, al], axis=1), rhs)
    half = y.shape[1] // 2
    return y[:, :half] + y[:, half:]


def _unit_lower_inverse_pairs(ms, left):
    n = ms[0].shape[0]
    row = lax.broadcasted_iota(jnp.int32, (n, 2 * n), 0)
    col = lax.broadcasted_iota(jnp.int32, (n, 2 * n), 1) % n
    eye2 = jnp.where(row == col, 1.0, 0.0).astype(F32)

    def times(a, b):
        return _dot(a.astype(BF16), _block_diag(b.astype(BF16), left))

    ts = [eye2 - m for m in ms]
    ps = [times(m, m) for m in ms]
    yield
    steps = max(1, (n - 1).bit_length()) - 1
    for _ in range(steps - 1):
        both = [_dot(jnp.concatenate([t.astype(BF16), p.astype(BF16)], axis=0), _block_diag(p.astype(BF16), left))
                for t, p in zip(ts, ps)]
        ts = [t + tp[:n, :] for t, tp in zip(ts, both)]
        ps = [tp[n:, :] for tp in both]
        yield
    ts = [t + times(t, p) for t, p in zip(ts, ps)]
    yield
    rs = [(eye2 - t) - _pair_split_product(m, t, left) for m, t in zip(ms, ts)]
    yield
    ts = [t + times(t, r) for t, r in zip(ts, rs)]
    yield
    return ts


def _full_spec(shape):
    nd = len(shape)
    return pl.BlockSpec(shape, lambda *_: (0,) * nd)


def _gdn_gates(ab, ab_t, alog_r, dtb_r, alog_c, dtb_c, tm, chunk):
    h = GDN_HEADS
    g_c = -jnp.exp(alog_r) * _softplus(ab[:, :h] + dtb_r)
    beta = jax.nn.sigmoid(ab[:, h:])
    g_r = -jnp.exp(alog_c) * _softplus(ab_t[:h, :] + dtb_c)
    row = lax.broadcasted_iota(jnp.int32, (tm, tm), 0)
    col = lax.broadcasted_iota(jnp.int32, (tm, tm), 1)
    same = (row // chunk) == (col // chunk)
    lower = jnp.where(same & (row >= col), 1.0, 0.0).astype(BF16)
    upper = jnp.where(same & (row <= col), 1.0, 0.0).astype(BF16)
    gc = _dot_exact_lhs(lower, g_c)
    gr = _dot_exact_rhs(g_r, upper)
    return beta, gc, gr


def _gdn_prompt_kernel(x_ref, norm_ref, win_ref, wabt_ref, convw_ref,
                       alog_r_ref, dtb_r_ref, alog_c_ref, dtb_c_ref, onorm_ref, wout_ref,
                       y_ref, conv_ref, ssm_ref,
                       pre_scr, gate_scr, s_scr, k_scr, kb_scr, q_scr, qd_scr, kd_scr,
                       rhs_scr, dec_scr, gt_scr, o_scr, lhs_scr, su_scr):
    tm = x_ref.shape[1]
    sub = SUB_A
    n_sub = tm // sub
    n_chunks = sub // CHUNK
    gate_w = GDN_HEADS * GDN_D
    l = pl.program_id(1)

    @pl.when(l == 0)
    def _():
        pre_scr[0:CARRY, :] = jnp.zeros((CARRY, QKV_W), F32)
        s_scr[...] = jnp.zeros(s_scr.shape, F32)

    ci = lax.broadcasted_iota(jnp.int32, (CHUNK, 2 * CHUNK), 0)
    cj = lax.broadcasted_iota(jnp.int32, (CHUNK, 2 * CHUNK), 1) % CHUNK
    causal = ci >= cj
    strict = ci > cj
    left = lax.broadcasted_iota(jnp.int32, (1, 2 * CHUNK), 1) < CHUNK
    heads = list(range(GDN_HEADS))

    def start(s):
        x = x_ref[0, s * sub:(s + 1) * sub, :]
        xn = x * lax.rsqrt(jnp.mean(x * x, axis=-1, keepdims=True) + EPS) * norm_ref[...]
        xb = xn.astype(BF16)
        ab = _dot(xb, win_ref[:, QKV_W + gate_w:])
        ab_t = _dot_nt(wabt_ref[...], xb)
        beta, gc, gr = _gdn_gates(ab, ab_t, alog_r_ref[...], dtb_r_ref[...],
                                  alog_c_ref[...], dtb_c_ref[...], sub, CHUNK)
        g_last = jnp.concatenate(
            [jnp.broadcast_to(gc[c * CHUNK + CHUNK - 1:c * CHUNK + CHUNK, :], (CHUNK, GDN_HEADS))
             for c in range(n_chunks)], axis=0)
        return dict(x=x, xb=xb, beta=beta, gc=gc, gr=gr, e_g=jnp.exp(gc),
                    e_kd=jnp.exp(g_last - gc), e_tot=jnp.exp(g_last))

    def project(s, st):
        for part in range(3):
            cols = slice(part * gate_w, (part + 1) * gate_w)
            pre_scr[CARRY + s * sub:CARRY + (s + 1) * sub, cols] = _dot(st["xb"], win_ref[:, cols])

    def project_gate(s, st):
        gate_scr[s * sub:(s + 1) * sub, :] = _dot(st["xb"], win_ref[:, QKV_W:QKV_W + gate_w])

    def conv_act(s, j):
        cols = slice(j * GDN_D, (j + 1) * GDN_D)
        ext = pre_scr[s * sub:s * sub + CARRY + sub, cols]
        half = ext[CARRY:, :] * (0.5 * convw_ref[CONV_W - 1:CONV_W, cols])
        for back in range(1, CONV_W):
            tap = CONV_W - 1 - back
            half = half + pltpu.roll(ext, back, 0)[CARRY:, :] * (0.5 * convw_ref[tap:tap + 1, cols])
        return half + half * jnp.tanh(half)

    def head_prep(s, st, h):
        rows = slice(s * sub, (s + 1) * sub)
        qh = conv_act(s, h)
        kh = conv_act(s, GDN_HEADS + h)
        vh = conv_act(s, 2 * GDN_HEADS + h)
        qn = qh * lax.rsqrt(jnp.sum(qh * qh, axis=-1, keepdims=True) + EPS) * (GDN_D ** -0.5)
        kn = kh * lax.rsqrt(jnp.sum(kh * kh, axis=-1, keepdims=True) + EPS)
        b_col = st["beta"][:, h:h + 1]
        eg_col = st["e_g"][:, h:h + 1]
        kb = kn * b_col
        k_scr[h, rows, :] = kn.astype(BF16)
        kb_scr[h, rows, :] = kb.astype(BF16)
        q_scr[h, rows, :] = qn.astype(BF16)
        qd_scr[h, rows, :] = qn * eg_col
        kd_scr[h, rows, :] = (kn * st["e_kd"][:, h:h + 1]).astype(BF16)
        rhs_scr[h, rows, 0:GDN_D] = (vh * b_col).astype(BF16)
        rhs_scr[h, rows, GDN_D:2 * GDN_D] = (kb * eg_col).astype(BF16)
        gc, gr = st["gc"], st["gr"]
        for c in range(n_chunks):
            gt_scr[h, s * n_chunks + c:s * n_chunks + c + 1, :] = jnp.broadcast_to(
                st["e_tot"][c * CHUNK:c * CHUNK + 1, h:h + 1], (1, GDN_D))
        for j in range(n_chunks // 2):
            r1 = slice(2 * j * CHUNK, (2 * j + 1) * CHUNK)
            r2 = slice((2 * j + 1) * CHUNK, (2 * j + 2) * CHUNK)
            diff = jnp.where(left, gc[r1, h:h + 1], gc[r2, h:h + 1]) - gr[h:h + 1, 2 * j * CHUNK:(2 * j + 2) * CHUNK]
            pair = s * (n_chunks // 2) + j
            dec_scr[h, pair * CHUNK:(pair + 1) * CHUNK, :] = jnp.exp(jnp.where(causal, diff, -jnp.inf))

    def chunk_terms(s):
        items = [(h, s * (n_chunks // 2) + j) for h in heads for j in range(n_chunks // 2)]
        aqs = []
        for h, pair in items:
            r12 = slice(2 * pair * CHUNK, (2 * pair + 2) * CHUNK)
            aqs.append(_dot_nt(jnp.concatenate([kb_scr[h, r12, :], q_scr[h, r12, :]], axis=0), k_scr[h, r12, :]))
        yield
        decs = [dec_scr[h, pair * CHUNK:(pair + 1) * CHUNK, :] for h, pair in items]
        ms = [jnp.where(strict, jnp.where(left, aq[:CHUNK, :], aq[CHUNK:2 * CHUNK, :]) * d, 0.0)
              for aq, d in zip(aqs, decs)]
        qks = [jnp.where(causal, jnp.where(left, aq[2 * CHUNK:3 * CHUNK, :], aq[3 * CHUNK:, :]) * d, 0.0)
               for aq, d in zip(aqs, decs)]
        ts = yield from _unit_lower_inverse_pairs(ms, left)
        sols = []
        for (h, pair), t, qk in zip(items, ts, qks):
            lo = t - t.astype(BF16).astype(F32)
            t_parts = (jnp.where(left, t, pltpu.roll(lo, CHUNK, 1)).astype(BF16),
                       jnp.where(left, pltpu.roll(t, CHUNK, 1), lo).astype(BF16))
            qk_parts = (qk[:, :CHUNK].astype(BF16), pltpu.roll(qk, CHUNK, 1)[:, :CHUNK].astype(BF16))
            for half in range(2):
                c = 2 * pair + half
                r = slice(c * CHUNK, (c + 1) * CHUNK)
                rhs = rhs_scr[h, r, :]
                sols.append((h, c, r, qk_parts[half],
                             _dot(t_parts[half], jnp.concatenate([rhs, rhs], axis=0)).astype(BF16)))
        yield
        outs = [(h, c, r, _dot_tn(kd_scr[h, r, :], sol),
                 _dot(qk_c, sol))
                for h, c, r, qk_c, sol in sols]
        for h, c, r, kd_uw, qk_uw in outs:
            lhs_scr[h, c, 0:GDN_D, :] = kd_uw[:, GDN_D:].astype(BF16)
            lhs_scr[h, c, GDN_D:GDN_D + CHUNK, :] = (qd_scr[h, r, :] - qk_uw[:, GDN_D:]).astype(BF16)
            su_scr[h, c] = kd_uw[:, :GDN_D]
            o_scr[h, r, :] = qk_uw[:, :GDN_D]
        yield

    def state_updates(s):
        for c in range(s * n_chunks, (s + 1) * n_chunks):
            rows = slice(c * CHUNK, (c + 1) * CHUNK)
            for h in heads:
                st_h = s_scr[h]
                prod = _dot(lhs_scr[h, c], st_h.astype(BF16))
                o_scr[h, rows, :] = o_scr[h, rows, :] + prod[GDN_D:, :]
                s_scr[h] = st_h * gt_scr[h, c:c + 1, :] + (su_scr[h, c] - prod[:GDN_D, :])
            yield

    def finish(s, st):
        rows = slice(s * sub, (s + 1) * sub)
        pieces = []
        for h in heads:
            o = o_scr[h, rows, :]
            on = o * lax.rsqrt(jnp.mean(o * o, axis=-1, keepdims=True) + EPS) * onorm_ref[...]
            pieces.append((on * _silu(gate_scr[rows, h * GDN_D:(h + 1) * GDN_D])).astype(BF16))
        on_all = jnp.concatenate(pieces, axis=1)
        for piece in range(D_MODEL // (2 * GDN_D)):
            cols = slice(piece * 2 * GDN_D, (piece + 1) * 2 * GDN_D)
            y_ref[0, rows, cols] = st["x"][:, cols] + _dot(on_all, wout_ref[:, cols])
            yield

    def run(gen):
        for _ in gen:
            pass

    def interleave(main, side, every):
        for n, _ in enumerate(main):
            if (n + 1) % every == 0:
                next(side, None)
        run(side)

    sts = [start(0)]
    project(0, sts[0])
    for s in range(n_sub):
        if s + 1 < n_sub:
            sts.append(start(s + 1))
        for h in heads:
            head_prep(s, sts[s], h)
        if s + 1 < n_sub:
            project(s + 1, sts[s + 1])
        project_gate(s, sts[s])
        if s == 0:
            run(chunk_terms(s))
        else:
            interleave(chunk_terms(s), state_updates(s - 1), 2)
            if s >= 2:
                run(finish(s - 2, sts[s - 2]))
    conv_ref[0] = pre_scr[tm + CARRY - 3:tm + CARRY, :]
    pre_scr[0:CARRY, :] = pre_scr[tm:tm + CARRY, :]
    if n_sub >= 2:
        interleave(state_updates(n_sub - 1), finish(n_sub - 2, sts[n_sub - 2]), 1)
    else:
        run(state_updates(0))

    @pl.when(l == pl.num_programs(1) - 1)
    def _():
        ssm_ref[0] = s_scr[...]

    run(finish(n_sub - 1, sts[n_sub - 1]))


def _gdn_prompt(x, norm, win, wabt, convw, alog, dtb, onorm, wout):
    b, l, d = x.shape
    tm = TM_A
    h = GDN_HEADS
    grid = (b, l // tm)
    in_specs = [
        pl.BlockSpec((1, tm, d), lambda i, j: (i, j, 0)),
        _full_spec((1, d)), _full_spec(win.shape), _full_spec(wabt.shape), _full_spec(convw.shape),
        _full_spec((1, h)), _full_spec((1, h)), _full_spec((h, 1)), _full_spec((h, 1)),
        _full_spec((1, GDN_D)), _full_spec(wout.shape),
    ]
    out_specs = [
        pl.BlockSpec((1, tm, d), lambda i, j: (i, j, 0)),
        pl.BlockSpec((1, CONV_W - 1, QKV_W), lambda i, j: (i, 0, 0)),
        pl.BlockSpec((1, h, GDN_D, GDN_D), lambda i, j: (i, 0, 0, 0)),
    ]
    out_shape = [
        jax.ShapeDtypeStruct((b, l, d), F32),
        jax.ShapeDtypeStruct((b, CONV_W - 1, QKV_W), F32),
        jax.ShapeDtypeStruct((b, h, GDN_D, GDN_D), F32),
    ]
    scratch = [
        pltpu.VMEM((tm + CARRY, QKV_W), F32),
        pltpu.VMEM((tm, h * GDN_D), F32),
        pltpu.VMEM((h, GDN_D, GDN_D), F32),
        pltpu.VMEM((h, tm, GDN_D), BF16),
        pltpu.VMEM((h, tm, GDN_D), BF16),
        pltpu.VMEM((h, tm, GDN_D), BF16),
        pltpu.VMEM((h, tm, GDN_D), F32),
        pltpu.VMEM((h, tm, GDN_D), BF16),
        pltpu.VMEM((h, tm, 2 * GDN_D), BF16),
        pltpu.VMEM((h, tm // 2, 2 * CHUNK), F32),
        pltpu.VMEM((h, tm // CHUNK, GDN_D), F32),
        pltpu.VMEM((h, tm, GDN_D), F32),
        pltpu.VMEM((h, tm // CHUNK, GDN_D + CHUNK, GDN_D), BF16),
        pltpu.VMEM((h, tm // CHUNK, GDN_D, GDN_D), F32),
    ]
    return pl.pallas_call(
        _gdn_prompt_kernel,
        grid=grid, in_specs=in_specs, out_specs=out_specs, out_shape=out_shape,
        scratch_shapes=scratch,
        compiler_params=pltpu.CompilerParams(
            dimension_semantics=("arbitrary", "arbitrary"), vmem_limit_bytes=VMEM_LIMIT),
        name="gdn_prompt",
    )(x, norm.reshape(1, d), win, wabt, convw,
      alog.reshape(1, h), dtb.reshape(1, h), alog.reshape(h, 1), dtb.reshape(h, 1),
      onorm.reshape(1, GDN_D), wout)


def _gdn_sample_front_kernel(x_ref, norm_ref, win_ref, convw_ref,
                             c0_ref, c1_ref, c2_ref, alog_r_ref, dtb_r_ref,
                             pre_ref, q_ref, k_ref, u_ref, w_ref, qd_ref, gate_ref, eg_ref):
    x = x_ref[...]
    xn = x * lax.rsqrt(jnp.mean(x * x, axis=-1, keepdims=True) + EPS) * norm_ref[...]
    xb = xn.astype(BF16)
    gate_w = GDN_HEADS * GDN_D
    pre = _dot(xb, win_ref[:, :QKV_W])
    pre_ref[...] = pre
    gate_ref[...] = _dot(xb, win_ref[:, QKV_W:QKV_W + gate_w])
    ab = _dot(xb, win_ref[:, QKV_W + gate_w:])
    g = -jnp.exp(alog_r_ref[...]) * _softplus(ab[:, :GDN_HEADS] + dtb_r_ref[...])
    beta = jax.nn.sigmoid(ab[:, GDN_HEADS:])
    e_g = jnp.exp(g)
    eg_ref[...] = e_g
    act = _silu(((c0_ref[...] * convw_ref[0:1, :] + c1_ref[...] * convw_ref[1:2, :])
                 + c2_ref[...] * convw_ref[2:3, :]) + pre * convw_ref[3:4, :])
    w_all = GDN_HEADS * GDN_D
    for h in range(GDN_HEADS):
        cols = slice(h * GDN_D, (h + 1) * GDN_D)
        qh = act[:, h * GDN_D:(h + 1) * GDN_D]
        kh = act[:, w_all + h * GDN_D:w_all + (h + 1) * GDN_D]
        vh = act[:, 2 * w_all + h * GDN_D:2 * w_all + (h + 1) * GDN_D]
        qn = qh * lax.rsqrt(jnp.sum(qh * qh, axis=-1, keepdims=True) + EPS) * (GDN_D ** -0.5)
        kn = kh * lax.rsqrt(jnp.sum(kh * kh, axis=-1, keepdims=True) + EPS)
        b_col = beta[:, h:h + 1]
        eg_col = e_g[:, h:h + 1]
        q_ref[:, cols] = qn
        k_ref[:, cols] = kn
        u_ref[:, cols] = vh * b_col
        w_ref[:, cols] = kn * b_col * eg_col
        qd_ref[:, cols] = qn * eg_col


def _gdn_sample_state_kernel(q_ref, k_ref, u_ref, w_ref, qd_ref, eg_ref, s_ref, o_ref, s_out_ref):
    bb = s_ref.shape[0]
    row = lax.broadcasted_iota(jnp.int32, (8, GDN_D), 0)
    pairs = [(b, h) for b in range(bb) for h in range(GDN_HEADS)]
    prods = []
    for b, h in pairs:
        cols = slice(h * GDN_D, (h + 1) * GDN_D)
        lhs = jnp.where(row == 0, w_ref[b:b + 1, cols], jnp.where(row == 1, qd_ref[b:b + 1, cols], 0.0))
        prods.append(_dot(lhs.astype(BF16), s_ref[b, h].astype(BF16)))
    for (b, h), prod in zip(pairs, prods):
        cols = slice(h * GDN_D, (h + 1) * GDN_D)
        q = q_ref[b:b + 1, cols].astype(BF16).astype(F32)
        k = k_ref[b:b + 1, cols].astype(BF16).astype(F32)
        v_new = u_ref[b:b + 1, cols] - prod[0:1, :]
        vb = v_new.astype(BF16).astype(F32)
        qk = jnp.sum(q * k, axis=-1, keepdims=True)
        o_ref[b:b + 1, cols] = prod[1:2, :] + qk.astype(BF16).astype(F32) * vb
        k8 = jnp.where(row == 0, k, 0.0).astype(BF16)
        v8 = jnp.where(row == 0, vb, 0.0).astype(BF16)
        s_out_ref[b, h] = s_ref[b, h] * eg_ref[b:b + 1, h:h + 1] + _dot_tn(k8, v8)


def _out_proj_kernel(x_ref, o_ref, gate_ref, onorm_ref, wout_ref, y_ref, *, head_w, norm_heads):
    pieces = []
    for h in range(x_ref.shape[1] // head_w):
        cols = slice(h * head_w, (h + 1) * head_w)
        o = o_ref[:, cols]
        if norm_heads:
            o = o * lax.rsqrt(jnp.mean(o * o, axis=-1, keepdims=True) + EPS) * onorm_ref[...]
        pieces.append((o * _silu(gate_ref[:, cols])).astype(BF16))
    y_ref[...] = x_ref[...] + _dot(jnp.concatenate(pieces, axis=1), wout_ref[...])


def _gdn_sample(x, conv_state, ssm_state, norm, win, convw, alog, dtb, onorm, wout):
    n, d = x.shape
    h = GDN_HEADS
    hw = h * GDN_D
    conv_t = jnp.transpose(conv_state, (1, 0, 2))
    f = lambda shape: jax.ShapeDtypeStruct(shape, F32)
    pre, q, k, u, w, qd, gate, eg = pl.pallas_call(
        _gdn_sample_front_kernel,
        out_shape=[f((n, QKV_W)), f((n, hw)), f((n, hw)), f((n, hw)), f((n, hw)), f((n, hw)),
                   f((n, hw)), f((n, h))],
        compiler_params=pltpu.CompilerParams(vmem_limit_bytes=VMEM_LIMIT),
        name="gdn_sample_front",
    )(x, norm.reshape(1, d), win, convw, conv_t[0], conv_t[1], conv_t[2],
      alog.reshape(1, h), dtb.reshape(1, h))
    bb = 8
    row_spec = lambda width: pl.BlockSpec((bb, width), lambda i: (i, 0))
    st_spec = pl.BlockSpec((bb, h, GDN_D, GDN_D), lambda i: (i, 0, 0, 0))
    o, s_new = pl.pallas_call(
        _gdn_sample_state_kernel,
        grid=(n // bb,),
        in_specs=[row_spec(hw)] * 5 + [row_spec(h), st_spec],
        out_specs=[row_spec(hw), st_spec],
        out_shape=[f((n, hw)), f(ssm_state.shape)],
        compiler_params=pltpu.CompilerParams(dimension_semantics=("arbitrary",)),
        name="gdn_sample_state",
    )(q, k, u, w, qd, eg, ssm_state)
    y = pl.pallas_call(
        functools.partial(_out_proj_kernel, head_w=GDN_D, norm_heads=True),
        out_shape=f((n, d)),
        name="gdn_sample_out",
    )(x, o, gate, onorm.reshape(1, GDN_D), wout)
    conv_new = jnp.stack([conv_t[1], conv_t[2], pre], axis=1)
    return y, conv_new, s_new


def _head_rms(x, gain_row, n_heads):
    pieces = []
    for h in range(n_heads):
        xh = x[:, h * HEAD_DIM:(h + 1) * HEAD_DIM]
        pieces.append(xh * lax.rsqrt(jnp.mean(xh * xh, axis=-1, keepdims=True) + EPS) * gain_row)
    return pieces


def _pair_rms(x, gain2, left):
    sq = x * x
    lo = jnp.sum(jnp.where(left, sq, 0.0), axis=-1, keepdims=True)
    hi = jnp.sum(jnp.where(left, 0.0, sq), axis=-1, keepdims=True)
    ms = jnp.where(left, lo, hi) * (1.0 / HEAD_DIM)
    return x * lax.rsqrt(ms + EPS) * gain2


def _swa_prompt_kernel(y_ref, kvnorm_ref, wkv_ref, knorm2_ref, normb_ref, win_ref, qnorm2_ref,
                       sinks_ref, wout_ref, out_ref, kwin_ref, vwin_ref, k_scr, v_scr, o_scr):
    tq = y_ref.shape[1]
    w = WINDOW
    step = pl.program_id(1)
    lane = lax.broadcasted_iota(jnp.int32, (1, 2 * HEAD_DIM), 1)
    left = lane < HEAD_DIM

    @pl.when(step == 0)
    def _():
        k_scr[:, 0:w, :] = jnp.zeros((N_KV_HEADS, w, 2 * HEAD_DIM), BF16)
        v_scr[:, 0:w, :] = jnp.zeros((N_KV_HEADS, w, 2 * HEAD_DIM), BF16)

    @pl.when(step > 0)
    def _():
        k_scr[:, 0:w, :] = k_scr[:, tq:tq + w, :]
        v_scr[:, 0:w, :] = v_scr[:, tq:tq + w, :]

    y = y_ref[0]
    yn = y * lax.rsqrt(jnp.mean(y * y, axis=-1, keepdims=True) + EPS)
    kv = _dot((yn * kvnorm_ref[...]).astype(BF16), wkv_ref[...])
    for c in range(N_KV_HEADS // 2):
        cols = slice(c * 2 * HEAD_DIM, (c + 1) * 2 * HEAD_DIM)
        kp = _pair_rms(kv[:, cols], knorm2_ref[...], left)
        vp = kv[:, KV_W + c * 2 * HEAD_DIM:KV_W + (c + 1) * 2 * HEAD_DIM]
        kwin_ref[0, :, cols] = kp[tq - w:, :]
        vwin_ref[0, :, cols] = vp[tq - w:, :]
        kr = pltpu.roll(kp, HEAD_DIM, 1)
        vr = pltpu.roll(vp, HEAD_DIM, 1)
        k_scr[2 * c, w:w + tq, :] = jnp.where(left, kp, kr).astype(BF16)
        k_scr[2 * c + 1, w:w + tq, :] = jnp.where(left, kr, kp).astype(BF16)
        v_scr[2 * c, w:w + tq, :] = jnp.where(left, vp, vr).astype(BF16)
        v_scr[2 * c + 1, w:w + tq, :] = jnp.where(left, vr, vp).astype(BF16)

    qg = _dot((yn * normb_ref[...]).astype(BF16), win_ref[...])
    q_pairs = [_pair_rms(qg[:, c * 2 * HEAD_DIM:(c + 1) * 2 * HEAD_DIM], qnorm2_ref[...], left)
               for c in range(N_Q_HEADS // 2)]

    kj = lax.broadcasted_iota(jnp.int32, (2 * w, w), 0)
    qi = lax.broadcasted_iota(jnp.int32, (2 * w, w), 1)
    dist = qi - kj + w
    band = (dist >= 0) & (dist <= w)
    j_rel = (kj - w).astype(F32)
    i_row = lax.broadcasted_iota(jnp.int32, (1, w), 1).astype(F32)

    def scores(blk, hk):
        q_rows = slice(blk * w, (blk + 1) * w)
        lhs = []
        for c in (2 * hk, 2 * hk + 1):
            qp = q_pairs[c][q_rows, :]
            lhs.append(jnp.where(left, qp, 0.0))
            lhs.append(jnp.where(left, 0.0, qp))
        return _dot_nt(k_scr[hk, blk * w:blk * w + 2 * w, :],
                       jnp.concatenate(lhs, axis=0).astype(BF16))

    def attend(blk, hk, s4):
        q_rows = slice(blk * w, (blk + 1) * w)
        valid = band & ((step > 0) | (kj >= w)) if blk == 0 else band
        ps = []
        for g in range(Q_GROUP):
            hq = hk * Q_GROUP + g
            slope = 2.0 ** (-8.0 * (hq + 1) / N_Q_HEADS)
            a = jnp.where(valid, s4[:, g * w:(g + 1) * w] + slope * j_rel, -jnp.inf)
            sink = sinks_ref[0:1, hq:hq + 1] + slope * i_row
            mx = jnp.maximum(jnp.max(a, axis=0, keepdims=True), sink)
            p = jnp.exp(a - mx)
            inv = 1.0 / (jnp.sum(p, axis=0, keepdims=True) + jnp.exp(sink - mx))
            ps.append((p * inv).astype(BF16))
        o4 = _dot_tn(jnp.concatenate(ps, axis=1),
                     v_scr[hk, blk * w:blk * w + 2 * w, :])
        for j, c in enumerate((2 * hk, 2 * hk + 1)):
            o_scr[q_rows, c * 2 * HEAD_DIM:(c + 1) * 2 * HEAD_DIM] = jnp.where(
                left, o4[(2 * j) * w:(2 * j + 1) * w, :], o4[(2 * j + 1) * w:(2 * j + 2) * w, :])

    items = [(blk, hk) for blk in range(tq // w) for hk in range(N_KV_HEADS)]
    s_next = scores(*items[0])
    for n, item in enumerate(items):
        s_cur = s_next
        if n + 1 < len(items):
            s_next = scores(*items[n + 1])
        attend(*item, s_cur)

    o = (o_scr[...] * _silu(qg[:, ATT_W:])).astype(BF16)
    out_ref[0] = y + _dot(o, wout_ref[...])


def _swa_prompt(y, kvnorm, wkv, knorm, normb, win, qnorm, sinks, wout):
    b, l, d = y.shape
    tq = TQ_B
    w = WINDOW
    f = lambda shape: jax.ShapeDtypeStruct(shape, F32)
    knorm2 = jnp.concatenate([knorm, knorm]).reshape(1, 2 * HEAD_DIM)
    qnorm2 = (jnp.concatenate([qnorm, qnorm]) * (HEAD_DIM ** -0.5)).reshape(1, 2 * HEAD_DIM)
    in_specs = [
        pl.BlockSpec((1, tq, d), lambda i, j: (i, j, 0)),
        _full_spec((1, d)), _full_spec(wkv.shape), _full_spec((1, 2 * HEAD_DIM)), _full_spec((1, d)),
        _full_spec(win.shape), _full_spec((1, 2 * HEAD_DIM)), _full_spec((1, N_Q_HEADS)),
        _full_spec(wout.shape),
    ]
    out_specs = [
        pl.BlockSpec((1, tq, d), lambda i, j: (i, j, 0)),
        pl.BlockSpec((1, w, KV_W), lambda i, j: (i, 0, 0)),
        pl.BlockSpec((1, w, KV_W), lambda i, j: (i, 0, 0)),
    ]
    return pl.pallas_call(
        _swa_prompt_kernel,
        grid=(b, l // tq), in_specs=in_specs, out_specs=out_specs,
        out_shape=[f((b, l, d)), f((b, w, KV_W)), f((b, w, KV_W))],
        scratch_shapes=[pltpu.VMEM((N_KV_HEADS, w + tq, 2 * HEAD_DIM), BF16),
                        pltpu.VMEM((N_KV_HEADS, w + tq, 2 * HEAD_DIM), BF16),
                        pltpu.VMEM((tq, ATT_W), F32)],
        compiler_params=pltpu.CompilerParams(
            dimension_semantics=("arbitrary", "arbitrary"), vmem_limit_bytes=VMEM_LIMIT),
        name="swa_prompt",
    )(y, kvnorm.reshape(1, d), wkv, knorm2, normb.reshape(1, d), win, qnorm2,
      sinks.reshape(1, N_Q_HEADS), wout)


def _swa_sample_front_kernel(y_ref, kvnorm_ref, wkv_ref, knorm_ref, normb_ref, win_ref, qnorm_ref,
                             k_ref, v_ref, q_ref, gate_ref):
    y = y_ref[...]
    yn = y * lax.rsqrt(jnp.mean(y * y, axis=-1, keepdims=True) + EPS)
    kv = _dot((yn * kvnorm_ref[...]).astype(BF16), wkv_ref[...])
    k_ref[...] = jnp.concatenate(_head_rms(kv[:, :KV_W], knorm_ref[...], N_KV_HEADS), axis=1)
    v_ref[...] = kv[:, KV_W:]
    qg = _dot((yn * normb_ref[...]).astype(BF16), win_ref[...])
    q_ref[...] = jnp.concatenate(
        _head_rms(qg[:, :ATT_W], qnorm_ref[...] * (HEAD_DIM ** -0.5), N_Q_HEADS), axis=1)
    gate_ref[...] = qg[:, ATT_W:]


def _swa_sample_attn_kernel(q_ref, kn_ref, vn_ref, kc_ref, vc_ref, sinks_ref, slopes_ref,
                            o_ref, kwin_ref, vwin_ref):
    bb = kc_ref.shape[0]
    w = kc_ref.shape[1]
    hrow = lax.broadcasted_iota(jnp.int32, (N_Q_HEADS, KV_W), 0) // Q_GROUP
    lblk = lax.broadcasted_iota(jnp.int32, (N_Q_HEADS, KV_W), 1) // HEAD_DIM
    own = hrow == lblk
    dist_c = (w - lax.broadcasted_iota(jnp.int32, (1, w), 1)).astype(F32)
    krow = lax.broadcasted_iota(jnp.int32, (w, KV_W), 0)
    slopes = slopes_ref[...]
    sink = sinks_ref[...]
    qms, scs = [], []
    for b in range(bb):
        q = q_ref[b]
        qm = jnp.where(own, jnp.concatenate([q] * N_KV_HEADS, axis=1), 0.0).astype(BF16)
        qms.append(qm)
        scs.append(_dot_nt(qm, kc_ref[b].astype(BF16)))
    pcs, pns = [], []
    for b in range(bb):
        s_c = scs[b] - slopes * dist_c
        s_n = jnp.sum(qms[b].astype(F32) * kn_ref[b].astype(BF16).astype(F32), axis=-1, keepdims=True)
        mx = jnp.maximum(jnp.maximum(jnp.max(s_c, axis=-1, keepdims=True), s_n), sink)
        p_c = jnp.exp(s_c - mx)
        p_n = jnp.exp(s_n - mx)
        den = jnp.sum(p_c, axis=-1, keepdims=True) + p_n + jnp.exp(sink - mx)
        pcs.append((p_c / den).astype(BF16))
        pns.append(p_n / den)
    rs = [_dot(pcs[b], vc_ref[b].astype(BF16)) for b in range(bb)]
    for b in range(bb):
        kn = kn_ref[b]
        vn = vn_ref[b]
        r = rs[b] + pns[b].astype(BF16).astype(F32) * vn.astype(BF16).astype(F32)
        r = jnp.where(own, r, 0.0)
        acc = r[:, 0:HEAD_DIM]
        for blk in range(1, N_KV_HEADS):
            acc = acc + r[:, blk * HEAD_DIM:(blk + 1) * HEAD_DIM]
        o_ref[b] = acc
        kwin_ref[b] = jnp.where(krow == w - 1, kn, pltpu.roll(kc_ref[b], w - 1, 0))
        vwin_ref[b] = jnp.where(krow == w - 1, vn, pltpu.roll(vc_ref[b], w - 1, 0))


def _swa_sample(y, cache_k, cache_v, kvnorm, wkv, knorm, normb, win, qnorm, sinks, wout):
    n, d = y.shape
    w = cache_k.shape[1]
    f = lambda shape: jax.ShapeDtypeStruct(shape, F32)
    kn, vn, q, gate = pl.pallas_call(
        _swa_sample_front_kernel,
        out_shape=[f((n, KV_W)), f((n, KV_W)), f((n, ATT_W)), f((n, ATT_W))],
        compiler_params=pltpu.CompilerParams(vmem_limit_bytes=VMEM_LIMIT),
        name="swa_sample_front",
    )(y, kvnorm.reshape(1, d), wkv, knorm.reshape(1, HEAD_DIM), normb.reshape(1, d), win,
      qnorm.reshape(1, HEAD_DIM))
    bb = 8
    slopes = (2.0 ** (-8.0 * jnp.arange(1, N_Q_HEADS + 1, dtype=F32) / N_Q_HEADS)).reshape(N_Q_HEADS, 1)
    spec3 = lambda a, c: pl.BlockSpec((bb, a, c), lambda i: (i, 0, 0))
    o, kwin, vwin = pl.pallas_call(
        _swa_sample_attn_kernel,
        grid=(n // bb,),
        in_specs=[spec3(N_Q_HEADS, HEAD_DIM), spec3(1, KV_W), spec3(1, KV_W), spec3(w, KV_W),
                  spec3(w, KV_W), _full_spec((N_Q_HEADS, 1)), _full_spec((N_Q_HEADS, 1))],
        out_specs=[spec3(N_Q_HEADS, HEAD_DIM), spec3(w, KV_W), spec3(w, KV_W)],
        out_shape=[f((n, N_Q_HEADS, HEAD_DIM)), f((n, w, KV_W)), f((n, w, KV_W))],
        compiler_params=pltpu.CompilerParams(dimension_semantics=("arbitrary",)),
        name="swa_sample_attn",
    )(q.reshape(n, N_Q_HEADS, HEAD_DIM), kn.reshape(n, 1, KV_W), vn.reshape(n, 1, KV_W),
      cache_k.reshape(n, w, KV_W), cache_v.reshape(n, w, KV_W),
      sinks.reshape(N_Q_HEADS, 1), slopes)
    out = pl.pallas_call(
        functools.partial(_out_proj_kernel, head_w=ATT_W, norm_heads=False),
        out_shape=f((n, d)),
        name="swa_sample_out",
    )(y, o.reshape(n, ATT_W), gate, jnp.ones((1, ATT_W), F32), wout)
    return out, kwin, vwin


def kernel(x_prompt, x_sample, state_conv, state_ssm, cache_k_win, cache_v_win, norm_a, w_in_a, conv_w_a, a_log, dt_bias, o_norm_a, w_out_a, kv_norm, w_kv, k_norm, norm_b, w_in_b, q_norm, sinks, w_out_b):
    n_a = w_in_a.shape[0]
    n_b = w_in_b.shape[0]
    assert n_a == 1 and n_b == 1, "kernel is written for DEPTH == 2"
    bp, lp, d = x_prompt.shape
    n = x_sample.shape[0]
    hw = GDN_HEADS * GDN_D

    hp, hs = x_prompt, x_sample.reshape(n, d)
    conv_p, ssm_p, conv_s, ssm_s = [], [], [], []
    for layer in range(n_a):
        win_a = w_in_a[layer].astype(BF16)
        wabt = win_a[:, QKV_W + hw:].T
        wout = w_out_a[layer].astype(BF16)
        hp, cbuf, st = _gdn_prompt(hp, norm_a[layer], win_a, wabt, conv_w_a[layer],
                                   a_log[layer], dt_bias[layer], o_norm_a[layer], wout)
        conv_p.append(cbuf)
        ssm_p.append(st)
        hs, cbuf, st = _gdn_sample(hs, state_conv[layer], state_ssm[layer], norm_a[layer], win_a,
                                   conv_w_a[layer], a_log[layer], dt_bias[layer],
                                   o_norm_a[layer], wout)
        conv_s.append(cbuf)
        ssm_s.append(st)

    wkv = w_kv.astype(BF16)
    win = w_in_b[0].astype(BF16)
    woutb = w_out_b[0].astype(BF16)
    hp, k_win_p, v_win_p = _swa_prompt(hp, kv_norm, wkv, k_norm, norm_b[0], win, q_norm[0], sinks[0], woutb)
    hs, k_win_s, v_win_s = _swa_sample(hs, cache_k_win, cache_v_win, kv_norm, wkv, k_norm, norm_b[0],
                                       win, q_norm[0], sinks[0], woutb)
    kv_shape = (N_KV_HEADS, HEAD_DIM)
    return (hp, hs.reshape(n, 1, d), jnp.stack(conv_p), jnp.stack(ssm_p),
            k_win_p.reshape(bp, WINDOW, *kv_shape), v_win_p.reshape(bp, WINDOW, *kv_shape),
            jnp.stack(conv_s), jnp.stack(ssm_s),
            k_win_s.reshape(n, WINDOW, *kv_shape), v_win_s.reshape(n, WINDOW, *kv_shape))
```

```python
import functools

import jax
import jax.numpy as jnp
from jax import lax
from jax.experimental import pallas as pl
from jax.experimental.pallas import tpu as pltpu

F32 = jnp.float32
BF16 = jnp.bfloat16
EPS = 1e-6

D_MODEL = 1024
GDN_HEADS = 8
GDN_D = 128
QKV_W = 3 * GDN_HEADS * GDN_D
CONV_W = 4
CHUNK = 64
N_Q_HEADS = 16
N_KV_HEADS = 4
Q_GROUP = N_Q_HEADS // N_KV_HEADS
HEAD_DIM = 64
KV_W = N_KV_HEADS * HEAD_DIM
ATT_W = N_Q_HEADS * HEAD_DIM
WINDOW = 128

TM_A = 512
SUB_A = 256
TQ_B = 1024
CARRY = 8
VMEM_LIMIT = 60 * 1024 * 1024

_NT = (((1,), (1,)), ((), ()))
_TN = (((0,), (0,)), ((), ()))


def _dot(a, b):
    return jnp.dot(a, b, preferred_element_type=F32)


def _dot_nt(a, b):
    return lax.dot_general(a, b, _NT, preferred_element_type=F32)


def _dot_tn(a, b):
    return lax.dot_general(a, b, _TN, preferred_element_type=F32)


def _split(x):
    hi = x.astype(BF16)
    lo = (x - hi.astype(F32)).astype(BF16)
    return hi, lo


def _dot_exact_lhs(a_bf, b):
    b0 = b.astype(BF16)
    r1 = b - b0.astype(F32)
    b1 = r1.astype(BF16)
    b2 = (r1 - b1.astype(F32)).astype(BF16)
    return (_dot(a_bf, b2) + _dot(a_bf, b1)) + _dot(a_bf, b0)


def _dot_exact_rhs(a, b_bf):
    a0 = a.astype(BF16)
    r1 = a - a0.astype(F32)
    a1 = r1.astype(BF16)
    a2 = (r1 - a1.astype(F32)).astype(BF16)
    return (_dot(a2, b_bf) + _dot(a1, b_bf)) + _dot(a0, b_bf)


def _silu(x):
    return x * jax.nn.sigmoid(x)


def _softplus(x):
    return jnp.maximum(x, 0.0) + jnp.log1p(jnp.exp(-jnp.abs(x)))


def _halves(x, left):
    zero = jnp.zeros_like(x)
    return jnp.where(left, x, zero), jnp.where(left, zero, x)


def _block_diag(x, left):
    return jnp.concatenate(_halves(x, left), axis=0)


def _pair_split_product(a, b, left):
    ah, al = _split(a)
    bh, bl = _split(b)
    bh1, bh2 = _halves(bh, left)
    bl1, bl2 = _halves(bl, left)
    zero = jnp.zeros_like(bh)
    rhs = jnp.concatenate([jnp.concatenate([bh1, bl1], axis=1), jnp.concatenate([bh2, bl2], axis=1),
                           jnp.concatenate([bh1, zero], axis=1), jnp.concatenate([bh2, zero], axis=1)], axis=0)
    y = _dot(jnp.concatenate([ah, al], axis=1), rhs)
    half = y.shape[1] // 2
    return y[:, :half] + y[:, half:]


def _unit_lower_inverse_pairs(ms, left):
    n = ms[0].shape[0]
    row = lax.broadcasted_iota(jnp.int32, (n, 2 * n), 0)
    col = lax.broadcasted_iota(jnp.int32, (n, 2 * n), 1) % n
    eye2 = jnp.where(row == col, 1.0, 0.0).astype(F32)

    def times(a, b):
        return _dot(a.astype(BF16), _block_diag(b.astype(BF16), left))

    ts = [eye2 - m for m in ms]
    ps = [times(m, m) for m in ms]
    yield
    steps = max(1, (n - 1).bit_length()) - 1
    for _ in range(steps - 1):
        both = [_dot(jnp.concatenate([t.astype(BF16), p.astype(BF16)], axis=0), _block_diag(p.astype(BF16), left))
                for t, p in zip(ts, ps)]
        ts = [t + tp[:n, :] for t, tp in zip(ts, both)]
        ps = [tp[n:, :] for tp in both]
        yield
    ts = [t + times(t, p) for t, p in zip(ts, ps)]
    yield
    rs = [(eye2 - t) - _pair_split_product(m, t, left) for m, t in zip(ms, ts)]
    yield
    ts = [t + times(t, r) for t, r in zip(ts, rs)]
    yield
    return ts


def _full_spec(shape):
    nd = len(shape)
    return pl.BlockSpec(shape, lambda *_: (0,) * nd)


def _gdn_gates(ab, ab_t, alog_r, dtb_r, alog_c, dtb_c, tm, chunk):
    h = GDN_HEADS
    g_c = -jnp.exp(alog_r) * _softplus(ab[:, :h] + dtb_r)
    beta = jax.nn.sigmoid(ab[:, h:])
    g_r = -jnp.exp(alog_c) * _softplus(ab_t[:h, :] + dtb_c)
    row = lax.broadcasted_iota(jnp.int32, (tm, tm), 0)
    col = lax.broadcasted_iota(jnp.int32, (tm, tm), 1)
    same = (row // chunk) == (col // chunk)
    lower = jnp.where(same & (row >= col), 1.0, 0.0).astype(BF16)
    upper = jnp.where(same & (row <= col), 1.0, 0.0).astype(BF16)
    gc = _dot_exact_lhs(lower, g_c)
    gr = _dot_exact_rhs(g_r, upper)
    return beta, gc, gr


def _sample_state_update(q_ref, k_ref, u_ref, w_ref, qd_ref, eg_ref, s_ref, o_ref, s_out_ref):
    bb = s_ref.shape[0]
    row = lax.broadcasted_iota(jnp.int32, (8, GDN_D), 0)
    pairs = [(b, h) for b in range(bb) for h in range(GDN_HEADS)]
    prods = []
    for b, h in pairs:
        cols = slice(h * GDN_D, (h + 1) * GDN_D)
        lhs = jnp.where(row == 0, w_ref[b, :, cols], jnp.where(row == 1, qd_ref[b, :, cols], 0.0))
        prods.append(_dot(lhs.astype(BF16), s_ref[b, h].astype(BF16)))
    for (b, h), prod in zip(pairs, prods):
        cols = slice(h * GDN_D, (h + 1) * GDN_D)
        q = q_ref[b, :, cols].astype(BF16).astype(F32)
        k = k_ref[b, :, cols].astype(BF16).astype(F32)
        v_new = u_ref[b, :, cols] - prod[0:1, :]
        vb = v_new.astype(BF16).astype(F32)
        qk = jnp.sum(q * k, axis=-1, keepdims=True)
        o_ref[b, :, cols] = prod[1:2, :] + qk.astype(BF16).astype(F32) * vb
        k8 = jnp.where(row == 0, k, 0.0).astype(BF16)
        v8 = jnp.where(row == 0, vb, 0.0).astype(BF16)
        s_out_ref[b, h] = s_ref[b, h] * eg_ref[b, :, h:h + 1] + _dot_tn(k8, v8)


def _gdn_prompt_kernel(x_ref, norm_ref, win_ref, wabt_ref, convw_ref,
                       alog_r_ref, dtb_r_ref, alog_c_ref, dtb_c_ref, onorm_ref, wout_ref,
                       sq_ref, sk_ref, su_ref, sw_ref, sqd_ref, seg_ref, sstate_ref,
                       y_ref, conv_ref, ssm_ref, so_ref, sstate_out_ref,
                       pre_scr, gate_scr, s_scr, k_scr, kb_scr, q_scr, qd_scr, kd_scr,
                       rhs_scr, dec_scr, gt_scr, o_scr, lhs_scr, su_scr):
    tm = x_ref.shape[1]
    sub = SUB_A
    n_sub = tm // sub
    n_chunks = sub // CHUNK
    gate_w = GDN_HEADS * GDN_D
    l = pl.program_id(1)

    @pl.when(l == 0)
    def _():
        pre_scr[0:CARRY, :] = jnp.zeros((CARRY, QKV_W), F32)
        s_scr[...] = jnp.zeros(s_scr.shape, F32)

    ci = lax.broadcasted_iota(jnp.int32, (CHUNK, 2 * CHUNK), 0)
    cj = lax.broadcasted_iota(jnp.int32, (CHUNK, 2 * CHUNK), 1) % CHUNK
    causal = ci >= cj
    strict = ci > cj
    left = lax.broadcasted_iota(jnp.int32, (1, 2 * CHUNK), 1) < CHUNK
    heads = list(range(GDN_HEADS))

    _sample_state_update(sq_ref, sk_ref, su_ref, sw_ref, sqd_ref, seg_ref, sstate_ref, so_ref, sstate_out_ref)

    def start(s):
        x = x_ref[0, s * sub:(s + 1) * sub, :]
        xn = x * lax.rsqrt(jnp.mean(x * x, axis=-1, keepdims=True) + EPS) * norm_ref[...]
        xb = xn.astype(BF16)
        ab = _dot(xb, win_ref[:, QKV_W + gate_w:])
        ab_t = _dot_nt(wabt_ref[...], xb)
        beta, gc, gr = _gdn_gates(ab, ab_t, alog_r_ref[...], dtb_r_ref[...],
                                  alog_c_ref[...], dtb_c_ref[...], sub, CHUNK)
        g_last = jnp.concatenate(
            [jnp.broadcast_to(gc[c * CHUNK + CHUNK - 1:c * CHUNK + CHUNK, :], (CHUNK, GDN_HEADS))
             for c in range(n_chunks)], axis=0)
        return dict(x=x, xb=xb, beta=beta, gc=gc, gr=gr, e_g=jnp.exp(gc),
                    e_kd=jnp.exp(g_last - gc), e_tot=jnp.exp(g_last))

    def project(s, st):
        for part in range(3):
            cols = slice(part * gate_w, (part + 1) * gate_w)
            pre_scr[CARRY + s * sub:CARRY + (s + 1) * sub, cols] = _dot(st["xb"], win_ref[:, cols])

    def project_gate(s, st):
        gate_scr[s * sub:(s + 1) * sub, :] = _dot(st["xb"], win_ref[:, QKV_W:QKV_W + gate_w])

    def conv_act(s, j):
        cols = slice(j * GDN_D, (j + 1) * GDN_D)
        ext = pre_scr[s * sub:s * sub + CARRY + sub, cols]
        half = ext[CARRY:, :] * (0.5 * convw_ref[CONV_W - 1:CONV_W, cols])
        for back in range(1, CONV_W):
            tap = CONV_W - 1 - back
            half = half + pltpu.roll(ext, back, 0)[CARRY:, :] * (0.5 * convw_ref[tap:tap + 1, cols])
        return half + half * jnp.tanh(half)

    def head_prep(s, st, h):
        rows = slice(s * sub, (s + 1) * sub)
        qh = conv_act(s, h)
        kh = conv_act(s, GDN_HEADS + h)
        vh = conv_act(s, 2 * GDN_HEADS + h)
        qn = qh * lax.rsqrt(jnp.sum(qh * qh, axis=-1, keepdims=True) + EPS) * (GDN_D ** -0.5)
        kn = kh * lax.rsqrt(jnp.sum(kh * kh, axis=-1, keepdims=True) + EPS)
        b_col = st["beta"][:, h:h + 1]
        eg_col = st["e_g"][:, h:h + 1]
        kb = kn * b_col
        k_scr[h, rows, :] = kn.astype(BF16)
        kb_scr[h, rows, :] = kb.astype(BF16)
        q_scr[h, rows, :] = qn.astype(BF16)
        qd_scr[h, rows, :] = qn * eg_col
        kd_scr[h, rows, :] = (kn * st["e_kd"][:, h:h + 1]).astype(BF16)
        rhs_scr[h, rows, 0:GDN_D] = (vh * b_col).astype(BF16)
        rhs_scr[h, rows, GDN_D:2 * GDN_D] = (kb * eg_col).astype(BF16)
        gc, gr = st["gc"], st["gr"]
        for c in range(n_chunks):
            gt_scr[h, s * n_chunks + c:s * n_chunks + c + 1, :] = jnp.broadcast_to(
                st["e_tot"][c * CHUNK:c * CHUNK + 1, h:h + 1], (1, GDN_D))
        for j in range(n_chunks // 2):
            r1 = slice(2 * j * CHUNK, (2 * j + 1) * CHUNK)
            r2 = slice((2 * j + 1) * CHUNK, (2 * j + 2) * CHUNK)
            diff = jnp.where(left, gc[r1, h:h + 1], gc[r2, h:h + 1]) - gr[h:h + 1, 2 * j * CHUNK:(2 * j + 2) * CHUNK]
            pair = s * (n_chunks // 2) + j
            dec_scr[h, pair * CHUNK:(pair + 1) * CHUNK, :] = jnp.exp(jnp.where(causal, diff, -jnp.inf))

    def chunk_terms(s):
        items = [(h, s * (n_chunks // 2) + j) for h in heads for j in range(n_chunks // 2)]
        aqs = []
        for h, pair in items:
            r12 = slice(2 * pair * CHUNK, (2 * pair + 2) * CHUNK)
            aqs.append(_dot_nt(jnp.concatenate([kb_scr[h, r12, :], q_scr[h, r12, :]], axis=0), k_scr[h, r12, :]))
        yield
        decs = [dec_scr[h, pair * CHUNK:(pair + 1) * CHUNK, :] for h, pair in items]
        ms = [jnp.where(strict, jnp.where(left, aq[:CHUNK, :], aq[CHUNK:2 * CHUNK, :]) * d, 0.0)
              for aq, d in zip(aqs, decs)]
        qks = [jnp.where(causal, jnp.where(left, aq[2 * CHUNK:3 * CHUNK, :], aq[3 * CHUNK:, :]) * d, 0.0)
               for aq, d in zip(aqs, decs)]
        ts = yield from _unit_lower_inverse_pairs(ms, left)
        sols = []
        for (h, pair), t, qk in zip(items, ts, qks):
            lo = t - t.astype(BF16).astype(F32)
            t_parts = (jnp.where(left, t, pltpu.roll(lo, CHUNK, 1)).astype(BF16),
                       jnp.where(left, pltpu.roll(t, CHUNK, 1), lo).astype(BF16))
            qk_parts = (qk[:, :CHUNK].astype(BF16), pltpu.roll(qk, CHUNK, 1)[:, :CHUNK].astype(BF16))
            for half in range(2):
                c = 2 * pair + half
                r = slice(c * CHUNK, (c + 1) * CHUNK)
                rhs = rhs_scr[h, r, :]
                sols.append((h, c, r, qk_parts[half],
                             _dot(t_parts[half], jnp.concatenate([rhs, rhs], axis=0)).astype(BF16)))
        yield
        outs = [(h, c, r, _dot_tn(kd_scr[h, r, :], sol),
                 _dot(qk_c, sol))
                for h, c, r, qk_c, sol in sols]
        for h, c, r, kd_uw, qk_uw in outs:
            lhs_scr[h, c, 0:GDN_D, :] = kd_uw[:, GDN_D:].astype(BF16)
            lhs_scr[h, c, GDN_D:GDN_D + CHUNK, :] = (qd_scr[h, r, :] - qk_uw[:, GDN_D:]).astype(BF16)
            su_scr[h, c] = kd_uw[:, :GDN_D]
            o_scr[h, r, :] = qk_uw[:, :GDN_D]
        yield

    def state_updates(s):
        for c in range(s * n_chunks, (s + 1) * n_chunks):
            rows = slice(c * CHUNK, (c + 1) * CHUNK)
            for h in heads:
                st_h = s_scr[h]
                prod = _dot(lhs_scr[h, c], st_h.astype(BF16))
                o_scr[h, rows, :] = o_scr[h, rows, :] + prod[GDN_D:, :]
                s_scr[h] = st_h * gt_scr[h, c:c + 1, :] + (su_scr[h, c] - prod[:GDN_D, :])
            yield

    def finish(s, st):
        rows = slice(s * sub, (s + 1) * sub)
        pieces = []
        for h in heads:
            o = o_scr[h, rows, :]
            on = o * lax.rsqrt(jnp.mean(o * o, axis=-1, keepdims=True) + EPS) * onorm_ref[...]
            pieces.append((on * _silu(gate_scr[rows, h * GDN_D:(h + 1) * GDN_D])).astype(BF16))
        on_all = jnp.concatenate(pieces, axis=1)
        for piece in range(D_MODEL // (2 * GDN_D)):
            cols = slice(piece * 2 * GDN_D, (piece + 1) * 2 * GDN_D)
            y_ref[0, rows, cols] = st["x"][:, cols] + _dot(on_all, wout_ref[:, cols])
            yield

    def run(gen):
        for _ in gen:
            pass

    def interleave(main, side, every):
        for n, _ in enumerate(main):
            if (n + 1) % every == 0:
                next(side, None)
        run(side)

    sts = [start(0)]
    project(0, sts[0])
    for s in range(n_sub):
        if s + 1 < n_sub:
            sts.append(start(s + 1))
        for h in heads:
            head_prep(s, sts[s], h)
        if s + 1 < n_sub:
            project(s + 1, sts[s + 1])
        project_gate(s, sts[s])
        if s == 0:
            run(chunk_terms(s))
        else:
            interleave(chunk_terms(s), state_updates(s - 1), 2)
            if s >= 2:
                run(finish(s - 2, sts[s - 2]))
    conv_ref[0] = pre_scr[tm + CARRY - 3:tm + CARRY, :]
    pre_scr[0:CARRY, :] = pre_scr[tm:tm + CARRY, :]
    if n_sub >= 2:
        interleave(state_updates(n_sub - 1), finish(n_sub - 2, sts[n_sub - 2]), 1)
    else:
        run(state_updates(0))

    @pl.when(l == pl.num_programs(1) - 1)
    def _():
        ssm_ref[0] = s_scr[...]

    run(finish(n_sub - 1, sts[n_sub - 1]))


def _gdn_prompt(x, norm, win, wabt, convw, alog, dtb, onorm, wout, sample_rows, sample_state):
    b, l, d = x.shape
    tm = TM_A
    h = GDN_HEADS
    grid = (b, l // tm)
    n = sample_state.shape[0]
    bb = n // (grid[0] * grid[1])
    assert bb * grid[0] * grid[1] == n
    sample_block = lambda i, j: (i * grid[1] + j, 0, 0)
    row_spec = lambda width: pl.BlockSpec((bb, 1, width), sample_block)
    state_spec = pl.BlockSpec((bb, h, GDN_D, GDN_D), lambda i, j: (i * grid[1] + j, 0, 0, 0))
    rows3 = [r.reshape(n, 1, r.shape[-1]) for r in sample_rows]
    in_specs = [
        pl.BlockSpec((1, tm, d), lambda i, j: (i, j, 0)),
        _full_spec((1, d)), _full_spec(win.shape), _full_spec(wabt.shape), _full_spec(convw.shape),
        _full_spec((1, h)), _full_spec((1, h)), _full_spec((h, 1)), _full_spec((h, 1)),
        _full_spec((1, GDN_D)), _full_spec(wout.shape),
    ] + [row_spec(r.shape[-1]) for r in rows3] + [state_spec]
    out_specs = [
        pl.BlockSpec((1, tm, d), lambda i, j: (i, j, 0)),
        pl.BlockSpec((1, CONV_W - 1, QKV_W), lambda i, j: (i, 0, 0)),
        pl.BlockSpec((1, h, GDN_D, GDN_D), lambda i, j: (i, 0, 0, 0)),
        row_spec(h * GDN_D), state_spec,
    ]
    out_shape = [
        jax.ShapeDtypeStruct((b, l, d), F32),
        jax.ShapeDtypeStruct((b, CONV_W - 1, QKV_W), F32),
        jax.ShapeDtypeStruct((b, h, GDN_D, GDN_D), F32),
        jax.ShapeDtypeStruct((n, 1, h * GDN_D), F32),
        jax.ShapeDtypeStruct(sample_state.shape, F32),
    ]
    scratch = [
        pltpu.VMEM((tm + CARRY, QKV_W), F32),
        pltpu.VMEM((tm, h * GDN_D), F32),
        pltpu.VMEM((h, GDN_D, GDN_D), F32),
        pltpu.VMEM((h, tm, GDN_D), BF16),
        pltpu.VMEM((h, tm, GDN_D), BF16),
        pltpu.VMEM((h, tm, GDN_D), BF16),
        pltpu.VMEM((h, tm, GDN_D), F32),
        pltpu.VMEM((h, tm, GDN_D), BF16),
        pltpu.VMEM((h, tm, 2 * GDN_D), BF16),
        pltpu.VMEM((h, tm // 2, 2 * CHUNK), F32),
        pltpu.VMEM((h, tm // CHUNK, GDN_D), F32),
        pltpu.VMEM((h, tm, GDN_D), F32),
        pltpu.VMEM((h, tm // CHUNK, GDN_D + CHUNK, GDN_D), BF16),
        pltpu.VMEM((h, tm // CHUNK, GDN_D, GDN_D), F32),
    ]
    y, conv, ssm, o_s, state_s = pl.pallas_call(
        _gdn_prompt_kernel,
        grid=grid, in_specs=in_specs, out_specs=out_specs, out_shape=out_shape,
        scratch_shapes=scratch,
        compiler_params=pltpu.CompilerParams(
            dimension_semantics=("arbitrary", "arbitrary"), vmem_limit_bytes=VMEM_LIMIT),
        name="gdn_prompt",
    )(x, norm.reshape(1, d), win, wabt, convw,
      alog.reshape(1, h), dtb.reshape(1, h), alog.reshape(h, 1), dtb.reshape(h, 1),
      onorm.reshape(1, GDN_D), wout, *rows3, sample_state)
    return y, conv, ssm, o_s.reshape(n, h * GDN_D), state_s


def _gdn_sample_front_kernel(x_ref, norm_ref, win_ref, convw_ref,
                             c0_ref, c1_ref, c2_ref, alog_r_ref, dtb_r_ref,
                             pre_ref, q_ref, k_ref, u_ref, w_ref, qd_ref, gate_ref, eg_ref):
    x = x_ref[...]
    xn = x * lax.rsqrt(jnp.mean(x * x, axis=-1, keepdims=True) + EPS) * norm_ref[...]
    xb = xn.astype(BF16)
    gate_w = GDN_HEADS * GDN_D
    pre = _dot(xb, win_ref[:, :QKV_W])
    pre_ref[...] = pre
    gate_ref[...] = _dot(xb, win_ref[:, QKV_W:QKV_W + gate_w])
    ab = _dot(xb, win_ref[:, QKV_W + gate_w:])
    g = -jnp.exp(alog_r_ref[...]) * _softplus(ab[:, :GDN_HEADS] + dtb_r_ref[...])
    beta = jax.nn.sigmoid(ab[:, GDN_HEADS:])
    e_g = jnp.exp(g)
    eg_ref[...] = e_g
    act = _silu(((c0_ref[...] * convw_ref[0:1, :] + c1_ref[...] * convw_ref[1:2, :])
                 + c2_ref[...] * convw_ref[2:3, :]) + pre * convw_ref[3:4, :])
    w_all = GDN_HEADS * GDN_D
    for h in range(GDN_HEADS):
        cols = slice(h * GDN_D, (h + 1) * GDN_D)
        qh = act[:, h * GDN_D:(h + 1) * GDN_D]
        kh = act[:, w_all + h * GDN_D:w_all + (h + 1) * GDN_D]
        vh = act[:, 2 * w_all + h * GDN_D:2 * w_all + (h + 1) * GDN_D]
        qn = qh * lax.rsqrt(jnp.sum(qh * qh, axis=-1, keepdims=True) + EPS) * (GDN_D ** -0.5)
        kn = kh * lax.rsqrt(jnp.sum(kh * kh, axis=-1, keepdims=True) + EPS)
        b_col = beta[:, h:h + 1]
        eg_col = e_g[:, h:h + 1]
        q_ref[:, cols] = qn
        k_ref[:, cols] = kn
        u_ref[:, cols] = vh * b_col
        w_ref[:, cols] = kn * b_col * eg_col
        qd_ref[:, cols] = qn * eg_col


def _out_proj_kernel(x_ref, o_ref, gate_ref, onorm_ref, wout_ref, y_ref, *, head_w, norm_heads):
    pieces = []
    for h in range(x_ref.shape[1] // head_w):
        cols = slice(h * head_w, (h + 1) * head_w)
        o = o_ref[:, cols]
        if norm_heads:
            o = o * lax.rsqrt(jnp.mean(o * o, axis=-1, keepdims=True) + EPS) * onorm_ref[...]
        pieces.append((o * _silu(gate_ref[:, cols])).astype(BF16))
    y_ref[...] = x_ref[...] + _dot(jnp.concatenate(pieces, axis=1), wout_ref[...])


def _gdn_sample_front(x, conv_state, norm, win, convw, alog, dtb):
    n, d = x.shape
    h = GDN_HEADS
    hw = h * GDN_D
    conv_t = jnp.transpose(conv_state, (1, 0, 2))
    f = lambda shape: jax.ShapeDtypeStruct(shape, F32)
    pre, q, k, u, w, qd, gate, eg = pl.pallas_call(
        _gdn_sample_front_kernel,
        out_shape=[f((n, QKV_W)), f((n, hw)), f((n, hw)), f((n, hw)), f((n, hw)), f((n, hw)),
                   f((n, hw)), f((n, h))],
        compiler_params=pltpu.CompilerParams(vmem_limit_bytes=VMEM_LIMIT),
        name="gdn_sample_front",
    )(x, norm.reshape(1, d), win, convw, conv_t[0], conv_t[1], conv_t[2],
      alog.reshape(1, h), dtb.reshape(1, h))
    conv_new = jnp.stack([conv_t[1], conv_t[2], pre], axis=1)
    return (q, k, u, w, qd, eg), gate, conv_new


def _gdn_sample_out(x, o, gate, onorm, wout):
    return pl.pallas_call(
        functools.partial(_out_proj_kernel, head_w=GDN_D, norm_heads=True),
        out_shape=jax.ShapeDtypeStruct(x.shape, F32),
        name="gdn_sample_out",
    )(x, o, gate, onorm.reshape(1, GDN_D), wout)


def _head_rms(x, gain_row, n_heads):
    pieces = []
    for h in range(n_heads):
        xh = x[:, h * HEAD_DIM:(h + 1) * HEAD_DIM]
        pieces.append(xh * lax.rsqrt(jnp.mean(xh * xh, axis=-1, keepdims=True) + EPS) * gain_row)
    return pieces


def _pair_rms(x, gain2, left):
    sq = x * x
    lo = jnp.sum(jnp.where(left, sq, 0.0), axis=-1, keepdims=True)
    hi = jnp.sum(jnp.where(left, 0.0, sq), axis=-1, keepdims=True)
    ms = jnp.where(left, lo, hi) * (1.0 / HEAD_DIM)
    return x * lax.rsqrt(ms + EPS) * gain2


def _swa_prompt_kernel(y_ref, kvnorm_ref, wkv_ref, knorm2_ref, normb_ref, win_ref, qnorm2_ref,
                       sinks_ref, wout_ref, out_ref, kwin_ref, vwin_ref, k_scr, v_scr, o_scr):
    tq = y_ref.shape[1]
    w = WINDOW
    step = pl.program_id(1)
    lane = lax.broadcasted_iota(jnp.int32, (1, 2 * HEAD_DIM), 1)
    left = lane < HEAD_DIM

    @pl.when(step == 0)
    def _():
        k_scr[:, 0:w, :] = jnp.zeros((N_KV_HEADS, w, 2 * HEAD_DIM), BF16)
        v_scr[:, 0:w, :] = jnp.zeros((N_KV_HEADS, w, 2 * HEAD_DIM), BF16)

    @pl.when(step > 0)
    def _():
        k_scr[:, 0:w, :] = k_scr[:, tq:tq + w, :]
        v_scr[:, 0:w, :] = v_scr[:, tq:tq + w, :]

    y = y_ref[0]
    yn = y * lax.rsqrt(jnp.mean(y * y, axis=-1, keepdims=True) + EPS)
    kv = _dot((yn * kvnorm_ref[...]).astype(BF16), wkv_ref[...])
    for c in range(N_KV_HEADS // 2):
        cols = slice(c * 2 * HEAD_DIM, (c + 1) * 2 * HEAD_DIM)
        kp = _pair_rms(kv[:, cols], knorm2_ref[...], left)
        vp = kv[:, KV_W + c * 2 * HEAD_DIM:KV_W + (c + 1) * 2 * HEAD_DIM]
        kwin_ref[0, :, cols] = kp[tq - w:, :]
        vwin_ref[0, :, cols] = vp[tq - w:, :]
        kr = pltpu.roll(kp, HEAD_DIM, 1)
        vr = pltpu.roll(vp, HEAD_DIM, 1)
        k_scr[2 * c, w:w + tq, :] = jnp.where(left, kp, kr).astype(BF16)
        k_scr[2 * c + 1, w:w + tq, :] = jnp.where(left, kr, kp).astype(BF16)
        v_scr[2 * c, w:w + tq, :] = jnp.where(left, vp, vr).astype(BF16)
        v_scr[2 * c + 1, w:w + tq, :] = jnp.where(left, vr, vp).astype(BF16)

    qg = _dot((yn * normb_ref[...]).astype(BF16), win_ref[...])
    q_pairs = [_pair_rms(qg[:, c * 2 * HEAD_DIM:(c + 1) * 2 * HEAD_DIM], qnorm2_ref[...], left)
               for c in range(N_Q_HEADS // 2)]

    kj = lax.broadcasted_iota(jnp.int32, (2 * w, w), 0)
    qi = lax.broadcasted_iota(jnp.int32, (2 * w, w), 1)
    dist = qi - kj + w
    band = (dist >= 0) & (dist <= w)
    j_rel = (kj - w).astype(F32)
    i_row = lax.broadcasted_iota(jnp.int32, (1, w), 1).astype(F32)

    def scores(blk, hk):
        q_rows = slice(blk * w, (blk + 1) * w)
        lhs = []
        for c in (2 * hk, 2 * hk + 1):
            qp = q_pairs[c][q_rows, :]
            lhs.append(jnp.where(left, qp, 0.0))
            lhs.append(jnp.where(left, 0.0, qp))
        return _dot_nt(k_scr[hk, blk * w:blk * w + 2 * w, :],
                       jnp.concatenate(lhs, axis=0).astype(BF16))

    def attend(blk, hk, s4):
        q_rows = slice(blk * w, (blk + 1) * w)
        valid = band & ((step > 0) | (kj >= w)) if blk == 0 else band
        ps = []
        for g in range(Q_GROUP):
            hq = hk * Q_GROUP + g
            slope = 2.0 ** (-8.0 * (hq + 1) / N_Q_HEADS)
            a = jnp.where(valid, s4[:, g * w:(g + 1) * w] + slope * j_rel, -jnp.inf)
            sink = sinks_ref[0:1, hq:hq + 1] + slope * i_row
            mx = jnp.maximum(jnp.max(a, axis=0, keepdims=True), sink)
            p = jnp.exp(a - mx)
            inv = 1.0 / (jnp.sum(p, axis=0, keepdims=True) + jnp.exp(sink - mx))
            ps.append((p * inv).astype(BF16))
        o4 = _dot_tn(jnp.concatenate(ps, axis=1),
                     v_scr[hk, blk * w:blk * w + 2 * w, :])
        for j, c in enumerate((2 * hk, 2 * hk + 1)):
            o_scr[q_rows, c * 2 * HEAD_DIM:(c + 1) * 2 * HEAD_DIM] = jnp.where(
                left, o4[(2 * j) * w:(2 * j + 1) * w, :], o4[(2 * j + 1) * w:(2 * j + 2) * w, :])

    items = [(blk, hk) for blk in range(tq // w) for hk in range(N_KV_HEADS)]
    s_next = scores(*items[0])
    for n, item in enumerate(items):
        s_cur = s_next
        if n + 1 < len(items):
            s_next = scores(*items[n + 1])
        attend(*item, s_cur)

    o = (o_scr[...] * _silu(qg[:, ATT_W:])).astype(BF16)
    out_ref[0] = y + _dot(o, wout_ref[...])


def _swa_prompt(y, kvnorm, wkv, knorm, normb, win, qnorm, sinks, wout):
    b, l, d = y.shape
    tq = TQ_B
    w = WINDOW
    f = lambda shape: jax.ShapeDtypeStruct(shape, F32)
    knorm2 = jnp.concatenate([knorm, knorm]).reshape(1, 2 * HEAD_DIM)
    qnorm2 = (jnp.concatenate([qnorm, qnorm]) * (HEAD_DIM ** -0.5)).reshape(1, 2 * HEAD_DIM)
    in_specs = [
        pl.BlockSpec((1, tq, d), lambda i, j: (i, j, 0)),
        _full_spec((1, d)), _full_spec(wkv.shape), _full_spec((1, 2 * HEAD_DIM)), _full_spec((1, d)),
        _full_spec(win.shape), _full_spec((1, 2 * HEAD_DIM)), _full_spec((1, N_Q_HEADS)),
        _full_spec(wout.shape),
    ]
    out_specs = [
        pl.BlockSpec((1, tq, d), lambda i, j: (i, j, 0)),
        pl.BlockSpec((1, w, KV_W), lambda i, j: (i, 0, 0)),
        pl.BlockSpec((1, w, KV_W), lambda i, j: (i, 0, 0)),
    ]
    return pl.pallas_call(
        _swa_prompt_kernel,
        grid=(b, l // tq), in_specs=in_specs, out_specs=out_specs,
        out_shape=[f((b, l, d)), f((b, w, KV_W)), f((b, w, KV_W))],
        scratch_shapes=[pltpu.VMEM((N_KV_HEADS, w + tq, 2 * HEAD_DIM), BF16),
                        pltpu.VMEM((N_KV_HEADS, w + tq, 2 * HEAD_DIM), BF16),
                        pltpu.VMEM((tq, ATT_W), F32)],
        compiler_params=pltpu.CompilerParams(
            dimension_semantics=("arbitrary", "arbitrary"), vmem_limit_bytes=VMEM_LIMIT),
        name="swa_prompt",
    )(y, kvnorm.reshape(1, d), wkv, knorm2, normb.reshape(1, d), win, qnorm2,
      sinks.reshape(1, N_Q_HEADS), wout)


def _swa_sample_front_kernel(y_ref, kvnorm_ref, wkv_ref, knorm_ref, normb_ref, win_ref, qnorm_ref,
                             k_ref, v_ref, q_ref, gate_ref):
    y = y_ref[...]
    yn = y * lax.rsqrt(jnp.mean(y * y, axis=-1, keepdims=True) + EPS)
    kv = _dot((yn * kvnorm_ref[...]).astype(BF16), wkv_ref[...])
    k_ref[...] = jnp.concatenate(_head_rms(kv[:, :KV_W], knorm_ref[...], N_KV_HEADS), axis=1)
    v_ref[...] = kv[:, KV_W:]
    qg = _dot((yn * normb_ref[...]).astype(BF16), win_ref[...])
    q_ref[...] = jnp.concatenate(
        _head_rms(qg[:, :ATT_W], qnorm_ref[...] * (HEAD_DIM ** -0.5), N_Q_HEADS), axis=1)
    gate_ref[...] = qg[:, ATT_W:]


def _swa_sample_attn_kernel(q_ref, kn_ref, vn_ref, kc_ref, vc_ref, sinks_ref, slopes_ref,
                            o_ref, kwin_ref, vwin_ref):
    bb = kc_ref.shape[0]
    w = kc_ref.shape[1]
    hrow = lax.broadcasted_iota(jnp.int32, (N_Q_HEADS, KV_W), 0) // Q_GROUP
    lblk = lax.broadcasted_iota(jnp.int32, (N_Q_HEADS, KV_W), 1) // HEAD_DIM
    own = hrow == lblk
    dist_c = (w - lax.broadcasted_iota(jnp.int32, (1, w), 1)).astype(F32)
    krow = lax.broadcasted_iota(jnp.int32, (w, KV_W), 0)
    slopes = slopes_ref[...]
    sink = sinks_ref[...]
    qms, scs = [], []
    for b in range(bb):
        q = q_ref[b]
        qm = jnp.where(own, jnp.concatenate([q] * N_KV_HEADS, axis=1), 0.0).astype(BF16)
        qms.append(qm)
        scs.append(_dot_nt(qm, kc_ref[b].astype(BF16)))
    pcs, pns = [], []
    for b in range(bb):
        s_c = scs[b] - slopes * dist_c
        s_n = jnp.sum(qms[b].astype(F32) * kn_ref[b].astype(BF16).astype(F32), axis=-1, keepdims=True)
        mx = jnp.maximum(jnp.maximum(jnp.max(s_c, axis=-1, keepdims=True), s_n), sink)
        p_c = jnp.exp(s_c - mx)
        p_n = jnp.exp(s_n - mx)
        den = jnp.sum(p_c, axis=-1, keepdims=True) + p_n + jnp.exp(sink - mx)
        pcs.append((p_c / den).astype(BF16))
        pns.append(p_n / den)
    rs = [_dot(pcs[b], vc_ref[b].astype(BF16)) for b in range(bb)]
    for b in range(bb):
        kn = kn_ref[b]
        vn = vn_ref[b]
        r = rs[b] + pns[b].astype(BF16).astype(F32) * vn.astype(BF16).astype(F32)
        r = jnp.where(own, r, 0.0)
        acc = r[:, 0:HEAD_DIM]
        for blk in range(1, N_KV_HEADS):
            acc = acc + r[:, blk * HEAD_DIM:(blk + 1) * HEAD_DIM]
        o_ref[b] = acc
        kwin_ref[b] = jnp.where(krow == w - 1, kn, pltpu.roll(kc_ref[b], w - 1, 0))
        vwin_ref[b] = jnp.where(krow == w - 1, vn, pltpu.roll(vc_ref[b], w - 1, 0))


def _swa_sample(y, cache_k, cache_v, kvnorm, wkv, knorm, normb, win, qnorm, sinks, wout):
    n, d = y.shape
    w = cache_k.shape[1]
    f = lambda shape: jax.ShapeDtypeStruct(shape, F32)
    kn, vn, q, gate = pl.pallas_call(
        _swa_sample_front_kernel,
        out_shape=[f((n, KV_W)), f((n, KV_W)), f((n, ATT_W)), f((n, ATT_W))],
        compiler_params=pltpu.CompilerParams(vmem_limit_bytes=VMEM_LIMIT),
        name="swa_sample_front",
    )(y, kvnorm.reshape(1, d), wkv, knorm.reshape(1, HEAD_DIM), normb.reshape(1, d), win,
      qnorm.reshape(1, HEAD_DIM))
    bb = 8
    slopes = (2.0 ** (-8.0 * jnp.arange(1, N_Q_HEADS + 1, dtype=F32) / N_Q_HEADS)).reshape(N_Q_HEADS, 1)
    spec3 = lambda a, c: pl.BlockSpec((bb, a, c), lambda i: (i, 0, 0))
    o, kwin, vwin = pl.pallas_call(
        _swa_sample_attn_kernel,
        grid=(n // bb,),
        in_specs=[spec3(N_Q_HEADS, HEAD_DIM), spec3(1, KV_W), spec3(1, KV_W), spec3(w, KV_W),
                  spec3(w, KV_W), _full_spec((N_Q_HEADS, 1)), _full_spec((N_Q_HEADS, 1))],
        out_specs=[spec3(N_Q_HEADS, HEAD_DIM), spec3(w, KV_W), spec3(w, KV_W)],
        out_shape=[f((n, N_Q_HEADS, HEAD_DIM)), f((n, w, KV_W)), f((n, w, KV_W))],
        compiler_params=pltpu.CompilerParams(dimension_semantics=("arbitrary",)),
        name="swa_sample_attn",
    )(q.reshape(n, N_Q_HEADS, HEAD_DIM), kn.reshape(n, 1, KV_W), vn.reshape(n, 1, KV_W),
      cache_k.reshape(n, w, KV_W), cache_v.reshape(n, w, KV_W),
      sinks.reshape(N_Q_HEADS, 1), slopes)
    out = pl.pallas_call(
        functools.partial(_out_proj_kernel, head_w=ATT_W, norm_heads=False),
        out_shape=f((n, d)),
        name="swa_sample_out",
    )(y, o.reshape(n, ATT_W), gate, jnp.ones((1, ATT_W), F32), wout)
    return out, kwin, vwin


def kernel(x_prompt, x_sample, state_conv, state_ssm, cache_k_win, cache_v_win, norm_a, w_in_a, conv_w_a, a_log, dt_bias, o_norm_a, w_out_a, kv_norm, w_kv, k_norm, norm_b, w_in_b, q_norm, sinks, w_out_b):
    n_a = w_in_a.shape[0]
    n_b = w_in_b.shape[0]
    assert n_a == 1 and n_b == 1, "kernel is written for DEPTH == 2"
    bp, lp, d = x_prompt.shape
    n = x_sample.shape[0]
    hw = GDN_HEADS * GDN_D

    hp, hs = x_prompt, x_sample.reshape(n, d)
    conv_p, ssm_p, conv_s, ssm_s = [], [], [], []
    for layer in range(n_a):
        win_a = w_in_a[layer].astype(BF16)
        wabt = win_a[:, QKV_W + hw:].T
        wout = w_out_a[layer].astype(BF16)
        rows_s, gate_s, cbuf_s = _gdn_sample_front(hs, state_conv[layer], norm_a[layer], win_a,
                                                   conv_w_a[layer], a_log[layer], dt_bias[layer])
        hp, cbuf, st, o_s, st_s = _gdn_prompt(hp, norm_a[layer], win_a, wabt, conv_w_a[layer],
                                              a_log[layer], dt_bias[layer], o_norm_a[layer], wout,
                                              rows_s, state_ssm[layer])
        conv_p.append(cbuf)
        ssm_p.append(st)
        hs = _gdn_sample_out(hs, o_s, gate_s, o_norm_a[layer], wout)
        conv_s.append(cbuf_s)
        ssm_s.append(st_s)

    wkv = w_kv.astype(BF16)
    win = w_in_b[0].astype(BF16)
    woutb = w_out_b[0].astype(BF16)
    hp, k_win_p, v_win_p = _swa_prompt(hp, kv_norm, wkv, k_norm, norm_b[0], win, q_norm[0], sinks[0], woutb)
    hs, k_win_s, v_win_s = _swa_sample(hs, cache_k_win, cache_v_win, kv_norm, wkv, k_norm, norm_b[0],
                                       win, q_norm[0], sinks[0], woutb)
    kv_shape = (N_KV_HEADS, HEAD_DIM)
    return (hp, hs.reshape(n, 1, d), jnp.stack(conv_p), jnp.stack(ssm_p),
            k_win_p.reshape(bp, WINDOW, *kv_shape), v_win_p.reshape(bp, WINDOW, *kv_shape),
            jnp.stack(conv_s), jnp.stack(ssm_s),
            k_win_s.reshape(n, WINDOW, *kv_shape), v_win_s.reshape(n, WINDOW, *kv_shape))
```

```python
import functools

import jax
import jax.numpy as jnp
from jax import lax
from jax.experimental import pallas as pl
from jax.experimental.pallas import tpu as pltpu

F32 = jnp.float32
BF16 = jnp.bfloat16
EPS = 1e-6

D_MODEL = 1024
GDN_HEADS = 8
GDN_D = 128
QKV_W = 3 * GDN_HEADS * GDN_D
CONV_W = 4
CHUNK = 64
N_Q_HEADS = 16
N_KV_HEADS = 4
Q_GROUP = N_Q_HEADS // N_KV_HEADS
HEAD_DIM = 64
KV_W = N_KV_HEADS * HEAD_DIM
ATT_W = N_Q_HEADS * HEAD_DIM
WINDOW = 128

TM_A = 512
SUB_A = 256
TQ_B = 1024
CARRY = 8
VMEM_LIMIT = 52 * 1024 * 1024
VMEM_LIMIT_A = 62 * 1024 * 1024

_NT = (((1,), (1,)), ((), ()))
_TN = (((0,), (0,)), ((), ()))


def _dot(a, b):
    return jnp.dot(a, b, preferred_element_type=F32)


def _dot_nt(a, b):
    return lax.dot_general(a, b, _NT, preferred_element_type=F32)


def _dot_tn(a, b):
    return lax.dot_general(a, b, _TN, preferred_element_type=F32)


def _split(x):
    hi = x.astype(BF16)
    lo = (x - hi.astype(F32)).astype(BF16)
    return hi, lo


def _dot_exact_lhs(a_bf, b):
    b0 = b.astype(BF16)
    r1 = b - b0.astype(F32)
    b1 = r1.astype(BF16)
    b2 = (r1 - b1.astype(F32)).astype(BF16)
    return (_dot(a_bf, b2) + _dot(a_bf, b1)) + _dot(a_bf, b0)


def _dot_exact_rhs(a, b_bf):
    a0 = a.astype(BF16)
    r1 = a - a0.astype(F32)
    a1 = r1.astype(BF16)
    a2 = (r1 - a1.astype(F32)).astype(BF16)
    return (_dot(a2, b_bf) + _dot(a1, b_bf)) + _dot(a0, b_bf)


def _silu(x):
    return x * jax.nn.sigmoid(x)


def _softplus(x):
    return jnp.maximum(x, 0.0) + jnp.log1p(jnp.exp(-jnp.abs(x)))


def _halves(x, left):
    zero = jnp.zeros_like(x)
    return jnp.where(left, x, zero), jnp.where(left, zero, x)


def _block_diag(x, left):
    return jnp.concatenate(_halves(x, left), axis=0)


def _pair_split_product(a, b, left):
    ah, al = _split(a)
    bh, bl = _split(b)
    bh1, bh2 = _halves(bh, left)
    bl1, bl2 = _halves(bl, left)
    zero = jnp.zeros_like(bh)
    rhs = jnp.concatenate([jnp.concatenate([bh1, bl1], axis=1), jnp.concatenate([bh2, bl2], axis=1),
                           jnp.concatenate([bh1, zero], axis=1), jnp.concatenate([bh2, zero], axis=1)], axis=0)
    y = _dot(jnp.concatenate([ah, al], axis=1), rhs)
    half = y.shape[1] // 2
    return y[:, :half] + y[:, half:]


def _unit_lower_inverse_pairs(ms, left):
    n = ms[0].shape[0]
    row = lax.broadcasted_iota(jnp.int32, (n, 2 * n), 0)
    col = lax.broadcasted_iota(jnp.int32, (n, 2 * n), 1) % n
    eye2 = jnp.where(row == col, 1.0, 0.0).astype(F32)

    def times(a, b):
        return _dot(a.astype(BF16), _block_diag(b.astype(BF16), left))

    ts = [eye2 - m for m in ms]
    ps = [times(m, m) for m in ms]
    yield
    steps = max(1, (n - 1).bit_length()) - 1
    for _ in range(steps - 1):
        both = [_dot(jnp.concatenate([t.astype(BF16), p.astype(BF16)], axis=0), _block_diag(p.astype(BF16), left))
                for t, p in zip(ts, ps)]
        ts = [t + tp[:n, :] for t, tp in zip(ts, both)]
        ps = [tp[n:, :] for tp in both]
        yield
    ts = [t + times(t, p) for t, p in zip(ts, ps)]
    yield
    rs = [(eye2 - t) - _pair_split_product(m, t, left) for m, t in zip(ms, ts)]
    yield
    ts = [t + times(t, r) for t, r in zip(ts, rs)]
    yield
    return ts


def _full_spec(shape):
    nd = len(shape)
    return pl.BlockSpec(shape, lambda *_: (0,) * nd, pipeline_mode=pl.Buffered(1))


def _gdn_gates(ab, ab_t, alog_r, dtb_r, alog_c, dtb_c, tm, chunk):
    h = GDN_HEADS
    g_c = -jnp.exp(alog_r) * _softplus(ab[:, :h] + dtb_r)
    beta = jax.nn.sigmoid(ab[:, h:])
    g_r = -jnp.exp(alog_c) * _softplus(ab_t[:h, :] + dtb_c)
    row = lax.broadcasted_iota(jnp.int32, (tm, tm), 0)
    col = lax.broadcasted_iota(jnp.int32, (tm, tm), 1)
    same = (row // chunk) == (col // chunk)
    lower = jnp.where(same & (row >= col), 1.0, 0.0).astype(BF16)
    upper = jnp.where(same & (row <= col), 1.0, 0.0).astype(BF16)
    gc = _dot_exact_lhs(lower, g_c)
    gr = _dot_exact_rhs(g_r, upper)
    return beta, gc, gr


def _sample_state_update(row0, q_ref, k_ref, u_ref, w_ref, qd_ref, eg_ref, s_ref, o_ref, s_out_ref):
    bb = s_ref.shape[0]
    group = 8
    assert group % bb == 0
    base = pl.multiple_of((row0 // group) * group, group)
    part = (row0 - base) // bb

    def rows(ref, cols):
        x = ref[pl.ds(base, group), cols]
        out = x[0:bb, :]
        for p in range(1, group // bb):
            out = jnp.where(part == p, x[p * bb:(p + 1) * bb, :], out)
        return out

    row = lax.broadcasted_iota(jnp.int32, (8, GDN_D), 0)
    decay_all = rows(eg_ref, slice(None))
    prods, ins = [], []
    for h in range(GDN_HEADS):
        cols = slice(h * GDN_D, (h + 1) * GDN_D)
        q, k, u, w, qd = (rows(r, cols) for r in (q_ref, k_ref, u_ref, w_ref, qd_ref))
        for b in range(bb):
            lhs = jnp.where(row == 0, w[b:b + 1, :], jnp.where(row == 1, qd[b:b + 1, :], 0.0))
            prods.append(_dot(lhs.astype(BF16), s_ref[b, h].astype(BF16)))
            ins.append((b, h, cols, q[b:b + 1, :], k[b:b + 1, :], u[b:b + 1, :]))
    for (b, h, cols, q, k, u), prod in zip(ins, prods):
        q = q.astype(BF16).astype(F32)
        k = k.astype(BF16).astype(F32)
        v_new = u - prod[0:1, :]
        vb = v_new.astype(BF16).astype(F32)
        qk = jnp.sum(q * k, axis=-1, keepdims=True)
        o_ref[b, :, cols] = prod[1:2, :] + qk.astype(BF16).astype(F32) * vb
        k8 = jnp.where(row == 0, k, 0.0).astype(BF16)
        v8 = jnp.where(row == 0, vb, 0.0).astype(BF16)
        s_out_ref[b, h] = s_ref[b, h] * decay_all[b:b + 1, h:h + 1] + _dot_tn(k8, v8)


def _gdn_prompt_kernel(x_ref, norm_ref, win_ref, wabt_ref, convw_ref,
                       alog_r_ref, dtb_r_ref, alog_c_ref, dtb_c_ref, onorm_ref, wout_ref,
                       sq_ref, sk_ref, su_ref, sw_ref, sqd_ref, seg_ref, sstate_ref,
                       y_ref, conv_ref, ssm_ref, so_ref, sstate_out_ref,
                       pre_scr, gate_scr, s_scr, k_scr, kb_scr, q_scr, qd_scr, kd_scr,
                       rhs_scr, dec_scr, gt_scr, o_scr, lhs_scr, su_scr):
    tm = x_ref.shape[1]
    sub = SUB_A
    n_sub = tm // sub
    n_chunks = sub // CHUNK
    gate_w = GDN_HEADS * GDN_D
    l = pl.program_id(1)

    @pl.when(l == 0)
    def _():
        pre_scr[0:CARRY, :] = jnp.zeros((CARRY, QKV_W), F32)
        s_scr[...] = jnp.zeros(s_scr.shape, F32)

    ci = lax.broadcasted_iota(jnp.int32, (CHUNK, 2 * CHUNK), 0)
    cj = lax.broadcasted_iota(jnp.int32, (CHUNK, 2 * CHUNK), 1) % CHUNK
    causal = ci >= cj
    strict = ci > cj
    left = lax.broadcasted_iota(jnp.int32, (1, 2 * CHUNK), 1) < CHUNK
    heads = list(range(GDN_HEADS))

    sample_row0 = (pl.program_id(0) * pl.num_programs(1) + l) * sstate_ref.shape[0]
    _sample_state_update(sample_row0, sq_ref, sk_ref, su_ref, sw_ref, sqd_ref, seg_ref, sstate_ref,
                         so_ref, sstate_out_ref)

    def start(s):
        x = x_ref[0, s * sub:(s + 1) * sub, :]
        xn = x * lax.rsqrt(jnp.mean(x * x, axis=-1, keepdims=True) + EPS) * norm_ref[...]
        xb = xn.astype(BF16)
        ab = _dot(xb, win_ref[:, QKV_W + gate_w:])
        ab_t = _dot_nt(wabt_ref[...], xb)
        beta, gc, gr = _gdn_gates(ab, ab_t, alog_r_ref[...], dtb_r_ref[...],
                                  alog_c_ref[...], dtb_c_ref[...], sub, CHUNK)
        g_last = jnp.concatenate(
            [jnp.broadcast_to(gc[c * CHUNK + CHUNK - 1:c * CHUNK + CHUNK, :], (CHUNK, GDN_HEADS))
             for c in range(n_chunks)], axis=0)
        return dict(x=x, xb=xb, beta=beta, gc=gc, gr=gr, e_g=jnp.exp(gc),
                    e_kd=jnp.exp(g_last - gc), e_tot=jnp.exp(g_last))

    def project(s, st):
        for part in range(3):
            cols = slice(part * gate_w, (part + 1) * gate_w)
            pre_scr[CARRY + s * sub:CARRY + (s + 1) * sub, cols] = _dot(st["xb"], win_ref[:, cols])

    def project_gate(s, st):
        gate_scr[s * sub:(s + 1) * sub, :] = _dot(st["xb"], win_ref[:, QKV_W:QKV_W + gate_w])

    def conv_act(s, j):
        cols = slice(j * GDN_D, (j + 1) * GDN_D)
        ext = pre_scr[s * sub:s * sub + CARRY + sub, cols]
        half = ext[CARRY:, :] * (0.5 * convw_ref[CONV_W - 1:CONV_W, cols])
        for back in range(1, CONV_W):
            tap = CONV_W - 1 - back
            half = half + pltpu.roll(ext, back, 0)[CARRY:, :] * (0.5 * convw_ref[tap:tap + 1, cols])
        return half + half * jnp.tanh(half)

    def head_prep(s, st, h):
        rows = slice(s * sub, (s + 1) * sub)
        qh = conv_act(s, h)
        kh = conv_act(s, GDN_HEADS + h)
        vh = conv_act(s, 2 * GDN_HEADS + h)
        qn = qh * lax.rsqrt(jnp.sum(qh * qh, axis=-1, keepdims=True) + EPS) * (GDN_D ** -0.5)
        kn = kh * lax.rsqrt(jnp.sum(kh * kh, axis=-1, keepdims=True) + EPS)
        b_col = st["beta"][:, h:h + 1]
        eg_col = st["e_g"][:, h:h + 1]
        kb = kn * b_col
        k_scr[h, rows, :] = kn.astype(BF16)
        kb_scr[h, rows, :] = kb.astype(BF16)
        q_scr[h, rows, :] = qn.astype(BF16)
        qd_scr[h, rows, :] = qn * eg_col
        kd_scr[h, rows, :] = (kn * st["e_kd"][:, h:h + 1]).astype(BF16)
        rhs_scr[h, rows, 0:GDN_D] = (vh * b_col).astype(BF16)
        rhs_scr[h, rows, GDN_D:2 * GDN_D] = (kb * eg_col).astype(BF16)
        gc, gr = st["gc"], st["gr"]
        for c in range(n_chunks):
            gt_scr[h, s * n_chunks + c:s * n_chunks + c + 1, :] = jnp.broadcast_to(
                st["e_tot"][c * CHUNK:c * CHUNK + 1, h:h + 1], (1, GDN_D))
        for j in range(n_chunks // 2):
            r1 = slice(2 * j * CHUNK, (2 * j + 1) * CHUNK)
            r2 = slice((2 * j + 1) * CHUNK, (2 * j + 2) * CHUNK)
            diff = jnp.where(left, gc[r1, h:h + 1], gc[r2, h:h + 1]) - gr[h:h + 1, 2 * j * CHUNK:(2 * j + 2) * CHUNK]
            pair = s * (n_chunks // 2) + j
            dec_scr[h, pair * CHUNK:(pair + 1) * CHUNK, :] = jnp.exp(jnp.where(causal, diff, -jnp.inf))

    def chunk_terms(s):
        items = [(h, s * (n_chunks // 2) + j) for h in heads for j in range(n_chunks // 2)]
        aqs = []
        for h, pair in items:
            r12 = slice(2 * pair * CHUNK, (2 * pair + 2) * CHUNK)
            aqs.append(_dot_nt(jnp.concatenate([kb_scr[h, r12, :], q_scr[h, r12, :]], axis=0), k_scr[h, r12, :]))
        yield
        decs = [dec_scr[h, pair * CHUNK:(pair + 1) * CHUNK, :] for h, pair in items]
        ms = [jnp.where(strict, jnp.where(left, aq[:CHUNK, :], aq[CHUNK:2 * CHUNK, :]) * d, 0.0)
              for aq, d in zip(aqs, decs)]
        qks = [jnp.where(causal, jnp.where(left, aq[2 * CHUNK:3 * CHUNK, :], aq[3 * CHUNK:, :]) * d, 0.0)
               for aq, d in zip(aqs, decs)]
        ts = yield from _unit_lower_inverse_pairs(ms, left)
        sols = []
        for (h, pair), t, qk in zip(items, ts, qks):
            lo = t - t.astype(BF16).astype(F32)
            t_parts = (jnp.where(left, t, pltpu.roll(lo, CHUNK, 1)).astype(BF16),
                       jnp.where(left, pltpu.roll(t, CHUNK, 1), lo).astype(BF16))
            qk_parts = (qk[:, :CHUNK].astype(BF16), pltpu.roll(qk, CHUNK, 1)[:, :CHUNK].astype(BF16))
            for half in range(2):
                c = 2 * pair + half
                r = slice(c * CHUNK, (c + 1) * CHUNK)
                rhs = rhs_scr[h, r, :]
                sols.append((h, c, r, qk_parts[half],
                             _dot(t_parts[half], jnp.concatenate([rhs, rhs], axis=0)).astype(BF16)))
        yield
        outs = [(h, c, r, _dot_tn(kd_scr[h, r, :], sol),
                 _dot(qk_c, sol))
                for h, c, r, qk_c, sol in sols]
        for h, c, r, kd_uw, qk_uw in outs:
            lhs_scr[h, c, 0:GDN_D, :] = kd_uw[:, GDN_D:].astype(BF16)
            lhs_scr[h, c, GDN_D:GDN_D + CHUNK, :] = (qd_scr[h, r, :] - qk_uw[:, GDN_D:]).astype(BF16)
            su_scr[h, c] = kd_uw[:, :GDN_D]
            o_scr[h, r, :] = qk_uw[:, :GDN_D]
        yield

    def state_updates(s):
        for c in range(s * n_chunks, (s + 1) * n_chunks):
            rows = slice(c * CHUNK, (c + 1) * CHUNK)
            for h in heads:
                st_h = s_scr[h]
                prod = _dot(lhs_scr[h, c], st_h.astype(BF16))
                o_scr[h, rows, :] = o_scr[h, rows, :] + prod[GDN_D:, :]
                s_scr[h] = st_h * gt_scr[h, c:c + 1, :] + (su_scr[h, c] - prod[:GDN_D, :])
            yield

    def finish(s, st):
        rows = slice(s * sub, (s + 1) * sub)
        pieces = []
        for h in heads:
            o = o_scr[h, rows, :]
            on = o * lax.rsqrt(jnp.mean(o * o, axis=-1, keepdims=True) + EPS) * onorm_ref[...]
            pieces.append((on * _silu(gate_scr[rows, h * GDN_D:(h + 1) * GDN_D])).astype(BF16))
        on_all = jnp.concatenate(pieces, axis=1)
        for piece in range(D_MODEL // (2 * GDN_D)):
            cols = slice(piece * 2 * GDN_D, (piece + 1) * 2 * GDN_D)
            y_ref[0, rows, cols] = st["x"][:, cols] + _dot(on_all, wout_ref[:, cols])
            yield

    def run(gen):
        for _ in gen:
            pass

    def interleave(main, side, every):
        for n, _ in enumerate(main):
            if (n + 1) % every == 0:
                next(side, None)
        run(side)

    sts = [start(0)]
    project(0, sts[0])
    for s in range(n_sub):
        if s + 1 < n_sub:
            sts.append(start(s + 1))
        for h in heads:
            head_prep(s, sts[s], h)
        if s + 1 < n_sub:
            project(s + 1, sts[s + 1])
        project_gate(s, sts[s])
        if s == 0:
            run(chunk_terms(s))
        else:
            interleave(chunk_terms(s), state_updates(s - 1), 2)
            if s >= 2:
                run(finish(s - 2, sts[s - 2]))
    conv_ref[0] = pre_scr[tm + CARRY - 3:tm + CARRY, :]
    pre_scr[0:CARRY, :] = pre_scr[tm:tm + CARRY, :]
    if n_sub >= 2:
        interleave(state_updates(n_sub - 1), finish(n_sub - 2, sts[n_sub - 2]), 1)
    else:
        run(state_updates(0))

    @pl.when(l == pl.num_programs(1) - 1)
    def _():
        ssm_ref[0] = s_scr[...]

    run(finish(n_sub - 1, sts[n_sub - 1]))


def _gdn_prompt(x, norm, win, wabt, convw, alog, dtb, onorm, wout, sample_rows, sample_state):
    b, l, d = x.shape
    tm = TM_A
    h = GDN_HEADS
    grid = (b, l // tm)
    n = sample_state.shape[0]
    bb = n // (grid[0] * grid[1])
    assert bb * grid[0] * grid[1] == n
    sample_block = lambda i, j: (i * grid[1] + j, 0, 0)
    row_spec = lambda width: pl.BlockSpec((bb, 1, width), sample_block)
    state_spec = pl.BlockSpec((bb, h, GDN_D, GDN_D), lambda i, j: (i * grid[1] + j, 0, 0, 0))
    in_specs = [
        pl.BlockSpec((1, tm, d), lambda i, j: (i, j, 0)),
        _full_spec((1, d)), _full_spec(win.shape), _full_spec(wabt.shape), _full_spec(convw.shape),
        _full_spec((1, h)), _full_spec((1, h)), _full_spec((h, 1)), _full_spec((h, 1)),
        _full_spec((1, GDN_D)), _full_spec(wout.shape),
    ] + [_full_spec(r.shape) for r in sample_rows] + [state_spec]
    out_specs = [
        pl.BlockSpec((1, tm, d), lambda i, j: (i, j, 0)),
        pl.BlockSpec((1, CONV_W - 1, QKV_W), lambda i, j: (i, 0, 0)),
        pl.BlockSpec((1, h, GDN_D, GDN_D), lambda i, j: (i, 0, 0, 0)),
        row_spec(h * GDN_D), state_spec,
    ]
    out_shape = [
        jax.ShapeDtypeStruct((b, l, d), F32),
        jax.ShapeDtypeStruct((b, CONV_W - 1, QKV_W), F32),
        jax.ShapeDtypeStruct((b, h, GDN_D, GDN_D), F32),
        jax.ShapeDtypeStruct((n, 1, h * GDN_D), F32),
        jax.ShapeDtypeStruct(sample_state.shape, F32),
    ]
    scratch = [
        pltpu.VMEM((tm + CARRY, QKV_W), F32),
        pltpu.VMEM((tm, h * GDN_D), F32),
        pltpu.VMEM((h, GDN_D, GDN_D), F32),
        pltpu.VMEM((h, tm, GDN_D), BF16),
        pltpu.VMEM((h, tm, GDN_D), BF16),
        pltpu.VMEM((h, tm, GDN_D), BF16),
        pltpu.VMEM((h, tm, GDN_D), F32),
        pltpu.VMEM((h, tm, GDN_D), BF16),
        pltpu.VMEM((h, tm, 2 * GDN_D), BF16),
        pltpu.VMEM((h, tm // 2, 2 * CHUNK), F32),
        pltpu.VMEM((h, tm // CHUNK, GDN_D), F32),
        pltpu.VMEM((h, tm, GDN_D), F32),
        pltpu.VMEM((h, tm // CHUNK, GDN_D + CHUNK, GDN_D), BF16),
        pltpu.VMEM((h, tm // CHUNK, GDN_D, GDN_D), F32),
    ]
    y, conv, ssm, o_s, state_s = pl.pallas_call(
        _gdn_prompt_kernel,
        grid=grid, in_specs=in_specs, out_specs=out_specs, out_shape=out_shape,
        scratch_shapes=scratch,
        compiler_params=pltpu.CompilerParams(
            dimension_semantics=("arbitrary", "arbitrary"), vmem_limit_bytes=VMEM_LIMIT_A),
        name="gdn_prompt",
    )(x, norm.reshape(1, d), win, wabt, convw,
      alog.reshape(1, h), dtb.reshape(1, h), alog.reshape(h, 1), dtb.reshape(h, 1),
      onorm.reshape(1, GDN_D), wout, *sample_rows, sample_state)
    return y, conv, ssm, o_s.reshape(n, h * GDN_D), state_s


def _gdn_sample_front_kernel(x_ref, norm_ref, win_ref, convw_ref,
                             c0_ref, c1_ref, c2_ref, alog_r_ref, dtb_r_ref,
                             pre_ref, q_ref, k_ref, u_ref, w_ref, qd_ref, gate_ref, eg_ref):
    x = x_ref[...]
    xn = x * lax.rsqrt(jnp.mean(x * x, axis=-1, keepdims=True) + EPS) * norm_ref[...]
    xb = xn.astype(BF16)
    gate_w = GDN_HEADS * GDN_D
    pre = _dot(xb, win_ref[:, :QKV_W])
    pre_ref[...] = pre
    gate_ref[...] = _dot(xb, win_ref[:, QKV_W:QKV_W + gate_w])
    ab = _dot(xb, win_ref[:, QKV_W + gate_w:])
    g = -jnp.exp(alog_r_ref[...]) * _softplus(ab[:, :GDN_HEADS] + dtb_r_ref[...])
    beta = jax.nn.sigmoid(ab[:, GDN_HEADS:])
    e_g = jnp.exp(g)
    eg_ref[...] = e_g
    act = _silu(((c0_ref[...] * convw_ref[0:1, :] + c1_ref[...] * convw_ref[1:2, :])
                 + c2_ref[...] * convw_ref[2:3, :]) + pre * convw_ref[3:4, :])
    w_all = GDN_HEADS * GDN_D
    for h in range(GDN_HEADS):
        cols = slice(h * GDN_D, (h + 1) * GDN_D)
        qh = act[:, h * GDN_D:(h + 1) * GDN_D]
        kh = act[:, w_all + h * GDN_D:w_all + (h + 1) * GDN_D]
        vh = act[:, 2 * w_all + h * GDN_D:2 * w_all + (h + 1) * GDN_D]
        qn = qh * lax.rsqrt(jnp.sum(qh * qh, axis=-1, keepdims=True) + EPS) * (GDN_D ** -0.5)
        kn = kh * lax.rsqrt(jnp.sum(kh * kh, axis=-1, keepdims=True) + EPS)
        b_col = beta[:, h:h + 1]
        eg_col = e_g[:, h:h + 1]
        q_ref[:, cols] = qn
        k_ref[:, cols] = kn
        u_ref[:, cols] = vh * b_col
        w_ref[:, cols] = kn * b_col * eg_col
        qd_ref[:, cols] = qn * eg_col


def _out_proj_kernel(x_ref, o_ref, gate_ref, onorm_ref, wout_ref, y_ref, *, head_w, norm_heads):
    pieces = []
    for h in range(x_ref.shape[1] // head_w):
        cols = slice(h * head_w, (h + 1) * head_w)
        o = o_ref[:, cols]
        if norm_heads:
            o = o * lax.rsqrt(jnp.mean(o * o, axis=-1, keepdims=True) + EPS) * onorm_ref[...]
        pieces.append((o * _silu(gate_ref[:, cols])).astype(BF16))
    y_ref[...] = x_ref[...] + _dot(jnp.concatenate(pieces, axis=1), wout_ref[...])


def _gdn_sample_front(x, conv_state, norm, win, convw, alog, dtb):
    n, d = x.shape
    h = GDN_HEADS
    hw = h * GDN_D
    conv_t = jnp.transpose(conv_state, (1, 0, 2))
    f = lambda shape: jax.ShapeDtypeStruct(shape, F32)
    pre, q, k, u, w, qd, gate, eg = pl.pallas_call(
        _gdn_sample_front_kernel,
        out_shape=[f((n, QKV_W)), f((n, hw)), f((n, hw)), f((n, hw)), f((n, hw)), f((n, hw)),
                   f((n, hw)), f((n, h))],
        compiler_params=pltpu.CompilerParams(vmem_limit_bytes=VMEM_LIMIT),
        name="gdn_sample_front",
    )(x, norm.reshape(1, d), win, convw, conv_t[0], conv_t[1], conv_t[2],
      alog.reshape(1, h), dtb.reshape(1, h))
    conv_new = jnp.stack([conv_t[1], conv_t[2], pre], axis=1)
    return (q, k, u, w, qd, eg), gate, conv_new


def _gdn_sample_out(x, o, gate, onorm, wout):
    return pl.pallas_call(
        functools.partial(_out_proj_kernel, head_w=GDN_D, norm_heads=True),
        out_shape=jax.ShapeDtypeStruct(x.shape, F32),
        name="gdn_sample_out",
    )(x, o, gate, onorm.reshape(1, GDN_D), wout)


def _head_rms(x, gain_row, n_heads):
    pieces = []
    for h in range(n_heads):
        xh = x[:, h * HEAD_DIM:(h + 1) * HEAD_DIM]
        pieces.append(xh * lax.rsqrt(jnp.mean(xh * xh, axis=-1, keepdims=True) + EPS) * gain_row)
    return pieces


def _pair_rms(x, gain2, left):
    sq = x * x
    lo = jnp.sum(jnp.where(left, sq, 0.0), axis=-1, keepdims=True)
    hi = jnp.sum(jnp.where(left, 0.0, sq), axis=-1, keepdims=True)
    ms = jnp.where(left, lo, hi) * (1.0 / HEAD_DIM)
    return x * lax.rsqrt(ms + EPS) * gain2


def _swa_prompt_kernel(y_ref, kvnorm_ref, wkv_ref, knorm2_ref, normb_ref, win_ref, qnorm2_ref,
                       sinks_ref, wout_ref, out_ref, kwin_ref, vwin_ref, k_scr, v_scr, o_scr):
    tq = y_ref.shape[1]
    w = WINDOW
    step = pl.program_id(1)
    lane = lax.broadcasted_iota(jnp.int32, (1, 2 * HEAD_DIM), 1)
    left = lane < HEAD_DIM

    @pl.when(step == 0)
    def _():
        k_scr[:, 0:w, :] = jnp.zeros((N_KV_HEADS, w, 2 * HEAD_DIM), BF16)
        v_scr[:, 0:w, :] = jnp.zeros((N_KV_HEADS, w, 2 * HEAD_DIM), BF16)

    @pl.when(step > 0)
    def _():
        k_scr[:, 0:w, :] = k_scr[:, tq:tq + w, :]
        v_scr[:, 0:w, :] = v_scr[:, tq:tq + w, :]

    y = y_ref[0]
    yn = y * lax.rsqrt(jnp.mean(y * y, axis=-1, keepdims=True) + EPS)
    kv = _dot((yn * kvnorm_ref[...]).astype(BF16), wkv_ref[...])
    for c in range(N_KV_HEADS // 2):
        cols = slice(c * 2 * HEAD_DIM, (c + 1) * 2 * HEAD_DIM)
        kp = _pair_rms(kv[:, cols], knorm2_ref[...], left)
        vp = kv[:, KV_W + c * 2 * HEAD_DIM:KV_W + (c + 1) * 2 * HEAD_DIM]
        kwin_ref[0, :, cols] = kp[tq - w:, :]
        vwin_ref[0, :, cols] = vp[tq - w:, :]
        kr = pltpu.roll(kp, HEAD_DIM, 1)
        vr = pltpu.roll(vp, HEAD_DIM, 1)
        k_scr[2 * c, w:w + tq, :] = jnp.where(left, kp, kr).astype(BF16)
        k_scr[2 * c + 1, w:w + tq, :] = jnp.where(left, kr, kp).astype(BF16)
        v_scr[2 * c, w:w + tq, :] = jnp.where(left, vp, vr).astype(BF16)
        v_scr[2 * c + 1, w:w + tq, :] = jnp.where(left, vr, vp).astype(BF16)

    qg = _dot((yn * normb_ref[...]).astype(BF16), win_ref[...])
    q_pairs = [_pair_rms(qg[:, c * 2 * HEAD_DIM:(c + 1) * 2 * HEAD_DIM], qnorm2_ref[...], left)
               for c in range(N_Q_HEADS // 2)]

    kj = lax.broadcasted_iota(jnp.int32, (2 * w, w), 0)
    qi = lax.broadcasted_iota(jnp.int32, (2 * w, w), 1)
    dist = qi - kj + w
    band = (dist >= 0) & (dist <= w)
    j_rel = (kj - w).astype(F32)
    i_row = lax.broadcasted_iota(jnp.int32, (1, w), 1).astype(F32)

    def scores(blk, hk):
        q_rows = slice(blk * w, (blk + 1) * w)
        lhs = []
        for c in (2 * hk, 2 * hk + 1):
            qp = q_pairs[c][q_rows, :]
            lhs.append(jnp.where(left, qp, 0.0))
            lhs.append(jnp.where(left, 0.0, qp))
        return _dot_nt(k_scr[hk, blk * w:blk * w + 2 * w, :],
                       jnp.concatenate(lhs, axis=0).astype(BF16))

    def attend(blk, hk, s4):
        q_rows = slice(blk * w, (blk + 1) * w)
        valid = band & ((step > 0) | (kj >= w)) if blk == 0 else band
        ps = []
        for g in range(Q_GROUP):
            hq = hk * Q_GROUP + g
            slope = 2.0 ** (-8.0 * (hq + 1) / N_Q_HEADS)
            a = jnp.where(valid, s4[:, g * w:(g + 1) * w] + slope * j_rel, -jnp.inf)
            sink = sinks_ref[0:1, hq:hq + 1] + slope * i_row
            mx = jnp.maximum(jnp.max(a, axis=0, keepdims=True), sink)
            p = jnp.exp(a - mx)
            inv = 1.0 / (jnp.sum(p, axis=0, keepdims=True) + jnp.exp(sink - mx))
            ps.append((p * inv).astype(BF16))
        o4 = _dot_tn(jnp.concatenate(ps, axis=1),
                     v_scr[hk, blk * w:blk * w + 2 * w, :])
        for j, c in enumerate((2 * hk, 2 * hk + 1)):
            o_scr[q_rows, c * 2 * HEAD_DIM:(c + 1) * 2 * HEAD_DIM] = jnp.where(
                left, o4[(2 * j) * w:(2 * j + 1) * w, :], o4[(2 * j + 1) * w:(2 * j + 2) * w, :])

    items = [(blk, hk) for blk in range(tq // w) for hk in range(N_KV_HEADS)]
    s_next = scores(*items[0])
    for n, item in enumerate(items):
        s_cur = s_next
        if n + 1 < len(items):
            s_next = scores(*items[n + 1])
        attend(*item, s_cur)

    o = (o_scr[...] * _silu(qg[:, ATT_W:])).astype(BF16)
    out_ref[0] = y + _dot(o, wout_ref[...])


def _swa_prompt(y, kvnorm, wkv, knorm, normb, win, qnorm, sinks, wout):
    b, l, d = y.shape
    tq = TQ_B
    w = WINDOW
    f = lambda shape: jax.ShapeDtypeStruct(shape, F32)
    knorm2 = jnp.concatenate([knorm, knorm]).reshape(1, 2 * HEAD_DIM)
    qnorm2 = (jnp.concatenate([qnorm, qnorm]) * (HEAD_DIM ** -0.5)).reshape(1, 2 * HEAD_DIM)
    in_specs = [
        pl.BlockSpec((1, tq, d), lambda i, j: (i, j, 0)),
        _full_spec((1, d)), _full_spec(wkv.shape), _full_spec((1, 2 * HEAD_DIM)), _full_spec((1, d)),
        _full_spec(win.shape), _full_spec((1, 2 * HEAD_DIM)), _full_spec((1, N_Q_HEADS)),
        _full_spec(wout.shape),
    ]
    out_specs = [
        pl.BlockSpec((1, tq, d), lambda i, j: (i, j, 0)),
        pl.BlockSpec((1, w, KV_W), lambda i, j: (i, 0, 0)),
        pl.BlockSpec((1, w, KV_W), lambda i, j: (i, 0, 0)),
    ]
    return pl.pallas_call(
        _swa_prompt_kernel,
        grid=(b, l // tq), in_specs=in_specs, out_specs=out_specs,
        out_shape=[f((b, l, d)), f((b, w, KV_W)), f((b, w, KV_W))],
        scratch_shapes=[pltpu.VMEM((N_KV_HEADS, w + tq, 2 * HEAD_DIM), BF16),
                        pltpu.VMEM((N_KV_HEADS, w + tq, 2 * HEAD_DIM), BF16),
                        pltpu.VMEM((tq, ATT_W), F32)],
        compiler_params=pltpu.CompilerParams(
            dimension_semantics=("arbitrary", "arbitrary"), vmem_limit_bytes=VMEM_LIMIT),
        name="swa_prompt",
    )(y, kvnorm.reshape(1, d), wkv, knorm2, normb.reshape(1, d), win, qnorm2,
      sinks.reshape(1, N_Q_HEADS), wout)


def _swa_sample_front_kernel(y_ref, kvnorm_ref, wkv_ref, knorm_ref, normb_ref, win_ref, qnorm_ref,
                             k_ref, v_ref, q_ref, gate_ref):
    y = y_ref[...]
    yn = y * lax.rsqrt(jnp.mean(y * y, axis=-1, keepdims=True) + EPS)
    kv = _dot((yn * kvnorm_ref[...]).astype(BF16), wkv_ref[...])
    k_ref[...] = jnp.concatenate(_head_rms(kv[:, :KV_W], knorm_ref[...], N_KV_HEADS), axis=1)
    v_ref[...] = kv[:, KV_W:]
    qg = _dot((yn * normb_ref[...]).astype(BF16), win_ref[...])
    q_ref[...] = jnp.concatenate(
        _head_rms(qg[:, :ATT_W], qnorm_ref[...] * (HEAD_DIM ** -0.5), N_Q_HEADS), axis=1)
    gate_ref[...] = qg[:, ATT_W:]


def _swa_sample_attn_kernel(q_ref, kn_ref, vn_ref, kc_ref, vc_ref, sinks_ref, slopes_ref,
                            o_ref, kwin_ref, vwin_ref):
    bb = kc_ref.shape[0]
    w = kc_ref.shape[1]
    hrow = lax.broadcasted_iota(jnp.int32, (N_Q_HEADS, KV_W), 0) // Q_GROUP
    lblk = lax.broadcasted_iota(jnp.int32, (N_Q_HEADS, KV_W), 1) // HEAD_DIM
    own = hrow == lblk
    dist_c = (w - lax.broadcasted_iota(jnp.int32, (1, w), 1)).astype(F32)
    krow = lax.broadcasted_iota(jnp.int32, (w, KV_W), 0)
    slopes = slopes_ref[...]
    sink = sinks_ref[...]
    qms, scs = [], []
    for b in range(bb):
        q = q_ref[b]
        qm = jnp.where(own, jnp.concatenate([q] * N_KV_HEADS, axis=1), 0.0).astype(BF16)
        qms.append(qm)
        scs.append(_dot_nt(qm, kc_ref[b].astype(BF16)))
    pcs, pns = [], []
    for b in range(bb):
        s_c = scs[b] - slopes * dist_c
        s_n = jnp.sum(qms[b].astype(F32) * kn_ref[b].astype(BF16).astype(F32), axis=-1, keepdims=True)
        mx = jnp.maximum(jnp.maximum(jnp.max(s_c, axis=-1, keepdims=True), s_n), sink)
        p_c = jnp.exp(s_c - mx)
        p_n = jnp.exp(s_n - mx)
        den = jnp.sum(p_c, axis=-1, keepdims=True) + p_n + jnp.exp(sink - mx)
        pcs.append((p_c / den).astype(BF16))
        pns.append(p_n / den)
    rs = [_dot(pcs[b], vc_ref[b].astype(BF16)) for b in range(bb)]
    for b in range(bb):
        kn = kn_ref[b]
        vn = vn_ref[b]
        r = rs[b] + pns[b].astype(BF16).astype(F32) * vn.astype(BF16).astype(F32)
        r = jnp.where(own, r, 0.0)
        acc = r[:, 0:HEAD_DIM]
        for blk in range(1, N_KV_HEADS):
            acc = acc + r[:, blk * HEAD_DIM:(blk + 1) * HEAD_DIM]
        o_ref[b] = acc
        kwin_ref[b] = jnp.where(krow == w - 1, kn, pltpu.roll(kc_ref[b], w - 1, 0))
        vwin_ref[b] = jnp.where(krow == w - 1, vn, pltpu.roll(vc_ref[b], w - 1, 0))


def _swa_sample(y, cache_k, cache_v, kvnorm, wkv, knorm, normb, win, qnorm, sinks, wout):
    n, d = y.shape
    w = cache_k.shape[1]
    f = lambda shape: jax.ShapeDtypeStruct(shape, F32)
    kn, vn, q, gate = pl.pallas_call(
        _swa_sample_front_kernel,
        out_shape=[f((n, KV_W)), f((n, KV_W)), f((n, ATT_W)), f((n, ATT_W))],
        compiler_params=pltpu.CompilerParams(vmem_limit_bytes=VMEM_LIMIT),
        name="swa_sample_front",
    )(y, kvnorm.reshape(1, d), wkv, knorm.reshape(1, HEAD_DIM), normb.reshape(1, d), win,
      qnorm.reshape(1, HEAD_DIM))
    bb = 8
    slopes = (2.0 ** (-8.0 * jnp.arange(1, N_Q_HEADS + 1, dtype=F32) / N_Q_HEADS)).reshape(N_Q_HEADS, 1)
    spec3 = lambda a, c: pl.BlockSpec((bb, a, c), lambda i: (i, 0, 0))
    o, kwin, vwin = pl.pallas_call(
        _swa_sample_attn_kernel,
        grid=(n // bb,),
        in_specs=[spec3(N_Q_HEADS, HEAD_DIM), spec3(1, KV_W), spec3(1, KV_W), spec3(w, KV_W),
                  spec3(w, KV_W), _full_spec((N_Q_HEADS, 1)), _full_spec((N_Q_HEADS, 1))],
        out_specs=[spec3(N_Q_HEADS, HEAD_DIM), spec3(w, KV_W), spec3(w, KV_W)],
        out_shape=[f((n, N_Q_HEADS, HEAD_DIM)), f((n, w, KV_W)), f((n, w, KV_W))],
        compiler_params=pltpu.CompilerParams(dimension_semantics=("arbitrary",)),
        name="swa_sample_attn",
    )(q.reshape(n, N_Q_HEADS, HEAD_DIM), kn.reshape(n, 1, KV_W), vn.reshape(n, 1, KV_W),
      cache_k.reshape(n, w, KV_W), cache_v.reshape(n, w, KV_W),
      sinks.reshape(N_Q_HEADS, 1), slopes)
    out = pl.pallas_call(
        functools.partial(_out_proj_kernel, head_w=ATT_W, norm_heads=False),
        out_shape=f((n, d)),
        name="swa_sample_out",
    )(y, o.reshape(n, ATT_W), gate, jnp.ones((1, ATT_W), F32), wout)
    return out, kwin, vwin


def kernel(x_prompt, x_sample, state_conv, state_ssm, cache_k_win, cache_v_win, norm_a, w_in_a, conv_w_a, a_log, dt_bias, o_norm_a, w_out_a, kv_norm, w_kv, k_norm, norm_b, w_in_b, q_norm, sinks, w_out_b):
    n_a = w_in_a.shape[0]
    n_b = w_in_b.shape[0]
    assert n_a == 1 and n_b == 1, "kernel is written for DEPTH == 2"
    bp, lp, d = x_prompt.shape
    n = x_sample.shape[0]
    hw = GDN_HEADS * GDN_D

    hp, hs = x_prompt, x_sample.reshape(n, d)
    conv_p, ssm_p, conv_s, ssm_s = [], [], [], []
    for layer in range(n_a):
        win_a = w_in_a[layer].astype(BF16)
        wabt = win_a[:, QKV_W + hw:].T
        wout = w_out_a[layer].astype(BF16)
        rows_s, gate_s, cbuf_s = _gdn_sample_front(hs, state_conv[layer], norm_a[layer], win_a,
                                                   conv_w_a[layer], a_log[layer], dt_bias[layer])
        hp, cbuf, st, o_s, st_s = _gdn_prompt(hp, norm_a[layer], win_a, wabt, conv_w_a[layer],
                                              a_log[layer], dt_bias[layer], o_norm_a[layer], wout,
                                              rows_s, state_ssm[layer])
        conv_p.append(cbuf)
        ssm_p.append(st)
        hs = _gdn_sample_out(hs, o_s, gate_s, o_norm_a[layer], wout)
        conv_s.append(cbuf_s)
        ssm_s.append(st_s)

    wkv = w_kv.astype(BF16)
    win = w_in_b[0].astype(BF16)
    woutb = w_out_b[0].astype(BF16)
    hp, k_win_p, v_win_p = _swa_prompt(hp, kv_norm, wkv, k_norm, norm_b[0], win, q_norm[0], sinks[0], woutb)
    hs, k_win_s, v_win_s = _swa_sample(hs, cache_k_win, cache_v_win, kv_norm, wkv, k_norm, norm_b[0],
                                       win, q_norm[0], sinks[0], woutb)
    kv_shape = (N_KV_HEADS, HEAD_DIM)
    return (hp, hs.reshape(n, 1, d), jnp.stack(conv_p), jnp.stack(ssm_p),
            k_win_p.reshape(bp, WINDOW, *kv_shape), v_win_p.reshape(bp, WINDOW, *kv_shape),
            jnp.stack(conv_s), jnp.stack(ssm_s),
            k_win_s.reshape(n, WINDOW, *kv_shape), v_win_s.reshape(n, WINDOW, *kv_shape))
```

```python
import functools

import jax
import jax.numpy as jnp
from jax import lax
from jax.experimental import pallas as pl
from jax.experimental.pallas import tpu as pltpu

F32 = jnp.float32
BF16 = jnp.bfloat16
EPS = 1e-6

D_MODEL = 1024
GDN_HEADS = 8
GDN_D = 128
QKV_W = 3 * GDN_HEADS * GDN_D
CONV_W = 4
CHUNK = 64
N_Q_HEADS = 16
N_KV_HEADS = 4
Q_GROUP = N_Q_HEADS // N_KV_HEADS
HEAD_DIM = 64
KV_W = N_KV_HEADS * HEAD_DIM
ATT_W = N_Q_HEADS * HEAD_DIM
WINDOW = 128

TM_A = 512
SUB_A = 256
TQ_B = 1024
CARRY = 8
VMEM_LIMIT = 52 * 1024 * 1024
VMEM_LIMIT_A = 62 * 1024 * 1024

_NT = (((1,), (1,)), ((), ()))
_TN = (((0,), (0,)), ((), ()))


def _dot(a, b):
    return jnp.dot(a, b, preferred_element_type=F32)


def _dot_nt(a, b):
    return lax.dot_general(a, b, _NT, preferred_element_type=F32)


def _dot_tn(a, b):
    return lax.dot_general(a, b, _TN, preferred_element_type=F32)


def _split(x):
    hi = x.astype(BF16)
    lo = (x - hi.astype(F32)).astype(BF16)
    return hi, lo


def _dot_exact_lhs(a_bf, b):
    b0 = b.astype(BF16)
    r1 = b - b0.astype(F32)
    b1 = r1.astype(BF16)
    b2 = (r1 - b1.astype(F32)).astype(BF16)
    return (_dot(a_bf, b2) + _dot(a_bf, b1)) + _dot(a_bf, b0)


def _dot_exact_rhs(a, b_bf):
    a0 = a.astype(BF16)
    r1 = a - a0.astype(F32)
    a1 = r1.astype(BF16)
    a2 = (r1 - a1.astype(F32)).astype(BF16)
    return (_dot(a2, b_bf) + _dot(a1, b_bf)) + _dot(a0, b_bf)


def _silu(x):
    return x * jax.nn.sigmoid(x)


def _softplus(x):
    return jnp.maximum(x, 0.0) + jnp.log1p(jnp.exp(-jnp.abs(x)))


def _halves(x, left):
    zero = jnp.zeros_like(x)
    return jnp.where(left, x, zero), jnp.where(left, zero, x)


def _block_diag(x, left):
    return jnp.concatenate(_halves(x, left), axis=0)


def _pair_split_product(a, b, left):
    ah, al = _split(a)
    bh, bl = _split(b)
    bh1, bh2 = _halves(bh, left)
    bl1, bl2 = _halves(bl, left)
    zero = jnp.zeros_like(bh)
    rhs = jnp.concatenate([jnp.concatenate([bh1, bl1], axis=1), jnp.concatenate([bh2, bl2], axis=1),
                           jnp.concatenate([bh1, zero], axis=1), jnp.concatenate([bh2, zero], axis=1)], axis=0)
    y = _dot(jnp.concatenate([ah, al], axis=1), rhs)
    half = y.shape[1] // 2
    return y[:, :half] + y[:, half:]


def _unit_lower_inverse_pairs(ms, left):
    n = ms[0].shape[0]
    row = lax.broadcasted_iota(jnp.int32, (n, 2 * n), 0)
    col = lax.broadcasted_iota(jnp.int32, (n, 2 * n), 1) % n
    eye2 = jnp.where(row == col, 1.0, 0.0).astype(F32)

    def times(a, b):
        return _dot(a.astype(BF16), _block_diag(b.astype(BF16), left))

    ts = [eye2 - m for m in ms]
    ps = [times(m, m) for m in ms]
    yield
    steps = max(1, (n - 1).bit_length()) - 1
    for _ in range(steps - 1):
        both = [_dot(jnp.concatenate([t.astype(BF16), p.astype(BF16)], axis=0), _block_diag(p.astype(BF16), left))
                for t, p in zip(ts, ps)]
        ts = [t + tp[:n, :] for t, tp in zip(ts, both)]
        ps = [tp[n:, :] for tp in both]
        yield
    ts = [t + times(t, p) for t, p in zip(ts, ps)]
    yield
    rs = [(eye2 - t) - _pair_split_product(m, t, left) for m, t in zip(ms, ts)]
    yield
    ts = [t + times(t, r) for t, r in zip(ts, rs)]
    yield
    return ts


def _full_spec(shape):
    nd = len(shape)
    return pl.BlockSpec(shape, lambda *_: (0,) * nd, pipeline_mode=pl.Buffered(1))


def _gdn_gates(ab, ab_t, alog_r, dtb_r, alog_c, dtb_c, tm, chunk):
    h = GDN_HEADS
    g_c = -jnp.exp(alog_r) * _softplus(ab[:, :h] + dtb_r)
    beta = jax.nn.sigmoid(ab[:, h:])
    g_r = -jnp.exp(alog_c) * _softplus(ab_t[:h, :] + dtb_c)
    row = lax.broadcasted_iota(jnp.int32, (tm, tm), 0)
    col = lax.broadcasted_iota(jnp.int32, (tm, tm), 1)
    same = (row // chunk) == (col // chunk)
    lower = jnp.where(same & (row >= col), 1.0, 0.0).astype(BF16)
    upper = jnp.where(same & (row <= col), 1.0, 0.0).astype(BF16)
    gc = _dot_exact_lhs(lower, g_c)
    gr = _dot_exact_rhs(g_r, upper)
    return beta, gc, gr


def _sample_state_update(row0, q_ref, k_ref, u_ref, w_ref, qd_ref, eg_ref, s_ref, o_ref, s_out_ref):
    bb = s_ref.shape[0]
    group = 8
    assert group % bb == 0
    base = pl.multiple_of((row0 // group) * group, group)
    part = (row0 - base) // bb

    def rows(ref, cols):
        x = ref[pl.ds(base, group), cols]
        out = x[0:bb, :]
        for p in range(1, group // bb):
            out = jnp.where(part == p, x[p * bb:(p + 1) * bb, :], out)
        return out

    row = lax.broadcasted_iota(jnp.int32, (8, GDN_D), 0)
    decay_all = rows(eg_ref, slice(None))
    prods, ins = [], []
    for h in range(GDN_HEADS):
        cols = slice(h * GDN_D, (h + 1) * GDN_D)
        q, k, u, w, qd = (rows(r, cols) for r in (q_ref, k_ref, u_ref, w_ref, qd_ref))
        for b in range(bb):
            lhs = jnp.where(row == 0, w[b:b + 1, :], jnp.where(row == 1, qd[b:b + 1, :], 0.0))
            prods.append(_dot(lhs.astype(BF16), s_ref[b, h].astype(BF16)))
            ins.append((b, h, cols, q[b:b + 1, :], k[b:b + 1, :], u[b:b + 1, :]))
    for (b, h, cols, q, k, u), prod in zip(ins, prods):
        q = q.astype(BF16).astype(F32)
        k = k.astype(BF16).astype(F32)
        v_new = u - prod[0:1, :]
        vb = v_new.astype(BF16).astype(F32)
        qk = jnp.sum(q * k, axis=-1, keepdims=True)
        o_ref[b, :, cols] = prod[1:2, :] + qk.astype(BF16).astype(F32) * vb
        k8 = jnp.where(row == 0, k, 0.0).astype(BF16)
        v8 = jnp.where(row == 0, vb, 0.0).astype(BF16)
        s_out_ref[b, h] = s_ref[b, h] * decay_all[b:b + 1, h:h + 1] + _dot_tn(k8, v8)


def _gdn_prompt_kernel(x_ref, norm_ref, win_ref, wabt_ref, convw_ref,
                       alog_r_ref, dtb_r_ref, alog_c_ref, dtb_c_ref, onorm_ref, wout_ref,
                       sq_ref, sk_ref, su_ref, sw_ref, sqd_ref, seg_ref, sstate_ref,
                       y_ref, conv_ref, ssm_ref, so_ref, sstate_out_ref,
                       pre_scr, gate_scr, s_scr, k_scr, kb_scr, q_scr, qd_scr, kd_scr,
                       rhs_scr, dec_scr, gt_scr, o_scr, lhs_scr, su_scr):
    tm = x_ref.shape[1]
    sub = SUB_A
    n_sub = tm // sub
    n_chunks = sub // CHUNK
    gate_w = GDN_HEADS * GDN_D
    l = pl.program_id(1)

    @pl.when(l == 0)
    def _():
        pre_scr[0:CARRY, :] = jnp.zeros((CARRY, QKV_W), F32)
        s_scr[...] = jnp.zeros(s_scr.shape, F32)

    ci = lax.broadcasted_iota(jnp.int32, (CHUNK, 2 * CHUNK), 0)
    cj = lax.broadcasted_iota(jnp.int32, (CHUNK, 2 * CHUNK), 1) % CHUNK
    causal = ci >= cj
    strict = ci > cj
    left = lax.broadcasted_iota(jnp.int32, (1, 2 * CHUNK), 1) < CHUNK
    heads = list(range(GDN_HEADS))

    sample_row0 = (pl.program_id(0) * pl.num_programs(1) + l) * sstate_ref.shape[0]
    _sample_state_update(sample_row0, sq_ref, sk_ref, su_ref, sw_ref, sqd_ref, seg_ref, sstate_ref,
                         so_ref, sstate_out_ref)

    def start(s):
        x = x_ref[0, s * sub:(s + 1) * sub, :]
        xn = x * lax.rsqrt(jnp.mean(x * x, axis=-1, keepdims=True) + EPS) * norm_ref[...]
        xb = xn.astype(BF16)
        ab = _dot(xb, win_ref[:, QKV_W + gate_w:])
        ab_t = _dot_nt(wabt_ref[...], xb)
        beta, gc, gr = _gdn_gates(ab, ab_t, alog_r_ref[...], dtb_r_ref[...],
                                  alog_c_ref[...], dtb_c_ref[...], sub, CHUNK)
        g_last = jnp.concatenate(
            [jnp.broadcast_to(gc[c * CHUNK + CHUNK - 1:c * CHUNK + CHUNK, :], (CHUNK, GDN_HEADS))
             for c in range(n_chunks)], axis=0)
        return dict(x=x, xb=xb, beta=beta, gc=gc, gr=gr, e_g=jnp.exp(gc),
                    e_kd=jnp.exp(g_last - gc), e_tot=jnp.exp(g_last))

    def project(s, st):
        for part in range(3):
            cols = slice(part * gate_w, (part + 1) * gate_w)
            pre_scr[CARRY + s * sub:CARRY + (s + 1) * sub, cols] = _dot(st["xb"], win_ref[:, cols])

    def project_gate(s, st):
        gate_scr[s * sub:(s + 1) * sub, :] = _dot(st["xb"], win_ref[:, QKV_W:QKV_W + gate_w])

    def conv_act(s, j):
        cols = slice(j * GDN_D, (j + 1) * GDN_D)
        ext = pre_scr[s * sub:s * sub + CARRY + sub, cols]
        half = ext[CARRY:, :] * (0.5 * convw_ref[CONV_W - 1:CONV_W, cols])
        for back in range(1, CONV_W):
            tap = CONV_W - 1 - back
            half = half + pltpu.roll(ext, back, 0)[CARRY:, :] * (0.5 * convw_ref[tap:tap + 1, cols])
        return half + half * jnp.tanh(half)

    def head_prep(s, st, h):
        rows = slice(s * sub, (s + 1) * sub)
        qh = conv_act(s, h)
        kh = conv_act(s, GDN_HEADS + h)
        vh = conv_act(s, 2 * GDN_HEADS + h)
        qn = qh * lax.rsqrt(jnp.sum(qh * qh, axis=-1, keepdims=True) + EPS) * (GDN_D ** -0.5)
        kn = kh * lax.rsqrt(jnp.sum(kh * kh, axis=-1, keepdims=True) + EPS)
        b_col = st["beta"][:, h:h + 1]
        eg_col = st["e_g"][:, h:h + 1]
        kb = kn * b_col
        k_scr[h, rows, :] = kn.astype(BF16)
        kb_scr[h, rows, :] = kb.astype(BF16)
        q_scr[h, rows, :] = qn.astype(BF16)
        qd_scr[h, rows, :] = qn * eg_col
        kd_scr[h, rows, :] = (kn * st["e_kd"][:, h:h + 1]).astype(BF16)
        rhs_scr[h, rows, 0:GDN_D] = (vh * b_col).astype(BF16)
        rhs_scr[h, rows, GDN_D:2 * GDN_D] = (kb * eg_col).astype(BF16)
        gc, gr = st["gc"], st["gr"]
        for c in range(n_chunks):
            gt_scr[h, s * n_chunks + c:s * n_chunks + c + 1, :] = jnp.broadcast_to(
                st["e_tot"][c * CHUNK:c * CHUNK + 1, h:h + 1], (1, GDN_D))
        for j in range(n_chunks // 2):
            r1 = slice(2 * j * CHUNK, (2 * j + 1) * CHUNK)
            r2 = slice((2 * j + 1) * CHUNK, (2 * j + 2) * CHUNK)
            diff = jnp.where(left, gc[r1, h:h + 1], gc[r2, h:h + 1]) - gr[h:h + 1, 2 * j * CHUNK:(2 * j + 2) * CHUNK]
            pair = s * (n_chunks // 2) + j
            dec_scr[h, pair * CHUNK:(pair + 1) * CHUNK, :] = jnp.exp(jnp.where(causal, diff, -jnp.inf))

    def chunk_terms(s):
        items = [(h, s * (n_chunks // 2) + j) for h in heads for j in range(n_chunks // 2)]
        aqs = []
        for h, pair in items:
            r12 = slice(2 * pair * CHUNK, (2 * pair + 2) * CHUNK)
            aqs.append(_dot_nt(jnp.concatenate([kb_scr[h, r12, :], q_scr[h, r12, :]], axis=0), k_scr[h, r12, :]))
        yield
        decs = [dec_scr[h, pair * CHUNK:(pair + 1) * CHUNK, :] for h, pair in items]
        ms = [jnp.where(strict, jnp.where(left, aq[:CHUNK, :], aq[CHUNK:2 * CHUNK, :]) * d, 0.0)
              for aq, d in zip(aqs, decs)]
        qks = [jnp.where(causal, jnp.where(left, aq[2 * CHUNK:3 * CHUNK, :], aq[3 * CHUNK:, :]) * d, 0.0)
               for aq, d in zip(aqs, decs)]
        ts = yield from _unit_lower_inverse_pairs(ms, left)
        sols = []
        for (h, pair), t, qk in zip(items, ts, qks):
            lo = t - t.astype(BF16).astype(F32)
            t_parts = (jnp.where(left, t, pltpu.roll(lo, CHUNK, 1)).astype(BF16),
                       jnp.where(left, pltpu.roll(t, CHUNK, 1), lo).astype(BF16))
            qk_parts = (qk[:, :CHUNK].astype(BF16), pltpu.roll(qk, CHUNK, 1)[:, :CHUNK].astype(BF16))
            for half in range(2):
                c = 2 * pair + half
                r = slice(c * CHUNK, (c + 1) * CHUNK)
                rhs = rhs_scr[h, r, :]
                sols.append((h, c, r, qk_parts[half],
                             _dot(t_parts[half], jnp.concatenate([rhs, rhs], axis=0)).astype(BF16)))
        yield
        outs = [(h, c, r, _dot_tn(kd_scr[h, r, :], sol),
                 _dot(qk_c, sol))
                for h, c, r, qk_c, sol in sols]
        for h, c, r, kd_uw, qk_uw in outs:
            lhs_scr[h, c, 0:GDN_D, :] = kd_uw[:, GDN_D:].astype(BF16)
            lhs_scr[h, c, GDN_D:GDN_D + CHUNK, :] = (qd_scr[h, r, :] - qk_uw[:, GDN_D:]).astype(BF16)
            su_scr[h, c] = kd_uw[:, :GDN_D]
            o_scr[h, r, :] = qk_uw[:, :GDN_D]
        yield

    def state_updates(s):
        for c in range(s * n_chunks, (s + 1) * n_chunks):
            rows = slice(c * CHUNK, (c + 1) * CHUNK)
            for h in heads:
                st_h = s_scr[h]
                prod = _dot(lhs_scr[h, c], st_h.astype(BF16))
                o_scr[h, rows, :] = o_scr[h, rows, :] + prod[GDN_D:, :]
                s_scr[h] = st_h * gt_scr[h, c:c + 1, :] + (su_scr[h, c] - prod[:GDN_D, :])
            yield

    def finish(s, st):
        rows = slice(s * sub, (s + 1) * sub)
        pieces = []
        for h in heads:
            o = o_scr[h, rows, :]
            on = o * lax.rsqrt(jnp.mean(o * o, axis=-1, keepdims=True) + EPS) * onorm_ref[...]
            pieces.append((on * _silu(gate_scr[rows, h * GDN_D:(h + 1) * GDN_D])).astype(BF16))
        on_all = jnp.concatenate(pieces, axis=1)
        for piece in range(D_MODEL // (2 * GDN_D)):
            cols = slice(piece * 2 * GDN_D, (piece + 1) * 2 * GDN_D)
            y_ref[0, rows, cols] = st["x"][:, cols] + _dot(on_all, wout_ref[:, cols])
            yield

    def run(gen):
        for _ in gen:
            pass

    def interleave(main, side, every):
        for n, _ in enumerate(main):
            if (n + 1) % every == 0:
                next(side, None)
        run(side)

    sts = [start(0)]
    project(0, sts[0])
    for s in range(n_sub):
        if s + 1 < n_sub:
            sts.append(start(s + 1))
        for h in heads:
            head_prep(s, sts[s], h)
        if s + 1 < n_sub:
            project(s + 1, sts[s + 1])
        project_gate(s, sts[s])
        if s == 0:
            run(chunk_terms(s))
        else:
            interleave(chunk_terms(s), state_updates(s - 1), 2)
            if s >= 2:
                run(finish(s - 2, sts[s - 2]))
    conv_ref[0] = pre_scr[tm + CARRY - 3:tm + CARRY, :]
    pre_scr[0:CARRY, :] = pre_scr[tm:tm + CARRY, :]
    if n_sub >= 2:
        interleave(state_updates(n_sub - 1), finish(n_sub - 2, sts[n_sub - 2]), 1)
    else:
        run(state_updates(0))

    @pl.when(l == pl.num_programs(1) - 1)
    def _():
        ssm_ref[0] = s_scr[...]

    run(finish(n_sub - 1, sts[n_sub - 1]))


def _gdn_prompt(x, norm, win, wabt, convw, alog, dtb, onorm, wout, sample_rows, sample_state):
    b, l, d = x.shape
    tm = TM_A
    h = GDN_HEADS
    grid = (b, l // tm)
    n = sample_state.shape[0]
    bb = n // (grid[0] * grid[1])
    assert bb * grid[0] * grid[1] == n
    sample_block = lambda i, j: (i * grid[1] + j, 0, 0)
    row_spec = lambda width: pl.BlockSpec((bb, 1, width), sample_block)
    state_spec = pl.BlockSpec((bb, h, GDN_D, GDN_D), lambda i, j: (i * grid[1] + j, 0, 0, 0))
    in_specs = [
        pl.BlockSpec((1, tm, d), lambda i, j: (i, j, 0)),
        _full_spec((1, d)), _full_spec(win.shape), _full_spec(wabt.shape), _full_spec(convw.shape),
        _full_spec((1, h)), _full_spec((1, h)), _full_spec((h, 1)), _full_spec((h, 1)),
        _full_spec((1, GDN_D)), _full_spec(wout.shape),
    ] + [_full_spec(r.shape) for r in sample_rows] + [state_spec]
    out_specs = [
        pl.BlockSpec((1, tm, d), lambda i, j: (i, j, 0)),
        pl.BlockSpec((1, CONV_W - 1, QKV_W), lambda i, j: (i, 0, 0)),
        pl.BlockSpec((1, h, GDN_D, GDN_D), lambda i, j: (i, 0, 0, 0)),
        row_spec(h * GDN_D), state_spec,
    ]
    out_shape = [
        jax.ShapeDtypeStruct((b, l, d), F32),
        jax.ShapeDtypeStruct((b, CONV_W - 1, QKV_W), F32),
        jax.ShapeDtypeStruct((b, h, GDN_D, GDN_D), F32),
        jax.ShapeDtypeStruct((n, 1, h * GDN_D), F32),
        jax.ShapeDtypeStruct(sample_state.shape, F32),
    ]
    scratch = [
        pltpu.VMEM((tm + CARRY, QKV_W), F32),
        pltpu.VMEM((tm, h * GDN_D), F32),
        pltpu.VMEM((h, GDN_D, GDN_D), F32),
        pltpu.VMEM((h, tm, GDN_D), BF16),
        pltpu.VMEM((h, tm, GDN_D), BF16),
        pltpu.VMEM((h, tm, GDN_D), BF16),
        pltpu.VMEM((h, tm, GDN_D), F32),
        pltpu.VMEM((h, tm, GDN_D), BF16),
        pltpu.VMEM((h, tm, 2 * GDN_D), BF16),
        pltpu.VMEM((h, tm // 2, 2 * CHUNK), F32),
        pltpu.VMEM((h, tm // CHUNK, GDN_D), F32),
        pltpu.VMEM((h, tm, GDN_D), F32),
        pltpu.VMEM((h, tm // CHUNK, GDN_D + CHUNK, GDN_D), BF16),
        pltpu.VMEM((h, tm // CHUNK, GDN_D, GDN_D), F32),
    ]
    y, conv, ssm, o_s, state_s = pl.pallas_call(
        _gdn_prompt_kernel,
        grid=grid, in_specs=in_specs, out_specs=out_specs, out_shape=out_shape,
        scratch_shapes=scratch,
        compiler_params=pltpu.CompilerParams(
            dimension_semantics=("arbitrary", "arbitrary"), vmem_limit_bytes=VMEM_LIMIT_A),
        name="gdn_prompt",
    )(x, norm.reshape(1, d), win, wabt, convw,
      alog.reshape(1, h), dtb.reshape(1, h), alog.reshape(h, 1), dtb.reshape(h, 1),
      onorm.reshape(1, GDN_D), wout, *sample_rows, sample_state)
    return y, conv, ssm, o_s.reshape(n, h * GDN_D), state_s


def _gdn_sample_front_kernel(x_ref, norm_ref, win_ref, convw_ref,
                             c0_ref, c1_ref, c2_ref, alog_r_ref, dtb_r_ref,
                             pre_ref, q_ref, k_ref, u_ref, w_ref, qd_ref, gate_ref, eg_ref):
    x = x_ref[...]
    xn = x * lax.rsqrt(jnp.mean(x * x, axis=-1, keepdims=True) + EPS) * norm_ref[...]
    xb = xn.astype(BF16)
    gate_w = GDN_HEADS * GDN_D
    pre = _dot(xb, win_ref[:, :QKV_W])
    pre_ref[...] = pre
    gate_ref[...] = _dot(xb, win_ref[:, QKV_W:QKV_W + gate_w])
    ab = _dot(xb, win_ref[:, QKV_W + gate_w:])
    g = -jnp.exp(alog_r_ref[...]) * _softplus(ab[:, :GDN_HEADS] + dtb_r_ref[...])
    beta = jax.nn.sigmoid(ab[:, GDN_HEADS:])
    e_g = jnp.exp(g)
    eg_ref[...] = e_g
    act = _silu(((c0_ref[...] * convw_ref[0:1, :] + c1_ref[...] * convw_ref[1:2, :])
                 + c2_ref[...] * convw_ref[2:3, :]) + pre * convw_ref[3:4, :])
    w_all = GDN_HEADS * GDN_D
    for h in range(GDN_HEADS):
        cols = slice(h * GDN_D, (h + 1) * GDN_D)
        qh = act[:, h * GDN_D:(h + 1) * GDN_D]
        kh = act[:, w_all + h * GDN_D:w_all + (h + 1) * GDN_D]
        vh = act[:, 2 * w_all + h * GDN_D:2 * w_all + (h + 1) * GDN_D]
        qn = qh * lax.rsqrt(jnp.sum(qh * qh, axis=-1, keepdims=True) + EPS) * (GDN_D ** -0.5)
        kn = kh * lax.rsqrt(jnp.sum(kh * kh, axis=-1, keepdims=True) + EPS)
        b_col = beta[:, h:h + 1]
        eg_col = e_g[:, h:h + 1]
        q_ref[:, cols] = qn
        k_ref[:, cols] = kn
        u_ref[:, cols] = vh * b_col
        w_ref[:, cols] = kn * b_col * eg_col
        qd_ref[:, cols] = qn * eg_col


def _out_proj_kernel(x_ref, o_ref, gate_ref, onorm_ref, wout_ref, y_ref, *, head_w, norm_heads):
    pieces = []
    for h in range(x_ref.shape[1] // head_w):
        cols = slice(h * head_w, (h + 1) * head_w)
        o = o_ref[:, cols]
        if norm_heads:
            o = o * lax.rsqrt(jnp.mean(o * o, axis=-1, keepdims=True) + EPS) * onorm_ref[...]
        pieces.append((o * _silu(gate_ref[:, cols])).astype(BF16))
    y_ref[...] = x_ref[...] + _dot(jnp.concatenate(pieces, axis=1), wout_ref[...])


def _gdn_sample_front(x, conv_state, norm, win, convw, alog, dtb):
    n, d = x.shape
    h = GDN_HEADS
    hw = h * GDN_D
    conv_t = jnp.transpose(conv_state, (1, 0, 2))
    f = lambda shape: jax.ShapeDtypeStruct(shape, F32)
    pre, q, k, u, w, qd, gate, eg = pl.pallas_call(
        _gdn_sample_front_kernel,
        out_shape=[f((n, QKV_W)), f((n, hw)), f((n, hw)), f((n, hw)), f((n, hw)), f((n, hw)),
                   f((n, hw)), f((n, h))],
        compiler_params=pltpu.CompilerParams(vmem_limit_bytes=VMEM_LIMIT),
        name="gdn_sample_front",
    )(x, norm.reshape(1, d), win, convw, conv_t[0], conv_t[1], conv_t[2],
      alog.reshape(1, h), dtb.reshape(1, h))
    conv_new = jnp.stack([conv_t[1], conv_t[2], pre], axis=1)
    return (q, k, u, w, qd, eg), gate, conv_new


def _gdn_sample_out(x, o, gate, onorm, wout):
    return pl.pallas_call(
        functools.partial(_out_proj_kernel, head_w=GDN_D, norm_heads=True),
        out_shape=jax.ShapeDtypeStruct(x.shape, F32),
        name="gdn_sample_out",
    )(x, o, gate, onorm.reshape(1, GDN_D), wout)


def _head_rms(x, gain_row, n_heads):
    pieces = []
    for h in range(n_heads):
        xh = x[:, h * HEAD_DIM:(h + 1) * HEAD_DIM]
        pieces.append(xh * lax.rsqrt(jnp.mean(xh * xh, axis=-1, keepdims=True) + EPS) * gain_row)
    return pieces


def _pair_rms(x, gain2, left):
    sq = x * x
    lo = jnp.sum(jnp.where(left, sq, 0.0), axis=-1, keepdims=True)
    hi = jnp.sum(jnp.where(left, 0.0, sq), axis=-1, keepdims=True)
    ms = jnp.where(left, lo, hi) * (1.0 / HEAD_DIM)
    return x * lax.rsqrt(ms + EPS) * gain2


def _swa_prompt_kernel(y_ref, kvnorm_ref, wkv_ref, knorm2_ref, normb_ref, win_ref, qnorm2_ref,
                       sinks_ref, wout_ref,
                       sq_ref, skn_ref, svn_ref, skc_ref, svc_ref, ssinks_ref, sslopes_ref,
                       out_ref, kwin_ref, vwin_ref, so_ref, skwin_ref, svwin_ref,
                       k_scr, v_scr, o_scr):
    tq = y_ref.shape[1]
    w = WINDOW
    step = pl.program_id(1)
    lane = lax.broadcasted_iota(jnp.int32, (1, 2 * HEAD_DIM), 1)
    left = lane < HEAD_DIM

    @pl.when(step == 0)
    def _():
        k_scr[:, 0:w, :] = jnp.zeros((N_KV_HEADS, w, 2 * HEAD_DIM), BF16)
        v_scr[:, 0:w, :] = jnp.zeros((N_KV_HEADS, w, 2 * HEAD_DIM), BF16)

    @pl.when(step > 0)
    def _():
        k_scr[:, 0:w, :] = k_scr[:, tq:tq + w, :]
        v_scr[:, 0:w, :] = v_scr[:, tq:tq + w, :]

    _sample_attention(sq_ref, skn_ref, svn_ref, skc_ref, svc_ref, ssinks_ref, sslopes_ref,
                      so_ref, skwin_ref, svwin_ref)

    y = y_ref[0]
    yn = y * lax.rsqrt(jnp.mean(y * y, axis=-1, keepdims=True) + EPS)
    kv = _dot((yn * kvnorm_ref[...]).astype(BF16), wkv_ref[...])
    for c in range(N_KV_HEADS // 2):
        cols = slice(c * 2 * HEAD_DIM, (c + 1) * 2 * HEAD_DIM)
        kp = _pair_rms(kv[:, cols], knorm2_ref[...], left)
        vp = kv[:, KV_W + c * 2 * HEAD_DIM:KV_W + (c + 1) * 2 * HEAD_DIM]
        kwin_ref[0, :, cols] = kp[tq - w:, :]
        vwin_ref[0, :, cols] = vp[tq - w:, :]
        kr = pltpu.roll(kp, HEAD_DIM, 1)
        vr = pltpu.roll(vp, HEAD_DIM, 1)
        k_scr[2 * c, w:w + tq, :] = jnp.where(left, kp, kr).astype(BF16)
        k_scr[2 * c + 1, w:w + tq, :] = jnp.where(left, kr, kp).astype(BF16)
        v_scr[2 * c, w:w + tq, :] = jnp.where(left, vp, vr).astype(BF16)
        v_scr[2 * c + 1, w:w + tq, :] = jnp.where(left, vr, vp).astype(BF16)

    qg = _dot((yn * normb_ref[...]).astype(BF16), win_ref[...])
    q_pairs = [_pair_rms(qg[:, c * 2 * HEAD_DIM:(c + 1) * 2 * HEAD_DIM], qnorm2_ref[...], left)
               for c in range(N_Q_HEADS // 2)]

    kj = lax.broadcasted_iota(jnp.int32, (2 * w, w), 0)
    qi = lax.broadcasted_iota(jnp.int32, (2 * w, w), 1)
    dist = qi - kj + w
    band = (dist >= 0) & (dist <= w)
    j_rel = (kj - w).astype(F32)
    i_row = lax.broadcasted_iota(jnp.int32, (1, w), 1).astype(F32)

    def scores(blk, hk):
        q_rows = slice(blk * w, (blk + 1) * w)
        lhs = []
        for c in (2 * hk, 2 * hk + 1):
            qp = q_pairs[c][q_rows, :]
            lhs.append(jnp.where(left, qp, 0.0))
            lhs.append(jnp.where(left, 0.0, qp))
        return _dot_nt(k_scr[hk, blk * w:blk * w + 2 * w, :],
                       jnp.concatenate(lhs, axis=0).astype(BF16))

    def attend(blk, hk, s4):
        q_rows = slice(blk * w, (blk + 1) * w)
        valid = band & ((step > 0) | (kj >= w)) if blk == 0 else band
        ps = []
        for g in range(Q_GROUP):
            hq = hk * Q_GROUP + g
            slope = 2.0 ** (-8.0 * (hq + 1) / N_Q_HEADS)
            a = jnp.where(valid, s4[:, g * w:(g + 1) * w] + slope * j_rel, -jnp.inf)
            sink = sinks_ref[0:1, hq:hq + 1] + slope * i_row
            mx = jnp.maximum(jnp.max(a, axis=0, keepdims=True), sink)
            p = jnp.exp(a - mx)
            inv = 1.0 / (jnp.sum(p, axis=0, keepdims=True) + jnp.exp(sink - mx))
            ps.append((p * inv).astype(BF16))
        o4 = _dot_tn(jnp.concatenate(ps, axis=1),
                     v_scr[hk, blk * w:blk * w + 2 * w, :])
        for j, c in enumerate((2 * hk, 2 * hk + 1)):
            o_scr[q_rows, c * 2 * HEAD_DIM:(c + 1) * 2 * HEAD_DIM] = jnp.where(
                left, o4[(2 * j) * w:(2 * j + 1) * w, :], o4[(2 * j + 1) * w:(2 * j + 2) * w, :])

    items = [(blk, hk) for blk in range(tq // w) for hk in range(N_KV_HEADS)]
    s_next = scores(*items[0])
    for n, item in enumerate(items):
        s_cur = s_next
        if n + 1 < len(items):
            s_next = scores(*items[n + 1])
        attend(*item, s_cur)

    o = (o_scr[...] * _silu(qg[:, ATT_W:])).astype(BF16)
    out_ref[0] = y + _dot(o, wout_ref[...])


def _swa_prompt(y, kvnorm, wkv, knorm, normb, win, qnorm, sinks, wout, sample_q, sample_kn, sample_vn,
                cache_k, cache_v):
    b, l, d = y.shape
    tq = TQ_B
    w = WINDOW
    steps = l // tq
    n = cache_k.shape[0]
    bb = n // (b * steps)
    assert bb * b * steps == n
    f = lambda shape: jax.ShapeDtypeStruct(shape, F32)
    knorm2 = jnp.concatenate([knorm, knorm]).reshape(1, 2 * HEAD_DIM)
    qnorm2 = (jnp.concatenate([qnorm, qnorm]) * (HEAD_DIM ** -0.5)).reshape(1, 2 * HEAD_DIM)
    slopes = (2.0 ** (-8.0 * jnp.arange(1, N_Q_HEADS + 1, dtype=F32) / N_Q_HEADS)).reshape(N_Q_HEADS, 1)
    spec3 = lambda r, c: pl.BlockSpec((bb, r, c), lambda i, j: (i * steps + j, 0, 0))
    in_specs = [
        pl.BlockSpec((1, tq, d), lambda i, j: (i, j, 0)),
        _full_spec((1, d)), _full_spec(wkv.shape), _full_spec((1, 2 * HEAD_DIM)), _full_spec((1, d)),
        _full_spec(win.shape), _full_spec((1, 2 * HEAD_DIM)), _full_spec((1, N_Q_HEADS)),
        _full_spec(wout.shape),
        spec3(N_Q_HEADS, HEAD_DIM), spec3(1, KV_W), spec3(1, KV_W), spec3(w, KV_W), spec3(w, KV_W),
        _full_spec((N_Q_HEADS, 1)), _full_spec((N_Q_HEADS, 1)),
    ]
    out_specs = [
        pl.BlockSpec((1, tq, d), lambda i, j: (i, j, 0)),
        pl.BlockSpec((1, w, KV_W), lambda i, j: (i, 0, 0)),
        pl.BlockSpec((1, w, KV_W), lambda i, j: (i, 0, 0)),
        spec3(N_Q_HEADS, HEAD_DIM), spec3(w, KV_W), spec3(w, KV_W),
    ]
    out, kwin, vwin, o_s, kwin_s, vwin_s = pl.pallas_call(
        _swa_prompt_kernel,
        grid=(b, steps), in_specs=in_specs, out_specs=out_specs,
        out_shape=[f((b, l, d)), f((b, w, KV_W)), f((b, w, KV_W)),
                   f((n, N_Q_HEADS, HEAD_DIM)), f((n, w, KV_W)), f((n, w, KV_W))],
        scratch_shapes=[pltpu.VMEM((N_KV_HEADS, w + tq, 2 * HEAD_DIM), BF16),
                        pltpu.VMEM((N_KV_HEADS, w + tq, 2 * HEAD_DIM), BF16),
                        pltpu.VMEM((tq, ATT_W), F32)],
        compiler_params=pltpu.CompilerParams(
            dimension_semantics=("arbitrary", "arbitrary"), vmem_limit_bytes=VMEM_LIMIT),
        name="swa_prompt",
    )(y, kvnorm.reshape(1, d), wkv, knorm2, normb.reshape(1, d), win, qnorm2,
      sinks.reshape(1, N_Q_HEADS), wout,
      sample_q.reshape(n, N_Q_HEADS, HEAD_DIM), sample_kn.reshape(n, 1, KV_W), sample_vn.reshape(n, 1, KV_W),
      cache_k, cache_v, sinks.reshape(N_Q_HEADS, 1), slopes)
    return out, kwin, vwin, o_s.reshape(n, ATT_W), kwin_s, vwin_s


def _swa_sample_front_kernel(y_ref, kvnorm_ref, wkv_ref, knorm_ref, normb_ref, win_ref, qnorm_ref,
                             k_ref, v_ref, q_ref, gate_ref):
    y = y_ref[...]
    yn = y * lax.rsqrt(jnp.mean(y * y, axis=-1, keepdims=True) + EPS)
    kv = _dot((yn * kvnorm_ref[...]).astype(BF16), wkv_ref[...])
    k_ref[...] = jnp.concatenate(_head_rms(kv[:, :KV_W], knorm_ref[...], N_KV_HEADS), axis=1)
    v_ref[...] = kv[:, KV_W:]
    qg = _dot((yn * normb_ref[...]).astype(BF16), win_ref[...])
    q_ref[...] = jnp.concatenate(
        _head_rms(qg[:, :ATT_W], qnorm_ref[...] * (HEAD_DIM ** -0.5), N_Q_HEADS), axis=1)
    gate_ref[...] = qg[:, ATT_W:]


def _sample_attention(q_ref, kn_ref, vn_ref, kc_ref, vc_ref, sinks_ref, slopes_ref, o_ref, kwin_ref, vwin_ref):
    bb = kc_ref.shape[0]
    w = kc_ref.shape[1]
    hrow = lax.broadcasted_iota(jnp.int32, (N_Q_HEADS, KV_W), 0) // Q_GROUP
    lblk = lax.broadcasted_iota(jnp.int32, (N_Q_HEADS, KV_W), 1) // HEAD_DIM
    own = hrow == lblk
    dist_c = (w - lax.broadcasted_iota(jnp.int32, (1, w), 1)).astype(F32)
    krow = lax.broadcasted_iota(jnp.int32, (w, KV_W), 0)
    slopes = slopes_ref[...]
    sink = sinks_ref[...]
    qms, scs = [], []
    for b in range(bb):
        q = q_ref[b]
        qm = jnp.where(own, jnp.concatenate([q] * N_KV_HEADS, axis=1), 0.0).astype(BF16)
        qms.append(qm)
        scs.append(_dot_nt(qm, kc_ref[b].astype(BF16)))
    pcs, pns = [], []
    for b in range(bb):
        s_c = scs[b] - slopes * dist_c
        s_n = jnp.sum(qms[b].astype(F32) * kn_ref[b].astype(BF16).astype(F32), axis=-1, keepdims=True)
        mx = jnp.maximum(jnp.maximum(jnp.max(s_c, axis=-1, keepdims=True), s_n), sink)
        p_c = jnp.exp(s_c - mx)
        p_n = jnp.exp(s_n - mx)
        den = jnp.sum(p_c, axis=-1, keepdims=True) + p_n + jnp.exp(sink - mx)
        pcs.append((p_c / den).astype(BF16))
        pns.append(p_n / den)
    rs = [_dot(pcs[b], vc_ref[b].astype(BF16)) for b in range(bb)]
    for b in range(bb):
        kn = kn_ref[b]
        vn = vn_ref[b]
        r = rs[b] + pns[b].astype(BF16).astype(F32) * vn.astype(BF16).astype(F32)
        r = jnp.where(own, r, 0.0)
        acc = r[:, 0:HEAD_DIM]
        for blk in range(1, N_KV_HEADS):
            acc = acc + r[:, blk * HEAD_DIM:(blk + 1) * HEAD_DIM]
        o_ref[b] = acc
        kwin_ref[b] = jnp.where(krow == w - 1, kn, pltpu.roll(kc_ref[b], w - 1, 0))
        vwin_ref[b] = jnp.where(krow == w - 1, vn, pltpu.roll(vc_ref[b], w - 1, 0))


def _swa_sample_front(y, kvnorm, wkv, knorm, normb, win, qnorm):
    n, d = y.shape
    f = lambda shape: jax.ShapeDtypeStruct(shape, F32)
    return pl.pallas_call(
        _swa_sample_front_kernel,
        out_shape=[f((n, KV_W)), f((n, KV_W)), f((n, ATT_W)), f((n, ATT_W))],
        compiler_params=pltpu.CompilerParams(vmem_limit_bytes=VMEM_LIMIT),
        name="swa_sample_front",
    )(y, kvnorm.reshape(1, d), wkv, knorm.reshape(1, HEAD_DIM), normb.reshape(1, d), win,
      qnorm.reshape(1, HEAD_DIM))


def _swa_sample_out(y, o, gate, wout):
    return pl.pallas_call(
        functools.partial(_out_proj_kernel, head_w=ATT_W, norm_heads=False),
        out_shape=jax.ShapeDtypeStruct(y.shape, F32),
        name="swa_sample_out",
    )(y, o, gate, jnp.ones((1, ATT_W), F32), wout)


def kernel(x_prompt, x_sample, state_conv, state_ssm, cache_k_win, cache_v_win, norm_a, w_in_a, conv_w_a, a_log, dt_bias, o_norm_a, w_out_a, kv_norm, w_kv, k_norm, norm_b, w_in_b, q_norm, sinks, w_out_b):
    n_a = w_in_a.shape[0]
    n_b = w_in_b.shape[0]
    assert n_a == 1 and n_b == 1, "kernel is written for DEPTH == 2"
    bp, lp, d = x_prompt.shape
    n = x_sample.shape[0]
    hw = GDN_HEADS * GDN_D

    hp, hs = x_prompt, x_sample.reshape(n, d)
    conv_p, ssm_p, conv_s, ssm_s = [], [], [], []
    for layer in range(n_a):
        win_a = w_in_a[layer].astype(BF16)
        wabt = win_a[:, QKV_W + hw:].T
        wout = w_out_a[layer].astype(BF16)
        rows_s, gate_s, cbuf_s = _gdn_sample_front(hs, state_conv[layer], norm_a[layer], win_a,
                                                   conv_w_a[layer], a_log[layer], dt_bias[layer])
        hp, cbuf, st, o_s, st_s = _gdn_prompt(hp, norm_a[layer], win_a, wabt, conv_w_a[layer],
                                              a_log[layer], dt_bias[layer], o_norm_a[layer], wout,
                                              rows_s, state_ssm[layer])
        conv_p.append(cbuf)
        ssm_p.append(st)
        hs = _gdn_sample_out(hs, o_s, gate_s, o_norm_a[layer], wout)
        conv_s.append(cbuf_s)
        ssm_s.append(st_s)

    wkv = w_kv.astype(BF16)
    win = w_in_b[0].astype(BF16)
    woutb = w_out_b[0].astype(BF16)
    kn_s, vn_s, q_s, gate_s = _swa_sample_front(hs, kv_norm, wkv, k_norm, norm_b[0], win, q_norm[0])
    hp, k_win_p, v_win_p, o_s, k_win_s, v_win_s = _swa_prompt(
        hp, kv_norm, wkv, k_norm, norm_b[0], win, q_norm[0], sinks[0], woutb, q_s, kn_s, vn_s,
        cache_k_win.reshape(n, WINDOW, KV_W), cache_v_win.reshape(n, WINDOW, KV_W))
    hs = _swa_sample_out(hs, o_s, gate_s, woutb)
    kv_shape = (N_KV_HEADS, HEAD_DIM)
    return (hp, hs.reshape(n, 1, d), jnp.stack(conv_p), jnp.stack(ssm_p),
            k_win_p.reshape(bp, WINDOW, *kv_shape), v_win_p.reshape(bp, WINDOW, *kv_shape),
            jnp.stack(conv_s), jnp.stack(ssm_s),
            k_win_s.reshape(n, WINDOW, *kv_shape), v_win_s.reshape(n, WINDOW, *kv_shape))
```

```python
import functools

import jax
import jax.numpy as jnp
from jax import lax
from jax.experimental import pallas as pl
from jax.experimental.pallas import tpu as pltpu

F32 = jnp.float32
BF16 = jnp.bfloat16
EPS = 1e-6

D_MODEL = 1024
GDN_HEADS = 8
GDN_D = 128
QKV_W = 3 * GDN_HEADS * GDN_D
CONV_W = 4
CHUNK = 64
N_Q_HEADS = 16
N_KV_HEADS = 4
Q_GROUP = N_Q_HEADS // N_KV_HEADS
HEAD_DIM = 64
KV_W = N_KV_HEADS * HEAD_DIM
ATT_W = N_Q_HEADS * HEAD_DIM
WINDOW = 128

TM_A = 512
SUB_A = 256
TQ_B = 1024
CARRY = 8
VMEM_LIMIT = 52 * 1024 * 1024
VMEM_LIMIT_A = 62 * 1024 * 1024

_NT = (((1,), (1,)), ((), ()))
_TN = (((0,), (0,)), ((), ()))


def _dot(a, b):
    return jnp.dot(a, b, preferred_element_type=F32)


def _dot_nt(a, b):
    return lax.dot_general(a, b, _NT, preferred_element_type=F32)


def _dot_tn(a, b):
    return lax.dot_general(a, b, _TN, preferred_element_type=F32)


def _split(x):
    hi = x.astype(BF16)
    lo = (x - hi.astype(F32)).astype(BF16)
    return hi, lo


def _dot_exact_lhs(a_bf, b):
    b0 = b.astype(BF16)
    r1 = b - b0.astype(F32)
    b1 = r1.astype(BF16)
    b2 = (r1 - b1.astype(F32)).astype(BF16)
    return (_dot(a_bf, b2) + _dot(a_bf, b1)) + _dot(a_bf, b0)


def _dot_exact_rhs(a, b_bf):
    a0 = a.astype(BF16)
    r1 = a - a0.astype(F32)
    a1 = r1.astype(BF16)
    a2 = (r1 - a1.astype(F32)).astype(BF16)
    return (_dot(a2, b_bf) + _dot(a1, b_bf)) + _dot(a0, b_bf)


def _silu(x):
    return x * jax.nn.sigmoid(x)


def _softplus(x):
    return jnp.maximum(x, 0.0) + jnp.log1p(jnp.exp(-jnp.abs(x)))


def _halves(x, left):
    zero = jnp.zeros_like(x)
    return jnp.where(left, x, zero), jnp.where(left, zero, x)


def _block_diag(x, left):
    return jnp.concatenate(_halves(x, left), axis=0)


def _pair_split_product(a, b, left):
    ah, al = _split(a)
    bh, bl = _split(b)
    bh1, bh2 = _halves(bh, left)
    bl1, bl2 = _halves(bl, left)
    zero = jnp.zeros_like(bh)
    rhs = jnp.concatenate([jnp.concatenate([bh1, bl1], axis=1), jnp.concatenate([bh2, bl2], axis=1),
                           jnp.concatenate([bh1, zero], axis=1), jnp.concatenate([bh2, zero], axis=1)], axis=0)
    y = _dot(jnp.concatenate([ah, al], axis=1), rhs)
    half = y.shape[1] // 2
    return y[:, :half] + y[:, half:]


def _unit_lower_inverse_pairs(ms, left):
    n = ms[0].shape[0]
    row = lax.broadcasted_iota(jnp.int32, (n, 2 * n), 0)
    col = lax.broadcasted_iota(jnp.int32, (n, 2 * n), 1) % n
    eye2 = jnp.where(row == col, 1.0, 0.0).astype(F32)

    def times(a, b):
        return _dot(a.astype(BF16), _block_diag(b.astype(BF16), left))

    ts = [eye2 - m for m in ms]
    ps = [times(m, m) for m in ms]
    yield
    steps = max(1, (n - 1).bit_length()) - 1
    for _ in range(steps - 1):
        both = [_dot(jnp.concatenate([t.astype(BF16), p.astype(BF16)], axis=0), _block_diag(p.astype(BF16), left))
                for t, p in zip(ts, ps)]
        ts = [t + tp[:n, :] for t, tp in zip(ts, both)]
        ps = [tp[n:, :] for tp in both]
        yield
    ts = [t + times(t, p) for t, p in zip(ts, ps)]
    yield
    rs = [(eye2 - t) - _pair_split_product(m, t, left) for m, t in zip(ms, ts)]
    yield
    ts = [t + times(t, r) for t, r in zip(ts, rs)]
    yield
    return ts


def _full_spec(shape):
    nd = len(shape)
    return pl.BlockSpec(shape, lambda *_: (0,) * nd, pipeline_mode=pl.Buffered(1))


def _gdn_gates(ab, ab_t, alog_r, dtb_r, alog_c, dtb_c, tm, chunk):
    h = GDN_HEADS
    g_c = -jnp.exp(alog_r) * _softplus(ab[:, :h] + dtb_r)
    beta = jax.nn.sigmoid(ab[:, h:])
    g_r = -jnp.exp(alog_c) * _softplus(ab_t[:h, :] + dtb_c)
    row = lax.broadcasted_iota(jnp.int32, (tm, tm), 0)
    col = lax.broadcasted_iota(jnp.int32, (tm, tm), 1)
    same = (row // chunk) == (col // chunk)
    lower = jnp.where(same & (row >= col), 1.0, 0.0).astype(BF16)
    upper = jnp.where(same & (row <= col), 1.0, 0.0).astype(BF16)
    gc = _dot_exact_lhs(lower, g_c)
    gr = _dot_exact_rhs(g_r, upper)
    return beta, gc, gr


def _sample_state_update(row0, q_ref, k_ref, u_ref, w_ref, qd_ref, eg_ref, s_ref, o_ref, s_out_ref):
    bb = s_ref.shape[0]
    group = 8
    assert group % bb == 0
    base = pl.multiple_of((row0 // group) * group, group)
    part = (row0 - base) // bb

    def rows(ref, cols):
        x = ref[pl.ds(base, group), cols]
        out = x[0:bb, :]
        for p in range(1, group // bb):
            out = jnp.where(part == p, x[p * bb:(p + 1) * bb, :], out)
        return out

    row = lax.broadcasted_iota(jnp.int32, (8, GDN_D), 0)
    decay_all = rows(eg_ref, slice(None))
    prods, ins = [], []
    for h in range(GDN_HEADS):
        cols = slice(h * GDN_D, (h + 1) * GDN_D)
        q, k, u, w, qd = (rows(r, cols) for r in (q_ref, k_ref, u_ref, w_ref, qd_ref))
        for b in range(bb):
            lhs = jnp.where(row == 0, w[b:b + 1, :], jnp.where(row == 1, qd[b:b + 1, :], 0.0))
            prods.append(_dot(lhs.astype(BF16), s_ref[b, h].astype(BF16)))
            ins.append((b, h, cols, q[b:b + 1, :], k[b:b + 1, :], u[b:b + 1, :]))
    for (b, h, cols, q, k, u), prod in zip(ins, prods):
        q = q.astype(BF16).astype(F32)
        k = k.astype(BF16).astype(F32)
        v_new = u - prod[0:1, :]
        vb = v_new.astype(BF16).astype(F32)
        qk = jnp.sum(q * k, axis=-1, keepdims=True)
        o_ref[b, :, cols] = prod[1:2, :] + qk.astype(BF16).astype(F32) * vb
        k8 = jnp.where(row == 0, k, 0.0).astype(BF16)
        v8 = jnp.where(row == 0, vb, 0.0).astype(BF16)
        s_out_ref[b, h] = s_ref[b, h] * decay_all[b:b + 1, h:h + 1] + _dot_tn(k8, v8)


def _gdn_prompt_kernel(x_ref, norm_ref, win_ref, wabt_ref, convw_ref,
                       alog_r_ref, dtb_r_ref, alog_c_ref, dtb_c_ref, onorm_ref, wout_ref,
                       sq_ref, sk_ref, su_ref, sw_ref, sqd_ref, seg_ref, sstate_ref,
                       y_ref, conv_ref, ssm_ref, so_ref, sstate_out_ref,
                       pre_scr, gate_scr, s_scr, k_scr, kb_scr, q_scr, qd_scr, kd_scr,
                       rhs_scr, dec_scr, gt_scr, o_scr, lhs_scr, su_scr):
    tm = x_ref.shape[1]
    sub = SUB_A
    n_sub = tm // sub
    n_chunks = sub // CHUNK
    gate_w = GDN_HEADS * GDN_D
    l = pl.program_id(1)

    @pl.when(l == 0)
    def _():
        pre_scr[0:CARRY, :] = jnp.zeros((CARRY, QKV_W), F32)
        s_scr[...] = jnp.zeros(s_scr.shape, F32)

    ci = lax.broadcasted_iota(jnp.int32, (CHUNK, 2 * CHUNK), 0)
    cj = lax.broadcasted_iota(jnp.int32, (CHUNK, 2 * CHUNK), 1) % CHUNK
    causal = ci >= cj
    strict = ci > cj
    left = lax.broadcasted_iota(jnp.int32, (1, 2 * CHUNK), 1) < CHUNK
    heads = list(range(GDN_HEADS))

    sample_row0 = (pl.program_id(0) * pl.num_programs(1) + l) * sstate_ref.shape[0]
    _sample_state_update(sample_row0, sq_ref, sk_ref, su_ref, sw_ref, sqd_ref, seg_ref, sstate_ref,
                         so_ref, sstate_out_ref)

    def start(s):
        x = x_ref[0, s * sub:(s + 1) * sub, :]
        xn = x * lax.rsqrt(jnp.mean(x * x, axis=-1, keepdims=True) + EPS) * norm_ref[...]
        xb = xn.astype(BF16)
        ab = _dot(xb, win_ref[:, QKV_W + gate_w:])
        ab_t = _dot_nt(wabt_ref[...], xb)
        beta, gc, gr = _gdn_gates(ab, ab_t, alog_r_ref[...], dtb_r_ref[...],
                                  alog_c_ref[...], dtb_c_ref[...], sub, CHUNK)
        g_last = jnp.concatenate(
            [jnp.broadcast_to(gc[c * CHUNK + CHUNK - 1:c * CHUNK + CHUNK, :], (CHUNK, GDN_HEADS))
             for c in range(n_chunks)], axis=0)
        return dict(x=x, xb=xb, beta=beta, gc=gc, gr=gr, e_g=jnp.exp(gc),
                    e_kd=jnp.exp(g_last - gc), e_tot=jnp.exp(g_last))

    def project(s, st):
        for part in range(3):
            cols = slice(part * gate_w, (part + 1) * gate_w)
            pre_scr[CARRY + s * sub:CARRY + (s + 1) * sub, cols] = _dot(st["xb"], win_ref[:, cols])

    def project_gate(s, st):
        gate_scr[s * sub:(s + 1) * sub, :] = _dot(st["xb"], win_ref[:, QKV_W:QKV_W + gate_w])

    def conv_act(s, j):
        cols = slice(j * GDN_D, (j + 1) * GDN_D)
        ext = pre_scr[s * sub:s * sub + CARRY + sub, cols]
        half = ext[CARRY:, :] * (0.5 * convw_ref[CONV_W - 1:CONV_W, cols])
        for back in range(1, CONV_W):
            tap = CONV_W - 1 - back
            half = half + pltpu.roll(ext, back, 0)[CARRY:, :] * (0.5 * convw_ref[tap:tap + 1, cols])
        return half + half * jnp.tanh(half)

    def head_prep(s, st, h):
        rows = slice(s * sub, (s + 1) * sub)
        qh = conv_act(s, h)
        kh = conv_act(s, GDN_HEADS + h)
        vh = conv_act(s, 2 * GDN_HEADS + h)
        qn = qh * lax.rsqrt(jnp.sum(qh * qh, axis=-1, keepdims=True) + EPS) * (GDN_D ** -0.5)
        kn = kh * lax.rsqrt(jnp.sum(kh * kh, axis=-1, keepdims=True) + EPS)
        b_col = st["beta"][:, h:h + 1]
        eg_col = st["e_g"][:, h:h + 1]
        kb = kn * b_col
        k_scr[h, rows, :] = kn.astype(BF16)
        kb_scr[h, rows, :] = kb.astype(BF16)
        q_scr[h, rows, :] = qn.astype(BF16)
        qd_scr[h, rows, :] = qn * eg_col
        kd_scr[h, rows, :] = (kn * st["e_kd"][:, h:h + 1]).astype(BF16)
        rhs_scr[h, rows, 0:GDN_D] = (vh * b_col).astype(BF16)
        rhs_scr[h, rows, GDN_D:2 * GDN_D] = (kb * eg_col).astype(BF16)
        gc, gr = st["gc"], st["gr"]
        for c in range(n_chunks):
            gt_scr[h, s * n_chunks + c:s * n_chunks + c + 1, :] = jnp.broadcast_to(
                st["e_tot"][c * CHUNK:c * CHUNK + 1, h:h + 1], (1, GDN_D))
        for j in range(n_chunks // 2):
            r1 = slice(2 * j * CHUNK, (2 * j + 1) * CHUNK)
            r2 = slice((2 * j + 1) * CHUNK, (2 * j + 2) * CHUNK)
            diff = jnp.where(left, gc[r1, h:h + 1], gc[r2, h:h + 1]) - gr[h:h + 1, 2 * j * CHUNK:(2 * j + 2) * CHUNK]
            pair = s * (n_chunks // 2) + j
            dec_scr[h, pair * CHUNK:(pair + 1) * CHUNK, :] = jnp.exp(jnp.where(causal, diff, -jnp.inf))

    def chunk_terms(s):
        items = [(h, s * (n_chunks // 2) + j) for h in heads for j in range(n_chunks // 2)]
        aqs = []
        for h, pair in items:
            r12 = slice(2 * pair * CHUNK, (2 * pair + 2) * CHUNK)
            aqs.append(_dot_nt(jnp.concatenate([kb_scr[h, r12, :], q_scr[h, r12, :]], axis=0), k_scr[h, r12, :]))
        yield
        decs = [dec_scr[h, pair * CHUNK:(pair + 1) * CHUNK, :] for h, pair in items]
        ms = [jnp.where(strict, jnp.where(left, aq[:CHUNK, :], aq[CHUNK:2 * CHUNK, :]) * d, 0.0)
              for aq, d in zip(aqs, decs)]
        qks = [jnp.where(causal, jnp.where(left, aq[2 * CHUNK:3 * CHUNK, :], aq[3 * CHUNK:, :]) * d, 0.0)
               for aq, d in zip(aqs, decs)]
        ts = yield from _unit_lower_inverse_pairs(ms, left)
        sols = []
        for (h, pair), t, qk in zip(items, ts, qks):
            lo = t - t.astype(BF16).astype(F32)
            t_parts = (jnp.where(left, t, pltpu.roll(lo, CHUNK, 1)).astype(BF16),
                       jnp.where(left, pltpu.roll(t, CHUNK, 1), lo).astype(BF16))
            qk_parts = (qk[:, :CHUNK].astype(BF16), pltpu.roll(qk, CHUNK, 1)[:, :CHUNK].astype(BF16))
            for half in range(2):
                c = 2 * pair + half
                r = slice(c * CHUNK, (c + 1) * CHUNK)
                rhs = rhs_scr[h, r, :]
                sols.append((h, c, r, qk_parts[half],
                             _dot(t_parts[half], jnp.concatenate([rhs, rhs], axis=0)).astype(BF16)))
        yield
        outs = [(h, c, r, _dot_tn(kd_scr[h, r, :], sol),
                 _dot(qk_c, sol))
                for h, c, r, qk_c, sol in sols]
        for h, c, r, kd_uw, qk_uw in outs:
            lhs_scr[h, c, 0:GDN_D, :] = kd_uw[:, GDN_D:].astype(BF16)
            lhs_scr[h, c, GDN_D:GDN_D + CHUNK, :] = (qd_scr[h, r, :] - qk_uw[:, GDN_D:]).astype(BF16)
            su_scr[h, c] = kd_uw[:, :GDN_D]
            o_scr[h, r, :] = qk_uw[:, :GDN_D]
        yield

    def state_updates(s):
        for c in range(s * n_chunks, (s + 1) * n_chunks):
            rows = slice(c * CHUNK, (c + 1) * CHUNK)
            for h in heads:
                st_h = s_scr[h]
                prod = _dot(lhs_scr[h, c], st_h.astype(BF16))
                o_scr[h, rows, :] = o_scr[h, rows, :] + prod[GDN_D:, :]
                s_scr[h] = st_h * gt_scr[h, c:c + 1, :] + (su_scr[h, c] - prod[:GDN_D, :])
            yield

    def finish(s, st):
        rows = slice(s * sub, (s + 1) * sub)
        pieces = []
        for h in heads:
            o = o_scr[h, rows, :]
            on = o * lax.rsqrt(jnp.mean(o * o, axis=-1, keepdims=True) + EPS) * onorm_ref[...]
            pieces.append((on * _silu(gate_scr[rows, h * GDN_D:(h + 1) * GDN_D])).astype(BF16))
        on_all = jnp.concatenate(pieces, axis=1)
        for piece in range(D_MODEL // (2 * GDN_D)):
            cols = slice(piece * 2 * GDN_D, (piece + 1) * 2 * GDN_D)
            y_ref[0, rows, cols] = st["x"][:, cols] + _dot(on_all, wout_ref[:, cols])
            yield

    def run(gen):
        for _ in gen:
            pass

    def interleave(main, side, every):
        for n, _ in enumerate(main):
            if (n + 1) % every == 0:
                next(side, None)
        run(side)

    sts = [start(0)]
    project(0, sts[0])
    for s in range(n_sub):
        if s + 1 < n_sub:
            sts.append(start(s + 1))
        for h in heads:
            head_prep(s, sts[s], h)
        if s + 1 < n_sub:
            project(s + 1, sts[s + 1])
        project_gate(s, sts[s])
        if s == 0:
            run(chunk_terms(s))
        else:
            interleave(chunk_terms(s), state_updates(s - 1), 2)
            if s >= 2:
                run(finish(s - 2, sts[s - 2]))
    conv_ref[0] = pre_scr[tm + CARRY - 3:tm + CARRY, :]
    pre_scr[0:CARRY, :] = pre_scr[tm:tm + CARRY, :]
    if n_sub >= 2:
        interleave(state_updates(n_sub - 1), finish(n_sub - 2, sts[n_sub - 2]), 1)
    else:
        run(state_updates(0))

    @pl.when(l == pl.num_programs(1) - 1)
    def _():
        ssm_ref[0] = s_scr[...]

    run(finish(n_sub - 1, sts[n_sub - 1]))


def _gdn_prompt(x, norm, win, wabt, convw, alog, dtb, onorm, wout, sample_rows, sample_state):
    b, l, d = x.shape
    tm = TM_A
    h = GDN_HEADS
    grid = (b, l // tm)
    n = sample_state.shape[0]
    bb = n // (grid[0] * grid[1])
    assert bb * grid[0] * grid[1] == n
    sample_block = lambda i, j: (i * grid[1] + j, 0, 0)
    row_spec = lambda width: pl.BlockSpec((bb, 1, width), sample_block)
    state_spec = pl.BlockSpec((bb, h, GDN_D, GDN_D), lambda i, j: (i * grid[1] + j, 0, 0, 0))
    in_specs = [
        pl.BlockSpec((1, tm, d), lambda i, j: (i, j, 0)),
        _full_spec((1, d)), _full_spec(win.shape), _full_spec(wabt.shape), _full_spec(convw.shape),
        _full_spec((1, h)), _full_spec((1, h)), _full_spec((h, 1)), _full_spec((h, 1)),
        _full_spec((1, GDN_D)), _full_spec(wout.shape),
    ] + [_full_spec(r.shape) for r in sample_rows] + [state_spec]
    out_specs = [
        pl.BlockSpec((1, tm, d), lambda i, j: (i, j, 0)),
        pl.BlockSpec((1, CONV_W - 1, QKV_W), lambda i, j: (i, 0, 0)),
        pl.BlockSpec((1, h, GDN_D, GDN_D), lambda i, j: (i, 0, 0, 0)),
        row_spec(h * GDN_D), state_spec,
    ]
    out_shape = [
        jax.ShapeDtypeStruct((b, l, d), F32),
        jax.ShapeDtypeStruct((b, CONV_W - 1, QKV_W), F32),
        jax.ShapeDtypeStruct((b, h, GDN_D, GDN_D), F32),
        jax.ShapeDtypeStruct((n, 1, h * GDN_D), F32),
        jax.ShapeDtypeStruct(sample_state.shape, F32),
    ]
    scratch = [
        pltpu.VMEM((tm + CARRY, QKV_W), F32),
        pltpu.VMEM((tm, h * GDN_D), F32),
        pltpu.VMEM((h, GDN_D, GDN_D), F32),
        pltpu.VMEM((h, tm, GDN_D), BF16),
        pltpu.VMEM((h, tm, GDN_D), BF16),
        pltpu.VMEM((h, tm, GDN_D), BF16),
        pltpu.VMEM((h, tm, GDN_D), F32),
        pltpu.VMEM((h, tm, GDN_D), BF16),
        pltpu.VMEM((h, tm, 2 * GDN_D), BF16),
        pltpu.VMEM((h, tm // 2, 2 * CHUNK), F32),
        pltpu.VMEM((h, tm // CHUNK, GDN_D), F32),
        pltpu.VMEM((h, tm, GDN_D), F32),
        pltpu.VMEM((h, tm // CHUNK, GDN_D + CHUNK, GDN_D), BF16),
        pltpu.VMEM((h, tm // CHUNK, GDN_D, GDN_D), F32),
    ]
    y, conv, ssm, o_s, state_s = pl.pallas_call(
        _gdn_prompt_kernel,
        grid=grid, in_specs=in_specs, out_specs=out_specs, out_shape=out_shape,
        scratch_shapes=scratch,
        compiler_params=pltpu.CompilerParams(
            dimension_semantics=("arbitrary", "arbitrary"), vmem_limit_bytes=VMEM_LIMIT_A),
        name="gdn_prompt",
    )(x, norm.reshape(1, d), win, wabt, convw,
      alog.reshape(1, h), dtb.reshape(1, h), alog.reshape(h, 1), dtb.reshape(h, 1),
      onorm.reshape(1, GDN_D), wout, *sample_rows, sample_state)
    return y, conv, ssm, o_s.reshape(n, h * GDN_D), state_s


def _gdn_sample_front_kernel(x_ref, norm_ref, win32_ref, wout32_ref, convw_ref,
                             c0_ref, c1_ref, c2_ref, alog_r_ref, dtb_r_ref,
                             pre_ref, q_ref, k_ref, u_ref, w_ref, qd_ref, gate_ref, eg_ref, win_ref, wout_ref):
    win_ref[...] = win32_ref[...].astype(BF16)
    wout_ref[...] = wout32_ref[...].astype(BF16)
    x = x_ref[...]
    xn = x * lax.rsqrt(jnp.mean(x * x, axis=-1, keepdims=True) + EPS) * norm_ref[...]
    xb = xn.astype(BF16)
    gate_w = GDN_HEADS * GDN_D
    pre = _dot(xb, win_ref[:, :QKV_W])
    pre_ref[...] = pre
    gate_ref[...] = _dot(xb, win_ref[:, QKV_W:QKV_W + gate_w])
    ab = _dot(xb, win_ref[:, QKV_W + gate_w:])
    g = -jnp.exp(alog_r_ref[...]) * _softplus(ab[:, :GDN_HEADS] + dtb_r_ref[...])
    beta = jax.nn.sigmoid(ab[:, GDN_HEADS:])
    e_g = jnp.exp(g)
    eg_ref[...] = e_g
    act = _silu(((c0_ref[...] * convw_ref[0:1, :] + c1_ref[...] * convw_ref[1:2, :])
                 + c2_ref[...] * convw_ref[2:3, :]) + pre * convw_ref[3:4, :])
    w_all = GDN_HEADS * GDN_D
    for h in range(GDN_HEADS):
        cols = slice(h * GDN_D, (h + 1) * GDN_D)
        qh = act[:, h * GDN_D:(h + 1) * GDN_D]
        kh = act[:, w_all + h * GDN_D:w_all + (h + 1) * GDN_D]
        vh = act[:, 2 * w_all + h * GDN_D:2 * w_all + (h + 1) * GDN_D]
        qn = qh * lax.rsqrt(jnp.sum(qh * qh, axis=-1, keepdims=True) + EPS) * (GDN_D ** -0.5)
        kn = kh * lax.rsqrt(jnp.sum(kh * kh, axis=-1, keepdims=True) + EPS)
        b_col = beta[:, h:h + 1]
        eg_col = e_g[:, h:h + 1]
        q_ref[:, cols] = qn
        k_ref[:, cols] = kn
        u_ref[:, cols] = vh * b_col
        w_ref[:, cols] = kn * b_col * eg_col
        qd_ref[:, cols] = qn * eg_col


def _out_proj_kernel(x_ref, o_ref, gate_ref, onorm_ref, wout_ref, y_ref, *, head_w, norm_heads):
    pieces = []
    for h in range(x_ref.shape[1] // head_w):
        cols = slice(h * head_w, (h + 1) * head_w)
        o = o_ref[:, cols]
        if norm_heads:
            o = o * lax.rsqrt(jnp.mean(o * o, axis=-1, keepdims=True) + EPS) * onorm_ref[...]
        pieces.append((o * _silu(gate_ref[:, cols])).astype(BF16))
    y_ref[...] = x_ref[...] + _dot(jnp.concatenate(pieces, axis=1), wout_ref[...])


def _gdn_sample_front(x, conv_state, norm, win32, wout32, convw, alog, dtb):
    n, d = x.shape
    h = GDN_HEADS
    hw = h * GDN_D
    conv_t = jnp.transpose(conv_state, (1, 0, 2))
    f = lambda shape: jax.ShapeDtypeStruct(shape, F32)
    pre, q, k, u, w, qd, gate, eg, win, wout = pl.pallas_call(
        _gdn_sample_front_kernel,
        out_shape=[f((n, QKV_W)), f((n, hw)), f((n, hw)), f((n, hw)), f((n, hw)), f((n, hw)),
                   f((n, hw)), f((n, h)),
                   jax.ShapeDtypeStruct(win32.shape, BF16), jax.ShapeDtypeStruct(wout32.shape, BF16)],
        compiler_params=pltpu.CompilerParams(vmem_limit_bytes=VMEM_LIMIT),
        name="gdn_sample_front",
    )(x, norm.reshape(1, d), win32, wout32, convw, conv_t[0], conv_t[1], conv_t[2],
      alog.reshape(1, h), dtb.reshape(1, h))
    conv_new = jnp.stack([conv_t[1], conv_t[2], pre], axis=1)
    return (q, k, u, w, qd, eg), gate, conv_new, win, wout


def _gdn_sample_out(x, o, gate, onorm, wout):
    return pl.pallas_call(
        functools.partial(_out_proj_kernel, head_w=GDN_D, norm_heads=True),
        out_shape=jax.ShapeDtypeStruct(x.shape, F32),
        name="gdn_sample_out",
    )(x, o, gate, onorm.reshape(1, GDN_D), wout)


def _head_rms(x, gain_row, n_heads):
    pieces = []
    for h in range(n_heads):
        xh = x[:, h * HEAD_DIM:(h + 1) * HEAD_DIM]
        pieces.append(xh * lax.rsqrt(jnp.mean(xh * xh, axis=-1, keepdims=True) + EPS) * gain_row)
    return pieces


def _pair_rms(x, gain2, left):
    sq = x * x
    lo = jnp.sum(jnp.where(left, sq, 0.0), axis=-1, keepdims=True)
    hi = jnp.sum(jnp.where(left, 0.0, sq), axis=-1, keepdims=True)
    ms = jnp.where(left, lo, hi) * (1.0 / HEAD_DIM)
    return x * lax.rsqrt(ms + EPS) * gain2


def _swa_prompt_kernel(y_ref, kvnorm_ref, wkv_ref, knorm2_ref, normb_ref, win_ref, qnorm2_ref,
                       sinks_ref, wout_ref,
                       sq_ref, skn_ref, svn_ref, skc_ref, svc_ref, ssinks_ref, sslopes_ref,
                       out_ref, kwin_ref, vwin_ref, so_ref, skwin_ref, svwin_ref,
                       k_scr, v_scr, o_scr):
    tq = y_ref.shape[1]
    w = WINDOW
    step = pl.program_id(1)
    lane = lax.broadcasted_iota(jnp.int32, (1, 2 * HEAD_DIM), 1)
    left = lane < HEAD_DIM

    @pl.when(step == 0)
    def _():
        k_scr[:, 0:w, :] = jnp.zeros((N_KV_HEADS, w, 2 * HEAD_DIM), BF16)
        v_scr[:, 0:w, :] = jnp.zeros((N_KV_HEADS, w, 2 * HEAD_DIM), BF16)

    @pl.when(step > 0)
    def _():
        k_scr[:, 0:w, :] = k_scr[:, tq:tq + w, :]
        v_scr[:, 0:w, :] = v_scr[:, tq:tq + w, :]

    _sample_attention(sq_ref, skn_ref, svn_ref, skc_ref, svc_ref, ssinks_ref, sslopes_ref,
                      so_ref, skwin_ref, svwin_ref)

    y = y_ref[0]
    yn = y * lax.rsqrt(jnp.mean(y * y, axis=-1, keepdims=True) + EPS)
    kv = _dot((yn * kvnorm_ref[...]).astype(BF16), wkv_ref[...])
    for c in range(N_KV_HEADS // 2):
        cols = slice(c * 2 * HEAD_DIM, (c + 1) * 2 * HEAD_DIM)
        kp = _pair_rms(kv[:, cols], knorm2_ref[...], left)
        vp = kv[:, KV_W + c * 2 * HEAD_DIM:KV_W + (c + 1) * 2 * HEAD_DIM]
        kwin_ref[0, :, cols] = kp[tq - w:, :]
        vwin_ref[0, :, cols] = vp[tq - w:, :]
        kr = pltpu.roll(kp, HEAD_DIM, 1)
        vr = pltpu.roll(vp, HEAD_DIM, 1)
        k_scr[2 * c, w:w + tq, :] = jnp.where(left, kp, kr).astype(BF16)
        k_scr[2 * c + 1, w:w + tq, :] = jnp.where(left, kr, kp).astype(BF16)
        v_scr[2 * c, w:w + tq, :] = jnp.where(left, vp, vr).astype(BF16)
        v_scr[2 * c + 1, w:w + tq, :] = jnp.where(left, vr, vp).astype(BF16)

    qg = _dot((yn * normb_ref[...]).astype(BF16), win_ref[...])
    q_pairs = [_pair_rms(qg[:, c * 2 * HEAD_DIM:(c + 1) * 2 * HEAD_DIM], qnorm2_ref[...], left)
               for c in range(N_Q_HEADS // 2)]

    kj = lax.broadcasted_iota(jnp.int32, (2 * w, w), 0)
    qi = lax.broadcasted_iota(jnp.int32, (2 * w, w), 1)
    dist = qi - kj + w
    band = (dist >= 0) & (dist <= w)
    j_rel = (kj - w).astype(F32)
    i_row = lax.broadcasted_iota(jnp.int32, (1, w), 1).astype(F32)

    def scores(blk, hk):
        q_rows = slice(blk * w, (blk + 1) * w)
        lhs = []
        for c in (2 * hk, 2 * hk + 1):
            qp = q_pairs[c][q_rows, :]
            lhs.append(jnp.where(left, qp, 0.0))
            lhs.append(jnp.where(left, 0.0, qp))
        return _dot_nt(k_scr[hk, blk * w:blk * w + 2 * w, :],
                       jnp.concatenate(lhs, axis=0).astype(BF16))

    def attend(blk, hk, s4):
        q_rows = slice(blk * w, (blk + 1) * w)
        valid = band & ((step > 0) | (kj >= w)) if blk == 0 else band
        ps = []
        for g in range(Q_GROUP):
            hq = hk * Q_GROUP + g
            slope = 2.0 ** (-8.0 * (hq + 1) / N_Q_HEADS)
            a = jnp.where(valid, s4[:, g * w:(g + 1) * w] + slope * j_rel, -jnp.inf)
            sink = sinks_ref[0:1, hq:hq + 1] + slope * i_row
            mx = jnp.maximum(jnp.max(a, axis=0, keepdims=True), sink)
            p = jnp.exp(a - mx)
            inv = 1.0 / (jnp.sum(p, axis=0, keepdims=True) + jnp.exp(sink - mx))
            ps.append((p * inv).astype(BF16))
        o4 = _dot_tn(jnp.concatenate(ps, axis=1),
                     v_scr[hk, blk * w:blk * w + 2 * w, :])
        for j, c in enumerate((2 * hk, 2 * hk + 1)):
            o_scr[q_rows, c * 2 * HEAD_DIM:(c + 1) * 2 * HEAD_DIM] = jnp.where(
                left, o4[(2 * j) * w:(2 * j + 1) * w, :], o4[(2 * j + 1) * w:(2 * j + 2) * w, :])

    items = [(blk, hk) for blk in range(tq // w) for hk in range(N_KV_HEADS)]
    s_next = scores(*items[0])
    for n, item in enumerate(items):
        s_cur = s_next
        if n + 1 < len(items):
            s_next = scores(*items[n + 1])
        attend(*item, s_cur)

    o = (o_scr[...] * _silu(qg[:, ATT_W:])).astype(BF16)
    out_ref[0] = y + _dot(o, wout_ref[...])


def _swa_prompt(y, kvnorm, wkv, knorm, normb, win, qnorm, sinks, wout, sample_q, sample_kn, sample_vn,
                cache_k, cache_v):
    b, l, d = y.shape
    tq = TQ_B
    w = WINDOW
    steps = l // tq
    n = cache_k.shape[0]
    bb = n // (b * steps)
    assert bb * b * steps == n
    f = lambda shape: jax.ShapeDtypeStruct(shape, F32)
    knorm2 = jnp.concatenate([knorm, knorm]).reshape(1, 2 * HEAD_DIM)
    qnorm2 = (jnp.concatenate([qnorm, qnorm]) * (HEAD_DIM ** -0.5)).reshape(1, 2 * HEAD_DIM)
    slopes = (2.0 ** (-8.0 * jnp.arange(1, N_Q_HEADS + 1, dtype=F32) / N_Q_HEADS)).reshape(N_Q_HEADS, 1)
    spec3 = lambda r, c: pl.BlockSpec((bb, r, c), lambda i, j: (i * steps + j, 0, 0))
    in_specs = [
        pl.BlockSpec((1, tq, d), lambda i, j: (i, j, 0)),
        _full_spec((1, d)), _full_spec(wkv.shape), _full_spec((1, 2 * HEAD_DIM)), _full_spec((1, d)),
        _full_spec(win.shape), _full_spec((1, 2 * HEAD_DIM)), _full_spec((1, N_Q_HEADS)),
        _full_spec(wout.shape),
        spec3(N_Q_HEADS, HEAD_DIM), spec3(1, KV_W), spec3(1, KV_W), spec3(w, KV_W), spec3(w, KV_W),
        _full_spec((N_Q_HEADS, 1)), _full_spec((N_Q_HEADS, 1)),
    ]
    out_specs = [
        pl.BlockSpec((1, tq, d), lambda i, j: (i, j, 0)),
        pl.BlockSpec((1, w, KV_W), lambda i, j: (i, 0, 0)),
        pl.BlockSpec((1, w, KV_W), lambda i, j: (i, 0, 0)),
        spec3(N_Q_HEADS, HEAD_DIM), spec3(w, KV_W), spec3(w, KV_W),
    ]
    out, kwin, vwin, o_s, kwin_s, vwin_s = pl.pallas_call(
        _swa_prompt_kernel,
        grid=(b, steps), in_specs=in_specs, out_specs=out_specs,
        out_shape=[f((b, l, d)), f((b, w, KV_W)), f((b, w, KV_W)),
                   f((n, N_Q_HEADS, HEAD_DIM)), f((n, w, KV_W)), f((n, w, KV_W))],
        scratch_shapes=[pltpu.VMEM((N_KV_HEADS, w + tq, 2 * HEAD_DIM), BF16),
                        pltpu.VMEM((N_KV_HEADS, w + tq, 2 * HEAD_DIM), BF16),
                        pltpu.VMEM((tq, ATT_W), F32)],
        compiler_params=pltpu.CompilerParams(
            dimension_semantics=("arbitrary", "arbitrary"), vmem_limit_bytes=VMEM_LIMIT),
        name="swa_prompt",
    )(y, kvnorm.reshape(1, d), wkv, knorm2, normb.reshape(1, d), win, qnorm2,
      sinks.reshape(1, N_Q_HEADS), wout,
      sample_q.reshape(n, N_Q_HEADS, HEAD_DIM), sample_kn.reshape(n, 1, KV_W), sample_vn.reshape(n, 1, KV_W),
      cache_k, cache_v, sinks.reshape(N_Q_HEADS, 1), slopes)
    return out, kwin, vwin, o_s.reshape(n, ATT_W), kwin_s, vwin_s


def _swa_sample_front_kernel(y_ref, kvnorm_ref, wkv32_ref, knorm_ref, normb_ref, win32_ref, qnorm_ref,
                             wout32_ref, k_ref, v_ref, q_ref, gate_ref, wkv_ref, win_ref, wout_ref):
    wkv_ref[...] = wkv32_ref[...].astype(BF16)
    win_ref[...] = win32_ref[...].astype(BF16)
    wout_ref[...] = wout32_ref[...].astype(BF16)
    y = y_ref[...]
    yn = y * lax.rsqrt(jnp.mean(y * y, axis=-1, keepdims=True) + EPS)
    kv = _dot((yn * kvnorm_ref[...]).astype(BF16), wkv_ref[...])
    k_ref[...] = jnp.concatenate(_head_rms(kv[:, :KV_W], knorm_ref[...], N_KV_HEADS), axis=1)
    v_ref[...] = kv[:, KV_W:]
    qg = _dot((yn * normb_ref[...]).astype(BF16), win_ref[...])
    q_ref[...] = jnp.concatenate(
        _head_rms(qg[:, :ATT_W], qnorm_ref[...] * (HEAD_DIM ** -0.5), N_Q_HEADS), axis=1)
    gate_ref[...] = qg[:, ATT_W:]


def _sample_attention(q_ref, kn_ref, vn_ref, kc_ref, vc_ref, sinks_ref, slopes_ref, o_ref, kwin_ref, vwin_ref):
    bb = kc_ref.shape[0]
    w = kc_ref.shape[1]
    hrow = lax.broadcasted_iota(jnp.int32, (N_Q_HEADS, KV_W), 0) // Q_GROUP
    lblk = lax.broadcasted_iota(jnp.int32, (N_Q_HEADS, KV_W), 1) // HEAD_DIM
    own = hrow == lblk
    dist_c = (w - lax.broadcasted_iota(jnp.int32, (1, w), 1)).astype(F32)
    krow = lax.broadcasted_iota(jnp.int32, (w, KV_W), 0)
    slopes = slopes_ref[...]
    sink = sinks_ref[...]
    qms, scs = [], []
    for b in range(bb):
        q = q_ref[b]
        qm = jnp.where(own, jnp.concatenate([q] * N_KV_HEADS, axis=1), 0.0).astype(BF16)
        qms.append(qm)
        scs.append(_dot_nt(qm, kc_ref[b].astype(BF16)))
    pcs, pns = [], []
    for b in range(bb):
        s_c = scs[b] - slopes * dist_c
        s_n = jnp.sum(qms[b].astype(F32) * kn_ref[b].astype(BF16).astype(F32), axis=-1, keepdims=True)
        mx = jnp.maximum(jnp.maximum(jnp.max(s_c, axis=-1, keepdims=True), s_n), sink)
        p_c = jnp.exp(s_c - mx)
        p_n = jnp.exp(s_n - mx)
        den = jnp.sum(p_c, axis=-1, keepdims=True) + p_n + jnp.exp(sink - mx)
        pcs.append((p_c / den).astype(BF16))
        pns.append(p_n / den)
    rs = [_dot(pcs[b], vc_ref[b].astype(BF16)) for b in range(bb)]
    for b in range(bb):
        kn = kn_ref[b]
        vn = vn_ref[b]
        r = rs[b] + pns[b].astype(BF16).astype(F32) * vn.astype(BF16).astype(F32)
        r = jnp.where(own, r, 0.0)
        acc = r[:, 0:HEAD_DIM]
        for blk in range(1, N_KV_HEADS):
            acc = acc + r[:, blk * HEAD_DIM:(blk + 1) * HEAD_DIM]
        o_ref[b] = acc
        kwin_ref[b] = jnp.where(krow == w - 1, kn, pltpu.roll(kc_ref[b], w - 1, 0))
        vwin_ref[b] = jnp.where(krow == w - 1, vn, pltpu.roll(vc_ref[b], w - 1, 0))


def _swa_sample_front(y, kvnorm, wkv32, knorm, normb, win32, qnorm, wout32):
    n, d = y.shape
    f = lambda shape: jax.ShapeDtypeStruct(shape, F32)
    bf = lambda a: jax.ShapeDtypeStruct(a.shape, BF16)
    return pl.pallas_call(
        _swa_sample_front_kernel,
        out_shape=[f((n, KV_W)), f((n, KV_W)), f((n, ATT_W)), f((n, ATT_W)), bf(wkv32), bf(win32), bf(wout32)],
        compiler_params=pltpu.CompilerParams(vmem_limit_bytes=VMEM_LIMIT),
        name="swa_sample_front",
    )(y, kvnorm.reshape(1, d), wkv32, knorm.reshape(1, HEAD_DIM), normb.reshape(1, d), win32,
      qnorm.reshape(1, HEAD_DIM), wout32)


def _swa_sample_out(y, o, gate, wout):
    return pl.pallas_call(
        functools.partial(_out_proj_kernel, head_w=ATT_W, norm_heads=False),
        out_shape=jax.ShapeDtypeStruct(y.shape, F32),
        name="swa_sample_out",
    )(y, o, gate, jnp.ones((1, ATT_W), F32), wout)


def kernel(x_prompt, x_sample, state_conv, state_ssm, cache_k_win, cache_v_win, norm_a, w_in_a, conv_w_a, a_log, dt_bias, o_norm_a, w_out_a, kv_norm, w_kv, k_norm, norm_b, w_in_b, q_norm, sinks, w_out_b):
    n_a = w_in_a.shape[0]
    n_b = w_in_b.shape[0]
    assert n_a == 1 and n_b == 1, "kernel is written for DEPTH == 2"
    bp, lp, d = x_prompt.shape
    n = x_sample.shape[0]
    hw = GDN_HEADS * GDN_D

    hp, hs = x_prompt, x_sample.reshape(n, d)
    conv_p, ssm_p, conv_s, ssm_s = [], [], [], []
    for layer in range(n_a):
        rows_s, gate_s, cbuf_s, win_a, wout = _gdn_sample_front(
            hs, state_conv[layer], norm_a[layer], w_in_a[layer], w_out_a[layer], conv_w_a[layer],
            a_log[layer], dt_bias[layer])
        wabt = win_a[:, QKV_W + hw:].T
        hp, cbuf, st, o_s, st_s = _gdn_prompt(hp, norm_a[layer], win_a, wabt, conv_w_a[layer],
                                              a_log[layer], dt_bias[layer], o_norm_a[layer], wout,
                                              rows_s, state_ssm[layer])
        conv_p.append(cbuf)
        ssm_p.append(st)
        hs = _gdn_sample_out(hs, o_s, gate_s, o_norm_a[layer], wout)
        conv_s.append(cbuf_s)
        ssm_s.append(st_s)

    kn_s, vn_s, q_s, gate_s, wkv, win, woutb = _swa_sample_front(
        hs, kv_norm, w_kv, k_norm, norm_b[0], w_in_b[0], q_norm[0], w_out_b[0])
    hp, k_win_p, v_win_p, o_s, k_win_s, v_win_s = _swa_prompt(
        hp, kv_norm, wkv, k_norm, norm_b[0], win, q_norm[0], sinks[0], woutb, q_s, kn_s, vn_s,
        cache_k_win.reshape(n, WINDOW, KV_W), cache_v_win.reshape(n, WINDOW, KV_W))
    hs = _swa_sample_out(hs, o_s, gate_s, woutb)
    kv_shape = (N_KV_HEADS, HEAD_DIM)
    return (hp, hs.reshape(n, 1, d), jnp.stack(conv_p), jnp.stack(ssm_p),
            k_win_p.reshape(bp, WINDOW, *kv_shape), v_win_p.reshape(bp, WINDOW, *kv_shape),
            jnp.stack(conv_s), jnp.stack(ssm_s),
            k_win_s.reshape(n, WINDOW, *kv_shape), v_win_s.reshape(n, WINDOW, *kv_shape))
```

```python
import functools

import jax
import jax.numpy as jnp
from jax import lax
from jax.experimental import pallas as pl
from jax.experimental.pallas import tpu as pltpu

F32 = jnp.float32
BF16 = jnp.bfloat16
EPS = 1e-6

D_MODEL = 1024
GDN_HEADS = 8
GDN_D = 128
QKV_W = 3 * GDN_HEADS * GDN_D
CONV_W = 4
CHUNK = 64
N_Q_HEADS = 16
N_KV_HEADS = 4
Q_GROUP = N_Q_HEADS // N_KV_HEADS
HEAD_DIM = 64
KV_W = N_KV_HEADS * HEAD_DIM
ATT_W = N_Q_HEADS * HEAD_DIM
WINDOW = 128

TM_A = 512
SUB_A = 256
TQ_B = 1024
SUB_B = 256
CARRY = 8
VMEM_LIMIT = 52 * 1024 * 1024
VMEM_LIMIT_A = 62 * 1024 * 1024

_NT = (((1,), (1,)), ((), ()))
_TN = (((0,), (0,)), ((), ()))


def _dot(a, b):
    return jnp.dot(a, b, preferred_element_type=F32)


def _dot_nt(a, b):
    return lax.dot_general(a, b, _NT, preferred_element_type=F32)


def _dot_tn(a, b):
    return lax.dot_general(a, b, _TN, preferred_element_type=F32)


def _split(x):
    hi = x.astype(BF16)
    lo = (x - hi.astype(F32)).astype(BF16)
    return hi, lo


def _dot_exact_lhs(a_bf, b):
    b0 = b.astype(BF16)
    r1 = b - b0.astype(F32)
    b1 = r1.astype(BF16)
    b2 = (r1 - b1.astype(F32)).astype(BF16)
    return (_dot(a_bf, b2) + _dot(a_bf, b1)) + _dot(a_bf, b0)


def _dot_exact_rhs(a, b_bf):
    a0 = a.astype(BF16)
    r1 = a - a0.astype(F32)
    a1 = r1.astype(BF16)
    a2 = (r1 - a1.astype(F32)).astype(BF16)
    return (_dot(a2, b_bf) + _dot(a1, b_bf)) + _dot(a0, b_bf)


def _silu(x):
    return x * jax.nn.sigmoid(x)


def _softplus(x):
    return jnp.maximum(x, 0.0) + jnp.log1p(jnp.exp(-jnp.abs(x)))


def _halves(x, left):
    zero = jnp.zeros_like(x)
    return jnp.where(left, x, zero), jnp.where(left, zero, x)


def _block_diag(x, left):
    return jnp.concatenate(_halves(x, left), axis=0)


def _pair_split_product(a, b, left):
    ah, al = _split(a)
    bh, bl = _split(b)
    bh1, bh2 = _halves(bh, left)
    bl1, bl2 = _halves(bl, left)
    zero = jnp.zeros_like(bh)
    rhs = jnp.concatenate([jnp.concatenate([bh1, bl1], axis=1), jnp.concatenate([bh2, bl2], axis=1),
                           jnp.concatenate([bh1, zero], axis=1), jnp.concatenate([bh2, zero], axis=1)], axis=0)
    y = _dot(jnp.concatenate([ah, al], axis=1), rhs)
    half = y.shape[1] // 2
    return y[:, :half] + y[:, half:]


def _unit_lower_inverse_pairs(ms, left):
    n = ms[0].shape[0]
    row = lax.broadcasted_iota(jnp.int32, (n, 2 * n), 0)
    col = lax.broadcasted_iota(jnp.int32, (n, 2 * n), 1) % n
    eye2 = jnp.where(row == col, 1.0, 0.0).astype(F32)

    def times(a, b):
        return _dot(a.astype(BF16), _block_diag(b.astype(BF16), left))

    ts = [eye2 - m for m in ms]
    ps = [times(m, m) for m in ms]
    yield
    steps = max(1, (n - 1).bit_length()) - 1
    for _ in range(steps - 1):
        both = [_dot(jnp.concatenate([t.astype(BF16), p.astype(BF16)], axis=0), _block_diag(p.astype(BF16), left))
                for t, p in zip(ts, ps)]
        ts = [t + tp[:n, :] for t, tp in zip(ts, both)]
        ps = [tp[n:, :] for tp in both]
        yield
    ts = [t + times(t, p) for t, p in zip(ts, ps)]
    yield
    rs = [(eye2 - t) - _pair_split_product(m, t, left) for m, t in zip(ms, ts)]
    yield
    ts = [t + times(t, r) for t, r in zip(ts, rs)]
    yield
    return ts


def _full_spec(shape):
    nd = len(shape)
    return pl.BlockSpec(shape, lambda *_: (0,) * nd, pipeline_mode=pl.Buffered(1))


def _gdn_gates(ab, ab_t, alog_r, dtb_r, alog_c, dtb_c, tm, chunk):
    h = GDN_HEADS
    g_c = -jnp.exp(alog_r) * _softplus(ab[:, :h] + dtb_r)
    beta = jax.nn.sigmoid(ab[:, h:])
    g_r = -jnp.exp(alog_c) * _softplus(ab_t[:h, :] + dtb_c)
    row = lax.broadcasted_iota(jnp.int32, (tm, tm), 0)
    col = lax.broadcasted_iota(jnp.int32, (tm, tm), 1)
    same = (row // chunk) == (col // chunk)
    lower = jnp.where(same & (row >= col), 1.0, 0.0).astype(BF16)
    upper = jnp.where(same & (row <= col), 1.0, 0.0).astype(BF16)
    gc = _dot_exact_lhs(lower, g_c)
    gr = _dot_exact_rhs(g_r, upper)
    return beta, gc, gr


def _sample_state_update(row0, q_ref, k_ref, u_ref, w_ref, qd_ref, eg_ref, s_ref, o_ref, s_out_ref):
    bb = s_ref.shape[0]
    group = 8
    assert group % bb == 0
    base = pl.multiple_of((row0 // group) * group, group)
    part = (row0 - base) // bb

    def rows(ref, cols):
        x = ref[pl.ds(base, group), cols]
        out = x[0:bb, :]
        for p in range(1, group // bb):
            out = jnp.where(part == p, x[p * bb:(p + 1) * bb, :], out)
        return out

    row = lax.broadcasted_iota(jnp.int32, (8, GDN_D), 0)
    decay_all = rows(eg_ref, slice(None))
    prods, ins = [], []
    for h in range(GDN_HEADS):
        cols = slice(h * GDN_D, (h + 1) * GDN_D)
        q, k, u, w, qd = (rows(r, cols) for r in (q_ref, k_ref, u_ref, w_ref, qd_ref))
        for b in range(bb):
            lhs = jnp.where(row == 0, w[b:b + 1, :], jnp.where(row == 1, qd[b:b + 1, :], 0.0))
            prods.append(_dot(lhs.astype(BF16), s_ref[b, h].astype(BF16)))
            ins.append((b, h, cols, q[b:b + 1, :], k[b:b + 1, :], u[b:b + 1, :]))
    for (b, h, cols, q, k, u), prod in zip(ins, prods):
        q = q.astype(BF16).astype(F32)
        k = k.astype(BF16).astype(F32)
        v_new = u - prod[0:1, :]
        vb = v_new.astype(BF16).astype(F32)
        qk = jnp.sum(q * k, axis=-1, keepdims=True)
        o_ref[b, :, cols] = prod[1:2, :] + qk.astype(BF16).astype(F32) * vb
        k8 = jnp.where(row == 0, k, 0.0).astype(BF16)
        v8 = jnp.where(row == 0, vb, 0.0).astype(BF16)
        s_out_ref[b, h] = s_ref[b, h] * decay_all[b:b + 1, h:h + 1] + _dot_tn(k8, v8)


def _gdn_prompt_kernel(x_ref, norm_ref, win_ref, wabt_ref, convw_ref,
                       alog_r_ref, dtb_r_ref, alog_c_ref, dtb_c_ref, onorm_ref, wout_ref,
                       sq_ref, sk_ref, su_ref, sw_ref, sqd_ref, seg_ref, sstate_ref,
                       y_ref, conv_ref, ssm_ref, so_ref, sstate_out_ref,
                       pre_scr, gate_scr, s_scr, k_scr, kb_scr, q_scr, qd_scr, kd_scr,
                       rhs_scr, dec_scr, gt_scr, o_scr, lhs_scr, su_scr):
    tm = x_ref.shape[1]
    sub = SUB_A
    n_sub = tm // sub
    n_chunks = sub // CHUNK
    gate_w = GDN_HEADS * GDN_D
    l = pl.program_id(1)

    @pl.when(l == 0)
    def _():
        pre_scr[0:CARRY, :] = jnp.zeros((CARRY, QKV_W), F32)
        s_scr[...] = jnp.zeros(s_scr.shape, F32)

    ci = lax.broadcasted_iota(jnp.int32, (CHUNK, 2 * CHUNK), 0)
    cj = lax.broadcasted_iota(jnp.int32, (CHUNK, 2 * CHUNK), 1) % CHUNK
    causal = ci >= cj
    strict = ci > cj
    left = lax.broadcasted_iota(jnp.int32, (1, 2 * CHUNK), 1) < CHUNK
    heads = list(range(GDN_HEADS))

    sample_row0 = (pl.program_id(0) * pl.num_programs(1) + l) * sstate_ref.shape[0]
    _sample_state_update(sample_row0, sq_ref, sk_ref, su_ref, sw_ref, sqd_ref, seg_ref, sstate_ref,
                         so_ref, sstate_out_ref)

    def start(s):
        x = x_ref[0, s * sub:(s + 1) * sub, :]
        xn = x * lax.rsqrt(jnp.mean(x * x, axis=-1, keepdims=True) + EPS) * norm_ref[...]
        xb = xn.astype(BF16)
        ab = _dot(xb, win_ref[:, QKV_W + gate_w:])
        ab_t = _dot_nt(wabt_ref[...], xb)
        beta, gc, gr = _gdn_gates(ab, ab_t, alog_r_ref[...], dtb_r_ref[...],
                                  alog_c_ref[...], dtb_c_ref[...], sub, CHUNK)
        g_last = jnp.concatenate(
            [jnp.broadcast_to(gc[c * CHUNK + CHUNK - 1:c * CHUNK + CHUNK, :], (CHUNK, GDN_HEADS))
             for c in range(n_chunks)], axis=0)
        return dict(x=x, xb=xb, beta=beta, gc=gc, gr=gr, e_g=jnp.exp(gc),
                    e_kd=jnp.exp(g_last - gc), e_tot=jnp.exp(g_last))

    def project(s, st):
        for part in range(3):
            cols = slice(part * gate_w, (part + 1) * gate_w)
            pre_scr[CARRY + s * sub:CARRY + (s + 1) * sub, cols] = _dot(st["xb"], win_ref[:, cols])

    def project_gate(s, st):
        gate_scr[s * sub:(s + 1) * sub, :] = _dot(st["xb"], win_ref[:, QKV_W:QKV_W + gate_w])

    def conv_act(s, j):
        cols = slice(j * GDN_D, (j + 1) * GDN_D)
        ext = pre_scr[s * sub:s * sub + CARRY + sub, cols]
        half = ext[CARRY:, :] * (0.5 * convw_ref[CONV_W - 1:CONV_W, cols])
        for back in range(1, CONV_W):
            tap = CONV_W - 1 - back
            half = half + pltpu.roll(ext, back, 0)[CARRY:, :] * (0.5 * convw_ref[tap:tap + 1, cols])
        return half + half * jnp.tanh(half)

    def head_prep(s, st, h):
        rows = slice(s * sub, (s + 1) * sub)
        qh = conv_act(s, h)
        kh = conv_act(s, GDN_HEADS + h)
        vh = conv_act(s, 2 * GDN_HEADS + h)
        qn = qh * lax.rsqrt(jnp.sum(qh * qh, axis=-1, keepdims=True) + EPS) * (GDN_D ** -0.5)
        kn = kh * lax.rsqrt(jnp.sum(kh * kh, axis=-1, keepdims=True) + EPS)
        b_col = st["beta"][:, h:h + 1]
        eg_col = st["e_g"][:, h:h + 1]
        kb = kn * b_col
        k_scr[h, rows, :] = kn.astype(BF16)
        kb_scr[h, rows, :] = kb.astype(BF16)
        q_scr[h, rows, :] = qn.astype(BF16)
        qd_scr[h, rows, :] = qn * eg_col
        kd_scr[h, rows, :] = (kn * st["e_kd"][:, h:h + 1]).astype(BF16)
        rhs_scr[h, rows, 0:GDN_D] = (vh * b_col).astype(BF16)
        rhs_scr[h, rows, GDN_D:2 * GDN_D] = (kb * eg_col).astype(BF16)
        gc, gr = st["gc"], st["gr"]
        for c in range(n_chunks):
            gt_scr[h, s * n_chunks + c:s * n_chunks + c + 1, :] = jnp.broadcast_to(
                st["e_tot"][c * CHUNK:c * CHUNK + 1, h:h + 1], (1, GDN_D))
        for j in range(n_chunks // 2):
            r1 = slice(2 * j * CHUNK, (2 * j + 1) * CHUNK)
            r2 = slice((2 * j + 1) * CHUNK, (2 * j + 2) * CHUNK)
            diff = jnp.where(left, gc[r1, h:h + 1], gc[r2, h:h + 1]) - gr[h:h + 1, 2 * j * CHUNK:(2 * j + 2) * CHUNK]
            pair = s * (n_chunks // 2) + j
            dec_scr[h, pair * CHUNK:(pair + 1) * CHUNK, :] = jnp.exp(jnp.where(causal, diff, -jnp.inf))

    def chunk_terms(s):
        items = [(h, s * (n_chunks // 2) + j) for h in heads for j in range(n_chunks // 2)]
        aqs = []
        for h, pair in items:
            r12 = slice(2 * pair * CHUNK, (2 * pair + 2) * CHUNK)
            aqs.append(_dot_nt(jnp.concatenate([kb_scr[h, r12, :], q_scr[h, r12, :]], axis=0), k_scr[h, r12, :]))
        yield
        decs = [dec_scr[h, pair * CHUNK:(pair + 1) * CHUNK, :] for h, pair in items]
        ms = [jnp.where(strict, jnp.where(left, aq[:CHUNK, :], aq[CHUNK:2 * CHUNK, :]) * d, 0.0)
              for aq, d in zip(aqs, decs)]
        qks = [jnp.where(causal, jnp.where(left, aq[2 * CHUNK:3 * CHUNK, :], aq[3 * CHUNK:, :]) * d, 0.0)
               for aq, d in zip(aqs, decs)]
        ts = yield from _unit_lower_inverse_pairs(ms, left)
        sols = []
        for (h, pair), t, qk in zip(items, ts, qks):
            lo = t - t.astype(BF16).astype(F32)
            t_parts = (jnp.where(left, t, pltpu.roll(lo, CHUNK, 1)).astype(BF16),
                       jnp.where(left, pltpu.roll(t, CHUNK, 1), lo).astype(BF16))
            qk_parts = (qk[:, :CHUNK].astype(BF16), pltpu.roll(qk, CHUNK, 1)[:, :CHUNK].astype(BF16))
            for half in range(2):
                c = 2 * pair + half
                r = slice(c * CHUNK, (c + 1) * CHUNK)
                rhs = rhs_scr[h, r, :]
                sols.append((h, c, r, qk_parts[half],
                             _dot(t_parts[half], jnp.concatenate([rhs, rhs], axis=0)).astype(BF16)))
        yield
        outs = [(h, c, r, _dot_tn(kd_scr[h, r, :], sol),
                 _dot(qk_c, sol))
                for h, c, r, qk_c, sol in sols]
        for h, c, r, kd_uw, qk_uw in outs:
            lhs_scr[h, c, 0:GDN_D, :] = kd_uw[:, GDN_D:].astype(BF16)
            lhs_scr[h, c, GDN_D:GDN_D + CHUNK, :] = (qd_scr[h, r, :] - qk_uw[:, GDN_D:]).astype(BF16)
            su_scr[h, c] = kd_uw[:, :GDN_D]
            o_scr[h, r, :] = qk_uw[:, :GDN_D]
        yield

    def state_updates(s):
        for c in range(s * n_chunks, (s + 1) * n_chunks):
            rows = slice(c * CHUNK, (c + 1) * CHUNK)
            for h in heads:
                st_h = s_scr[h]
                prod = _dot(lhs_scr[h, c], st_h.astype(BF16))
                o_scr[h, rows, :] = o_scr[h, rows, :] + prod[GDN_D:, :]
                s_scr[h] = st_h * gt_scr[h, c:c + 1, :] + (su_scr[h, c] - prod[:GDN_D, :])
            yield

    def finish(s, st):
        rows = slice(s * sub, (s + 1) * sub)
        pieces = []
        for h in heads:
            o = o_scr[h, rows, :]
            on = o * lax.rsqrt(jnp.mean(o * o, axis=-1, keepdims=True) + EPS) * onorm_ref[...]
            pieces.append((on * _silu(gate_scr[rows, h * GDN_D:(h + 1) * GDN_D])).astype(BF16))
        on_all = jnp.concatenate(pieces, axis=1)
        for piece in range(D_MODEL // (2 * GDN_D)):
            cols = slice(piece * 2 * GDN_D, (piece + 1) * 2 * GDN_D)
            y_ref[0, rows, cols] = st["x"][:, cols] + _dot(on_all, wout_ref[:, cols])
            yield

    def run(gen):
        for _ in gen:
            pass

    def interleave(main, side, every):
        for n, _ in enumerate(main):
            if (n + 1) % every == 0:
                next(side, None)
        run(side)

    sts = [start(0)]
    project(0, sts[0])
    for s in range(n_sub):
        if s + 1 < n_sub:
            sts.append(start(s + 1))
        for h in heads:
            head_prep(s, sts[s], h)
        if s + 1 < n_sub:
            project(s + 1, sts[s + 1])
        project_gate(s, sts[s])
        if s == 0:
            run(chunk_terms(s))
        else:
            interleave(chunk_terms(s), state_updates(s - 1), 2)
            if s >= 2:
                run(finish(s - 2, sts[s - 2]))
    conv_ref[0] = pre_scr[tm + CARRY - 3:tm + CARRY, :]
    pre_scr[0:CARRY, :] = pre_scr[tm:tm + CARRY, :]
    if n_sub >= 2:
        interleave(state_updates(n_sub - 1), finish(n_sub - 2, sts[n_sub - 2]), 1)
    else:
        run(state_updates(0))

    @pl.when(l == pl.num_programs(1) - 1)
    def _():
        ssm_ref[0] = s_scr[...]

    run(finish(n_sub - 1, sts[n_sub - 1]))


def _gdn_prompt(x, norm, win, wabt, convw, alog, dtb, onorm, wout, sample_rows, sample_state):
    b, l, d = x.shape
    tm = TM_A
    h = GDN_HEADS
    grid = (b, l // tm)
    n = sample_state.shape[0]
    bb = n // (grid[0] * grid[1])
    assert bb * grid[0] * grid[1] == n
    sample_block = lambda i, j: (i * grid[1] + j, 0, 0)
    row_spec = lambda width: pl.BlockSpec((bb, 1, width), sample_block)
    state_spec = pl.BlockSpec((bb, h, GDN_D, GDN_D), lambda i, j: (i * grid[1] + j, 0, 0, 0))
    in_specs = [
        pl.BlockSpec((1, tm, d), lambda i, j: (i, j, 0)),
        _full_spec((1, d)), _full_spec(win.shape), _full_spec(wabt.shape), _full_spec(convw.shape),
        _full_spec((1, h)), _full_spec((1, h)), _full_spec((h, 1)), _full_spec((h, 1)),
        _full_spec((1, GDN_D)), _full_spec(wout.shape),
    ] + [_full_spec(r.shape) for r in sample_rows] + [state_spec]
    out_specs = [
        pl.BlockSpec((1, tm, d), lambda i, j: (i, j, 0)),
        pl.BlockSpec((1, CONV_W - 1, QKV_W), lambda i, j: (i, 0, 0)),
        pl.BlockSpec((1, h, GDN_D, GDN_D), lambda i, j: (i, 0, 0, 0)),
        row_spec(h * GDN_D), state_spec,
    ]
    out_shape = [
        jax.ShapeDtypeStruct((b, l, d), F32),
        jax.ShapeDtypeStruct((b, CONV_W - 1, QKV_W), F32),
        jax.ShapeDtypeStruct((b, h, GDN_D, GDN_D), F32),
        jax.ShapeDtypeStruct((n, 1, h * GDN_D), F32),
        jax.ShapeDtypeStruct(sample_state.shape, F32),
    ]
    scratch = [
        pltpu.VMEM((tm + CARRY, QKV_W), F32),
        pltpu.VMEM((tm, h * GDN_D), F32),
        pltpu.VMEM((h, GDN_D, GDN_D), F32),
        pltpu.VMEM((h, tm, GDN_D), BF16),
        pltpu.VMEM((h, tm, GDN_D), BF16),
        pltpu.VMEM((h, tm, GDN_D), BF16),
        pltpu.VMEM((h, tm, GDN_D), F32),
        pltpu.VMEM((h, tm, GDN_D), BF16),
        pltpu.VMEM((h, tm, 2 * GDN_D), BF16),
        pltpu.VMEM((h, tm // 2, 2 * CHUNK), F32),
        pltpu.VMEM((h, tm // CHUNK, GDN_D), F32),
        pltpu.VMEM((h, tm, GDN_D), F32),
        pltpu.VMEM((h, tm // CHUNK, GDN_D + CHUNK, GDN_D), BF16),
        pltpu.VMEM((h, tm // CHUNK, GDN_D, GDN_D), F32),
    ]
    y, conv, ssm, o_s, state_s = pl.pallas_call(
        _gdn_prompt_kernel,
        grid=grid, in_specs=in_specs, out_specs=out_specs, out_shape=out_shape,
        scratch_shapes=scratch,
        compiler_params=pltpu.CompilerParams(
            dimension_semantics=("arbitrary", "arbitrary"), vmem_limit_bytes=VMEM_LIMIT_A),
        name="gdn_prompt",
    )(x, norm.reshape(1, d), win, wabt, convw,
      alog.reshape(1, h), dtb.reshape(1, h), alog.reshape(h, 1), dtb.reshape(h, 1),
      onorm.reshape(1, GDN_D), wout, *sample_rows, sample_state)
    return y, conv, ssm, o_s.reshape(n, h * GDN_D), state_s


def _gdn_sample_front_kernel(x_ref, norm_ref, win_ref, convw_ref,
                             c0_ref, c1_ref, c2_ref, alog_r_ref, dtb_r_ref,
                             pre_ref, q_ref, k_ref, u_ref, w_ref, qd_ref, gate_ref, eg_ref):
    x = x_ref[...]
    xn = x * lax.rsqrt(jnp.mean(x * x, axis=-1, keepdims=True) + EPS) * norm_ref[...]
    xb = xn.astype(BF16)
    gate_w = GDN_HEADS * GDN_D
    pre = _dot(xb, win_ref[:, :QKV_W])
    pre_ref[...] = pre
    gate_ref[...] = _dot(xb, win_ref[:, QKV_W:QKV_W + gate_w])
    ab = _dot(xb, win_ref[:, QKV_W + gate_w:])
    g = -jnp.exp(alog_r_ref[...]) * _softplus(ab[:, :GDN_HEADS] + dtb_r_ref[...])
    beta = jax.nn.sigmoid(ab[:, GDN_HEADS:])
    e_g = jnp.exp(g)
    eg_ref[...] = e_g
    act = _silu(((c0_ref[...] * convw_ref[0:1, :] + c1_ref[...] * convw_ref[1:2, :])
                 + c2_ref[...] * convw_ref[2:3, :]) + pre * convw_ref[3:4, :])
    w_all = GDN_HEADS * GDN_D
    for h in range(GDN_HEADS):
        cols = slice(h * GDN_D, (h + 1) * GDN_D)
        qh = act[:, h * GDN_D:(h + 1) * GDN_D]
        kh = act[:, w_all + h * GDN_D:w_all + (h + 1) * GDN_D]
        vh = act[:, 2 * w_all + h * GDN_D:2 * w_all + (h + 1) * GDN_D]
        qn = qh * lax.rsqrt(jnp.sum(qh * qh, axis=-1, keepdims=True) + EPS) * (GDN_D ** -0.5)
        kn = kh * lax.rsqrt(jnp.sum(kh * kh, axis=-1, keepdims=True) + EPS)
        b_col = beta[:, h:h + 1]
        eg_col = e_g[:, h:h + 1]
        q_ref[:, cols] = qn
        k_ref[:, cols] = kn
        u_ref[:, cols] = vh * b_col
        w_ref[:, cols] = kn * b_col * eg_col
        qd_ref[:, cols] = qn * eg_col


def _out_proj_kernel(x_ref, o_ref, gate_ref, onorm_ref, wout_ref, y_ref, *, head_w, norm_heads):
    pieces = []
    for h in range(x_ref.shape[1] // head_w):
        cols = slice(h * head_w, (h + 1) * head_w)
        o = o_ref[:, cols]
        if norm_heads:
            o = o * lax.rsqrt(jnp.mean(o * o, axis=-1, keepdims=True) + EPS) * onorm_ref[...]
        pieces.append((o * _silu(gate_ref[:, cols])).astype(BF16))
    y_ref[...] = x_ref[...] + _dot(jnp.concatenate(pieces, axis=1), wout_ref[...])


def _gdn_sample_front(x, conv_state, norm, win, convw, alog, dtb):
    n, d = x.shape
    h = GDN_HEADS
    hw = h * GDN_D
    conv_t = jnp.transpose(conv_state, (1, 0, 2))
    f = lambda shape: jax.ShapeDtypeStruct(shape, F32)
    pre, q, k, u, w, qd, gate, eg = pl.pallas_call(
        _gdn_sample_front_kernel,
        out_shape=[f((n, QKV_W)), f((n, hw)), f((n, hw)), f((n, hw)), f((n, hw)), f((n, hw)),
                   f((n, hw)), f((n, h))],
        compiler_params=pltpu.CompilerParams(vmem_limit_bytes=VMEM_LIMIT),
        name="gdn_sample_front",
    )(x, norm.reshape(1, d), win, convw, conv_t[0], conv_t[1], conv_t[2],
      alog.reshape(1, h), dtb.reshape(1, h))
    conv_new = jnp.stack([conv_t[1], conv_t[2], pre], axis=1)
    return (q, k, u, w, qd, eg), gate, conv_new


def _gdn_sample_out(x, o, gate, onorm, wout):
    return pl.pallas_call(
        functools.partial(_out_proj_kernel, head_w=GDN_D, norm_heads=True),
        out_shape=jax.ShapeDtypeStruct(x.shape, F32),
        name="gdn_sample_out",
    )(x, o, gate, onorm.reshape(1, GDN_D), wout)


def _head_rms(x, gain_row, n_heads):
    pieces = []
    for h in range(n_heads):
        xh = x[:, h * HEAD_DIM:(h + 1) * HEAD_DIM]
        pieces.append(xh * lax.rsqrt(jnp.mean(xh * xh, axis=-1, keepdims=True) + EPS) * gain_row)
    return pieces


def _pair_rms(x, gain2, left):
    sq = x * x
    lo = jnp.sum(jnp.where(left, sq, 0.0), axis=-1, keepdims=True)
    hi = jnp.sum(jnp.where(left, 0.0, sq), axis=-1, keepdims=True)
    ms = jnp.where(left, lo, hi) * (1.0 / HEAD_DIM)
    return x * lax.rsqrt(ms + EPS) * gain2


def _swa_prompt_kernel(y_ref, kvnorm_ref, wkv_ref, knorm2_ref, normb_ref, win_ref, qnorm2_ref,
                       sinks_ref, wout_ref,
                       sq_ref, skn_ref, svn_ref, skc_ref, svc_ref, ssinks_ref, sslopes_ref,
                       out_ref, kwin_ref, vwin_ref, so_ref, skwin_ref, svwin_ref,
                       k_scr, v_scr, o_scr):
    tq = y_ref.shape[1]
    w = WINDOW
    step = pl.program_id(1)
    lane = lax.broadcasted_iota(jnp.int32, (1, 2 * HEAD_DIM), 1)
    left = lane < HEAD_DIM

    @pl.when(step == 0)
    def _():
        k_scr[:, 0:w, :] = jnp.zeros((N_KV_HEADS, w, 2 * HEAD_DIM), BF16)
        v_scr[:, 0:w, :] = jnp.zeros((N_KV_HEADS, w, 2 * HEAD_DIM), BF16)

    @pl.when(step > 0)
    def _():
        k_scr[:, 0:w, :] = k_scr[:, tq:tq + w, :]
        v_scr[:, 0:w, :] = v_scr[:, tq:tq + w, :]

    _sample_attention(sq_ref, skn_ref, svn_ref, skc_ref, svc_ref, ssinks_ref, sslopes_ref,
                      so_ref, skwin_ref, svwin_ref)

    sub = SUB_B
    n_sub = tq // sub
    pieces_q = 2 * ATT_W // (4 * HEAD_DIM)

    def start(s):
        y = y_ref[0, s * sub:(s + 1) * sub, :]
        yn = y * lax.rsqrt(jnp.mean(y * y, axis=-1, keepdims=True) + EPS)
        return dict(y=y, y_kv=(yn * kvnorm_ref[...]).astype(BF16), y_q=(yn * normb_ref[...]).astype(BF16),
                    q_pairs=[None] * (N_Q_HEADS // 2), gate=[None] * (pieces_q // 2))

    def project(s, st):
        kv = _dot(st["y_kv"], wkv_ref[...])
        rows = slice(w + s * sub, w + (s + 1) * sub)
        for c in range(N_KV_HEADS // 2):
            cols = slice(c * 2 * HEAD_DIM, (c + 1) * 2 * HEAD_DIM)
            kp = _pair_rms(kv[:, cols], knorm2_ref[...], left)
            vp = kv[:, KV_W + c * 2 * HEAD_DIM:KV_W + (c + 1) * 2 * HEAD_DIM]
            if s == n_sub - 1:
                kwin_ref[0, :, cols] = kp[sub - w:, :]
                vwin_ref[0, :, cols] = vp[sub - w:, :]
            kr = pltpu.roll(kp, HEAD_DIM, 1)
            vr = pltpu.roll(vp, HEAD_DIM, 1)
            k_scr[2 * c, rows, :] = jnp.where(left, kp, kr).astype(BF16)
            k_scr[2 * c + 1, rows, :] = jnp.where(left, kr, kp).astype(BF16)
            v_scr[2 * c, rows, :] = jnp.where(left, vp, vr).astype(BF16)
            v_scr[2 * c + 1, rows, :] = jnp.where(left, vr, vp).astype(BF16)
        yield
        for piece in range(pieces_q):
            cols = slice(piece * 4 * HEAD_DIM, (piece + 1) * 4 * HEAD_DIM)
            part = _dot(st["y_q"], win_ref[:, cols])
            if piece < pieces_q // 2:
                for half in range(2):
                    st["q_pairs"][2 * piece + half] = _pair_rms(
                        part[:, half * 2 * HEAD_DIM:(half + 1) * 2 * HEAD_DIM], qnorm2_ref[...], left)
            else:
                st["gate"][piece - pieces_q // 2] = part
            yield

    kj = lax.broadcasted_iota(jnp.int32, (2 * w, w), 0)
    qi = lax.broadcasted_iota(jnp.int32, (2 * w, w), 1)
    dist = qi - kj + w
    band = (dist >= 0) & (dist <= w)
    j_rel = (kj - w).astype(F32)
    i_row = lax.broadcasted_iota(jnp.int32, (1, w), 1).astype(F32)

    def scores(st, blk, local, hk):
        lhs = []
        for c in (2 * hk, 2 * hk + 1):
            qp = st["q_pairs"][c][local * w:(local + 1) * w, :]
            lhs.append(jnp.where(left, qp, 0.0))
            lhs.append(jnp.where(left, 0.0, qp))
        return _dot_nt(k_scr[hk, blk * w:blk * w + 2 * w, :],
                       jnp.concatenate(lhs, axis=0).astype(BF16))

    def attend(blk, hk, s4):
        q_rows = slice(blk * w, (blk + 1) * w)
        valid = band & ((step > 0) | (kj >= w)) if blk == 0 else band
        ps = []
        for g in range(Q_GROUP):
            hq = hk * Q_GROUP + g
            slope = 2.0 ** (-8.0 * (hq + 1) / N_Q_HEADS)
            a = jnp.where(valid, s4[:, g * w:(g + 1) * w] + slope * j_rel, -jnp.inf)
            sink = sinks_ref[0:1, hq:hq + 1] + slope * i_row
            mx = jnp.maximum(jnp.max(a, axis=0, keepdims=True), sink)
            p = jnp.exp(a - mx)
            inv = 1.0 / (jnp.sum(p, axis=0, keepdims=True) + jnp.exp(sink - mx))
            ps.append((p * inv).astype(BF16))
        o4 = _dot_tn(jnp.concatenate(ps, axis=1),
                     v_scr[hk, blk * w:blk * w + 2 * w, :])
        for j, c in enumerate((2 * hk, 2 * hk + 1)):
            o_scr[q_rows, c * 2 * HEAD_DIM:(c + 1) * 2 * HEAD_DIM] = jnp.where(
                left, o4[(2 * j) * w:(2 * j + 1) * w, :], o4[(2 * j + 1) * w:(2 * j + 2) * w, :])

    def attention(s, st):
        items = [(s * (sub // w) + local, local, hk) for local in range(sub // w) for hk in range(N_KV_HEADS)]
        s_next = scores(st, *items[0])
        for n, (blk, _, hk) in enumerate(items):
            s_cur = s_next
            if n + 1 < len(items):
                s_next = scores(st, *items[n + 1])
            yield
            attend(blk, hk, s_cur)

    def finish(s, st):
        rows = slice(s * sub, (s + 1) * sub)
        o = (o_scr[rows, :] * _silu(jnp.concatenate(st["gate"], axis=1))).astype(BF16)
        for piece in range(d_model // (4 * HEAD_DIM)):
            cols = slice(piece * 4 * HEAD_DIM, (piece + 1) * 4 * HEAD_DIM)
            out_ref[0, rows, cols] = st["y"][:, cols] + _dot(o, wout_ref[:, cols])
            yield

    def run(gen):
        for _ in gen:
            pass

    def carry(main, *sides):
        for _ in main:
            for g in sides:
                next(g, None)
        for g in sides:
            run(g)

    d_model = y_ref.shape[2]
    sts = [start(0)]
    run(project(0, sts[0]))
    for s in range(n_sub):
        sides = []
        if s + 1 < n_sub:
            sts.append(start(s + 1))
            sides.append(project(s + 1, sts[s + 1]))
        if s >= 1:
            sides.append(finish(s - 1, sts[s - 1]))
        carry(attention(s, sts[s]), *sides)
    run(finish(n_sub - 1, sts[n_sub - 1]))


def _swa_prompt(y, kvnorm, wkv, knorm, normb, win, qnorm, sinks, wout, sample_q, sample_kn, sample_vn,
                cache_k, cache_v):
    b, l, d = y.shape
    tq = TQ_B
    w = WINDOW
    steps = l // tq
    n = cache_k.shape[0]
    bb = n // (b * steps)
    assert bb * b * steps == n
    f = lambda shape: jax.ShapeDtypeStruct(shape, F32)
    knorm2 = jnp.concatenate([knorm, knorm]).reshape(1, 2 * HEAD_DIM)
    qnorm2 = (jnp.concatenate([qnorm, qnorm]) * (HEAD_DIM ** -0.5)).reshape(1, 2 * HEAD_DIM)
    slopes = (2.0 ** (-8.0 * jnp.arange(1, N_Q_HEADS + 1, dtype=F32) / N_Q_HEADS)).reshape(N_Q_HEADS, 1)
    spec3 = lambda r, c: pl.BlockSpec((bb, r, c), lambda i, j: (i * steps + j, 0, 0))
    in_specs = [
        pl.BlockSpec((1, tq, d), lambda i, j: (i, j, 0)),
        _full_spec((1, d)), _full_spec(wkv.shape), _full_spec((1, 2 * HEAD_DIM)), _full_spec((1, d)),
        _full_spec(win.shape), _full_spec((1, 2 * HEAD_DIM)), _full_spec((1, N_Q_HEADS)),
        _full_spec(wout.shape),
        spec3(N_Q_HEADS, HEAD_DIM), spec3(1, KV_W), spec3(1, KV_W), spec3(w, KV_W), spec3(w, KV_W),
        _full_spec((N_Q_HEADS, 1)), _full_spec((N_Q_HEADS, 1)),
    ]
    out_specs = [
        pl.BlockSpec((1, tq, d), lambda i, j: (i, j, 0)),
        pl.BlockSpec((1, w, KV_W), lambda i, j: (i, 0, 0)),
        pl.BlockSpec((1, w, KV_W), lambda i, j: (i, 0, 0)),
        spec3(N_Q_HEADS, HEAD_DIM), spec3(w, KV_W), spec3(w, KV_W),
    ]
    out, kwin, vwin, o_s, kwin_s, vwin_s = pl.pallas_call(
        _swa_prompt_kernel,
        grid=(b, steps), in_specs=in_specs, out_specs=out_specs,
        out_shape=[f((b, l, d)), f((b, w, KV_W)), f((b, w, KV_W)),
                   f((n, N_Q_HEADS, HEAD_DIM)), f((n, w, KV_W)), f((n, w, KV_W))],
        scratch_shapes=[pltpu.VMEM((N_KV_HEADS, w + tq, 2 * HEAD_DIM), BF16),
                        pltpu.VMEM((N_KV_HEADS, w + tq, 2 * HEAD_DIM), BF16),
                        pltpu.VMEM((tq, ATT_W), F32)],
        compiler_params=pltpu.CompilerParams(
            dimension_semantics=("arbitrary", "arbitrary"), vmem_limit_bytes=VMEM_LIMIT),
        name="swa_prompt",
    )(y, kvnorm.reshape(1, d), wkv, knorm2, normb.reshape(1, d), win, qnorm2,
      sinks.reshape(1, N_Q_HEADS), wout,
      sample_q.reshape(n, N_Q_HEADS, HEAD_DIM), sample_kn.reshape(n, 1, KV_W), sample_vn.reshape(n, 1, KV_W),
      cache_k, cache_v, sinks.reshape(N_Q_HEADS, 1), slopes)
    return out, kwin, vwin, o_s.reshape(n, ATT_W), kwin_s, vwin_s


def _swa_sample_front_kernel(y_ref, kvnorm_ref, wkv_ref, knorm_ref, normb_ref, win_ref, qnorm_ref,
                             k_ref, v_ref, q_ref, gate_ref):
    y = y_ref[...]
    yn = y * lax.rsqrt(jnp.mean(y * y, axis=-1, keepdims=True) + EPS)
    kv = _dot((yn * kvnorm_ref[...]).astype(BF16), wkv_ref[...])
    k_ref[...] = jnp.concatenate(_head_rms(kv[:, :KV_W], knorm_ref[...], N_KV_HEADS), axis=1)
    v_ref[...] = kv[:, KV_W:]
    qg = _dot((yn * normb_ref[...]).astype(BF16), win_ref[...])
    q_ref[...] = jnp.concatenate(
        _head_rms(qg[:, :ATT_W], qnorm_ref[...] * (HEAD_DIM ** -0.5), N_Q_HEADS), axis=1)
    gate_ref[...] = qg[:, ATT_W:]


def _sample_attention(q_ref, kn_ref, vn_ref, kc_ref, vc_ref, sinks_ref, slopes_ref, o_ref, kwin_ref, vwin_ref):
    bb = kc_ref.shape[0]
    w = kc_ref.shape[1]
    hrow = lax.broadcasted_iota(jnp.int32, (N_Q_HEADS, KV_W), 0) // Q_GROUP
    lblk = lax.broadcasted_iota(jnp.int32, (N_Q_HEADS, KV_W), 1) // HEAD_DIM
    own = hrow == lblk
    dist_c = (w - lax.broadcasted_iota(jnp.int32, (1, w), 1)).astype(F32)
    krow = lax.broadcasted_iota(jnp.int32, (w, KV_W), 0)
    slopes = slopes_ref[...]
    sink = sinks_ref[...]
    qms, scs = [], []
    for b in range(bb):
        q = q_ref[b]
        qm = jnp.where(own, jnp.concatenate([q] * N_KV_HEADS, axis=1), 0.0).astype(BF16)
        qms.append(qm)
        scs.append(_dot_nt(qm, kc_ref[b].astype(BF16)))
    pcs, pns = [], []
    for b in range(bb):
        s_c = scs[b] - slopes * dist_c
        s_n = jnp.sum(qms[b].astype(F32) * kn_ref[b].astype(BF16).astype(F32), axis=-1, keepdims=True)
        mx = jnp.maximum(jnp.maximum(jnp.max(s_c, axis=-1, keepdims=True), s_n), sink)
        p_c = jnp.exp(s_c - mx)
        p_n = jnp.exp(s_n - mx)
        den = jnp.sum(p_c, axis=-1, keepdims=True) + p_n + jnp.exp(sink - mx)
        pcs.append((p_c / den).astype(BF16))
        pns.append(p_n / den)
    rs = [_dot(pcs[b], vc_ref[b].astype(BF16)) for b in range(bb)]
    for b in range(bb):
        kn = kn_ref[b]
        vn = vn_ref[b]
        r = rs[b] + pns[b].astype(BF16).astype(F32) * vn.astype(BF16).astype(F32)
        r = jnp.where(own, r, 0.0)
        acc = r[:, 0:HEAD_DIM]
        for blk in range(1, N_KV_HEADS):
            acc = acc + r[:, blk * HEAD_DIM:(blk + 1) * HEAD_DIM]
        o_ref[b] = acc
        kwin_ref[b] = jnp.where(krow == w - 1, kn, pltpu.roll(kc_ref[b], w - 1, 0))
        vwin_ref[b] = jnp.where(krow == w - 1, vn, pltpu.roll(vc_ref[b], w - 1, 0))


def _swa_sample_front(y, kvnorm, wkv, knorm, normb, win, qnorm):
    n, d = y.shape
    f = lambda shape: jax.ShapeDtypeStruct(shape, F32)
    return pl.pallas_call(
        _swa_sample_front_kernel,
        out_shape=[f((n, KV_W)), f((n, KV_W)), f((n, ATT_W)), f((n, ATT_W))],
        compiler_params=pltpu.CompilerParams(vmem_limit_bytes=VMEM_LIMIT),
        name="swa_sample_front",
    )(y, kvnorm.reshape(1, d), wkv, knorm.reshape(1, HEAD_DIM), normb.reshape(1, d), win,
      qnorm.reshape(1, HEAD_DIM))


def _swa_sample_out(y, o, gate, wout):
    return pl.pallas_call(
        functools.partial(_out_proj_kernel, head_w=ATT_W, norm_heads=False),
        out_shape=jax.ShapeDtypeStruct(y.shape, F32),
        name="swa_sample_out",
    )(y, o, gate, jnp.ones((1, ATT_W), F32), wout)


def kernel(x_prompt, x_sample, state_conv, state_ssm, cache_k_win, cache_v_win, norm_a, w_in_a, conv_w_a, a_log, dt_bias, o_norm_a, w_out_a, kv_norm, w_kv, k_norm, norm_b, w_in_b, q_norm, sinks, w_out_b):
    n_a = w_in_a.shape[0]
    n_b = w_in_b.shape[0]
    assert n_a == 1 and n_b == 1, "kernel is written for DEPTH == 2"
    bp, lp, d = x_prompt.shape
    n = x_sample.shape[0]
    hw = GDN_HEADS * GDN_D

    hp, hs = x_prompt, x_sample.reshape(n, d)
    conv_p, ssm_p, conv_s, ssm_s = [], [], [], []
    for layer in range(n_a):
        win_a = w_in_a[layer].astype(BF16)
        wabt = win_a[:, QKV_W + hw:].T
        wout = w_out_a[layer].astype(BF16)
        rows_s, gate_s, cbuf_s = _gdn_sample_front(hs, state_conv[layer], norm_a[layer], win_a,
                                                   conv_w_a[layer], a_log[layer], dt_bias[layer])
        hp, cbuf, st, o_s, st_s = _gdn_prompt(hp, norm_a[layer], win_a, wabt, conv_w_a[layer],
                                              a_log[layer], dt_bias[layer], o_norm_a[layer], wout,
                                              rows_s, state_ssm[layer])
        conv_p.append(cbuf)
        ssm_p.append(st)
        hs = _gdn_sample_out(hs, o_s, gate_s, o_norm_a[layer], wout)
        conv_s.append(cbuf_s)
        ssm_s.append(st_s)

    wkv = w_kv.astype(BF16)
    win = w_in_b[0].astype(BF16)
    woutb = w_out_b[0].astype(BF16)
    kn_s, vn_s, q_s, gate_s = _swa_sample_front(hs, kv_norm, wkv, k_norm, norm_b[0], win, q_norm[0])
    hp, k_win_p, v_win_p, o_s, k_win_s, v_win_s = _swa_prompt(
        hp, kv_norm, wkv, k_norm, norm_b[0], win, q_norm[0], sinks[0], woutb, q_s, kn_s, vn_s,
        cache_k_win.reshape(n, WINDOW, KV_W), cache_v_win.reshape(n, WINDOW, KV_W))
    hs = _swa_sample_out(hs, o_s, gate_s, woutb)
    kv_shape = (N_KV_HEADS, HEAD_DIM)
    return (hp, hs.reshape(n, 1, d), jnp.stack(conv_p), jnp.stack(ssm_p),
            k_win_p.reshape(bp, WINDOW, *kv_shape), v_win_p.reshape(bp, WINDOW, *kv_shape),
            jnp.stack(conv_s), jnp.stack(ssm_s),
            k_win_s.reshape(n, WINDOW, *kv_shape), v_win_s.reshape(n, WINDOW, *kv_shape))
```

```python
import functools

import jax
import jax.numpy as jnp
from jax import lax
from jax.experimental import pallas as pl
from jax.experimental.pallas import tpu as pltpu

F32 = jnp.float32
BF16 = jnp.bfloat16
EPS = 1e-6

D_MODEL = 1024
GDN_HEADS = 8
GDN_D = 128
QKV_W = 3 * GDN_HEADS * GDN_D
CONV_W = 4
CHUNK = 64
N_Q_HEADS = 16
N_KV_HEADS = 4
Q_GROUP = N_Q_HEADS // N_KV_HEADS
HEAD_DIM = 64
KV_W = N_KV_HEADS * HEAD_DIM
ATT_W = N_Q_HEADS * HEAD_DIM
WINDOW = 128

TM_A = 512
SUB_A = 256
TQ_B = 1024
CARRY = 8
VMEM_LIMIT = 52 * 1024 * 1024
VMEM_LIMIT_A = 62 * 1024 * 1024

_NT = (((1,), (1,)), ((), ()))
_TN = (((0,), (0,)), ((), ()))


def _dot(a, b):
    return jnp.dot(a, b, preferred_element_type=F32)


def _dot_nt(a, b):
    return lax.dot_general(a, b, _NT, preferred_element_type=F32)


def _dot_tn(a, b):
    return lax.dot_general(a, b, _TN, preferred_element_type=F32)


def _split(x):
    hi = x.astype(BF16)
    lo = (x - hi.astype(F32)).astype(BF16)
    return hi, lo


def _dot_exact_lhs(a_bf, b):
    b0 = b.astype(BF16)
    r1 = b - b0.astype(F32)
    b1 = r1.astype(BF16)
    b2 = (r1 - b1.astype(F32)).astype(BF16)
    return (_dot(a_bf, b2) + _dot(a_bf, b1)) + _dot(a_bf, b0)


def _dot_exact_rhs(a, b_bf):
    a0 = a.astype(BF16)
    r1 = a - a0.astype(F32)
    a1 = r1.astype(BF16)
    a2 = (r1 - a1.astype(F32)).astype(BF16)
    return (_dot(a2, b_bf) + _dot(a1, b_bf)) + _dot(a0, b_bf)


def _silu(x):
    return x * jax.nn.sigmoid(x)


def _softplus(x):
    return jnp.maximum(x, 0.0) + jnp.log1p(jnp.exp(-jnp.abs(x)))


def _halves(x, left):
    zero = jnp.zeros_like(x)
    return jnp.where(left, x, zero), jnp.where(left, zero, x)


def _block_diag(x, left):
    return jnp.concatenate(_halves(x, left), axis=0)


def _pair_split_product(a, b, left):
    ah, al = _split(a)
    bh, bl = _split(b)
    bh1, bh2 = _halves(bh, left)
    bl1, bl2 = _halves(bl, left)
    zero = jnp.zeros_like(bh)
    rhs = jnp.concatenate([jnp.concatenate([bh1, bl1], axis=1), jnp.concatenate([bh2, bl2], axis=1),
                           jnp.concatenate([bh1, zero], axis=1), jnp.concatenate([bh2, zero], axis=1)], axis=0)
    y = _dot(jnp.concatenate([ah, al], axis=1), rhs)
    half = y.shape[1] // 2
    return y[:, :half] + y[:, half:]


def _unit_lower_inverse_pairs(ms, left):
    n = ms[0].shape[0]
    row = lax.broadcasted_iota(jnp.int32, (n, 2 * n), 0)
    col = lax.broadcasted_iota(jnp.int32, (n, 2 * n), 1) % n
    eye2 = jnp.where(row == col, 1.0, 0.0).astype(F32)

    def times(a, b):
        return _dot(a.astype(BF16), _block_diag(b.astype(BF16), left))

    ts = [eye2 - m for m in ms]
    ps = [times(m, m) for m in ms]
    yield
    steps = max(1, (n - 1).bit_length()) - 1
    for _ in range(steps - 1):
        both = [_dot(jnp.concatenate([t.astype(BF16), p.astype(BF16)], axis=0), _block_diag(p.astype(BF16), left))
                for t, p in zip(ts, ps)]
        ts = [t + tp[:n, :] for t, tp in zip(ts, both)]
        ps = [tp[n:, :] for tp in both]
        yield
    ts = [t + times(t, p) for t, p in zip(ts, ps)]
    yield
    rs = [(eye2 - t) - _pair_split_product(m, t, left) for m, t in zip(ms, ts)]
    yield
    ts = [t + times(t, r) for t, r in zip(ts, rs)]
    yield
    return ts


def _full_spec(shape):
    nd = len(shape)
    return pl.BlockSpec(shape, lambda *_: (0,) * nd, pipeline_mode=pl.Buffered(1))


def _gdn_gates(ab, ab_t, alog_r, dtb_r, alog_c, dtb_c, tm, chunk):
    h = GDN_HEADS
    g_c = -jnp.exp(alog_r) * _softplus(ab[:, :h] + dtb_r)
    beta = jax.nn.sigmoid(ab[:, h:])
    g_r = -jnp.exp(alog_c) * _softplus(ab_t[:h, :] + dtb_c)
    row = lax.broadcasted_iota(jnp.int32, (tm, tm), 0)
    col = lax.broadcasted_iota(jnp.int32, (tm, tm), 1)
    same = (row // chunk) == (col // chunk)
    lower = jnp.where(same & (row >= col), 1.0, 0.0).astype(BF16)
    upper = jnp.where(same & (row <= col), 1.0, 0.0).astype(BF16)
    gc = _dot_exact_lhs(lower, g_c)
    gr = _dot_exact_rhs(g_r, upper)
    return beta, gc, gr


def _sample_state_update(row0, q_ref, k_ref, u_ref, w_ref, qd_ref, eg_ref, s_ref, o_ref, s_out_ref):
    bb = s_ref.shape[0]
    group = 8
    assert group % bb == 0
    base = pl.multiple_of((row0 // group) * group, group)
    part = (row0 - base) // bb

    def rows(ref, cols):
        x = ref[pl.ds(base, group), cols]
        out = x[0:bb, :]
        for p in range(1, group // bb):
            out = jnp.where(part == p, x[p * bb:(p + 1) * bb, :], out)
        return out

    row = lax.broadcasted_iota(jnp.int32, (8, GDN_D), 0)
    decay_all = rows(eg_ref, slice(None))
    prods, ins = [], []
    for h in range(GDN_HEADS):
        cols = slice(h * GDN_D, (h + 1) * GDN_D)
        q, k, u, w, qd = (rows(r, cols) for r in (q_ref, k_ref, u_ref, w_ref, qd_ref))
        for b in range(bb):
            lhs = jnp.where(row == 0, w[b:b + 1, :], jnp.where(row == 1, qd[b:b + 1, :], 0.0))
            prods.append(_dot(lhs.astype(BF16), s_ref[b, h].astype(BF16)))
            ins.append((b, h, cols, q[b:b + 1, :], k[b:b + 1, :], u[b:b + 1, :]))
    for (b, h, cols, q, k, u), prod in zip(ins, prods):
        q = q.astype(BF16).astype(F32)
        k = k.astype(BF16).astype(F32)
        v_new = u - prod[0:1, :]
        vb = v_new.astype(BF16).astype(F32)
        qk = jnp.sum(q * k, axis=-1, keepdims=True)
        o_ref[b, :, cols] = prod[1:2, :] + qk.astype(BF16).astype(F32) * vb
        k8 = jnp.where(row == 0, k, 0.0).astype(BF16)
        v8 = jnp.where(row == 0, vb, 0.0).astype(BF16)
        s_out_ref[b, h] = s_ref[b, h] * decay_all[b:b + 1, h:h + 1] + _dot_tn(k8, v8)


def _gdn_prompt_kernel(x_ref, norm_ref, win_ref, wabt_ref, convw_ref,
                       alog_r_ref, dtb_r_ref, alog_c_ref, dtb_c_ref, onorm_ref, wout_ref,
                       sq_ref, sk_ref, su_ref, sw_ref, sqd_ref, seg_ref, sstate_ref,
                       y_ref, conv_ref, ssm_ref, so_ref, sstate_out_ref,
                       pre_scr, gate_scr, s_scr, k_scr, kb_scr, q_scr, qd_scr, kd_scr,
                       rhs_scr, dec_scr, gt_scr, o_scr, lhs_scr, su_scr):
    tm = x_ref.shape[1]
    sub = SUB_A
    n_sub = tm // sub
    n_chunks = sub // CHUNK
    gate_w = GDN_HEADS * GDN_D
    l = pl.program_id(1)

    @pl.when(l == 0)
    def _():
        pre_scr[0:CARRY, :] = jnp.zeros((CARRY, QKV_W), F32)
        s_scr[...] = jnp.zeros(s_scr.shape, F32)

    ci = lax.broadcasted_iota(jnp.int32, (CHUNK, 2 * CHUNK), 0)
    cj = lax.broadcasted_iota(jnp.int32, (CHUNK, 2 * CHUNK), 1) % CHUNK
    causal = ci >= cj
    strict = ci > cj
    left = lax.broadcasted_iota(jnp.int32, (1, 2 * CHUNK), 1) < CHUNK
    heads = list(range(GDN_HEADS))

    sample_row0 = (pl.program_id(0) * pl.num_programs(1) + l) * sstate_ref.shape[0]
    _sample_state_update(sample_row0, sq_ref, sk_ref, su_ref, sw_ref, sqd_ref, seg_ref, sstate_ref,
                         so_ref, sstate_out_ref)

    def start(s):
        x = x_ref[0, s * sub:(s + 1) * sub, :]
        xn = x * lax.rsqrt(jnp.mean(x * x, axis=-1, keepdims=True) + EPS) * norm_ref[...]
        xb = xn.astype(BF16)
        ab = _dot(xb, win_ref[:, QKV_W + gate_w:])
        ab_t = _dot_nt(wabt_ref[...], xb)
        beta, gc, gr = _gdn_gates(ab, ab_t, alog_r_ref[...], dtb_r_ref[...],
                                  alog_c_ref[...], dtb_c_ref[...], sub, CHUNK)
        g_last = jnp.concatenate(
            [jnp.broadcast_to(gc[c * CHUNK + CHUNK - 1:c * CHUNK + CHUNK, :], (CHUNK, GDN_HEADS))
             for c in range(n_chunks)], axis=0)
        return dict(x=x, xb=xb, beta=beta, gc=gc, gr=gr, e_g=jnp.exp(gc),
                    e_kd=jnp.exp(g_last - gc), e_tot=jnp.exp(g_last))

    def project(s, st):
        for part in range(3):
            cols = slice(part * gate_w, (part + 1) * gate_w)
            pre_scr[CARRY + s * sub:CARRY + (s + 1) * sub, cols] = _dot(st["xb"], win_ref[:, cols])

    def project_gate(s, st):
        gate_scr[s * sub:(s + 1) * sub, :] = _dot(st["xb"], win_ref[:, QKV_W:QKV_W + gate_w])

    def conv_act(s, j):
        cols = slice(j * GDN_D, (j + 1) * GDN_D)
        ext = pre_scr[s * sub:s * sub + CARRY + sub, cols]
        half = ext[CARRY:, :] * (0.5 * convw_ref[CONV_W - 1:CONV_W, cols])
        for back in range(1, CONV_W):
            tap = CONV_W - 1 - back
            half = half + pltpu.roll(ext, back, 0)[CARRY:, :] * (0.5 * convw_ref[tap:tap + 1, cols])
        return half + half * jnp.tanh(half)

    def head_prep(s, st, h):
        rows = slice(s * sub, (s + 1) * sub)
        qh = conv_act(s, h)
        kh = conv_act(s, GDN_HEADS + h)
        vh = conv_act(s, 2 * GDN_HEADS + h)
        qn = qh * lax.rsqrt(jnp.sum(qh * qh, axis=-1, keepdims=True) + EPS) * (GDN_D ** -0.5)
        kn = kh * lax.rsqrt(jnp.sum(kh * kh, axis=-1, keepdims=True) + EPS)
        b_col = st["beta"][:, h:h + 1]
        eg_col = st["e_g"][:, h:h + 1]
        kb = kn * b_col
        k_scr[h, rows, :] = kn.astype(BF16)
        kb_scr[h, rows, :] = kb.astype(BF16)
        q_scr[h, rows, :] = qn.astype(BF16)
        qd_scr[h, rows, :] = qn * eg_col
        kd_scr[h, rows, :] = (kn * st["e_kd"][:, h:h + 1]).astype(BF16)
        rhs_scr[h, rows, 0:GDN_D] = (vh * b_col).astype(BF16)
        rhs_scr[h, rows, GDN_D:2 * GDN_D] = (kb * eg_col).astype(BF16)
        gc, gr = st["gc"], st["gr"]
        for c in range(n_chunks):
            gt_scr[h, s * n_chunks + c:s * n_chunks + c + 1, :] = jnp.broadcast_to(
                st["e_tot"][c * CHUNK:c * CHUNK + 1, h:h + 1], (1, GDN_D))
        for j in range(n_chunks // 2):
            r1 = slice(2 * j * CHUNK, (2 * j + 1) * CHUNK)
            r2 = slice((2 * j + 1) * CHUNK, (2 * j + 2) * CHUNK)
            diff = jnp.where(left, gc[r1, h:h + 1], gc[r2, h:h + 1]) - gr[h:h + 1, 2 * j * CHUNK:(2 * j + 2) * CHUNK]
            pair = s * (n_chunks // 2) + j
            dec_scr[h, pair * CHUNK:(pair + 1) * CHUNK, :] = jnp.exp(jnp.where(causal, diff, -jnp.inf))

    def chunk_terms(s):
        items = [(h, s * (n_chunks // 2) + j) for h in heads for j in range(n_chunks // 2)]
        aqs = []
        for h, pair in items:
            r12 = slice(2 * pair * CHUNK, (2 * pair + 2) * CHUNK)
            aqs.append(_dot_nt(jnp.concatenate([kb_scr[h, r12, :], q_scr[h, r12, :]], axis=0), k_scr[h, r12, :]))
        yield
        decs = [dec_scr[h, pair * CHUNK:(pair + 1) * CHUNK, :] for h, pair in items]
        ms = [jnp.where(strict, jnp.where(left, aq[:CHUNK, :], aq[CHUNK:2 * CHUNK, :]) * d, 0.0)
              for aq, d in zip(aqs, decs)]
        qks = [jnp.where(causal, jnp.where(left, aq[2 * CHUNK:3 * CHUNK, :], aq[3 * CHUNK:, :]) * d, 0.0)
               for aq, d in zip(aqs, decs)]
        ts = yield from _unit_lower_inverse_pairs(ms, left)
        sols = []
        for (h, pair), t, qk in zip(items, ts, qks):
            lo = t - t.astype(BF16).astype(F32)
            t_parts = (jnp.where(left, t, pltpu.roll(lo, CHUNK, 1)).astype(BF16),
                       jnp.where(left, pltpu.roll(t, CHUNK, 1), lo).astype(BF16))
            qk_parts = (qk[:, :CHUNK].astype(BF16), pltpu.roll(qk, CHUNK, 1)[:, :CHUNK].astype(BF16))
            for half in range(2):
                c = 2 * pair + half
                r = slice(c * CHUNK, (c + 1) * CHUNK)
                rhs = rhs_scr[h, r, :]
                sols.append((h, c, r, qk_parts[half],
                             _dot(t_parts[half], jnp.concatenate([rhs, rhs], axis=0)).astype(BF16)))
        yield
        outs = [(h, c, r, _dot_tn(kd_scr[h, r, :], sol),
                 _dot(qk_c, sol))
                for h, c, r, qk_c, sol in sols]
        for h, c, r, kd_uw, qk_uw in outs:
            lhs_scr[h, c, 0:GDN_D, :] = kd_uw[:, GDN_D:].astype(BF16)
            lhs_scr[h, c, GDN_D:GDN_D + CHUNK, :] = (qd_scr[h, r, :] - qk_uw[:, GDN_D:]).astype(BF16)
            su_scr[h, c] = kd_uw[:, :GDN_D]
            o_scr[h, r, :] = qk_uw[:, :GDN_D]
        yield

    def state_updates(s):
        for c in range(s * n_chunks, (s + 1) * n_chunks):
            rows = slice(c * CHUNK, (c + 1) * CHUNK)
            for h in heads:
                st_h = s_scr[h]
                prod = _dot(lhs_scr[h, c], st_h.astype(BF16))
                o_scr[h, rows, :] = o_scr[h, rows, :] + prod[GDN_D:, :]
                s_scr[h] = st_h * gt_scr[h, c:c + 1, :] + (su_scr[h, c] - prod[:GDN_D, :])
            yield

    def finish(s, st):
        rows = slice(s * sub, (s + 1) * sub)
        pieces = []
        for h in heads:
            o = o_scr[h, rows, :]
            on = o * lax.rsqrt(jnp.mean(o * o, axis=-1, keepdims=True) + EPS) * onorm_ref[...]
            pieces.append((on * _silu(gate_scr[rows, h * GDN_D:(h + 1) * GDN_D])).astype(BF16))
        on_all = jnp.concatenate(pieces, axis=1)
        for piece in range(D_MODEL // (2 * GDN_D)):
            cols = slice(piece * 2 * GDN_D, (piece + 1) * 2 * GDN_D)
            y_ref[0, rows, cols] = st["x"][:, cols] + _dot(on_all, wout_ref[:, cols])
            yield

    def run(gen):
        for _ in gen:
            pass

    def interleave(main, side, every):
        for n, _ in enumerate(main):
            if (n + 1) % every == 0:
                next(side, None)
        run(side)

    sts = [start(0)]
    project(0, sts[0])
    for s in range(n_sub):
        if s + 1 < n_sub:
            sts.append(start(s + 1))
        for h in heads:
            head_prep(s, sts[s], h)
        if s + 1 < n_sub:
            project(s + 1, sts[s + 1])
        project_gate(s, sts[s])
        if s == 0:
            run(chunk_terms(s))
        else:
            interleave(chunk_terms(s), state_updates(s - 1), 2)
            if s >= 2:
                run(finish(s - 2, sts[s - 2]))
    conv_ref[0] = pre_scr[tm + CARRY - 3:tm + CARRY, :]
    pre_scr[0:CARRY, :] = pre_scr[tm:tm + CARRY, :]
    if n_sub >= 2:
        interleave(state_updates(n_sub - 1), finish(n_sub - 2, sts[n_sub - 2]), 1)
    else:
        run(state_updates(0))

    @pl.when(l == pl.num_programs(1) - 1)
    def _():
        ssm_ref[0] = s_scr[...]

    run(finish(n_sub - 1, sts[n_sub - 1]))


def _gdn_prompt(x, norm, win, wabt, convw, alog, dtb, onorm, wout, sample_rows, sample_state):
    b, l, d = x.shape
    tm = TM_A
    h = GDN_HEADS
    grid = (b, l // tm)
    n = sample_state.shape[0]
    bb = n // (grid[0] * grid[1])
    assert bb * grid[0] * grid[1] == n
    sample_block = lambda i, j: (i * grid[1] + j, 0, 0)
    row_spec = lambda width: pl.BlockSpec((bb, 1, width), sample_block)
    state_spec = pl.BlockSpec((bb, h, GDN_D, GDN_D), lambda i, j: (i * grid[1] + j, 0, 0, 0))
    in_specs = [
        pl.BlockSpec((1, tm, d), lambda i, j: (i, j, 0)),
        _full_spec((1, d)), _full_spec(win.shape), _full_spec(wabt.shape), _full_spec(convw.shape),
        _full_spec((1, h)), _full_spec((1, h)), _full_spec((h, 1)), _full_spec((h, 1)),
        _full_spec((1, GDN_D)), _full_spec(wout.shape),
    ] + [_full_spec(r.shape) for r in sample_rows] + [state_spec]
    out_specs = [
        pl.BlockSpec((1, tm, d), lambda i, j: (i, j, 0)),
        pl.BlockSpec((1, CONV_W - 1, QKV_W), lambda i, j: (i, 0, 0)),
        pl.BlockSpec((1, h, GDN_D, GDN_D), lambda i, j: (i, 0, 0, 0)),
        row_spec(h * GDN_D), state_spec,
    ]
    out_shape = [
        jax.ShapeDtypeStruct((b, l, d), F32),
        jax.ShapeDtypeStruct((b, CONV_W - 1, QKV_W), F32),
        jax.ShapeDtypeStruct((b, h, GDN_D, GDN_D), F32),
        jax.ShapeDtypeStruct((n, 1, h * GDN_D), F32),
        jax.ShapeDtypeStruct(sample_state.shape, F32),
    ]
    scratch = [
        pltpu.VMEM((tm + CARRY, QKV_W), F32),
        pltpu.VMEM((tm, h * GDN_D), F32),
        pltpu.VMEM((h, GDN_D, GDN_D), F32),
        pltpu.VMEM((h, tm, GDN_D), BF16),
        pltpu.VMEM((h, tm, GDN_D), BF16),
        pltpu.VMEM((h, tm, GDN_D), BF16),
        pltpu.VMEM((h, tm, GDN_D), F32),
        pltpu.VMEM((h, tm, GDN_D), BF16),
        pltpu.VMEM((h, tm, 2 * GDN_D), BF16),
        pltpu.VMEM((h, tm // 2, 2 * CHUNK), F32),
        pltpu.VMEM((h, tm // CHUNK, GDN_D), F32),
        pltpu.VMEM((h, tm, GDN_D), F32),
        pltpu.VMEM((h, tm // CHUNK, GDN_D + CHUNK, GDN_D), BF16),
        pltpu.VMEM((h, tm // CHUNK, GDN_D, GDN_D), F32),
    ]
    y, conv, ssm, o_s, state_s = pl.pallas_call(
        _gdn_prompt_kernel,
        grid=grid, in_specs=in_specs, out_specs=out_specs, out_shape=out_shape,
        scratch_shapes=scratch,
        compiler_params=pltpu.CompilerParams(
            dimension_semantics=("arbitrary", "arbitrary"), vmem_limit_bytes=VMEM_LIMIT_A),
        name="gdn_prompt",
    )(x, norm.reshape(1, d), win, wabt, convw,
      alog.reshape(1, h), dtb.reshape(1, h), alog.reshape(h, 1), dtb.reshape(h, 1),
      onorm.reshape(1, GDN_D), wout, *sample_rows, sample_state)
    return y, conv, ssm, o_s.reshape(n, h * GDN_D), state_s


def _gdn_sample_front_kernel(x_ref, norm_ref, win_ref, convw_ref, conv_ref, alog_r_ref, dtb_r_ref,
                             conv_new_ref, q_ref, k_ref, u_ref, w_ref, qd_ref, gate_ref, eg_ref):
    x = x_ref[...]
    xn = x * lax.rsqrt(jnp.mean(x * x, axis=-1, keepdims=True) + EPS) * norm_ref[...]
    xb = xn.astype(BF16)
    gate_w = GDN_HEADS * GDN_D
    pre = _dot(xb, win_ref[:, :QKV_W])
    old = [conv_ref[:, r * QKV_W:(r + 1) * QKV_W] for r in range(CONV_W - 1)]
    for r in range(CONV_W - 2):
        conv_new_ref[:, r * QKV_W:(r + 1) * QKV_W] = old[r + 1]
    conv_new_ref[:, (CONV_W - 2) * QKV_W:] = pre
    gate_ref[...] = _dot(xb, win_ref[:, QKV_W:QKV_W + gate_w])
    ab = _dot(xb, win_ref[:, QKV_W + gate_w:])
    g = -jnp.exp(alog_r_ref[...]) * _softplus(ab[:, :GDN_HEADS] + dtb_r_ref[...])
    beta = jax.nn.sigmoid(ab[:, GDN_HEADS:])
    e_g = jnp.exp(g)
    eg_ref[...] = e_g
    act = _silu(((old[0] * convw_ref[0:1, :] + old[1] * convw_ref[1:2, :])
                 + old[2] * convw_ref[2:3, :]) + pre * convw_ref[3:4, :])
    w_all = GDN_HEADS * GDN_D
    for h in range(GDN_HEADS):
        cols = slice(h * GDN_D, (h + 1) * GDN_D)
        qh = act[:, h * GDN_D:(h + 1) * GDN_D]
        kh = act[:, w_all + h * GDN_D:w_all + (h + 1) * GDN_D]
        vh = act[:, 2 * w_all + h * GDN_D:2 * w_all + (h + 1) * GDN_D]
        qn = qh * lax.rsqrt(jnp.sum(qh * qh, axis=-1, keepdims=True) + EPS) * (GDN_D ** -0.5)
        kn = kh * lax.rsqrt(jnp.sum(kh * kh, axis=-1, keepdims=True) + EPS)
        b_col = beta[:, h:h + 1]
        eg_col = e_g[:, h:h + 1]
        q_ref[:, cols] = qn
        k_ref[:, cols] = kn
        u_ref[:, cols] = vh * b_col
        w_ref[:, cols] = kn * b_col * eg_col
        qd_ref[:, cols] = qn * eg_col


def _out_proj_kernel(x_ref, o_ref, gate_ref, onorm_ref, wout_ref, y_ref, *, head_w, norm_heads):
    pieces = []
    for h in range(x_ref.shape[1] // head_w):
        cols = slice(h * head_w, (h + 1) * head_w)
        o = o_ref[:, cols]
        if norm_heads:
            o = o * lax.rsqrt(jnp.mean(o * o, axis=-1, keepdims=True) + EPS) * onorm_ref[...]
        pieces.append((o * _silu(gate_ref[:, cols])).astype(BF16))
    y_ref[...] = x_ref[...] + _dot(jnp.concatenate(pieces, axis=1), wout_ref[...])


def _gdn_sample_front(x, conv_state, norm, win, convw, alog, dtb):
    n, d = x.shape
    h = GDN_HEADS
    hw = h * GDN_D
    rows = CONV_W - 1
    f = lambda shape: jax.ShapeDtypeStruct(shape, F32)
    conv_new, q, k, u, w, qd, gate, eg = pl.pallas_call(
        _gdn_sample_front_kernel,
        out_shape=[f((n, rows * QKV_W)), f((n, hw)), f((n, hw)), f((n, hw)), f((n, hw)), f((n, hw)),
                   f((n, hw)), f((n, h))],
        compiler_params=pltpu.CompilerParams(vmem_limit_bytes=VMEM_LIMIT),
        name="gdn_sample_front",
    )(x, norm.reshape(1, d), win, convw, conv_state.reshape(n, rows * QKV_W),
      alog.reshape(1, h), dtb.reshape(1, h))
    return (q, k, u, w, qd, eg), gate, conv_new.reshape(n, rows, QKV_W)


def _gdn_sample_out(x, o, gate, onorm, wout):
    return pl.pallas_call(
        functools.partial(_out_proj_kernel, head_w=GDN_D, norm_heads=True),
        out_shape=jax.ShapeDtypeStruct(x.shape, F32),
        name="gdn_sample_out",
    )(x, o, gate, onorm.reshape(1, GDN_D), wout)


def _head_rms(x, gain_row, n_heads):
    pieces = []
    for h in range(n_heads):
        xh = x[:, h * HEAD_DIM:(h + 1) * HEAD_DIM]
        pieces.append(xh * lax.rsqrt(jnp.mean(xh * xh, axis=-1, keepdims=True) + EPS) * gain_row)
    return pieces


def _pair_rms(x, gain2, left):
    sq = x * x
    lo = jnp.sum(jnp.where(left, sq, 0.0), axis=-1, keepdims=True)
    hi = jnp.sum(jnp.where(left, 0.0, sq), axis=-1, keepdims=True)
    ms = jnp.where(left, lo, hi) * (1.0 / HEAD_DIM)
    return x * lax.rsqrt(ms + EPS) * gain2


def _swa_prompt_kernel(y_ref, kvnorm_ref, wkv_ref, knorm2_ref, normb_ref, win_ref, qnorm2_ref,
                       sinks_ref, wout_ref,
                       sq_ref, skn_ref, svn_ref, skc_ref, svc_ref, ssinks_ref, sslopes_ref,
                       out_ref, kwin_ref, vwin_ref, so_ref, skwin_ref, svwin_ref,
                       k_scr, v_scr, o_scr):
    tq = y_ref.shape[1]
    w = WINDOW
    step = pl.program_id(1)
    lane = lax.broadcasted_iota(jnp.int32, (1, 2 * HEAD_DIM), 1)
    left = lane < HEAD_DIM

    @pl.when(step == 0)
    def _():
        k_scr[:, 0:w, :] = jnp.zeros((N_KV_HEADS, w, 2 * HEAD_DIM), BF16)
        v_scr[:, 0:w, :] = jnp.zeros((N_KV_HEADS, w, 2 * HEAD_DIM), BF16)

    @pl.when(step > 0)
    def _():
        k_scr[:, 0:w, :] = k_scr[:, tq:tq + w, :]
        v_scr[:, 0:w, :] = v_scr[:, tq:tq + w, :]

    _sample_attention(sq_ref, skn_ref, svn_ref, skc_ref, svc_ref, ssinks_ref, sslopes_ref,
                      so_ref, skwin_ref, svwin_ref)

    y = y_ref[0]
    yn = y * lax.rsqrt(jnp.mean(y * y, axis=-1, keepdims=True) + EPS)
    kv = _dot((yn * kvnorm_ref[...]).astype(BF16), wkv_ref[...])
    for c in range(N_KV_HEADS // 2):
        cols = slice(c * 2 * HEAD_DIM, (c + 1) * 2 * HEAD_DIM)
        kp = _pair_rms(kv[:, cols], knorm2_ref[...], left)
        vp = kv[:, KV_W + c * 2 * HEAD_DIM:KV_W + (c + 1) * 2 * HEAD_DIM]
        kwin_ref[0, :, cols] = kp[tq - w:, :]
        vwin_ref[0, :, cols] = vp[tq - w:, :]
        kr = pltpu.roll(kp, HEAD_DIM, 1)
        vr = pltpu.roll(vp, HEAD_DIM, 1)
        k_scr[2 * c, w:w + tq, :] = jnp.where(left, kp, kr).astype(BF16)
        k_scr[2 * c + 1, w:w + tq, :] = jnp.where(left, kr, kp).astype(BF16)
        v_scr[2 * c, w:w + tq, :] = jnp.where(left, vp, vr).astype(BF16)
        v_scr[2 * c + 1, w:w + tq, :] = jnp.where(left, vr, vp).astype(BF16)

    qg = _dot((yn * normb_ref[...]).astype(BF16), win_ref[...])
    q_pairs = [_pair_rms(qg[:, c * 2 * HEAD_DIM:(c + 1) * 2 * HEAD_DIM], qnorm2_ref[...], left)
               for c in range(N_Q_HEADS // 2)]

    kj = lax.broadcasted_iota(jnp.int32, (2 * w, w), 0)
    qi = lax.broadcasted_iota(jnp.int32, (2 * w, w), 1)
    dist = qi - kj + w
    band = (dist >= 0) & (dist <= w)
    j_rel = (kj - w).astype(F32)
    i_row = lax.broadcasted_iota(jnp.int32, (1, w), 1).astype(F32)

    def scores(blk, hk):
        q_rows = slice(blk * w, (blk + 1) * w)
        lhs = []
        for c in (2 * hk, 2 * hk + 1):
            qp = q_pairs[c][q_rows, :]
            lhs.append(jnp.where(left, qp, 0.0))
            lhs.append(jnp.where(left, 0.0, qp))
        return _dot_nt(k_scr[hk, blk * w:blk * w + 2 * w, :],
                       jnp.concatenate(lhs, axis=0).astype(BF16))

    def attend(blk, hk, s4):
        q_rows = slice(blk * w, (blk + 1) * w)
        valid = band & ((step > 0) | (kj >= w)) if blk == 0 else band
        ps = []
        for g in range(Q_GROUP):
            hq = hk * Q_GROUP + g
            slope = 2.0 ** (-8.0 * (hq + 1) / N_Q_HEADS)
            a = jnp.where(valid, s4[:, g * w:(g + 1) * w] + slope * j_rel, -jnp.inf)
            sink = sinks_ref[0:1, hq:hq + 1] + slope * i_row
            mx = jnp.maximum(jnp.max(a, axis=0, keepdims=True), sink)
            p = jnp.exp(a - mx)
            inv = 1.0 / (jnp.sum(p, axis=0, keepdims=True) + jnp.exp(sink - mx))
            ps.append((p * inv).astype(BF16))
        o4 = _dot_tn(jnp.concatenate(ps, axis=1),
                     v_scr[hk, blk * w:blk * w + 2 * w, :])
        for j, c in enumerate((2 * hk, 2 * hk + 1)):
            o_scr[q_rows, c * 2 * HEAD_DIM:(c + 1) * 2 * HEAD_DIM] = jnp.where(
                left, o4[(2 * j) * w:(2 * j + 1) * w, :], o4[(2 * j + 1) * w:(2 * j + 2) * w, :])

    items = [(blk, hk) for blk in range(tq // w) for hk in range(N_KV_HEADS)]
    s_next = scores(*items[0])
    for n, item in enumerate(items):
        s_cur = s_next
        if n + 1 < len(items):
            s_next = scores(*items[n + 1])
        attend(*item, s_cur)

    o = (o_scr[...] * _silu(qg[:, ATT_W:])).astype(BF16)
    out_ref[0] = y + _dot(o, wout_ref[...])


def _swa_prompt(y, kvnorm, wkv, knorm, normb, win, qnorm, sinks, wout, sample_q, sample_kn, sample_vn,
                cache_k, cache_v):
    b, l, d = y.shape
    tq = TQ_B
    w = WINDOW
    steps = l // tq
    n = cache_k.shape[0]
    bb = n // (b * steps)
    assert bb * b * steps == n
    f = lambda shape: jax.ShapeDtypeStruct(shape, F32)
    knorm2 = jnp.concatenate([knorm, knorm]).reshape(1, 2 * HEAD_DIM)
    qnorm2 = (jnp.concatenate([qnorm, qnorm]) * (HEAD_DIM ** -0.5)).reshape(1, 2 * HEAD_DIM)
    slopes = (2.0 ** (-8.0 * jnp.arange(1, N_Q_HEADS + 1, dtype=F32) / N_Q_HEADS)).reshape(N_Q_HEADS, 1)
    spec3 = lambda r, c: pl.BlockSpec((bb, r, c), lambda i, j: (i * steps + j, 0, 0))
    in_specs = [
        pl.BlockSpec((1, tq, d), lambda i, j: (i, j, 0)),
        _full_spec((1, d)), _full_spec(wkv.shape), _full_spec((1, 2 * HEAD_DIM)), _full_spec((1, d)),
        _full_spec(win.shape), _full_spec((1, 2 * HEAD_DIM)), _full_spec((1, N_Q_HEADS)),
        _full_spec(wout.shape),
        spec3(N_Q_HEADS, HEAD_DIM), spec3(1, KV_W), spec3(1, KV_W), spec3(w, KV_W), spec3(w, KV_W),
        _full_spec((N_Q_HEADS, 1)), _full_spec((N_Q_HEADS, 1)),
    ]
    out_specs = [
        pl.BlockSpec((1, tq, d), lambda i, j: (i, j, 0)),
        pl.BlockSpec((1, w, KV_W), lambda i, j: (i, 0, 0)),
        pl.BlockSpec((1, w, KV_W), lambda i, j: (i, 0, 0)),
        spec3(N_Q_HEADS, HEAD_DIM), spec3(w, KV_W), spec3(w, KV_W),
    ]
    out, kwin, vwin, o_s, kwin_s, vwin_s = pl.pallas_call(
        _swa_prompt_kernel,
        grid=(b, steps), in_specs=in_specs, out_specs=out_specs,
        out_shape=[f((b, l, d)), f((b, w, KV_W)), f((b, w, KV_W)),
                   f((n, N_Q_HEADS, HEAD_DIM)), f((n, w, KV_W)), f((n, w, KV_W))],
        scratch_shapes=[pltpu.VMEM((N_KV_HEADS, w + tq, 2 * HEAD_DIM), BF16),
                        pltpu.VMEM((N_KV_HEADS, w + tq, 2 * HEAD_DIM), BF16),
                        pltpu.VMEM((tq, ATT_W), F32)],
        compiler_params=pltpu.CompilerParams(
            dimension_semantics=("arbitrary", "arbitrary"), vmem_limit_bytes=VMEM_LIMIT),
        name="swa_prompt",
    )(y, kvnorm.reshape(1, d), wkv, knorm2, normb.reshape(1, d), win, qnorm2,
      sinks.reshape(1, N_Q_HEADS), wout,
      sample_q.reshape(n, N_Q_HEADS, HEAD_DIM), sample_kn.reshape(n, 1, KV_W), sample_vn.reshape(n, 1, KV_W),
      cache_k, cache_v, sinks.reshape(N_Q_HEADS, 1), slopes)
    return out, kwin, vwin, o_s.reshape(n, ATT_W), kwin_s, vwin_s


def _swa_sample_front_kernel(y_ref, kvnorm_ref, wkv_ref, knorm_ref, normb_ref, win_ref, qnorm_ref,
                             k_ref, v_ref, q_ref, gate_ref):
    y = y_ref[...]
    yn = y * lax.rsqrt(jnp.mean(y * y, axis=-1, keepdims=True) + EPS)
    kv = _dot((yn * kvnorm_ref[...]).astype(BF16), wkv_ref[...])
    k_ref[...] = jnp.concatenate(_head_rms(kv[:, :KV_W], knorm_ref[...], N_KV_HEADS), axis=1)
    v_ref[...] = kv[:, KV_W:]
    qg = _dot((yn * normb_ref[...]).astype(BF16), win_ref[...])
    q_ref[...] = jnp.concatenate(
        _head_rms(qg[:, :ATT_W], qnorm_ref[...] * (HEAD_DIM ** -0.5), N_Q_HEADS), axis=1)
    gate_ref[...] = qg[:, ATT_W:]


def _sample_attention(q_ref, kn_ref, vn_ref, kc_ref, vc_ref, sinks_ref, slopes_ref, o_ref, kwin_ref, vwin_ref):
    bb = kc_ref.shape[0]
    w = kc_ref.shape[1]
    hrow = lax.broadcasted_iota(jnp.int32, (N_Q_HEADS, KV_W), 0) // Q_GROUP
    lblk = lax.broadcasted_iota(jnp.int32, (N_Q_HEADS, KV_W), 1) // HEAD_DIM
    own = hrow == lblk
    dist_c = (w - lax.broadcasted_iota(jnp.int32, (1, w), 1)).astype(F32)
    krow = lax.broadcasted_iota(jnp.int32, (w, KV_W), 0)
    slopes = slopes_ref[...]
    sink = sinks_ref[...]
    qms, scs = [], []
    for b in range(bb):
        q = q_ref[b]
        qm = jnp.where(own, jnp.concatenate([q] * N_KV_HEADS, axis=1), 0.0).astype(BF16)
        qms.append(qm)
        scs.append(_dot_nt(qm, kc_ref[b].astype(BF16)))
    pcs, pns = [], []
    for b in range(bb):
        s_c = scs[b] - slopes * dist_c
        s_n = jnp.sum(qms[b].astype(F32) * kn_ref[b].astype(BF16).astype(F32), axis=-1, keepdims=True)
        mx = jnp.maximum(jnp.maximum(jnp.max(s_c, axis=-1, keepdims=True), s_n), sink)
        p_c = jnp.exp(s_c - mx)
        p_n = jnp.exp(s_n - mx)
        den = jnp.sum(p_c, axis=-1, keepdims=True) + p_n + jnp.exp(sink - mx)
        pcs.append((p_c / den).astype(BF16))
        pns.append(p_n / den)
    rs = [_dot(pcs[b], vc_ref[b].astype(BF16)) for b in range(bb)]
    for b in range(bb):
        kn = kn_ref[b]
        vn = vn_ref[b]
        r = rs[b] + pns[b].astype(BF16).astype(F32) * vn.astype(BF16).astype(F32)
        r = jnp.where(own, r, 0.0)
        acc = r[:, 0:HEAD_DIM]
        for blk in range(1, N_KV_HEADS):
            acc = acc + r[:, blk * HEAD_DIM:(blk + 1) * HEAD_DIM]
        o_ref[b] = acc
        kwin_ref[b] = jnp.where(krow == w - 1, kn, pltpu.roll(kc_ref[b], w - 1, 0))
        vwin_ref[b] = jnp.where(krow == w - 1, vn, pltpu.roll(vc_ref[b], w - 1, 0))


def _swa_sample_front(y, kvnorm, wkv, knorm, normb, win, qnorm):
    n, d = y.shape
    f = lambda shape: jax.ShapeDtypeStruct(shape, F32)
    return pl.pallas_call(
        _swa_sample_front_kernel,
        out_shape=[f((n, KV_W)), f((n, KV_W)), f((n, ATT_W)), f((n, ATT_W))],
        compiler_params=pltpu.CompilerParams(vmem_limit_bytes=VMEM_LIMIT),
        name="swa_sample_front",
    )(y, kvnorm.reshape(1, d), wkv, knorm.reshape(1, HEAD_DIM), normb.reshape(1, d), win,
      qnorm.reshape(1, HEAD_DIM))


def _swa_sample_out(y, o, gate, wout):
    return pl.pallas_call(
        functools.partial(_out_proj_kernel, head_w=ATT_W, norm_heads=False),
        out_shape=jax.ShapeDtypeStruct(y.shape, F32),
        name="swa_sample_out",
    )(y, o, gate, jnp.ones((1, ATT_W), F32), wout)


def kernel(x_prompt, x_sample, state_conv, state_ssm, cache_k_win, cache_v_win, norm_a, w_in_a, conv_w_a, a_log, dt_bias, o_norm_a, w_out_a, kv_norm, w_kv, k_norm, norm_b, w_in_b, q_norm, sinks, w_out_b):
    n_a = w_in_a.shape[0]
    n_b = w_in_b.shape[0]
    assert n_a == 1 and n_b == 1, "kernel is written for DEPTH == 2"
    bp, lp, d = x_prompt.shape
    n = x_sample.shape[0]
    hw = GDN_HEADS * GDN_D

    hp, hs = x_prompt, x_sample.reshape(n, d)
    conv_p, ssm_p, conv_s, ssm_s = [], [], [], []
    for layer in range(n_a):
        win_a = w_in_a[layer].astype(BF16)
        wabt = win_a[:, QKV_W + hw:].T
        wout = w_out_a[layer].astype(BF16)
        rows_s, gate_s, cbuf_s = _gdn_sample_front(hs, state_conv[layer], norm_a[layer], win_a,
                                                   conv_w_a[layer], a_log[layer], dt_bias[layer])
        hp, cbuf, st, o_s, st_s = _gdn_prompt(hp, norm_a[layer], win_a, wabt, conv_w_a[layer],
                                              a_log[layer], dt_bias[layer], o_norm_a[layer], wout,
                                              rows_s, state_ssm[layer])
        conv_p.append(cbuf)
        ssm_p.append(st)
        hs = _gdn_sample_out(hs, o_s, gate_s, o_norm_a[layer], wout)
        conv_s.append(cbuf_s)
        ssm_s.append(st_s)

    wkv = w_kv.astype(BF16)
    win = w_in_b[0].astype(BF16)
    woutb = w_out_b[0].astype(BF16)
    kn_s, vn_s, q_s, gate_s = _swa_sample_front(hs, kv_norm, wkv, k_norm, norm_b[0], win, q_norm[0])
    hp, k_win_p, v_win_p, o_s, k_win_s, v_win_s = _swa_prompt(
        hp, kv_norm, wkv, k_norm, norm_b[0], win, q_norm[0], sinks[0], woutb, q_s, kn_s, vn_s,
        cache_k_win.reshape(n, WINDOW, KV_W), cache_v_win.reshape(n, WINDOW, KV_W))
    hs = _swa_sample_out(hs, o_s, gate_s, woutb)
    kv_shape = (N_KV_HEADS, HEAD_DIM)
    return (hp, hs.reshape(n, 1, d), jnp.stack(conv_p), jnp.stack(ssm_p),
            k_win_p.reshape(bp, WINDOW, *kv_shape), v_win_p.reshape(bp, WINDOW, *kv_shape),
            jnp.stack(conv_s), jnp.stack(ssm_s),
            k_win_s.reshape(n, WINDOW, *kv_shape), v_win_s.reshape(n, WINDOW, *kv_shape))
```

```python
import functools

import jax
import jax.numpy as jnp
from jax import lax
from jax.experimental import pallas as pl
from jax.experimental.pallas import tpu as pltpu

F32 = jnp.float32
BF16 = jnp.bfloat16
EPS = 1e-6

D_MODEL = 1024
GDN_HEADS = 8
GDN_D = 128
QKV_W = 3 * GDN_HEADS * GDN_D
CONV_W = 4
CHUNK = 64
N_Q_HEADS = 16
N_KV_HEADS = 4
Q_GROUP = N_Q_HEADS // N_KV_HEADS
HEAD_DIM = 64
KV_W = N_KV_HEADS * HEAD_DIM
ATT_W = N_Q_HEADS * HEAD_DIM
WINDOW = 128

TM_A = 512
SUB_A = 256
TQ_B = 1024
CARRY = 8
VMEM_LIMIT = 52 * 1024 * 1024
VMEM_LIMIT_A = 62 * 1024 * 1024

_NT = (((1,), (1,)), ((), ()))
_TN = (((0,), (0,)), ((), ()))


def _dot(a, b):
    return jnp.dot(a, b, preferred_element_type=F32)


def _dot_nt(a, b):
    return lax.dot_general(a, b, _NT, preferred_element_type=F32)


def _dot_tn(a, b):
    return lax.dot_general(a, b, _TN, preferred_element_type=F32)


def _split(x):
    hi = x.astype(BF16)
    lo = (x - hi.astype(F32)).astype(BF16)
    return hi, lo


def _dot_exact_lhs(a_bf, b):
    b0 = b.astype(BF16)
    r1 = b - b0.astype(F32)
    b1 = r1.astype(BF16)
    b2 = (r1 - b1.astype(F32)).astype(BF16)
    return (_dot(a_bf, b2) + _dot(a_bf, b1)) + _dot(a_bf, b0)


def _dot_exact_rhs(a, b_bf):
    a0 = a.astype(BF16)
    r1 = a - a0.astype(F32)
    a1 = r1.astype(BF16)
    a2 = (r1 - a1.astype(F32)).astype(BF16)
    return (_dot(a2, b_bf) + _dot(a1, b_bf)) + _dot(a0, b_bf)


def _silu(x):
    return x * jax.nn.sigmoid(x)


def _softplus(x):
    return jnp.maximum(x, 0.0) + jnp.log1p(jnp.exp(-jnp.abs(x)))


def _halves(x, left):
    zero = jnp.zeros_like(x)
    return jnp.where(left, x, zero), jnp.where(left, zero, x)


def _block_diag(x, left):
    return jnp.concatenate(_halves(x, left), axis=0)


def _pair_split_product(a, b, left):
    ah, al = _split(a)
    bh, bl = _split(b)
    bh1, bh2 = _halves(bh, left)
    bl1, bl2 = _halves(bl, left)
    zero = jnp.zeros_like(bh)
    rhs = jnp.concatenate([jnp.concatenate([bh1, bl1], axis=1), jnp.concatenate([bh2, bl2], axis=1),
                           jnp.concatenate([bh1, zero], axis=1), jnp.concatenate([bh2, zero], axis=1)], axis=0)
    y = _dot(jnp.concatenate([ah, al], axis=1), rhs)
    half = y.shape[1] // 2
    return y[:, :half] + y[:, half:]


def _unit_lower_inverse_pairs(ms, left):
    n = ms[0].shape[0]
    row = lax.broadcasted_iota(jnp.int32, (n, 2 * n), 0)
    col = lax.broadcasted_iota(jnp.int32, (n, 2 * n), 1) % n
    eye2 = jnp.where(row == col, 1.0, 0.0).astype(F32)

    def times(a, b):
        return _dot(a.astype(BF16), _block_diag(b.astype(BF16), left))

    ts = [eye2 - m for m in ms]
    ps = [times(m, m) for m in ms]
    yield
    steps = max(1, (n - 1).bit_length()) - 1
    for _ in range(steps - 1):
        both = [_dot(jnp.concatenate([t.astype(BF16), p.astype(BF16)], axis=0), _block_diag(p.astype(BF16), left))
                for t, p in zip(ts, ps)]
        ts = [t + tp[:n, :] for t, tp in zip(ts, both)]
        ps = [tp[n:, :] for tp in both]
        yield
    ts = [t + times(t, p) for t, p in zip(ts, ps)]
    yield
    rs = [(eye2 - t) - _pair_split_product(m, t, left) for m, t in zip(ms, ts)]
    yield
    ts = [t + times(t, r) for t, r in zip(ts, rs)]
    yield
    return ts


def _full_spec(shape):
    nd = len(shape)
    return pl.BlockSpec(shape, lambda *_: (0,) * nd, pipeline_mode=pl.Buffered(1))


def _gdn_gates(ab, ab_t, alog_r, dtb_r, alog_c, dtb_c, tm, chunk):
    h = GDN_HEADS
    g_c = -jnp.exp(alog_r) * _softplus(ab[:, :h] + dtb_r)
    beta = jax.nn.sigmoid(ab[:, h:])
    g_r = -jnp.exp(alog_c) * _softplus(ab_t[:h, :] + dtb_c)
    row = lax.broadcasted_iota(jnp.int32, (tm, tm), 0)
    col = lax.broadcasted_iota(jnp.int32, (tm, tm), 1)
    same = (row // chunk) == (col // chunk)
    lower = jnp.where(same & (row >= col), 1.0, 0.0).astype(BF16)
    upper = jnp.where(same & (row <= col), 1.0, 0.0).astype(BF16)
    gc = _dot_exact_lhs(lower, g_c)
    gr = _dot_exact_rhs(g_r, upper)
    return beta, gc, gr


def _sample_state_update(row0, q_ref, k_ref, u_ref, w_ref, qd_ref, eg_ref, s_ref, o_ref, s_out_ref):
    bb = s_ref.shape[0]
    group = 8
    assert group % bb == 0
    base = pl.multiple_of((row0 // group) * group, group)
    part = (row0 - base) // bb

    def rows(ref, cols):
        x = ref[pl.ds(base, group), cols]
        out = x[0:bb, :]
        for p in range(1, group // bb):
            out = jnp.where(part == p, x[p * bb:(p + 1) * bb, :], out)
        return out

    row = lax.broadcasted_iota(jnp.int32, (8, GDN_D), 0)
    decay_all = rows(eg_ref, slice(None))
    prods, ins = [], []
    for h in range(GDN_HEADS):
        cols = slice(h * GDN_D, (h + 1) * GDN_D)
        q, k, u, w, qd = (rows(r, cols) for r in (q_ref, k_ref, u_ref, w_ref, qd_ref))
        for b in range(bb):
            lhs = jnp.where(row == 0, w[b:b + 1, :], jnp.where(row == 1, qd[b:b + 1, :], 0.0))
            prods.append(_dot(lhs.astype(BF16), s_ref[b, h].astype(BF16)))
            ins.append((b, h, cols, q[b:b + 1, :], k[b:b + 1, :], u[b:b + 1, :]))
    for (b, h, cols, q, k, u), prod in zip(ins, prods):
        q = q.astype(BF16).astype(F32)
        k = k.astype(BF16).astype(F32)
        v_new = u - prod[0:1, :]
        vb = v_new.astype(BF16).astype(F32)
        qk = jnp.sum(q * k, axis=-1, keepdims=True)
        o_ref[b, :, cols] = prod[1:2, :] + qk.astype(BF16).astype(F32) * vb
        k8 = jnp.where(row == 0, k, 0.0).astype(BF16)
        v8 = jnp.where(row == 0, vb, 0.0).astype(BF16)
        s_out_ref[b, h] = s_ref[b, h] * decay_all[b:b + 1, h:h + 1] + _dot_tn(k8, v8)


def _gdn_prompt_kernel(x_ref, norm_ref, win_ref, wabt_ref, convw_ref,
                       alog_r_ref, dtb_r_ref, alog_c_ref, dtb_c_ref, onorm_ref, wout_ref,
                       sq_ref, sk_ref, su_ref, sw_ref, sqd_ref, seg_ref, sstate_ref,
                       y_ref, conv_ref, ssm_ref, so_ref, sstate_out_ref,
                       pre_scr, gate_scr, s_scr, k_scr, kb_scr, q_scr, qd_scr, kd_scr,
                       rhs_scr, dec_scr, gt_scr, o_scr, lhs_scr, su_scr):
    tm = x_ref.shape[1]
    sub = SUB_A
    n_sub = tm // sub
    n_chunks = sub // CHUNK
    gate_w = GDN_HEADS * GDN_D
    l = pl.program_id(1)

    @pl.when(l == 0)
    def _():
        pre_scr[0:CARRY, :] = jnp.zeros((CARRY, QKV_W), F32)
        s_scr[...] = jnp.zeros(s_scr.shape, F32)

    ci = lax.broadcasted_iota(jnp.int32, (CHUNK, 2 * CHUNK), 0)
    cj = lax.broadcasted_iota(jnp.int32, (CHUNK, 2 * CHUNK), 1) % CHUNK
    causal = ci >= cj
    strict = ci > cj
    left = lax.broadcasted_iota(jnp.int32, (1, 2 * CHUNK), 1) < CHUNK
    heads = list(range(GDN_HEADS))

    def start(s):
        x = x_ref[0, s * sub:(s + 1) * sub, :]
        xn = x * lax.rsqrt(jnp.mean(x * x, axis=-1, keepdims=True) + EPS) * norm_ref[...]
        xb = xn.astype(BF16)
        ab = _dot(xb, win_ref[:, QKV_W + gate_w:])
        ab_t = _dot_nt(wabt_ref[...], xb)
        beta, gc, gr = _gdn_gates(ab, ab_t, alog_r_ref[...], dtb_r_ref[...],
                                  alog_c_ref[...], dtb_c_ref[...], sub, CHUNK)
        g_last = jnp.concatenate(
            [jnp.broadcast_to(gc[c * CHUNK + CHUNK - 1:c * CHUNK + CHUNK, :], (CHUNK, GDN_HEADS))
             for c in range(n_chunks)], axis=0)
        return dict(x=x, xb=xb, beta=beta, gc=gc, gr=gr, e_g=jnp.exp(gc),
                    e_kd=jnp.exp(g_last - gc), e_tot=jnp.exp(g_last))

    def project(s, st):
        for part in range(3):
            cols = slice(part * gate_w, (part + 1) * gate_w)
            pre_scr[CARRY + s * sub:CARRY + (s + 1) * sub, cols] = _dot(st["xb"], win_ref[:, cols])

    def project_gate(s, st):
        gate_scr[s * sub:(s + 1) * sub, :] = _dot(st["xb"], win_ref[:, QKV_W:QKV_W + gate_w])

    def conv_act(s, j):
        cols = slice(j * GDN_D, (j + 1) * GDN_D)
        ext = pre_scr[s * sub:s * sub + CARRY + sub, cols]
        half = ext[CARRY:, :] * (0.5 * convw_ref[CONV_W - 1:CONV_W, cols])
        for back in range(1, CONV_W):
            tap = CONV_W - 1 - back
            half = half + pltpu.roll(ext, back, 0)[CARRY:, :] * (0.5 * convw_ref[tap:tap + 1, cols])
        return half + half * jnp.tanh(half)

    def head_prep(s, st, h):
        rows = slice(s * sub, (s + 1) * sub)
        qh = conv_act(s, h)
        kh = conv_act(s, GDN_HEADS + h)
        vh = conv_act(s, 2 * GDN_HEADS + h)
        qn = qh * lax.rsqrt(jnp.sum(qh * qh, axis=-1, keepdims=True) + EPS) * (GDN_D ** -0.5)
        kn = kh * lax.rsqrt(jnp.sum(kh * kh, axis=-1, keepdims=True) + EPS)
        b_col = st["beta"][:, h:h + 1]
        eg_col = st["e_g"][:, h:h + 1]
        kb = kn * b_col
        k_scr[h, rows, :] = kn.astype(BF16)
        kb_scr[h, rows, :] = kb.astype(BF16)
        q_scr[h, rows, :] = qn.astype(BF16)
        qd_scr[h, rows, :] = qn * eg_col
        kd_scr[h, rows, :] = (kn * st["e_kd"][:, h:h + 1]).astype(BF16)
        rhs_scr[h, rows, 0:GDN_D] = (vh * b_col).astype(BF16)
        rhs_scr[h, rows, GDN_D:2 * GDN_D] = (kb * eg_col).astype(BF16)
        gc, gr = st["gc"], st["gr"]
        for c in range(n_chunks):
            gt_scr[h, s * n_chunks + c:s * n_chunks + c + 1, :] = jnp.broadcast_to(
                st["e_tot"][c * CHUNK:c * CHUNK + 1, h:h + 1], (1, GDN_D))
        for j in range(n_chunks // 2):
            r1 = slice(2 * j * CHUNK, (2 * j + 1) * CHUNK)
            r2 = slice((2 * j + 1) * CHUNK, (2 * j + 2) * CHUNK)
            diff = jnp.where(left, gc[r1, h:h + 1], gc[r2, h:h + 1]) - gr[h:h + 1, 2 * j * CHUNK:(2 * j + 2) * CHUNK]
            pair = s * (n_chunks // 2) + j
            dec_scr[h, pair * CHUNK:(pair + 1) * CHUNK, :] = jnp.exp(jnp.where(causal, diff, -jnp.inf))

    def chunk_terms(s):
        items = [(h, s * (n_chunks // 2) + j) for h in heads for j in range(n_chunks // 2)]
        aqs = []
        for h, pair in items:
            r12 = slice(2 * pair * CHUNK, (2 * pair + 2) * CHUNK)
            aqs.append(_dot_nt(jnp.concatenate([kb_scr[h, r12, :], q_scr[h, r12, :]], axis=0), k_scr[h, r12, :]))
        yield
        decs = [dec_scr[h, pair * CHUNK:(pair + 1) * CHUNK, :] for h, pair in items]
        ms = [jnp.where(strict, jnp.where(left, aq[:CHUNK, :], aq[CHUNK:2 * CHUNK, :]) * d, 0.0)
              for aq, d in zip(aqs, decs)]
        qks = [jnp.where(causal, jnp.where(left, aq[2 * CHUNK:3 * CHUNK, :], aq[3 * CHUNK:, :]) * d, 0.0)
               for aq, d in zip(aqs, decs)]
        ts = yield from _unit_lower_inverse_pairs(ms, left)
        sols = []
        for (h, pair), t, qk in zip(items, ts, qks):
            lo = t - t.astype(BF16).astype(F32)
            t_parts = (jnp.where(left, t, pltpu.roll(lo, CHUNK, 1)).astype(BF16),
                       jnp.where(left, pltpu.roll(t, CHUNK, 1), lo).astype(BF16))
            qk_parts = (qk[:, :CHUNK].astype(BF16), pltpu.roll(qk, CHUNK, 1)[:, :CHUNK].astype(BF16))
            for half in range(2):
                c = 2 * pair + half
                r = slice(c * CHUNK, (c + 1) * CHUNK)
                rhs = rhs_scr[h, r, :]
                sols.append((h, c, r, qk_parts[half],
                             _dot(t_parts[half], jnp.concatenate([rhs, rhs], axis=0)).astype(BF16)))
        yield
        outs = [(h, c, r, _dot_tn(kd_scr[h, r, :], sol),
                 _dot(qk_c, sol))
                for h, c, r, qk_c, sol in sols]
        for h, c, r, kd_uw, qk_uw in outs:
            lhs_scr[h, c, 0:GDN_D, :] = kd_uw[:, GDN_D:].astype(BF16)
            lhs_scr[h, c, GDN_D:GDN_D + CHUNK, :] = (qd_scr[h, r, :] - qk_uw[:, GDN_D:]).astype(BF16)
            su_scr[h, c] = kd_uw[:, :GDN_D]
            o_scr[h, r, :] = qk_uw[:, :GDN_D]
        yield

    def state_updates(s):
        for c in range(s * n_chunks, (s + 1) * n_chunks):
            rows = slice(c * CHUNK, (c + 1) * CHUNK)
            for h in heads:
                st_h = s_scr[h]
                prod = _dot(lhs_scr[h, c], st_h.astype(BF16))
                o_scr[h, rows, :] = o_scr[h, rows, :] + prod[GDN_D:, :]
                s_scr[h] = st_h * gt_scr[h, c:c + 1, :] + (su_scr[h, c] - prod[:GDN_D, :])
            yield

    def finish(s, st):
        rows = slice(s * sub, (s + 1) * sub)
        pieces = []
        for h in heads:
            o = o_scr[h, rows, :]
            on = o * lax.rsqrt(jnp.mean(o * o, axis=-1, keepdims=True) + EPS) * onorm_ref[...]
            pieces.append((on * _silu(gate_scr[rows, h * GDN_D:(h + 1) * GDN_D])).astype(BF16))
        on_all = jnp.concatenate(pieces, axis=1)
        for piece in range(D_MODEL // (2 * GDN_D)):
            cols = slice(piece * 2 * GDN_D, (piece + 1) * 2 * GDN_D)
            y_ref[0, rows, cols] = st["x"][:, cols] + _dot(on_all, wout_ref[:, cols])
            yield

    def run(gen):
        for _ in gen:
            pass

    def interleave(main, side, every):
        for n, _ in enumerate(main):
            if (n + 1) % every == 0:
                next(side, None)
        run(side)

    sts = [start(0)]
    project(0, sts[0])
    for s in range(n_sub):
        if s + 1 < n_sub:
            sts.append(start(s + 1))
        for h in heads:
            head_prep(s, sts[s], h)
        if s + 1 < n_sub:
            project(s + 1, sts[s + 1])
        project_gate(s, sts[s])
        if s == 0:
            run(chunk_terms(s))
        else:
            interleave(chunk_terms(s), state_updates(s - 1), 2)
            if s >= 2:
                run(finish(s - 2, sts[s - 2]))
    conv_ref[0] = pre_scr[tm + CARRY - 3:tm + CARRY, :]
    pre_scr[0:CARRY, :] = pre_scr[tm:tm + CARRY, :]
    sample_row0 = (pl.program_id(0) * pl.num_programs(1) + l) * sstate_ref.shape[0]
    _sample_state_update(sample_row0, sq_ref, sk_ref, su_ref, sw_ref, sqd_ref, seg_ref, sstate_ref,
                         so_ref, sstate_out_ref)
    if n_sub >= 2:
        interleave(state_updates(n_sub - 1), finish(n_sub - 2, sts[n_sub - 2]), 1)
    else:
        run(state_updates(0))

    @pl.when(l == pl.num_programs(1) - 1)
    def _():
        ssm_ref[0] = s_scr[...]

    run(finish(n_sub - 1, sts[n_sub - 1]))


def _gdn_prompt(x, norm, win, wabt, convw, alog, dtb, onorm, wout, sample_rows, sample_state):
    b, l, d = x.shape
    tm = TM_A
    h = GDN_HEADS
    grid = (b, l // tm)
    n = sample_state.shape[0]
    bb = n // (grid[0] * grid[1])
    assert bb * grid[0] * grid[1] == n
    sample_block = lambda i, j: (i * grid[1] + j, 0, 0)
    row_spec = lambda width: pl.BlockSpec((bb, 1, width), sample_block)
    state_spec = pl.BlockSpec((bb, h, GDN_D, GDN_D), lambda i, j: (i * grid[1] + j, 0, 0, 0))
    in_specs = [
        pl.BlockSpec((1, tm, d), lambda i, j: (i, j, 0)),
        _full_spec((1, d)), _full_spec(win.shape), _full_spec(wabt.shape), _full_spec(convw.shape),
        _full_spec((1, h)), _full_spec((1, h)), _full_spec((h, 1)), _full_spec((h, 1)),
        _full_spec((1, GDN_D)), _full_spec(wout.shape),
    ] + [_full_spec(r.shape) for r in sample_rows] + [state_spec]
    out_specs = [
        pl.BlockSpec((1, tm, d), lambda i, j: (i, j, 0)),
        pl.BlockSpec((1, CONV_W - 1, QKV_W), lambda i, j: (i, 0, 0)),
        pl.BlockSpec((1, h, GDN_D, GDN_D), lambda i, j: (i, 0, 0, 0)),
        row_spec(h * GDN_D), state_spec,
    ]
    out_shape = [
        jax.ShapeDtypeStruct((b, l, d), F32),
        jax.ShapeDtypeStruct((b, CONV_W - 1, QKV_W), F32),
        jax.ShapeDtypeStruct((b, h, GDN_D, GDN_D), F32),
        jax.ShapeDtypeStruct((n, 1, h * GDN_D), F32),
        jax.ShapeDtypeStruct(sample_state.shape, F32),
    ]
    scratch = [
        pltpu.VMEM((tm + CARRY, QKV_W), F32),
        pltpu.VMEM((tm, h * GDN_D), F32),
        pltpu.VMEM((h, GDN_D, GDN_D), F32),
        pltpu.VMEM((h, tm, GDN_D), BF16),
        pltpu.VMEM((h, tm, GDN_D), BF16),
        pltpu.VMEM((h, tm, GDN_D), BF16),
        pltpu.VMEM((h, tm, GDN_D), F32),
        pltpu.VMEM((h, tm, GDN_D), BF16),
        pltpu.VMEM((h, tm, 2 * GDN_D), BF16),
        pltpu.VMEM((h, tm // 2, 2 * CHUNK), F32),
        pltpu.VMEM((h, tm // CHUNK, GDN_D), F32),
        pltpu.VMEM((h, tm, GDN_D), F32),
        pltpu.VMEM((h, tm // CHUNK, GDN_D + CHUNK, GDN_D), BF16),
        pltpu.VMEM((h, tm // CHUNK, GDN_D, GDN_D), F32),
    ]
    y, conv, ssm, o_s, state_s = pl.pallas_call(
        _gdn_prompt_kernel,
        grid=grid, in_specs=in_specs, out_specs=out_specs, out_shape=out_shape,
        scratch_shapes=scratch,
        compiler_params=pltpu.CompilerParams(
            dimension_semantics=("arbitrary", "arbitrary"), vmem_limit_bytes=VMEM_LIMIT_A),
        name="gdn_prompt",
    )(x, norm.reshape(1, d), win, wabt, convw,
      alog.reshape(1, h), dtb.reshape(1, h), alog.reshape(h, 1), dtb.reshape(h, 1),
      onorm.reshape(1, GDN_D), wout, *sample_rows, sample_state)
    return y, conv, ssm, o_s.reshape(n, h * GDN_D), state_s


def _gdn_sample_front_kernel(x_ref, norm_ref, win_ref, convw_ref,
                             c0_ref, c1_ref, c2_ref, alog_r_ref, dtb_r_ref,
                             pre_ref, q_ref, k_ref, u_ref, w_ref, qd_ref, gate_ref, eg_ref):
    x = x_ref[...]
    xn = x * lax.rsqrt(jnp.mean(x * x, axis=-1, keepdims=True) + EPS) * norm_ref[...]
    xb = xn.astype(BF16)
    gate_w = GDN_HEADS * GDN_D
    pre = _dot(xb, win_ref[:, :QKV_W])
    pre_ref[...] = pre
    gate_ref[...] = _dot(xb, win_ref[:, QKV_W:QKV_W + gate_w])
    ab = _dot(xb, win_ref[:, QKV_W + gate_w:])
    g = -jnp.exp(alog_r_ref[...]) * _softplus(ab[:, :GDN_HEADS] + dtb_r_ref[...])
    beta = jax.nn.sigmoid(ab[:, GDN_HEADS:])
    e_g = jnp.exp(g)
    eg_ref[...] = e_g
    act = _silu(((c0_ref[...] * convw_ref[0:1, :] + c1_ref[...] * convw_ref[1:2, :])
                 + c2_ref[...] * convw_ref[2:3, :]) + pre * convw_ref[3:4, :])
    w_all = GDN_HEADS * GDN_D
    for h in range(GDN_HEADS):
        cols = slice(h * GDN_D, (h + 1) * GDN_D)
        qh = act[:, h * GDN_D:(h + 1) * GDN_D]
        kh = act[:, w_all + h * GDN_D:w_all + (h + 1) * GDN_D]
        vh = act[:, 2 * w_all + h * GDN_D:2 * w_all + (h + 1) * GDN_D]
        qn = qh * lax.rsqrt(jnp.sum(qh * qh, axis=-1, keepdims=True) + EPS) * (GDN_D ** -0.5)
        kn = kh * lax.rsqrt(jnp.sum(kh * kh, axis=-1, keepdims=True) + EPS)
        b_col = beta[:, h:h + 1]
        eg_col = e_g[:, h:h + 1]
        q_ref[:, cols] = qn
        k_ref[:, cols] = kn
        u_ref[:, cols] = vh * b_col
        w_ref[:, cols] = kn * b_col * eg_col
        qd_ref[:, cols] = qn * eg_col


def _out_proj_kernel(x_ref, o_ref, gate_ref, onorm_ref, wout_ref, y_ref, *, head_w, norm_heads):
    pieces = []
    for h in range(x_ref.shape[1] // head_w):
        cols = slice(h * head_w, (h + 1) * head_w)
        o = o_ref[:, cols]
        if norm_heads:
            o = o * lax.rsqrt(jnp.mean(o * o, axis=-1, keepdims=True) + EPS) * onorm_ref[...]
        pieces.append((o * _silu(gate_ref[:, cols])).astype(BF16))
    y_ref[...] = x_ref[...] + _dot(jnp.concatenate(pieces, axis=1), wout_ref[...])


def _gdn_sample_front(x, conv_state, norm, win, convw, alog, dtb):
    n, d = x.shape
    h = GDN_HEADS
    hw = h * GDN_D
    conv_t = jnp.transpose(conv_state, (1, 0, 2))
    f = lambda shape: jax.ShapeDtypeStruct(shape, F32)
    pre, q, k, u, w, qd, gate, eg = pl.pallas_call(
        _gdn_sample_front_kernel,
        out_shape=[f((n, QKV_W)), f((n, hw)), f((n, hw)), f((n, hw)), f((n, hw)), f((n, hw)),
                   f((n, hw)), f((n, h))],
        compiler_params=pltpu.CompilerParams(vmem_limit_bytes=VMEM_LIMIT),
        name="gdn_sample_front",
    )(x, norm.reshape(1, d), win, convw, conv_t[0], conv_t[1], conv_t[2],
      alog.reshape(1, h), dtb.reshape(1, h))
    conv_new = jnp.stack([conv_t[1], conv_t[2], pre], axis=1)
    return (q, k, u, w, qd, eg), gate, conv_new


def _gdn_sample_out(x, o, gate, onorm, wout):
    return pl.pallas_call(
        functools.partial(_out_proj_kernel, head_w=GDN_D, norm_heads=True),
        out_shape=jax.ShapeDtypeStruct(x.shape, F32),
        name="gdn_sample_out",
    )(x, o, gate, onorm.reshape(1, GDN_D), wout)


def _head_rms(x, gain_row, n_heads):
    pieces = []
    for h in range(n_heads):
        xh = x[:, h * HEAD_DIM:(h + 1) * HEAD_DIM]
        pieces.append(xh * lax.rsqrt(jnp.mean(xh * xh, axis=-1, keepdims=True) + EPS) * gain_row)
    return pieces


def _pair_rms(x, gain2, left):
    sq = x * x
    lo = jnp.sum(jnp.where(left, sq, 0.0), axis=-1, keepdims=True)
    hi = jnp.sum(jnp.where(left, 0.0, sq), axis=-1, keepdims=True)
    ms = jnp.where(left, lo, hi) * (1.0 / HEAD_DIM)
    return x * lax.rsqrt(ms + EPS) * gain2


def _swa_prompt_kernel(y_ref, kvnorm_ref, wkv_ref, knorm2_ref, normb_ref, win_ref, qnorm2_ref,
                       sinks_ref, wout_ref,
                       sq_ref, skn_ref, svn_ref, skc_ref, svc_ref, ssinks_ref, sslopes_ref,
                       out_ref, kwin_ref, vwin_ref, so_ref, skwin_ref, svwin_ref,
                       k_scr, v_scr, o_scr):
    tq = y_ref.shape[1]
    w = WINDOW
    step = pl.program_id(1)
    lane = lax.broadcasted_iota(jnp.int32, (1, 2 * HEAD_DIM), 1)
    left = lane < HEAD_DIM

    @pl.when(step == 0)
    def _():
        k_scr[:, 0:w, :] = jnp.zeros((N_KV_HEADS, w, 2 * HEAD_DIM), BF16)
        v_scr[:, 0:w, :] = jnp.zeros((N_KV_HEADS, w, 2 * HEAD_DIM), BF16)

    @pl.when(step > 0)
    def _():
        k_scr[:, 0:w, :] = k_scr[:, tq:tq + w, :]
        v_scr[:, 0:w, :] = v_scr[:, tq:tq + w, :]

    y = y_ref[0]
    yn = y * lax.rsqrt(jnp.mean(y * y, axis=-1, keepdims=True) + EPS)
    kv = _dot((yn * kvnorm_ref[...]).astype(BF16), wkv_ref[...])
    for c in range(N_KV_HEADS // 2):
        cols = slice(c * 2 * HEAD_DIM, (c + 1) * 2 * HEAD_DIM)
        kp = _pair_rms(kv[:, cols], knorm2_ref[...], left)
        vp = kv[:, KV_W + c * 2 * HEAD_DIM:KV_W + (c + 1) * 2 * HEAD_DIM]
        kwin_ref[0, :, cols] = kp[tq - w:, :]
        vwin_ref[0, :, cols] = vp[tq - w:, :]
        kr = pltpu.roll(kp, HEAD_DIM, 1)
        vr = pltpu.roll(vp, HEAD_DIM, 1)
        k_scr[2 * c, w:w + tq, :] = jnp.where(left, kp, kr).astype(BF16)
        k_scr[2 * c + 1, w:w + tq, :] = jnp.where(left, kr, kp).astype(BF16)
        v_scr[2 * c, w:w + tq, :] = jnp.where(left, vp, vr).astype(BF16)
        v_scr[2 * c + 1, w:w + tq, :] = jnp.where(left, vr, vp).astype(BF16)

    qg = _dot((yn * normb_ref[...]).astype(BF16), win_ref[...])
    q_pairs = [_pair_rms(qg[:, c * 2 * HEAD_DIM:(c + 1) * 2 * HEAD_DIM], qnorm2_ref[...], left)
               for c in range(N_Q_HEADS // 2)]

    kj = lax.broadcasted_iota(jnp.int32, (2 * w, w), 0)
    qi = lax.broadcasted_iota(jnp.int32, (2 * w, w), 1)
    dist = qi - kj + w
    band = (dist >= 0) & (dist <= w)
    j_rel = (kj - w).astype(F32)
    i_row = lax.broadcasted_iota(jnp.int32, (1, w), 1).astype(F32)

    def scores(blk, hk):
        q_rows = slice(blk * w, (blk + 1) * w)
        lhs = []
        for c in (2 * hk, 2 * hk + 1):
            qp = q_pairs[c][q_rows, :]
            lhs.append(jnp.where(left, qp, 0.0))
            lhs.append(jnp.where(left, 0.0, qp))
        return _dot_nt(k_scr[hk, blk * w:blk * w + 2 * w, :],
                       jnp.concatenate(lhs, axis=0).astype(BF16))

    def attend(blk, hk, s4):
        q_rows = slice(blk * w, (blk + 1) * w)
        valid = band & ((step > 0) | (kj >= w)) if blk == 0 else band
        ps = []
        for g in range(Q_GROUP):
            hq = hk * Q_GROUP + g
            slope = 2.0 ** (-8.0 * (hq + 1) / N_Q_HEADS)
            a = jnp.where(valid, s4[:, g * w:(g + 1) * w] + slope * j_rel, -jnp.inf)
            sink = sinks_ref[0:1, hq:hq + 1] + slope * i_row
            mx = jnp.maximum(jnp.max(a, axis=0, keepdims=True), sink)
            p = jnp.exp(a - mx)
            inv = 1.0 / (jnp.sum(p, axis=0, keepdims=True) + jnp.exp(sink - mx))
            ps.append((p * inv).astype(BF16))
        o4 = _dot_tn(jnp.concatenate(ps, axis=1),
                     v_scr[hk, blk * w:blk * w + 2 * w, :])
        for j, c in enumerate((2 * hk, 2 * hk + 1)):
            o_scr[q_rows, c * 2 * HEAD_DIM:(c + 1) * 2 * HEAD_DIM] = jnp.where(
                left, o4[(2 * j) * w:(2 * j + 1) * w, :], o4[(2 * j + 1) * w:(2 * j + 2) * w, :])

    items = [(blk, hk) for blk in range(tq // w) for hk in range(N_KV_HEADS)]
    s_next = scores(*items[0])
    for n, item in enumerate(items):
        s_cur = s_next
        if n + 1 < len(items):
            s_next = scores(*items[n + 1])
        attend(*item, s_cur)

    _sample_attention(sq_ref, skn_ref, svn_ref, skc_ref, svc_ref, ssinks_ref, sslopes_ref,
                      so_ref, skwin_ref, svwin_ref)

    o = (o_scr[...] * _silu(qg[:, ATT_W:])).astype(BF16)
    out_ref[0] = y + _dot(o, wout_ref[...])


def _swa_prompt(y, kvnorm, wkv, knorm, normb, win, qnorm, sinks, wout, sample_q, sample_kn, sample_vn,
                cache_k, cache_v):
    b, l, d = y.shape
    tq = TQ_B
    w = WINDOW
    steps = l // tq
    n = cache_k.shape[0]
    bb = n // (b * steps)
    assert bb * b * steps == n
    f = lambda shape: jax.ShapeDtypeStruct(shape, F32)
    knorm2 = jnp.concatenate([knorm, knorm]).reshape(1, 2 * HEAD_DIM)
    qnorm2 = (jnp.concatenate([qnorm, qnorm]) * (HEAD_DIM ** -0.5)).reshape(1, 2 * HEAD_DIM)
    slopes = (2.0 ** (-8.0 * jnp.arange(1, N_Q_HEADS + 1, dtype=F32) / N_Q_HEADS)).reshape(N_Q_HEADS, 1)
    spec3 = lambda r, c: pl.BlockSpec((bb, r, c), lambda i, j: (i * steps + j, 0, 0))
    in_specs = [
        pl.BlockSpec((1, tq, d), lambda i, j: (i, j, 0)),
        _full_spec((1, d)), _full_spec(wkv.shape), _full_spec((1, 2 * HEAD_DIM)), _full_spec((1, d)),
        _full_spec(win.shape), _full_spec((1, 2 * HEAD_DIM)), _full_spec((1, N_Q_HEADS)),
        _full_spec(wout.shape),
        spec3(N_Q_HEADS, HEAD_DIM), spec3(1, KV_W), spec3(1, KV_W), spec3(w, KV_W), spec3(w, KV_W),
        _full_spec((N_Q_HEADS, 1)), _full_spec((N_Q_HEADS, 1)),
    ]
    out_specs = [
        pl.BlockSpec((1, tq, d), lambda i, j: (i, j, 0)),
        pl.BlockSpec((1, w, KV_W), lambda i, j: (i, 0, 0)),
        pl.BlockSpec((1, w, KV_W), lambda i, j: (i, 0, 0)),
        spec3(N_Q_HEADS, HEAD_DIM), spec3(w, KV_W), spec3(w, KV_W),
    ]
    out, kwin, vwin, o_s, kwin_s, vwin_s = pl.pallas_call(
        _swa_prompt_kernel,
        grid=(b, steps), in_specs=in_specs, out_specs=out_specs,
        out_shape=[f((b, l, d)), f((b, w, KV_W)), f((b, w, KV_W)),
                   f((n, N_Q_HEADS, HEAD_DIM)), f((n, w, KV_W)), f((n, w, KV_W))],
        scratch_shapes=[pltpu.VMEM((N_KV_HEADS, w + tq, 2 * HEAD_DIM), BF16),
                        pltpu.VMEM((N_KV_HEADS, w + tq, 2 * HEAD_DIM), BF16),
                        pltpu.VMEM((tq, ATT_W), F32)],
        compiler_params=pltpu.CompilerParams(
            dimension_semantics=("arbitrary", "arbitrary"), vmem_limit_bytes=VMEM_LIMIT),
        name="swa_prompt",
    )(y, kvnorm.reshape(1, d), wkv, knorm2, normb.reshape(1, d), win, qnorm2,
      sinks.reshape(1, N_Q_HEADS), wout,
      sample_q.reshape(n, N_Q_HEADS, HEAD_DIM), sample_kn.reshape(n, 1, KV_W), sample_vn.reshape(n, 1, KV_W),
      cache_k, cache_v, sinks.reshape(N_Q_HEADS, 1), slopes)
    return out, kwin, vwin, o_s.reshape(n, ATT_W), kwin_s, vwin_s


def _swa_sample_front_kernel(y_ref, kvnorm_ref, wkv_ref, knorm_ref, normb_ref, win_ref, qnorm_ref,
                             k_ref, v_ref, q_ref, gate_ref):
    y = y_ref[...]
    yn = y * lax.rsqrt(jnp.mean(y * y, axis=-1, keepdims=True) + EPS)
    kv = _dot((yn * kvnorm_ref[...]).astype(BF16), wkv_ref[...])
    k_ref[...] = jnp.concatenate(_head_rms(kv[:, :KV_W], knorm_ref[...], N_KV_HEADS), axis=1)
    v_ref[...] = kv[:, KV_W:]
    qg = _dot((yn * normb_ref[...]).astype(BF16), win_ref[...])
    q_ref[...] = jnp.concatenate(
        _head_rms(qg[:, :ATT_W], qnorm_ref[...] * (HEAD_DIM ** -0.5), N_Q_HEADS), axis=1)
    gate_ref[...] = qg[:, ATT_W:]


def _sample_attention(q_ref, kn_ref, vn_ref, kc_ref, vc_ref, sinks_ref, slopes_ref, o_ref, kwin_ref, vwin_ref):
    bb = kc_ref.shape[0]
    w = kc_ref.shape[1]
    hrow = lax.broadcasted_iota(jnp.int32, (N_Q_HEADS, KV_W), 0) // Q_GROUP
    lblk = lax.broadcasted_iota(jnp.int32, (N_Q_HEADS, KV_W), 1) // HEAD_DIM
    own = hrow == lblk
    dist_c = (w - lax.broadcasted_iota(jnp.int32, (1, w), 1)).astype(F32)
    krow = lax.broadcasted_iota(jnp.int32, (w, KV_W), 0)
    slopes = slopes_ref[...]
    sink = sinks_ref[...]
    qms, scs = [], []
    for b in range(bb):
        q = q_ref[b]
        qm = jnp.where(own, jnp.concatenate([q] * N_KV_HEADS, axis=1), 0.0).astype(BF16)
        qms.append(qm)
        scs.append(_dot_nt(qm, kc_ref[b].astype(BF16)))
    pcs, pns = [], []
    for b in range(bb):
        s_c = scs[b] - slopes * dist_c
        s_n = jnp.sum(qms[b].astype(F32) * kn_ref[b].astype(BF16).astype(F32), axis=-1, keepdims=True)
        mx = jnp.maximum(jnp.maximum(jnp.max(s_c, axis=-1, keepdims=True), s_n), sink)
        p_c = jnp.exp(s_c - mx)
        p_n = jnp.exp(s_n - mx)
        den = jnp.sum(p_c, axis=-1, keepdims=True) + p_n + jnp.exp(sink - mx)
        pcs.append((p_c / den).astype(BF16))
        pns.append(p_n / den)
    rs = [_dot(pcs[b], vc_ref[b].astype(BF16)) for b in range(bb)]
    for b in range(bb):
        kn = kn_ref[b]
        vn = vn_ref[b]
        r = rs[b] + pns[b].astype(BF16).astype(F32) * vn.astype(BF16).astype(F32)
        r = jnp.where(own, r, 0.0)
        acc = r[:, 0:HEAD_DIM]
        for blk in range(1, N_KV_HEADS):
            acc = acc + r[:, blk * HEAD_DIM:(blk + 1) * HEAD_DIM]
        o_ref[b] = acc
        kwin_ref[b] = jnp.where(krow == w - 1, kn, pltpu.roll(kc_ref[b], w - 1, 0))
        vwin_ref[b] = jnp.where(krow == w - 1, vn, pltpu.roll(vc_ref[b], w - 1, 0))


def _swa_sample_front(y, kvnorm, wkv, knorm, normb, win, qnorm):
    n, d = y.shape
    f = lambda shape: jax.ShapeDtypeStruct(shape, F32)
    return pl.pallas_call(
        _swa_sample_front_kernel,
        out_shape=[f((n, KV_W)), f((n, KV_W)), f((n, ATT_W)), f((n, ATT_W))],
        compiler_params=pltpu.CompilerParams(vmem_limit_bytes=VMEM_LIMIT),
        name="swa_sample_front",
    )(y, kvnorm.reshape(1, d), wkv, knorm.reshape(1, HEAD_DIM), normb.reshape(1, d), win,
      qnorm.reshape(1, HEAD_DIM))


def _swa_sample_out(y, o, gate, wout):
    return pl.pallas_call(
        functools.partial(_out_proj_kernel, head_w=ATT_W, norm_heads=False),
        out_shape=jax.ShapeDtypeStruct(y.shape, F32),
        name="swa_sample_out",
    )(y, o, gate, jnp.ones((1, ATT_W), F32), wout)


def kernel(x_prompt, x_sample, state_conv, state_ssm, cache_k_win, cache_v_win, norm_a, w_in_a, conv_w_a, a_log, dt_bias, o_norm_a, w_out_a, kv_norm, w_kv, k_norm, norm_b, w_in_b, q_norm, sinks, w_out_b):
    n_a = w_in_a.shape[0]
    n_b = w_in_b.shape[0]
    assert n_a == 1 and n_b == 1, "kernel is written for DEPTH == 2"
    bp, lp, d = x_prompt.shape
    n = x_sample.shape[0]
    hw = GDN_HEADS * GDN_D

    hp, hs = x_prompt, x_sample.reshape(n, d)
    conv_p, ssm_p, conv_s, ssm_s = [], [], [], []
    for layer in range(n_a):
        win_a = w_in_a[layer].astype(BF16)
        wabt = win_a[:, QKV_W + hw:].T
        wout = w_out_a[layer].astype(BF16)
        rows_s, gate_s, cbuf_s = _gdn_sample_front(hs, state_conv[layer], norm_a[layer], win_a,
                                                   conv_w_a[layer], a_log[layer], dt_bias[layer])
        hp, cbuf, st, o_s, st_s = _gdn_prompt(hp, norm_a[layer], win_a, wabt, conv_w_a[layer],
                                              a_log[layer], dt_bias[layer], o_norm_a[layer], wout,
                                              rows_s, state_ssm[layer])
        conv_p.append(cbuf)
        ssm_p.append(st)
        hs = _gdn_sample_out(hs, o_s, gate_s, o_norm_a[layer], wout)
        conv_s.append(cbuf_s)
        ssm_s.append(st_s)

    wkv = w_kv.astype(BF16)
    win = w_in_b[0].astype(BF16)
    woutb = w_out_b[0].astype(BF16)
    kn_s, vn_s, q_s, gate_s = _swa_sample_front(hs, kv_norm, wkv, k_norm, norm_b[0], win, q_norm[0])
    hp, k_win_p, v_win_p, o_s, k_win_s, v_win_s = _swa_prompt(
        hp, kv_norm, wkv, k_norm, norm_b[0], win, q_norm[0], sinks[0], woutb, q_s, kn_s, vn_s,
        cache_k_win.reshape(n, WINDOW, KV_W), cache_v_win.reshape(n, WINDOW, KV_W))
    hs = _swa_sample_out(hs, o_s, gate_s, woutb)
    kv_shape = (N_KV_HEADS, HEAD_DIM)
    return (hp, hs.reshape(n, 1, d), jnp.stack(conv_p), jnp.stack(ssm_p),
            k_win_p.reshape(bp, WINDOW, *kv_shape), v_win_p.reshape(bp, WINDOW, *kv_shape),
            jnp.stack(conv_s), jnp.stack(ssm_s),
            k_win_s.reshape(n, WINDOW, *kv_shape), v_win_s.reshape(n, WINDOW, *kv_shape))
```

```python
import functools

import jax
import jax.numpy as jnp
from jax import lax
from jax.experimental import pallas as pl
from jax.experimental.pallas import tpu as pltpu

F32 = jnp.float32
BF16 = jnp.bfloat16
EPS = 1e-6

D_MODEL = 1024
GDN_HEADS = 8
GDN_D = 128
QKV_W = 3 * GDN_HEADS * GDN_D
CONV_W = 4
CHUNK = 64
N_Q_HEADS = 16
N_KV_HEADS = 4
Q_GROUP = N_Q_HEADS // N_KV_HEADS
HEAD_DIM = 64
KV_W = N_KV_HEADS * HEAD_DIM
ATT_W = N_Q_HEADS * HEAD_DIM
WINDOW = 128

TM_A = 512
SUB_A = 256
TQ_B = 1024
CARRY = 8
VMEM_LIMIT = 52 * 1024 * 1024
VMEM_LIMIT_A = 62 * 1024 * 1024

_NT = (((1,), (1,)), ((), ()))
_TN = (((0,), (0,)), ((), ()))


def _dot(a, b):
    return jnp.dot(a, b, preferred_element_type=F32)


def _dot_nt(a, b):
    return lax.dot_general(a, b, _NT, preferred_element_type=F32)


def _dot_tn(a, b):
    return lax.dot_general(a, b, _TN, preferred_element_type=F32)


def _split(x):
    hi = x.astype(BF16)
    lo = (x - hi.astype(F32)).astype(BF16)
    return hi, lo


def _dot_exact_lhs(a_bf, b):
    b0 = b.astype(BF16)
    r1 = b - b0.astype(F32)
    b1 = r1.astype(BF16)
    b2 = (r1 - b1.astype(F32)).astype(BF16)
    return (_dot(a_bf, b2) + _dot(a_bf, b1)) + _dot(a_bf, b0)


def _dot_exact_rhs(a, b_bf):
    a0 = a.astype(BF16)
    r1 = a - a0.astype(F32)
    a1 = r1.astype(BF16)
    a2 = (r1 - a1.astype(F32)).astype(BF16)
    return (_dot(a2, b_bf) + _dot(a1, b_bf)) + _dot(a0, b_bf)


def _silu(x):
    return x * jax.nn.sigmoid(x)


def _softplus(x):
    return jnp.maximum(x, 0.0) + jnp.log1p(jnp.exp(-jnp.abs(x)))


def _halves(x, left):
    zero = jnp.zeros_like(x)
    return jnp.where(left, x, zero), jnp.where(left, zero, x)


def _block_diag(x, left):
    return jnp.concatenate(_halves(x, left), axis=0)


def _pair_split_product(a, b, left):
    ah, al = _split(a)
    bh, bl = _split(b)
    bh1, bh2 = _halves(bh, left)
    bl1, bl2 = _halves(bl, left)
    zero = jnp.zeros_like(bh)
    rhs = jnp.concatenate([jnp.concatenate([bh1, bl1], axis=1), jnp.concatenate([bh2, bl2], axis=1),
                           jnp.concatenate([bh1, zero], axis=1), jnp.concatenate([bh2, zero], axis=1)], axis=0)
    y = _dot(jnp.concatenate([ah, al], axis=1), rhs)
    half = y.shape[1] // 2
    return y[:, :half] + y[:, half:]


def _unit_lower_inverse_pairs(ms, left):
    n = ms[0].shape[0]
    row = lax.broadcasted_iota(jnp.int32, (n, 2 * n), 0)
    col = lax.broadcasted_iota(jnp.int32, (n, 2 * n), 1) % n
    eye2 = jnp.where(row == col, 1.0, 0.0).astype(F32)

    def times(a, b):
        return _dot(a.astype(BF16), _block_diag(b.astype(BF16), left))

    ts = [eye2 - m for m in ms]
    ps = [times(m, m) for m in ms]
    yield
    steps = max(1, (n - 1).bit_length()) - 1
    for _ in range(steps - 1):
        both = [_dot(jnp.concatenate([t.astype(BF16), p.astype(BF16)], axis=0), _block_diag(p.astype(BF16), left))
                for t, p in zip(ts, ps)]
        ts = [t + tp[:n, :] for t, tp in zip(ts, both)]
        ps = [tp[n:, :] for tp in both]
        yield
    ts = [t + times(t, p) for t, p in zip(ts, ps)]
    yield
    rs = [(eye2 - t) - _pair_split_product(m, t, left) for m, t in zip(ms, ts)]
    yield
    ts = [t + times(t, r) for t, r in zip(ts, rs)]
    yield
    return ts


def _full_spec(shape):
    nd = len(shape)
    return pl.BlockSpec(shape, lambda *_: (0,) * nd, pipeline_mode=pl.Buffered(1))


def _gdn_gates(ab, ab_t, alog_r, dtb_r, alog_c, dtb_c, tm, chunk):
    h = GDN_HEADS
    g_c = -jnp.exp(alog_r) * _softplus(ab[:, :h] + dtb_r)
    beta = jax.nn.sigmoid(ab[:, h:])
    g_r = -jnp.exp(alog_c) * _softplus(ab_t[:h, :] + dtb_c)
    row = lax.broadcasted_iota(jnp.int32, (tm, tm), 0)
    col = lax.broadcasted_iota(jnp.int32, (tm, tm), 1)
    same = (row // chunk) == (col // chunk)
    lower = jnp.where(same & (row >= col), 1.0, 0.0).astype(BF16)
    upper = jnp.where(same & (row <= col), 1.0, 0.0).astype(BF16)
    gc = _dot_exact_lhs(lower, g_c)
    gr = _dot_exact_rhs(g_r, upper)
    return beta, gc, gr


def _sample_state_update(row0, q_ref, k_ref, u_ref, w_ref, qd_ref, eg_ref, s_ref, o_ref, s_out_ref):
    bb = s_ref.shape[0]
    group = 8
    assert group % bb == 0
    base = pl.multiple_of((row0 // group) * group, group)
    part = (row0 - base) // bb

    def rows(ref, cols):
        x = ref[pl.ds(base, group), cols]
        out = x[0:bb, :]
        for p in range(1, group // bb):
            out = jnp.where(part == p, x[p * bb:(p + 1) * bb, :], out)
        return out

    row = lax.broadcasted_iota(jnp.int32, (8, GDN_D), 0)
    decay_all = rows(eg_ref, slice(None))
    prods, ins = [], []
    for h in range(GDN_HEADS):
        cols = slice(h * GDN_D, (h + 1) * GDN_D)
        q, k, u, w, qd = (rows(r, cols) for r in (q_ref, k_ref, u_ref, w_ref, qd_ref))
        for b in range(bb):
            lhs = jnp.where(row == 0, w[b:b + 1, :], jnp.where(row == 1, qd[b:b + 1, :], 0.0))
            prods.append(_dot(lhs.astype(BF16), s_ref[b, h].astype(BF16)))
            ins.append((b, h, cols, q[b:b + 1, :], k[b:b + 1, :], u[b:b + 1, :]))
    for (b, h, cols, q, k, u), prod in zip(ins, prods):
        q = q.astype(BF16).astype(F32)
        k = k.astype(BF16).astype(F32)
        v_new = u - prod[0:1, :]
        vb = v_new.astype(BF16).astype(F32)
        qk = jnp.sum(q * k, axis=-1, keepdims=True)
        o_ref[b, :, cols] = prod[1:2, :] + qk.astype(BF16).astype(F32) * vb
        k8 = jnp.where(row == 0, k, 0.0).astype(BF16)
        v8 = jnp.where(row == 0, vb, 0.0).astype(BF16)
        s_out_ref[b, h] = s_ref[b, h] * decay_all[b:b + 1, h:h + 1] + _dot_tn(k8, v8)


def _gdn_prompt_kernel(x_ref, norm_ref, win_ref, wabt_ref, convw_ref,
                       alog_r_ref, dtb_r_ref, alog_c_ref, dtb_c_ref, onorm_ref, wout_ref,
                       sq_ref, sk_ref, su_ref, sw_ref, sqd_ref, seg_ref, sstate_ref,
                       y_ref, conv_ref, ssm_ref, so_ref, sstate_out_ref,
                       pre_scr, gate_scr, s_scr, k_scr, kb_scr, q_scr, qd_scr, kd_scr,
                       rhs_scr, dec_scr, gt_scr, o_scr, lhs_scr, su_scr):
    tm = x_ref.shape[1]
    sub = SUB_A
    n_sub = tm // sub
    n_chunks = sub // CHUNK
    gate_w = GDN_HEADS * GDN_D
    l = pl.program_id(1)

    @pl.when(l == 0)
    def _():
        pre_scr[0:CARRY, :] = jnp.zeros((CARRY, QKV_W), F32)
        s_scr[...] = jnp.zeros(s_scr.shape, F32)

    ci = lax.broadcasted_iota(jnp.int32, (CHUNK, 2 * CHUNK), 0)
    cj = lax.broadcasted_iota(jnp.int32, (CHUNK, 2 * CHUNK), 1) % CHUNK
    causal = ci >= cj
    strict = ci > cj
    left = lax.broadcasted_iota(jnp.int32, (1, 2 * CHUNK), 1) < CHUNK
    heads = list(range(GDN_HEADS))

    sample_row0 = (pl.program_id(0) * pl.num_programs(1) + l) * sstate_ref.shape[0]
    _sample_state_update(sample_row0, sq_ref, sk_ref, su_ref, sw_ref, sqd_ref, seg_ref, sstate_ref,
                         so_ref, sstate_out_ref)

    def start(s):
        x = x_ref[0, s * sub:(s + 1) * sub, :]
        xn = x * lax.rsqrt(jnp.mean(x * x, axis=-1, keepdims=True) + EPS) * norm_ref[...]
        xb = xn.astype(BF16)
        ab = _dot(xb, win_ref[:, QKV_W + gate_w:])
        ab_t = _dot_nt(wabt_ref[...], xb)
        beta, gc, gr = _gdn_gates(ab, ab_t, alog_r_ref[...], dtb_r_ref[...],
                                  alog_c_ref[...], dtb_c_ref[...], sub, CHUNK)
        g_last = jnp.concatenate(
            [jnp.broadcast_to(gc[c * CHUNK + CHUNK - 1:c * CHUNK + CHUNK, :], (CHUNK, GDN_HEADS))
             for c in range(n_chunks)], axis=0)
        return dict(x=x, xb=xb, beta=beta, gc=gc, gr=gr, e_g=jnp.exp(gc),
                    e_kd=jnp.exp(g_last - gc), e_tot=jnp.exp(g_last))

    def project(s, st):
        for part in range(3):
            cols = slice(part * gate_w, (part + 1) * gate_w)
            pre_scr[CARRY + s * sub:CARRY + (s + 1) * sub, cols] = _dot(st["xb"], win_ref[:, cols])

    def project_gate(s, st):
        gate_scr[s * sub:(s + 1) * sub, :] = _dot(st["xb"], win_ref[:, QKV_W:QKV_W + gate_w])

    def conv_act(s, j):
        cols = slice(j * GDN_D, (j + 1) * GDN_D)
        ext = pre_scr[s * sub:s * sub + CARRY + sub, cols]
        half = ext[CARRY:, :] * (0.5 * convw_ref[CONV_W - 1:CONV_W, cols])
        for back in range(1, CONV_W):
            tap = CONV_W - 1 - back
            half = half + pltpu.roll(ext, back, 0)[CARRY:, :] * (0.5 * convw_ref[tap:tap + 1, cols])
        return half + half * jnp.tanh(half)

    def head_prep(s, st, h):
        rows = slice(s * sub, (s + 1) * sub)
        qh = conv_act(s, h)
        kh = conv_act(s, GDN_HEADS + h)
        vh = conv_act(s, 2 * GDN_HEADS + h)
        qn = qh * lax.rsqrt(jnp.sum(qh * qh, axis=-1, keepdims=True) + EPS) * (GDN_D ** -0.5)
        kn = kh * lax.rsqrt(jnp.sum(kh * kh, axis=-1, keepdims=True) + EPS)
        b_col = st["beta"][:, h:h + 1]
        eg_col = st["e_g"][:, h:h + 1]
        kb = kn * b_col
        k_scr[h, rows, :] = kn.astype(BF16)
        kb_scr[h, rows, :] = kb.astype(BF16)
        q_scr[h, rows, :] = qn.astype(BF16)
        qd_scr[h, rows, :] = qn * eg_col
        kd_scr[h, rows, :] = (kn * st["e_kd"][:, h:h + 1]).astype(BF16)
        rhs_scr[h, rows, 0:GDN_D] = (vh * b_col).astype(BF16)
        rhs_scr[h, rows, GDN_D:2 * GDN_D] = (kb * eg_col).astype(BF16)
        gc, gr = st["gc"], st["gr"]
        for c in range(n_chunks):
            gt_scr[h, s * n_chunks + c:s * n_chunks + c + 1, :] = jnp.broadcast_to(
                st["e_tot"][c * CHUNK:c * CHUNK + 1, h:h + 1], (1, GDN_D))
        for j in range(n_chunks // 2):
            r1 = slice(2 * j * CHUNK, (2 * j + 1) * CHUNK)
            r2 = slice((2 * j + 1) * CHUNK, (2 * j + 2) * CHUNK)
            diff = jnp.where(left, gc[r1, h:h + 1], gc[r2, h:h + 1]) - gr[h:h + 1, 2 * j * CHUNK:(2 * j + 2) * CHUNK]
            pair = s * (n_chunks // 2) + j
            dec_scr[h, pair * CHUNK:(pair + 1) * CHUNK, :] = jnp.exp(jnp.where(causal, diff, -jnp.inf))

    def chunk_terms(s):
        items = [(h, s * (n_chunks // 2) + j) for h in heads for j in range(n_chunks // 2)]
        aqs = []
        for h, pair in items:
            r12 = slice(2 * pair * CHUNK, (2 * pair + 2) * CHUNK)
            aqs.append(_dot_nt(jnp.concatenate([kb_scr[h, r12, :], q_scr[h, r12, :]], axis=0), k_scr[h, r12, :]))
        yield
        decs = [dec_scr[h, pair * CHUNK:(pair + 1) * CHUNK, :] for h, pair in items]
        ms = [jnp.where(strict, jnp.where(left, aq[:CHUNK, :], aq[CHUNK:2 * CHUNK, :]) * d, 0.0)
              for aq, d in zip(aqs, decs)]
        qks = [jnp.where(causal, jnp.where(left, aq[2 * CHUNK:3 * CHUNK, :], aq[3 * CHUNK:, :]) * d, 0.0)
               for aq, d in zip(aqs, decs)]
        ts = yield from _unit_lower_inverse_pairs(ms, left)
        sols = []
        for (h, pair), t, qk in zip(items, ts, qks):
            lo = t - t.astype(BF16).astype(F32)
            t_parts = (jnp.where(left, t, pltpu.roll(lo, CHUNK, 1)).astype(BF16),
                       jnp.where(left, pltpu.roll(t, CHUNK, 1), lo).astype(BF16))
            qk_parts = (qk[:, :CHUNK].astype(BF16), pltpu.roll(qk, CHUNK, 1)[:, :CHUNK].astype(BF16))
            for half in range(2):
                c = 2 * pair + half
                r = slice(c * CHUNK, (c + 1) * CHUNK)
                rhs = rhs_scr[h, r, :]
                sols.append((h, c, r, qk_parts[half],
                             _dot(t_parts[half], jnp.concatenate([rhs, rhs], axis=0)).astype(BF16)))
        yield
        outs = [(h, c, r, _dot_tn(kd_scr[h, r, :], sol),
                 _dot(qk_c, sol))
                for h, c, r, qk_c, sol in sols]
        for h, c, r, kd_uw, qk_uw in outs:
            lhs_scr[h, c, 0:GDN_D, :] = kd_uw[:, GDN_D:].astype(BF16)
            lhs_scr[h, c, GDN_D:GDN_D + CHUNK, :] = (qd_scr[h, r, :] - qk_uw[:, GDN_D:]).astype(BF16)
            su_scr[h, c] = kd_uw[:, :GDN_D]
            o_scr[h, r, :] = qk_uw[:, :GDN_D]
        yield

    def state_updates(s):
        for c in range(s * n_chunks, (s + 1) * n_chunks):
            rows = slice(c * CHUNK, (c + 1) * CHUNK)
            for h in heads:
                st_h = s_scr[h]
                prod = _dot(lhs_scr[h, c], st_h.astype(BF16))
                o_scr[h, rows, :] = o_scr[h, rows, :] + prod[GDN_D:, :]
                s_scr[h] = st_h * gt_scr[h, c:c + 1, :] + (su_scr[h, c] - prod[:GDN_D, :])
            yield

    def finish(s, st):
        rows = slice(s * sub, (s + 1) * sub)
        pieces = []
        for h in heads:
            o = o_scr[h, rows, :]
            on = o * lax.rsqrt(jnp.mean(o * o, axis=-1, keepdims=True) + EPS) * onorm_ref[...]
            pieces.append((on * _silu(gate_scr[rows, h * GDN_D:(h + 1) * GDN_D])).astype(BF16))
        on_all = jnp.concatenate(pieces, axis=1)
        for piece in range(D_MODEL // (2 * GDN_D)):
            cols = slice(piece * 2 * GDN_D, (piece + 1) * 2 * GDN_D)
            y_ref[0, rows, cols] = st["x"][:, cols] + _dot(on_all, wout_ref[:, cols])
            yield

    def run(gen):
        for _ in gen:
            pass

    def interleave(main, side, every):
        for n, _ in enumerate(main):
            if (n + 1) % every == 0:
                next(side, None)
        run(side)

    sts = [start(0)]
    project(0, sts[0])
    for s in range(n_sub):
        if s + 1 < n_sub:
            sts.append(start(s + 1))
        for h in heads:
            head_prep(s, sts[s], h)
        if s + 1 < n_sub:
            project(s + 1, sts[s + 1])
        project_gate(s, sts[s])
        if s == 0:
            run(chunk_terms(s))
        else:
            interleave(chunk_terms(s), state_updates(s - 1), 2)
            if s >= 2:
                run(finish(s - 2, sts[s - 2]))
    conv_ref[0] = pre_scr[tm + CARRY - 3:tm + CARRY, :]
    pre_scr[0:CARRY, :] = pre_scr[tm:tm + CARRY, :]
    if n_sub >= 2:
        interleave(state_updates(n_sub - 1), finish(n_sub - 2, sts[n_sub - 2]), 1)
    else:
        run(state_updates(0))

    @pl.when(l == pl.num_programs(1) - 1)
    def _():
        ssm_ref[0] = s_scr[...]

    run(finish(n_sub - 1, sts[n_sub - 1]))


def _gdn_prompt(x, norm, win, wabt, convw, alog, dtb, onorm, wout, sample_rows, sample_state):
    b, l, d = x.shape
    tm = TM_A
    h = GDN_HEADS
    grid = (b, l // tm)
    n = sample_state.shape[0]
    bb = n // (grid[0] * grid[1])
    assert bb * grid[0] * grid[1] == n
    sample_block = lambda i, j: (i * grid[1] + j, 0, 0)
    row_spec = lambda width: pl.BlockSpec((bb, 1, width), sample_block)
    state_spec = pl.BlockSpec((bb, h, GDN_D, GDN_D), lambda i, j: (i * grid[1] + j, 0, 0, 0))
    in_specs = [
        pl.BlockSpec((1, tm, d), lambda i, j: (i, j, 0)),
        _full_spec((1, d)), _full_spec(win.shape), _full_spec(wabt.shape), _full_spec(convw.shape),
        _full_spec((1, h)), _full_spec((1, h)), _full_spec((h, 1)), _full_spec((h, 1)),
        _full_spec((1, GDN_D)), _full_spec(wout.shape),
    ] + [_full_spec(r.shape) for r in sample_rows] + [state_spec]
    out_specs = [
        pl.BlockSpec((1, tm, d), lambda i, j: (i, j, 0)),
        pl.BlockSpec((1, CONV_W - 1, QKV_W), lambda i, j: (i, 0, 0)),
        pl.BlockSpec((1, h, GDN_D, GDN_D), lambda i, j: (i, 0, 0, 0)),
        row_spec(h * GDN_D), state_spec,
    ]
    out_shape = [
        jax.ShapeDtypeStruct((b, l, d), F32),
        jax.ShapeDtypeStruct((b, CONV_W - 1, QKV_W), F32),
        jax.ShapeDtypeStruct((b, h, GDN_D, GDN_D), F32),
        jax.ShapeDtypeStruct((n, 1, h * GDN_D), F32),
        jax.ShapeDtypeStruct(sample_state.shape, F32),
    ]
    scratch = [
        pltpu.VMEM((tm + CARRY, QKV_W), F32),
        pltpu.VMEM((tm, h * GDN_D), F32),
        pltpu.VMEM((h, GDN_D, GDN_D), F32),
        pltpu.VMEM((h, tm, GDN_D), BF16),
        pltpu.VMEM((h, tm, GDN_D), BF16),
        pltpu.VMEM((h, tm, GDN_D), BF16),
        pltpu.VMEM((h, tm, GDN_D), F32),
        pltpu.VMEM((h, tm, GDN_D), BF16),
        pltpu.VMEM((h, tm, 2 * GDN_D), BF16),
        pltpu.VMEM((h, tm // 2, 2 * CHUNK), F32),
        pltpu.VMEM((h, tm // CHUNK, GDN_D), F32),
        pltpu.VMEM((h, tm, GDN_D), F32),
        pltpu.VMEM((h, tm // CHUNK, GDN_D + CHUNK, GDN_D), BF16),
        pltpu.VMEM((h, tm // CHUNK, GDN_D, GDN_D), F32),
    ]
    y, conv, ssm, o_s, state_s = pl.pallas_call(
        _gdn_prompt_kernel,
        grid=grid, in_specs=in_specs, out_specs=out_specs, out_shape=out_shape,
        scratch_shapes=scratch,
        compiler_params=pltpu.CompilerParams(
            dimension_semantics=("arbitrary", "arbitrary"), vmem_limit_bytes=VMEM_LIMIT_A),
        name="gdn_prompt",
    )(x, norm.reshape(1, d), win, wabt, convw,
      alog.reshape(1, h), dtb.reshape(1, h), alog.reshape(h, 1), dtb.reshape(h, 1),
      onorm.reshape(1, GDN_D), wout, *sample_rows, sample_state)
    return y, conv, ssm, o_s.reshape(n, h * GDN_D), state_s


def _gdn_sample_front_kernel(x_ref, norm_ref, win_ref, convw_ref,
                             c0_ref, c1_ref, c2_ref, alog_r_ref, dtb_r_ref,
                             pre_ref, q_ref, k_ref, u_ref, w_ref, qd_ref, gate_ref, eg_ref):
    x = x_ref[...]
    xn = x * lax.rsqrt(jnp.mean(x * x, axis=-1, keepdims=True) + EPS) * norm_ref[...]
    xb = xn.astype(BF16)
    gate_w = GDN_HEADS * GDN_D
    pre = _dot(xb, win_ref[:, :QKV_W])
    pre_ref[...] = pre
    gate_ref[...] = _dot(xb, win_ref[:, QKV_W:QKV_W + gate_w])
    ab = _dot(xb, win_ref[:, QKV_W + gate_w:])
    g = -jnp.exp(alog_r_ref[...]) * _softplus(ab[:, :GDN_HEADS] + dtb_r_ref[...])
    beta = jax.nn.sigmoid(ab[:, GDN_HEADS:])
    e_g = jnp.exp(g)
    eg_ref[...] = e_g
    act = _silu(((c0_ref[...] * convw_ref[0:1, :] + c1_ref[...] * convw_ref[1:2, :])
                 + c2_ref[...] * convw_ref[2:3, :]) + pre * convw_ref[3:4, :])
    w_all = GDN_HEADS * GDN_D
    for h in range(GDN_HEADS):
        cols = slice(h * GDN_D, (h + 1) * GDN_D)
        qh = act[:, h * GDN_D:(h + 1) * GDN_D]
        kh = act[:, w_all + h * GDN_D:w_all + (h + 1) * GDN_D]
        vh = act[:, 2 * w_all + h * GDN_D:2 * w_all + (h + 1) * GDN_D]
        qn = qh * lax.rsqrt(jnp.sum(qh * qh, axis=-1, keepdims=True) + EPS) * (GDN_D ** -0.5)
        kn = kh * lax.rsqrt(jnp.sum(kh * kh, axis=-1, keepdims=True) + EPS)
        b_col = beta[:, h:h + 1]
        eg_col = e_g[:, h:h + 1]
        q_ref[:, cols] = qn
        k_ref[:, cols] = kn
        u_ref[:, cols] = vh * b_col
        w_ref[:, cols] = kn * b_col * eg_col
        qd_ref[:, cols] = qn * eg_col


def _out_proj_kernel(x_ref, o_ref, gate_ref, onorm_ref, wout_ref, y_ref, *, head_w, norm_heads):
    pieces = []
    for h in range(x_ref.shape[1] // head_w):
        cols = slice(h * head_w, (h + 1) * head_w)
        o = o_ref[:, cols]
        if norm_heads:
            o = o * lax.rsqrt(jnp.mean(o * o, axis=-1, keepdims=True) + EPS) * onorm_ref[...]
        pieces.append((o * _silu(gate_ref[:, cols])).astype(BF16))
    y_ref[...] = x_ref[...] + _dot(jnp.concatenate(pieces, axis=1), wout_ref[...])


def _gdn_sample_front(x, conv_state, norm, win, convw, alog, dtb):
    n, d = x.shape
    h = GDN_HEADS
    hw = h * GDN_D
    conv_t = jnp.transpose(conv_state, (1, 0, 2))
    f = lambda shape: jax.ShapeDtypeStruct(shape, F32)
    pre, q, k, u, w, qd, gate, eg = pl.pallas_call(
        _gdn_sample_front_kernel,
        out_shape=[f((n, QKV_W)), f((n, hw)), f((n, hw)), f((n, hw)), f((n, hw)), f((n, hw)),
                   f((n, hw)), f((n, h))],
        compiler_params=pltpu.CompilerParams(vmem_limit_bytes=VMEM_LIMIT),
        name="gdn_sample_front",
    )(x, norm.reshape(1, d), win, convw, conv_t[0], conv_t[1], conv_t[2],
      alog.reshape(1, h), dtb.reshape(1, h))
    conv_new = jnp.stack([conv_t[1], conv_t[2], pre], axis=1)
    return (q, k, u, w, qd, eg), gate, conv_new


def _gdn_sample_out(x, o, gate, onorm, wout):
    return pl.pallas_call(
        functools.partial(_out_proj_kernel, head_w=GDN_D, norm_heads=True),
        out_shape=jax.ShapeDtypeStruct(x.shape, F32),
        name="gdn_sample_out",
    )(x, o, gate, onorm.reshape(1, GDN_D), wout)


def _head_rms(x, gain_row, n_heads):
    pieces = []
    for h in range(n_heads):
        xh = x[:, h * HEAD_DIM:(h + 1) * HEAD_DIM]
        pieces.append(xh * lax.rsqrt(jnp.mean(xh * xh, axis=-1, keepdims=True) + EPS) * gain_row)
    return pieces


def _pair_rms(x, gain2, left):
    sq = x * x
    lo = jnp.sum(jnp.where(left, sq, 0.0), axis=-1, keepdims=True)
    hi = jnp.sum(jnp.where(left, 0.0, sq), axis=-1, keepdims=True)
    ms = jnp.where(left, lo, hi) * (1.0 / HEAD_DIM)
    return x * lax.rsqrt(ms + EPS) * gain2


def _swa_prompt_kernel(y_ref, kvnorm_ref, wkv_ref, knorm2_ref, normb_ref, win_ref, qnorm2_ref,
                       sinks_ref, wout_ref,
                       sq_ref, skn_ref, svn_ref, skc_ref, svc_ref, ssinks_ref, sslopes_ref,
                       out_ref, kwin_ref, vwin_ref, so_ref, skwin_ref, svwin_ref,
                       k_scr, v_scr, o_scr):
    tq = y_ref.shape[1]
    w = WINDOW
    step = pl.program_id(1)
    lane = lax.broadcasted_iota(jnp.int32, (1, 2 * HEAD_DIM), 1)
    left = lane < HEAD_DIM

    @pl.when(step == 0)
    def _():
        k_scr[:, 0:w, :] = jnp.zeros((N_KV_HEADS, w, 2 * HEAD_DIM), BF16)
        v_scr[:, 0:w, :] = jnp.zeros((N_KV_HEADS, w, 2 * HEAD_DIM), BF16)

    @pl.when(step > 0)
    def _():
        k_scr[:, 0:w, :] = k_scr[:, tq:tq + w, :]
        v_scr[:, 0:w, :] = v_scr[:, tq:tq + w, :]

    _sample_attention(sq_ref, skn_ref, svn_ref, skc_ref, svc_ref, ssinks_ref, sslopes_ref,
                      so_ref, skwin_ref, svwin_ref)

    y = y_ref[0]
    yn = y * lax.rsqrt(jnp.mean(y * y, axis=-1, keepdims=True) + EPS)
    kv = _dot((yn * kvnorm_ref[...]).astype(BF16), wkv_ref[...])
    for c in range(N_KV_HEADS // 2):
        cols = slice(c * 2 * HEAD_DIM, (c + 1) * 2 * HEAD_DIM)
        kp = _pair_rms(kv[:, cols], knorm2_ref[...], left)
        vp = kv[:, KV_W + c * 2 * HEAD_DIM:KV_W + (c + 1) * 2 * HEAD_DIM]
        kwin_ref[0, :, cols] = kp[tq - w:, :]
        vwin_ref[0, :, cols] = vp[tq - w:, :]
        kr = pltpu.roll(kp, HEAD_DIM, 1)
        vr = pltpu.roll(vp, HEAD_DIM, 1)
        k_scr[2 * c, w:w + tq, :] = jnp.where(left, kp, kr).astype(BF16)
        k_scr[2 * c + 1, w:w + tq, :] = jnp.where(left, kr, kp).astype(BF16)
        v_scr[2 * c, w:w + tq, :] = jnp.where(left, vp, vr).astype(BF16)
        v_scr[2 * c + 1, w:w + tq, :] = jnp.where(left, vr, vp).astype(BF16)

    qg = _dot((yn * normb_ref[...]).astype(BF16), win_ref[...])
    q_pairs = [_pair_rms(qg[:, c * 2 * HEAD_DIM:(c + 1) * 2 * HEAD_DIM], qnorm2_ref[...], left)
               for c in range(N_Q_HEADS // 2)]

    kj = lax.broadcasted_iota(jnp.int32, (2 * w, w), 0)
    qi = lax.broadcasted_iota(jnp.int32, (2 * w, w), 1)
    dist = qi - kj + w
    band = (dist >= 0) & (dist <= w)
    j_rel = (kj - w).astype(F32)
    i_row = lax.broadcasted_iota(jnp.int32, (1, w), 1).astype(F32)

    def scores(blk, hk):
        q_rows = slice(blk * w, (blk + 1) * w)
        lhs = []
        for c in (2 * hk, 2 * hk + 1):
            qp = q_pairs[c][q_rows, :]
            lhs.append(jnp.where(left, qp, 0.0))
            lhs.append(jnp.where(left, 0.0, qp))
        return _dot_nt(k_scr[hk, blk * w:blk * w + 2 * w, :],
                       jnp.concatenate(lhs, axis=0).astype(BF16))

    def attend(blk, hk, s4):
        q_rows = slice(blk * w, (blk + 1) * w)
        valid = band & ((step > 0) | (kj >= w)) if blk == 0 else band
        ps = []
        for g in range(Q_GROUP):
            hq = hk * Q_GROUP + g
            slope = 2.0 ** (-8.0 * (hq + 1) / N_Q_HEADS)
            a = jnp.where(valid, s4[:, g * w:(g + 1) * w] + slope * j_rel, -jnp.inf)
            sink = sinks_ref[0:1, hq:hq + 1] + slope * i_row
            mx = jnp.maximum(jnp.max(a, axis=0, keepdims=True), sink)
            p = jnp.exp(a - mx)
            inv = 1.0 / (jnp.sum(p, axis=0, keepdims=True) + jnp.exp(sink - mx))
            ps.append((p * inv).astype(BF16))
        o4 = _dot_tn(jnp.concatenate(ps, axis=1),
                     v_scr[hk, blk * w:blk * w + 2 * w, :])
        for j, c in enumerate((2 * hk, 2 * hk + 1)):
            o_scr[q_rows, c * 2 * HEAD_DIM:(c + 1) * 2 * HEAD_DIM] = jnp.where(
                left, o4[(2 * j) * w:(2 * j + 1) * w, :], o4[(2 * j + 1) * w:(2 * j + 2) * w, :])

    items = [(blk, hk) for blk in range(tq // w) for hk in range(N_KV_HEADS)]
    s_next = scores(*items[0])
    for n, item in enumerate(items):
        s_cur = s_next
        if n + 1 < len(items):
            s_next = scores(*items[n + 1])
        attend(*item, s_cur)

    o = (o_scr[...] * _silu(qg[:, ATT_W:])).astype(BF16)
    out_ref[0] = y + _dot(o, wout_ref[...])


def _swa_prompt(y, kvnorm, wkv, knorm, normb, win, qnorm, sinks, wout, sample_q, sample_kn, sample_vn,
                cache_k, cache_v):
    b, l, d = y.shape
    tq = TQ_B
    w = WINDOW
    steps = l // tq
    n = cache_k.shape[0]
    bb = n // (b * steps)
    assert bb * b * steps == n
    f = lambda shape: jax.ShapeDtypeStruct(shape, F32)
    knorm2 = jnp.concatenate([knorm, knorm]).reshape(1, 2 * HEAD_DIM)
    qnorm2 = (jnp.concatenate([qnorm, qnorm]) * (HEAD_DIM ** -0.5)).reshape(1, 2 * HEAD_DIM)
    slopes = (2.0 ** (-8.0 * jnp.arange(1, N_Q_HEADS + 1, dtype=F32) / N_Q_HEADS)).reshape(N_Q_HEADS, 1)
    spec3 = lambda r, c: pl.BlockSpec((bb, r, c), lambda i, j: (i * steps + j, 0, 0))
    in_specs = [
        pl.BlockSpec((1, tq, d), lambda i, j: (i, j, 0)),
        _full_spec((1, d)), _full_spec(wkv.shape), _full_spec((1, 2 * HEAD_DIM)), _full_spec((1, d)),
        _full_spec(win.shape), _full_spec((1, 2 * HEAD_DIM)), _full_spec((1, N_Q_HEADS)),
        _full_spec(wout.shape),
        spec3(N_Q_HEADS, HEAD_DIM), spec3(1, KV_W), spec3(1, KV_W), spec3(w, KV_W), spec3(w, KV_W),
        _full_spec((N_Q_HEADS, 1)), _full_spec((N_Q_HEADS, 1)),
    ]
    out_specs = [
        pl.BlockSpec((1, tq, d), lambda i, j: (i, j, 0)),
        pl.BlockSpec((1, w, KV_W), lambda i, j: (i, 0, 0)),
        pl.BlockSpec((1, w, KV_W), lambda i, j: (i, 0, 0)),
        spec3(N_Q_HEADS, HEAD_DIM), spec3(w, KV_W), spec3(w, KV_W),
    ]
    out, kwin, vwin, o_s, kwin_s, vwin_s = pl.pallas_call(
        _swa_prompt_kernel,
        grid=(b, steps), in_specs=in_specs, out_specs=out_specs,
        out_shape=[f((b, l, d)), f((b, w, KV_W)), f((b, w, KV_W)),
                   f((n, N_Q_HEADS, HEAD_DIM)), f((n, w, KV_W)), f((n, w, KV_W))],
        scratch_shapes=[pltpu.VMEM((N_KV_HEADS, w + tq, 2 * HEAD_DIM), BF16),
                        pltpu.VMEM((N_KV_HEADS, w + tq, 2 * HEAD_DIM), BF16),
                        pltpu.VMEM((tq, ATT_W), F32)],
        compiler_params=pltpu.CompilerParams(
            dimension_semantics=("arbitrary", "arbitrary"), vmem_limit_bytes=VMEM_LIMIT),
        name="swa_prompt",
    )(y, kvnorm.reshape(1, d), wkv, knorm2, normb.reshape(1, d), win, qnorm2,
      sinks.reshape(1, N_Q_HEADS), wout,
      sample_q.reshape(n, N_Q_HEADS, HEAD_DIM), sample_kn.reshape(n, 1, KV_W), sample_vn.reshape(n, 1, KV_W),
      cache_k, cache_v, sinks.reshape(N_Q_HEADS, 1), slopes)
    return out, kwin, vwin, o_s.reshape(n, ATT_W), kwin_s, vwin_s


def _swa_sample_front_kernel(y_ref, kvnorm_ref, wkv_ref, knorm_ref, normb_ref, win_ref, qnorm_ref,
                             k_ref, v_ref, q_ref, gate_ref):
    y = y_ref[...]
    yn = y * lax.rsqrt(jnp.mean(y * y, axis=-1, keepdims=True) + EPS)
    kv = _dot((yn * kvnorm_ref[...]).astype(BF16), wkv_ref[...])
    k_ref[...] = jnp.concatenate(_head_rms(kv[:, :KV_W], knorm_ref[...], N_KV_HEADS), axis=1)
    v_ref[...] = kv[:, KV_W:]
    qg = _dot((yn * normb_ref[...]).astype(BF16), win_ref[...])
    q_ref[...] = jnp.concatenate(
        _head_rms(qg[:, :ATT_W], qnorm_ref[...] * (HEAD_DIM ** -0.5), N_Q_HEADS), axis=1)
    gate_ref[...] = qg[:, ATT_W:]


def _sample_attention(q_ref, kn_ref, vn_ref, kc_ref, vc_ref, sinks_ref, slopes_ref, o_ref, kwin_ref, vwin_ref):
    bb = kc_ref.shape[0]
    w = kc_ref.shape[1]
    hrow = lax.broadcasted_iota(jnp.int32, (N_Q_HEADS, KV_W), 0) // Q_GROUP
    lblk = lax.broadcasted_iota(jnp.int32, (N_Q_HEADS, KV_W), 1) // HEAD_DIM
    own = hrow == lblk
    dist_c = (w - lax.broadcasted_iota(jnp.int32, (1, w), 1)).astype(F32)
    krow = lax.broadcasted_iota(jnp.int32, (w, KV_W), 0)
    slopes = slopes_ref[...]
    sink = sinks_ref[...]
    qms, scs = [], []
    for b in range(bb):
        q = q_ref[b]
        qm = jnp.where(own, jnp.concatenate([q] * N_KV_HEADS, axis=1), 0.0).astype(BF16)
        qms.append(qm)
        scs.append(_dot_nt(qm, kc_ref[b].astype(BF16)))
    pcs, pns = [], []
    for b in range(bb):
        s_c = scs[b] - slopes * dist_c
        s_n = jnp.sum(qms[b].astype(F32) * kn_ref[b].astype(BF16).astype(F32), axis=-1, keepdims=True)
        mx = jnp.maximum(jnp.maximum(jnp.max(s_c, axis=-1, keepdims=True), s_n), sink)
        p_c = jnp.exp(s_c - mx)
        p_n = jnp.exp(s_n - mx)
        den = jnp.sum(p_c, axis=-1, keepdims=True) + p_n + jnp.exp(sink - mx)
        pcs.append((p_c / den).astype(BF16))
        pns.append(p_n / den)
    rs = [_dot(pcs[b], vc_ref[b].astype(BF16)) for b in range(bb)]
    for b in range(bb):
        kn = kn_ref[b]
        vn = vn_ref[b]
        r = rs[b] + pns[b].astype(BF16).astype(F32) * vn.astype(BF16).astype(F32)
        r = jnp.where(own, r, 0.0)
        acc = r[:, 0:HEAD_DIM]
        for blk in range(1, N_KV_HEADS):
            acc = acc + r[:, blk * HEAD_DIM:(blk + 1) * HEAD_DIM]
        o_ref[b] = acc
        kwin_ref[b] = jnp.where(krow == w - 1, kn, pltpu.roll(kc_ref[b], w - 1, 0))
        vwin_ref[b] = jnp.where(krow == w - 1, vn, pltpu.roll(vc_ref[b], w - 1, 0))


def _swa_sample_front(y, kvnorm, wkv, knorm, normb, win, qnorm):
    n, d = y.shape
    f = lambda shape: jax.ShapeDtypeStruct(shape, F32)
    return pl.pallas_call(
        _swa_sample_front_kernel,
        out_shape=[f((n, KV_W)), f((n, KV_W)), f((n, ATT_W)), f((n, ATT_W))],
        compiler_params=pltpu.CompilerParams(vmem_limit_bytes=VMEM_LIMIT),
        name="swa_sample_front",
    )(y, kvnorm.reshape(1, d), wkv, knorm.reshape(1, HEAD_DIM), normb.reshape(1, d), win,
      qnorm.reshape(1, HEAD_DIM))


def _swa_sample_out(y, o, gate, wout):
    return pl.pallas_call(
        functools.partial(_out_proj_kernel, head_w=ATT_W, norm_heads=False),
        out_shape=jax.ShapeDtypeStruct(y.shape, F32),
        name="swa_sample_out",
    )(y, o, gate, jnp.ones((1, ATT_W), F32), wout)


def kernel(x_prompt, x_sample, state_conv, state_ssm, cache_k_win, cache_v_win, norm_a, w_in_a, conv_w_a, a_log, dt_bias, o_norm_a, w_out_a, kv_norm, w_kv, k_norm, norm_b, w_in_b, q_norm, sinks, w_out_b):
    n_a = w_in_a.shape[0]
    n_b = w_in_b.shape[0]
    assert n_a == 1 and n_b == 1, "kernel is written for DEPTH == 2"
    bp, lp, d = x_prompt.shape
    n = x_sample.shape[0]
    hw = GDN_HEADS * GDN_D

    hp, hs = x_prompt, x_sample.reshape(n, d)
    conv_p, ssm_p, conv_s, ssm_s = [], [], [], []
    for layer in range(n_a):
        win_a = w_in_a[layer].astype(BF16)
        wabt = win_a[:, QKV_W + hw:].T
        wout = w_out_a[layer].astype(BF16)
        rows_s, gate_s, cbuf_s = _gdn_sample_front(hs, state_conv[layer], norm_a[layer], win_a,
                                                   conv_w_a[layer], a_log[layer], dt_bias[layer])
        hp, cbuf, st, o_s, st_s = _gdn_prompt(hp, norm_a[layer], win_a, wabt, conv_w_a[layer],
                                              a_log[layer], dt_bias[layer], o_norm_a[layer], wout,
                                              rows_s, state_ssm[layer])
        conv_p.append(cbuf)
        ssm_p.append(st)
        hs = _gdn_sample_out(hs, o_s, gate_s, o_norm_a[layer], wout)
        conv_s.append(cbuf_s)
        ssm_s.append(st_s)

    wkv = w_kv.astype(BF16)
    win = w_in_b[0].astype(BF16)
    woutb = w_out_b[0].astype(BF16)
    kn_s, vn_s, q_s, gate_s = _swa_sample_front(hs, kv_norm, wkv, k_norm, norm_b[0], win, q_norm[0])
    hp, k_win_p, v_win_p, o_s, k_win_s, v_win_s = _swa_prompt(
        hp, kv_norm, wkv, k_norm, norm_b[0], win, q_norm[0], sinks[0], woutb, q_s, kn_s, vn_s,
        cache_k_win.reshape(n, WINDOW, KV_W), cache_v_win.reshape(n, WINDOW, KV_W))
    hs = _swa_sample_out(hs, o_s, gate_s, woutb)
    kv_shape = (N_KV_HEADS, HEAD_DIM)
    return (hp, hs.reshape(n, 1, d), jnp.stack(conv_p), jnp.stack(ssm_p),
            k_win_p.reshape(bp, WINDOW, *kv_shape), v_win_p.reshape(bp, WINDOW, *kv_shape),
            jnp.stack(conv_s), jnp.stack(ssm_s),
            k_win_s.reshape(n, WINDOW, *kv_shape), v_win_s.reshape(n, WINDOW, *kv_shape))
```
